```python
import math
import jax
import jax.numpy as jnp
from jax import lax
import numpy as np

D_MODEL = 1024
BATCH = 16
SEQ = 4096
DEPTH = 4

GRID_W = 64
CTX_LEN = 256
D_MIX = D_MODEL
GROUP = D_MIX // 4

MLA_HEADS = 4
MLA_NOPE = 64
MLA_ROPE = 32
MLA_V = 64
MLA_Q_RANK = 192
MLA_KV_RANK = 128
ROPE_BASE = 10000.0
Q_BLOCK = 128

LRU_BLOCKS = 4
LRU_CONV = 4
LRU_CONV_LEFT = 2
LRU_C = 8.0

RWKV_HEADS = 4
RWKV_HEAD = GROUP // RWKV_HEADS
RWKV_DECAY_LORA = 32
RWKV_AAA_LORA = 32
RWKV_GATE_LORA = 64
RWKV_GN_EPS = 64e-5

HY_ORDER = 2
HY_SHORT = 3
HY_BANDS = 16
HY_EMB = 1 + 2 * HY_BANDS
HY_HIDDEN = 64
HY_SIN_FREQ = 1.0
HY_DECAY_MIN = math.log(1e-2) / 1.5
HY_DECAY_MAX = math.log(1e-2) / 0.3
HY_SHIFT = 0.05

FF_DENSE = 2816
N_EXPERTS = 8
TOP_K = 2
FF_EXPERT = 3584
MOE_BLOCK = 512

ALPHA = (2.0 * DEPTH) ** 0.25
BETA = (8.0 * DEPTH) ** -0.25

A_CQ = 0
A_CKV = A_CQ + MLA_Q_RANK
A_KR = A_CKV + MLA_KV_RANK
B_X = A_KR + MLA_ROPE
B_GATE = B_X + GROUP
C_OFF = B_GATE + GROUP
C_R = 0
C_K = GROUP
C_V = 2 * GROUP
C_WD = 3 * GROUP
C_AD = C_WD + 2 * RWKV_DECAY_LORA
C_GD = C_AD + 2 * RWKV_AAA_LORA
C_COLS = C_GD + RWKV_GATE_LORA
D_OFF = C_OFF + C_COLS
D_COLS = (HY_ORDER + 1) * GROUP
N_IN = D_OFF + D_COLS

kernel_name = 'hybrid_mla_rglru_rwkv7_hyena_moe_dit'


def modulate(x, shift, scale):
    return x * (1.0 + scale) + shift


def layer_norm(x, g, b, eps=1e-5):
    xf = x.astype(jnp.float32)
    mu = jnp.mean(xf, -1, keepdims=True)
    var = jnp.mean(jnp.square(xf - mu), -1, keepdims=True)
    return ((xf - mu) * lax.rsqrt(var + eps) * g + b).astype(x.dtype)


def rms_norm(x, g, eps=1e-6):
    xf = x.astype(jnp.float32)
    return (xf * lax.rsqrt(jnp.mean(jnp.square(xf), -1, keepdims=True) + eps) * g).astype(x.dtype)


def dwconv(x, w, b, left):
    K = w.shape[0]
    L = x.shape[1]
    xp = jnp.pad(x, ((0, 0), (left, K - 1 - left), (0, 0)))
    return sum(xp[:, j:j + L] * w[j] for j in range(K)) + b


def token_shift(z, mu_prev, mu_next):
    zp = jnp.pad(z, ((0, 0), (1, 1), (0, 0)))
    return z + mu_prev * (zp[:, :-2] - z) + mu_next * (zp[:, 2:] - z)


def axial_rope(rows):
    f32 = jnp.float32
    r, col = jnp.meshgrid(jnp.arange(rows, dtype=f32), jnp.arange(GRID_W, dtype=f32), indexing='ij')
    half = MLA_ROPE // 2
    inv = 1.0 / (ROPE_BASE ** (jnp.arange(0, half, 2, dtype=f32) / half))
    ang = jnp.concatenate([r.reshape(-1, 1) * inv, col.reshape(-1, 1) * inv], -1)
    return jnp.cos(ang), jnp.sin(ang)


def apply_rope(x, cos, sin):
    xf = x.astype(jnp.float32)
    h = xf.shape[-1] // 2
    x1, x2 = xf[..., :h], xf[..., h:]
    return jnp.concatenate([x1 * cos - x2 * sin, x1 * sin + x2 * cos], -1).astype(x.dtype)


def block_attention(q, k, v):
    B, Lq, H, Dh = q.shape
    nb = Lq // Q_BLOCK
    qb = jnp.moveaxis(q.reshape(B, nb, Q_BLOCK, H, Dh), 1, 0)
    scale = Dh ** -0.5

    def one(qblk):
        s = jnp.einsum('bqhd,bkhd->bhqk', qblk, k, preferred_element_type=jnp.float32) * scale
        p = jax.nn.softmax(s, axis=-1).astype(v.dtype)
        return jnp.einsum('bhqk,bkhd->bqhd', p, v)

    o = lax.map(one, qb)
    return jnp.moveaxis(o, 0, 1).reshape(B, Lq, H, v.shape[-1])


def mla_queries(P, q_norm, w_uq, cos, sin):
    B, L, _ = P.shape
    cq = rms_norm(P[..., A_CQ:A_CQ + MLA_Q_RANK], q_norm)
    q = (cq @ w_uq).reshape(B, L, MLA_HEADS, MLA_NOPE + MLA_ROPE)
    if cos is None:
        return q
    q_rope = apply_rope(q[..., MLA_NOPE:], cos[None, :, None, :], sin[None, :, None, :])
    return jnp.concatenate([q[..., :MLA_NOPE], q_rope], -1)


def mla_keys(P, kv_norm, w_ukv, cos, sin):
    B, L, _ = P.shape
    ckv = rms_norm(P[..., A_CKV:A_CKV + MLA_KV_RANK], kv_norm)
    kv = (ckv @ w_ukv).reshape(B, L, MLA_HEADS, MLA_NOPE + MLA_V)
    k_nope, v = kv[..., :MLA_NOPE], kv[..., MLA_NOPE:]
    k_rope = P[..., A_KR:A_KR + MLA_ROPE]
    if cos is not None:
        k_rope = apply_rope(k_rope, cos[None], sin[None])
    k_rope = jnp.broadcast_to(k_rope[:, :, None, :], (B, L, MLA_HEADS, MLA_ROPE))
    return jnp.concatenate([k_nope, k_rope], -1), v


def mla_mixer(Pl, Pc, q_norm, kv_norm, w_uq, w_ukv, out_norm, cos, sin, ctx_out):
    B, L, _ = Pl.shape
    kl, vl = mla_keys(Pl, kv_norm, w_ukv, cos, sin)
    kc, vc = mla_keys(Pc, kv_norm, w_ukv, None, None)
    ql = mla_queries(Pl, q_norm, w_uq, cos, sin)
    k_all = jnp.concatenate([kc, kl], 1)
    v_all = jnp.concatenate([vc, vl], 1)
    yl = rms_norm(block_attention(ql, k_all, v_all).reshape(B, L, GROUP), out_norm)
    yc = None
    if ctx_out:
        qc = mla_queries(Pc, q_norm, w_uq, None, None)
        yc = rms_norm(block_attention(qc, kc, vc).reshape(B, Pc.shape[1], GROUP), out_norm)
    return yl, yc


def block_diag(x, w, b):
    B, L, _ = x.shape
    xb = x.reshape(B, L, LRU_BLOCKS, -1)
    return jnp.einsum('blnc,ncd->blnd', xb, w).reshape(B, L, -1) + b


def linear_scan(a, b, h0, reverse):
    if reverse:
        a, b = jnp.flip(a, 1), jnp.flip(b, 1)
    b = b.at[:, 0].add(a[:, 0] * h0)
    _, h = lax.associative_scan(lambda l, r: (l[0] * r[0], r[0] * l[1] + r[1]), (a, b), axis=1)
    final = h[:, -1]
    if reverse:
        h = jnp.flip(h, 1)
    return h, final


def rglru_direction(u, w_r, b_r, w_i, b_i, lam, h0, reverse):
    r = jax.nn.sigmoid(block_diag(u, w_r, b_r))
    i = jax.nn.sigmoid(block_diag(u, w_i, b_i))
    log_a = -LRU_C * r * jax.nn.softplus(-lam)
    a = jnp.exp(log_a)
    b = jnp.sqrt(-jnp.expm1(2.0 * log_a)) * (i * u)
    return linear_scan(a, b, h0, reverse)


def rglru_mixer(Pl, Pc, conv_w, conv_b, w_r, b_r, w_i, b_i, lam, out_norm, ctx_out):
    f32 = jnp.float32
    ul = dwconv(Pl[..., B_X:B_X + GROUP], conv_w, conv_b, LRU_CONV_LEFT).astype(f32)
    uc = dwconv(Pc[..., B_X:B_X + GROUP], conv_w, conv_b, LRU_CONV_LEFT).astype(f32)
    h0 = jnp.zeros((Pc.shape[0], GROUP), f32)
    hl = 0.0
    hc = 0.0
    for d in range(2):
        rev = d == 1
        h_c, s_c = rglru_direction(uc, w_r[d], b_r[d], w_i[d], b_i[d], lam[d], h0, rev)
        h_l, _ = rglru_direction(ul, w_r[d], b_r[d], w_i[d], b_i[d], lam[d], s_c, rev)
        hl = hl + h_l
        hc = hc + h_c
    gl = jax.nn.gelu(Pl[..., B_GATE:B_GATE + GROUP].astype(f32))
    yl = rms_norm(hl * gl, out_norm).astype(Pl.dtype)
    yc = None
    if ctx_out:
        gc = jax.nn.gelu(Pc[..., B_GATE:B_GATE + GROUP].astype(f32))
        yc = rms_norm(hc * gc, out_norm).astype(Pc.dtype)
    return yl, yc


def rwkv_prepare(P, mu_prev, mu_next, w0, w_up, a0, a_up, k_k, k_a):
    z = token_shift(P[..., C_OFF:C_OFF + C_COLS], mu_prev, mu_next).astype(jnp.float32)
    B, L, _ = z.shape

    def heads(t):
        return t.reshape(B, L, RWKV_HEADS, RWKV_HEAD)

    r = heads(z[..., C_R:C_R + GROUP])
    k = heads(z[..., C_K:C_K + GROUP])
    v = heads(z[..., C_V:C_V + GROUP])
    kk = k * k_k.reshape(RWKV_HEADS, RWKV_HEAD)
    kk = kk * lax.rsqrt(jnp.maximum(jnp.sum(jnp.square(kk), -1, keepdims=True), 1e-24))
    dirs = []
    for d in range(2):
        wd = z[..., C_WD + d * RWKV_DECAY_LORA:C_WD + (d + 1) * RWKV_DECAY_LORA]
        ad = z[..., C_AD + d * RWKV_AAA_LORA:C_AD + (d + 1) * RWKV_AAA_LORA]
        log_w = -jnp.exp(-jax.nn.softplus(-(w0[d] + jnp.tanh(wd) @ w_up[d])) - 0.5)
        a = heads(jax.nn.sigmoid(a0[d] + ad @ a_up[d]))
        k_d = k * (1.0 + (a - 1.0) * k_a.reshape(RWKV_HEADS, RWKV_HEAD))
        dirs.append((heads(jnp.exp(log_w)), kk * a, k_d))
    gd = z[..., C_GD:C_GD + RWKV_GATE_LORA]
    return r, v, kk, dirs, gd


def rwkv_scan(r, decay, kk, kka, v, k, s0, reverse, emit):
    xs = tuple(jnp.moveaxis(t, 1, 0) for t in (r, decay, kk, kka, v, k))

    def step(S, inp):
        r_t, w_t, kk_t, kka_t, v_t, k_t = inp
        sa = jnp.einsum('bhvk,bhk->bhv', S, kk_t)
        S = S * w_t[:, :, None, :] - sa[..., None] * kka_t[:, :, None, :] + v_t[..., None] * k_t[:, :, None, :]
        return S, (jnp.einsum('bhvk,bhk->bhv', S, r_t) if emit else None)

    S, ys = lax.scan(step, s0, xs, reverse=reverse)
    return (jnp.moveaxis(ys, 0, 1) if emit else None), S


def rwkv_finish(y, r, v, dirs, gd, g_up, r_k, ln_g, ln_b):
    B, L = y.shape[:2]
    mu = jnp.mean(y, -1, keepdims=True)
    var = jnp.mean(jnp.square(y - mu), -1, keepdims=True)
    yn = (y - mu) * lax.rsqrt(var + RWKV_GN_EPS) * ln_g.reshape(RWKV_HEADS, RWKV_HEAD) + ln_b.reshape(RWKV_HEADS, RWKV_HEAD)
    bonus = sum(jnp.sum(r * kd * r_k, -1, keepdims=True) for (_, _, kd) in dirs) * v
    g = jax.nn.sigmoid(gd) @ g_up
    return (yn + bonus).reshape(B, L, GROUP) * g


def rwkv_mixer(Pl, Pc, mu_prev, mu_next, w0, w_up, a0, a_up, g_up, k_k, k_a, r_k, ln_g, ln_b, ctx_out):
    rl, vl, kkl, dl, gdl = rwkv_prepare(Pl, mu_prev, mu_next, w0, w_up, a0, a_up, k_k, k_a)
    rc, vc, kkc, dc, gdc = rwkv_prepare(Pc, mu_prev, mu_next, w0, w_up, a0, a_up, k_k, k_a)
    s0 = jnp.zeros((Pl.shape[0], RWKV_HEADS, RWKV_HEAD, RWKV_HEAD), jnp.float32)
    yl = 0.0
    yc = 0.0
    for d in range(2):
        rev = d == 1
        y_c, s_c = rwkv_scan(rc, dc[d][0], kkc, dc[d][1], vc, dc[d][2], s0, rev, ctx_out)
        y_l, _ = rwkv_scan(rl, dl[d][0], kkl, dl[d][1], vl, dl[d][2], s_c, rev, True)
        yl = yl + y_l
        if ctx_out:
            yc = yc + y_c
    out_l = rwkv_finish(yl, rl, vl, dl, gdl, g_up, r_k, ln_g, ln_b).astype(Pl.dtype)
    out_c = None
    if ctx_out:
        out_c = rwkv_finish(yc, rc, vc, dc, gdc, g_up, r_k, ln_g, ln_b).astype(Pc.dtype)
    return out_l, out_c


def hyena_filters(L, w1, b1, w2, b2, w3):
    f32 = jnp.float32
    t01 = jnp.linspace(0.0, 1.0, L, dtype=f32)[:, None]
    bands = jnp.linspace(1e-4, HY_BANDS - 1, HY_BANDS, dtype=f32)[None, :]
    wpos = (2.0 * math.pi / L) * jnp.arange(L, dtype=f32)[:, None]
    z = jnp.concatenate([t01, jnp.cos(bands * wpos), -jnp.sin(bands * wpos)], -1)
    h = jnp.sin(HY_SIN_FREQ * (z @ w1 + b1))
    h = jnp.sin(HY_SIN_FREQ * (h @ w2 + b2))
    h = (h @ w3).astype(f32).reshape(L, HY_ORDER, 2, GROUP)
    deltas = jnp.abs(jnp.linspace(HY_DECAY_MIN, HY_DECAY_MAX, GROUP, dtype=f32))
    window = jnp.exp(-t01 * deltas) + HY_SHIFT
    return h * window[:, None, None, :]


def two_sided_spectrum(h_fwd, h_bwd):
    L, C = h_fwd.shape
    k = jnp.concatenate([h_fwd, jnp.zeros((1, C), h_fwd.dtype), jnp.flip(h_bwd[1:], 0)], 0)
    return jnp.fft.rfft(k, axis=0)


def long_conv(u, spec, d_skip):
    L = u.shape[1]
    U = jnp.fft.rfft(u, n=2 * L, axis=1)
    y = jnp.fft.irfft(U * spec[None], n=2 * L, axis=1)[:, :L]
    return y + u * d_skip


def hyena_sequence(P, conv_w, conv_b, w1, b1, w2, b2, w3, d_skip):
    L = P.shape[1]
    z = dwconv(P[..., D_OFF:D_OFF + D_COLS], conv_w, conv_b, 1).astype(jnp.float32)
    v, x1, x2 = z[..., :GROUP], z[..., GROUP:2 * GROUP], z[..., 2 * GROUP:]
    h = hyena_filters(L, w1, b1, w2, b2, w3)
    u = x1 * long_conv(v, two_sided_spectrum(h[:, 0, 0], h[:, 0, 1]), d_skip[0])
    return x2 * long_conv(u, two_sided_spectrum(h[:, 1, 0], h[:, 1, 1]), d_skip[1])


def swiglu(x, wg, wu, wd):
    return (jax.nn.silu(x @ wg) * (x @ wu)) @ wd


def moe_swiglu(x, router, wg, wu, wd):
    N, D = x.shape
    logits = (x @ router).astype(jnp.float32)
    top_v, top_i = lax.top_k(logits, TOP_K)
    gates = jax.nn.softmax(top_v, axis=-1)
    A = N * TOP_K
    e_flat = top_i.reshape(-1)
    tok_flat = jnp.arange(A, dtype=jnp.int32) // TOP_K
    g_flat = gates.reshape(-1)
    order = jnp.argsort(e_flat)
    e_sorted = e_flat[order]
    counts = jnp.bincount(e_flat, length=N_EXPERTS)
    starts = jnp.cumsum(counts) - counts
    padded = (counts + MOE_BLOCK - 1) // MOE_BLOCK * MOE_BLOCK
    pends = jnp.cumsum(padded)
    pstarts = pends - padded
    dest = pstarts[e_sorted] + jnp.arange(A, dtype=jnp.int32) - starts[e_sorted]
    n_blocks = -(-A // MOE_BLOCK) + N_EXPERTS
    n_slots = n_blocks * MOE_BLOCK
    slot_tok = jnp.full((n_slots,), N, jnp.int32).at[dest].set(tok_flat[order])
    slot_gate = jnp.zeros((n_slots,), jnp.float32).at[dest].set(g_flat[order])
    block_expert = jnp.clip(jnp.searchsorted(pends, jnp.arange(n_blocks) * MOE_BLOCK, side='right'), 0, N_EXPERTS - 1)
    xp = jnp.concatenate([x, jnp.zeros((1, D), x.dtype)], 0)
    xs = xp[slot_tok].reshape(n_blocks, MOE_BLOCK, D)

    def expert_block(args):
        xb, e = args
        return swiglu(xb, wg[e], wu[e], wd[e])

    ys = lax.map(expert_block, (xs, block_expert)).reshape(n_slots, D)
    out = jnp.zeros((N + 1, D), jnp.float32).at[slot_tok].add(ys.astype(jnp.float32) * slot_gate[:, None])
    return out[:N].astype(x.dtype)


def setup_inputs(seed: int = 0) -> dict:
    key = jax.random.key(seed)
    keys = iter(jax.random.split(key, 80))
    f32 = jnp.float32

    def nrm(shape, std):
        return std * jax.random.normal(next(keys), shape, f32)

    def gain(shape):
        return 1.0 + nrm(shape, 0.02)

    def unif(shape, lo, hi):
        return jax.random.uniform(next(keys), shape, f32, lo, hi)

    n_dense = (DEPTH + 1) // 2
    n_moe = DEPTH // 2
    lru_s = unif((DEPTH, 2, GROUP), 0.9, 0.999) ** (1.0 / LRU_C)
    decay_base = jnp.linspace(-6.5, -1.5, GROUP, dtype=f32)
    return {
        'x': nrm((BATCH, SEQ, D_MODEL), 1.0),
        'c': nrm((BATCH, D_MODEL), 1.0),
        'ctx': nrm((BATCH, CTX_LEN, D_MODEL), 1.0),
        'c_ctx': nrm((D_MODEL,), 1.0),
        'ada_w': nrm((DEPTH, D_MODEL, 6 * D_MODEL), 0.5 * D_MODEL ** -0.5),
        'ada_b': nrm((DEPTH, 6 * D_MODEL), 0.02),
        'w_in': nrm((DEPTH, D_MODEL, N_IN), D_MODEL ** -0.5),
        'mla_q_norm': gain((DEPTH, MLA_Q_RANK)),
        'mla_kv_norm': gain((DEPTH, MLA_KV_RANK)),
        'mla_w_uq': nrm((DEPTH, MLA_Q_RANK, MLA_HEADS * (MLA_NOPE + MLA_ROPE)), MLA_Q_RANK ** -0.5),
        'mla_w_ukv': nrm((DEPTH, MLA_KV_RANK, MLA_HEADS * (MLA_NOPE + MLA_V)), MLA_KV_RANK ** -0.5),
        'mla_out_norm': gain((DEPTH, GROUP)),
        'lru_conv_w': nrm((DEPTH, LRU_CONV, GROUP), LRU_CONV ** -0.5),
        'lru_conv_b': nrm((DEPTH, GROUP), 0.02),
        'lru_w_r': nrm((DEPTH, 2, LRU_BLOCKS, GROUP // LRU_BLOCKS, GROUP // LRU_BLOCKS), (GROUP // LRU_BLOCKS) ** -0.5),
        'lru_b_r': nrm((DEPTH, 2, GROUP), 0.02),
        'lru_w_i': nrm((DEPTH, 2, LRU_BLOCKS, GROUP // LRU_BLOCKS, GROUP // LRU_BLOCKS), (GROUP // LRU_BLOCKS) ** -0.5),
        'lru_b_i': nrm((DEPTH, 2, GROUP), 0.02),
        'lru_lambda': jnp.log(lru_s) - jnp.log1p(-lru_s),
        'lru_out_norm': gain((DEPTH, GROUP)),
        'rwkv_mu_prev': unif((DEPTH, C_COLS), 0.0, 0.5),
        'rwkv_mu_next': unif((DEPTH, C_COLS), 0.0, 0.5),
        'rwkv_w0': decay_base + nrm((DEPTH, 2, GROUP), 0.1),
        'rwkv_w_up': nrm((DEPTH, 2, RWKV_DECAY_LORA, GROUP), 0.5 * RWKV_DECAY_LORA ** -0.5),
        'rwkv_a0': nrm((DEPTH, 2, GROUP), 0.1),
        'rwkv_a_up': nrm((DEPTH, 2, RWKV_AAA_LORA, GROUP), 0.5 * RWKV_AAA_LORA ** -0.5),
        'rwkv_g_up': nrm((DEPTH, RWKV_GATE_LORA, GROUP), RWKV_GATE_LORA ** -0.5),
        'rwkv_k_k': 0.85 + nrm((DEPTH, GROUP), 0.02),
        'rwkv_k_a': gain((DEPTH, GROUP)),
        'rwkv_r_k': -0.04 + nrm((DEPTH, RWKV_HEADS, RWKV_HEAD), 0.02),
        'rwkv_ln_g': gain((DEPTH, GROUP)),
        'rwkv_ln_b': nrm((DEPTH, GROUP), 0.02),
        'hy_conv_w': nrm((DEPTH, HY_SHORT, D_COLS), HY_SHORT ** -0.5),
        'hy_conv_b': nrm((DEPTH, D_COLS), 0.02),
        'hy_f_w1': nrm((DEPTH, HY_EMB, HY_HIDDEN), HY_EMB ** -0.5),
        'hy_f_b1': nrm((DEPTH, HY_HIDDEN), 0.02),
        'hy_f_w2': nrm((DEPTH, HY_HIDDEN, HY_HIDDEN), HY_HIDDEN ** -0.5),
        'hy_f_b2': nrm((DEPTH, HY_HIDDEN), 0.02),
        'hy_f_w3': nrm((DEPTH, HY_HIDDEN, HY_ORDER * 2 * GROUP), 0.02),
        'hy_d': nrm((DEPTH, HY_ORDER, GROUP), 0.1),
        'hy_out_norm': gain((DEPTH, GROUP)),
        'w_out': nrm((DEPTH, D_MIX, D_MODEL), BETA * D_MIX ** -0.5),
        'ln1_g': gain((DEPTH, D_MODEL)),
        'ln1_b': nrm((DEPTH, D_MODEL), 0.02),
        'ln2_g': gain((DEPTH, D_MODEL)),
        'ln2_b': nrm((DEPTH, D_MODEL), 0.02),
        'ffn_w_gate': nrm((n_dense, D_MODEL, FF_DENSE), D_MODEL ** -0.5),
        'ffn_w_up': nrm((n_dense, D_MODEL, FF_DENSE), D_MODEL ** -0.5),
        'ffn_w_down': nrm((n_dense, FF_DENSE, D_MODEL), BETA * FF_DENSE ** -0.5),
        'moe_router': nrm((n_moe, D_MODEL, N_EXPERTS), D_MODEL ** -0.5),
        'moe_w_gate': nrm((n_moe, N_EXPERTS, D_MODEL, FF_EXPERT), D_MODEL ** -0.5),
        'moe_w_up': nrm((n_moe, N_EXPERTS, D_MODEL, FF_EXPERT), D_MODEL ** -0.5),
        'moe_w_down': nrm((n_moe, N_EXPERTS, FF_EXPERT, D_MODEL), BETA * FF_EXPERT ** -0.5),
    }


def reference(x, c, ctx, c_ctx, ada_w, ada_b, w_in, mla_q_norm, mla_kv_norm, mla_w_uq, mla_w_ukv, mla_out_norm,
              lru_conv_w, lru_conv_b, lru_w_r, lru_b_r, lru_w_i, lru_b_i, lru_lambda, lru_out_norm,
              rwkv_mu_prev, rwkv_mu_next, rwkv_w0, rwkv_w_up, rwkv_a0, rwkv_a_up, rwkv_g_up, rwkv_k_k, rwkv_k_a,
              rwkv_r_k, rwkv_ln_g, rwkv_ln_b, hy_conv_w, hy_conv_b, hy_f_w1, hy_f_b1, hy_f_w2, hy_f_b2, hy_f_w3,
              hy_d, hy_out_norm, w_out, ln1_g, ln1_b, ln2_g, ln2_b, ffn_w_gate, ffn_w_up, ffn_w_down,
              moe_router, moe_w_gate, moe_w_up, moe_w_down):
    B, L, D = x.shape
    Lc = ctx.shape[1]
    rows = L // GRID_W
    cos, sin = axial_rope(rows)
    s_lat = jax.nn.silu(c)
    s_ctx = jax.nn.silu(c_ctx)
    xl, xc = x, ctx
    for li in range(DEPTH):
        ctx_out = li < DEPTH - 1
        mod_l = (s_lat @ ada_w[li] + ada_b[li]).reshape(B, 6, 1, D)
        mod_c = (s_ctx @ ada_w[li] + ada_b[li]).reshape(6, 1, D)

        hl = modulate(xl, mod_l[:, 0], mod_l[:, 1])
        hc = modulate(xc, mod_c[0], mod_c[1])
        w_in_l = w_in[li]
        Pl = hl @ w_in_l
        Pc = hc @ (w_in_l if ctx_out else w_in_l[:, :D_OFF])
        a_l, a_c = mla_mixer(Pl, Pc, mla_q_norm[li], mla_kv_norm[li], mla_w_uq[li], mla_w_ukv[li],
                             mla_out_norm[li], cos, sin, ctx_out)
        b_l, b_c = rglru_mixer(Pl, Pc, lru_conv_w[li], lru_conv_b[li], lru_w_r[li], lru_b_r[li], lru_w_i[li],
                               lru_b_i[li], lru_lambda[li], lru_out_norm[li], ctx_out)
        c_l, c_c = rwkv_mixer(Pl, Pc, rwkv_mu_prev[li], rwkv_mu_next[li], rwkv_w0[li], rwkv_w_up[li], rwkv_a0[li],
                              rwkv_a_up[li], rwkv_g_up[li], rwkv_k_k[li], rwkv_k_a[li], rwkv_r_k[li],
                              rwkv_ln_g[li], rwkv_ln_b[li], ctx_out)
        d_l = rms_norm(hyena_sequence(Pl, hy_conv_w[li], hy_conv_b[li], hy_f_w1[li], hy_f_b1[li], hy_f_w2[li],
                                      hy_f_b2[li], hy_f_w3[li], hy_d[li]), hy_out_norm[li]).astype(Pl.dtype)
        yl = jnp.concatenate([a_l, b_l, c_l, d_l], -1) @ w_out[li]
        xl = layer_norm(ALPHA * xl + mod_l[:, 2] * yl, ln1_g[li], ln1_b[li])
        if ctx_out:
            d_c = rms_norm(hyena_sequence(Pc, hy_conv_w[li], hy_conv_b[li], hy_f_w1[li], hy_f_b1[li], hy_f_w2[li],
                                          hy_f_b2[li], hy_f_w3[li], hy_d[li]), hy_out_norm[li]).astype(Pc.dtype)
            yc = jnp.concatenate([a_c, b_c, c_c, d_c], -1) @ w_out[li]
            xc = layer_norm(ALPHA * xc + mod_c[2] * yc, ln1_g[li], ln1_b[li])

        fl = modulate(xl, mod_l[:, 3], mod_l[:, 4]).reshape(B * L, D)
        if ctx_out:
            fc = modulate(xc, mod_c[3], mod_c[4]).reshape(B * Lc, D)
            tokens = jnp.concatenate([fl, fc], 0)
        else:
            tokens = fl
        j = li // 2
        if li % 2 == 0:
            out = swiglu(tokens, ffn_w_gate[j], ffn_w_up[j], ffn_w_down[j])
        else:
            out = moe_swiglu(tokens, moe_router[j], moe_w_gate[j], moe_w_up[j], moe_w_down[j])
        xl = layer_norm(ALPHA * xl + mod_l[:, 5] * out[:B * L].reshape(B, L, D), ln2_g[li], ln2_b[li])
        if ctx_out:
            xc = layer_norm(ALPHA * xc + mod_c[5] * out[B * L:].reshape(B, Lc, D), ln2_g[li], ln2_b[li])
    return xl
```

```python
import math
from functools import partial

import jax
import jax.numpy as jnp
from jax import lax
from jax.experimental import pallas as pl
from jax.experimental.pallas import tpu as pltpu

F32 = jnp.float32

SUBLANES = 8
LANES = 128

D_MODEL = 1024
DEPTH = 4
GRID_W = 64
GROUP = D_MODEL // 4

MLA_HEADS = 4
MLA_NOPE = 64
MLA_ROPE = 32
MLA_V = 64
MLA_Q_RANK = 192
MLA_KV_RANK = 128
ROPE_BASE = 10000.0
Q_BLOCK = 128

LRU_BLOCKS = 4
LRU_CONV = 4
LRU_CONV_LEFT = 2
LRU_C = 8.0

RWKV_HEADS = 4
RWKV_HEAD = GROUP // RWKV_HEADS
RWKV_DECAY_LORA = 32
RWKV_AAA_LORA = 32
RWKV_GATE_LORA = 64
RWKV_GN_EPS = 64e-5

HY_ORDER = 2
HY_SHORT = 3
HY_BANDS = 16
HY_EMB = 1 + 2 * HY_BANDS
HY_HIDDEN = 64
HY_SIN_FREQ = 1.0
HY_DECAY_MIN = math.log(1e-2) / 1.5
HY_DECAY_MAX = math.log(1e-2) / 0.3
HY_SHIFT = 0.05

N_EXPERTS = 8
TOP_K = 2
MOE_BLOCK = 512

ALPHA = (2.0 * DEPTH) ** 0.25

A_CQ = 0
A_CKV = A_CQ + MLA_Q_RANK
A_KR = A_CKV + MLA_KV_RANK
B_X = A_KR + MLA_ROPE
B_GATE = B_X + GROUP
C_OFF = B_GATE + GROUP
C_R = 0
C_K = GROUP
C_V = 2 * GROUP
C_WD = 3 * GROUP
C_AD = C_WD + 2 * RWKV_DECAY_LORA
C_GD = C_AD + 2 * RWKV_AAA_LORA
C_COLS = C_GD + RWKV_GATE_LORA
D_OFF = C_OFF + C_COLS
D_COLS = (HY_ORDER + 1) * GROUP


RWKV_TIME_BLOCK = 16
V_TILES = RWKV_HEAD // SUBLANES


def _rwkv_scan_kernel(r_ref, w_ref, kk_ref, kka_ref, v_ref, kd_ref, y_ref, s_ref):
    @pl.when(pl.program_id(0) == 0)
    def _():
        s_ref[...] = jnp.zeros_like(s_ref)

    n_t = r_ref.shape[0]
    p = r_ref.shape[2]

    def step(t, carry):
        def row(ref, k):
            return jnp.broadcast_to(ref[t, pl.ds(k, 1), :], (SUBLANES, p))[None]

        sa = jnp.zeros((V_TILES, SUBLANES, p), F32)
        for k in range(RWKV_HEAD):
            sa = sa + s_ref[k] * row(kk_ref, k)
        vt = v_ref[t].reshape(V_TILES, SUBLANES, p)
        y = jnp.zeros((V_TILES, SUBLANES, p), F32)
        for k in range(RWKV_HEAD):
            sn = s_ref[k] * row(w_ref, k) - sa * row(kka_ref, k) + vt * row(kd_ref, k)
            s_ref[k] = sn
            y = y + sn * row(r_ref, k)
        y_ref[t] = y.reshape(RWKV_HEAD, p)
        return carry

    lax.fori_loop(0, n_t, step, 0)


def rwkv_scan_pallas(r, w, kk, kka, v, kd):
    n_steps, n, p = r.shape
    tb = RWKV_TIME_BLOCK
    assert n == RWKV_HEAD and p == LANES and n_steps % tb == 0
    spec = pl.BlockSpec((tb, n, p), lambda i: (i, 0, 0))
    return pl.pallas_call(
        _rwkv_scan_kernel,
        grid=(n_steps // tb,),
        in_specs=[spec] * 6,
        out_specs=spec,
        out_shape=jax.ShapeDtypeStruct((n_steps, n, p), F32),
        scratch_shapes=[pltpu.VMEM((n, V_TILES, SUBLANES, p), F32)],
        compiler_params=pltpu.CompilerParams(dimension_semantics=("arbitrary",)),
    )(r, w, kk, kka, v, kd)


def modulate(x, shift, scale):
    return x * (1.0 + scale) + shift


def layer_norm(x, g, b, eps=1e-5):
    mu = jnp.mean(x, -1, keepdims=True)
    var = jnp.mean(jnp.square(x - mu), -1, keepdims=True)
    return (x - mu) * lax.rsqrt(var + eps) * g + b


def rms_norm(x, g, eps=1e-6):
    return x * lax.rsqrt(jnp.mean(jnp.square(x), -1, keepdims=True) + eps) * g


def dwconv(x, w, b, left):
    K = w.shape[0]
    L = x.shape[1]
    xp = jnp.pad(x, ((0, 0), (left, K - 1 - left), (0, 0)))
    return sum(xp[:, j:j + L] * w[j] for j in range(K)) + b


def token_shift(z, mu_prev, mu_next):
    zp = jnp.pad(z, ((0, 0), (1, 1), (0, 0)))
    return z + mu_prev * (zp[:, :-2] - z) + mu_next * (zp[:, 2:] - z)


def axial_rope(rows):
    r, col = jnp.meshgrid(jnp.arange(rows, dtype=F32), jnp.arange(GRID_W, dtype=F32), indexing='ij')
    half = MLA_ROPE // 2
    inv = 1.0 / (ROPE_BASE ** (jnp.arange(0, half, 2, dtype=F32) / half))
    ang = jnp.concatenate([r.reshape(-1, 1) * inv, col.reshape(-1, 1) * inv], -1)
    return jnp.cos(ang), jnp.sin(ang)


def apply_rope(x, cos, sin):
    h = x.shape[-1] // 2
    x1, x2 = x[..., :h], x[..., h:]
    return jnp.concatenate([x1 * cos - x2 * sin, x1 * sin + x2 * cos], -1)


def block_attention(q, k, v):
    B, Lq, H, Dh = q.shape
    nb = Lq // Q_BLOCK
    qb = jnp.moveaxis(q.reshape(B, nb, Q_BLOCK, H, Dh), 1, 0)
    scale = Dh ** -0.5

    def one(qblk):
        s = jnp.einsum('bqhd,bkhd->bhqk', qblk, k, preferred_element_type=F32) * scale
        p = jax.nn.softmax(s, axis=-1).astype(v.dtype)
        return jnp.einsum('bhqk,bkhd->bqhd', p, v)

    o = lax.map(one, qb)
    return jnp.moveaxis(o, 0, 1).reshape(B, Lq, H, v.shape[-1])


def mla_queries(P, q_norm, w_uq, cos, sin):
    B, L, _ = P.shape
    cq = rms_norm(P[..., A_CQ:A_CQ + MLA_Q_RANK], q_norm)
    q = (cq @ w_uq).reshape(B, L, MLA_HEADS, MLA_NOPE + MLA_ROPE)
    if cos is None:
        return q
    q_rope = apply_rope(q[..., MLA_NOPE:], cos[None, :, None, :], sin[None, :, None, :])
    return jnp.concatenate([q[..., :MLA_NOPE], q_rope], -1)


def mla_keys(P, kv_norm, w_ukv, cos, sin):
    B, L, _ = P.shape
    ckv = rms_norm(P[..., A_CKV:A_CKV + MLA_KV_RANK], kv_norm)
    kv = (ckv @ w_ukv).reshape(B, L, MLA_HEADS, MLA_NOPE + MLA_V)
    k_nope, v = kv[..., :MLA_NOPE], kv[..., MLA_NOPE:]
    k_rope = P[..., A_KR:A_KR + MLA_ROPE]
    if cos is not None:
        k_rope = apply_rope(k_rope, cos[None], sin[None])
    k_rope = jnp.broadcast_to(k_rope[:, :, None, :], (B, L, MLA_HEADS, MLA_ROPE))
    return jnp.concatenate([k_nope, k_rope], -1), v


def mla_mixer(Pl, Pc, q_norm, kv_norm, w_uq, w_ukv, out_norm, cos, sin, ctx_out):
    B, L, _ = Pl.shape
    kl, vl = mla_keys(Pl, kv_norm, w_ukv, cos, sin)
    kc, vc = mla_keys(Pc, kv_norm, w_ukv, None, None)
    ql = mla_queries(Pl, q_norm, w_uq, cos, sin)
    k_all = jnp.concatenate([kc, kl], 1)
    v_all = jnp.concatenate([vc, vl], 1)
    yl = rms_norm(block_attention(ql, k_all, v_all).reshape(B, L, GROUP), out_norm)
    yc = None
    if ctx_out:
        qc = mla_queries(Pc, q_norm, w_uq, None, None)
        yc = rms_norm(block_attention(qc, kc, vc).reshape(B, Pc.shape[1], GROUP), out_norm)
    return yl, yc


def block_diag(x, w, b):
    B, L, _ = x.shape
    xb = x.reshape(B, L, LRU_BLOCKS, -1)
    return jnp.einsum('blnc,ncd->blnd', xb, w).reshape(B, L, -1) + b


def linear_scan(a, b, h0, reverse):
    if reverse:
        a, b = jnp.flip(a, 1), jnp.flip(b, 1)
    b = b.at[:, 0].add(a[:, 0] * h0)
    _, h = lax.associative_scan(lambda l, r: (l[0] * r[0], r[0] * l[1] + r[1]), (a, b), axis=1)
    final = h[:, -1]
    if reverse:
        h = jnp.flip(h, 1)
    return h, final


def rglru_direction(u, w_r, b_r, w_i, b_i, lam, h0, reverse):
    r = jax.nn.sigmoid(block_diag(u, w_r, b_r))
    i = jax.nn.sigmoid(block_diag(u, w_i, b_i))
    log_a = -LRU_C * r * jax.nn.softplus(-lam)
    a = jnp.exp(log_a)
    b = jnp.sqrt(-jnp.expm1(2.0 * log_a)) * (i * u)
    return linear_scan(a, b, h0, reverse)


def rglru_mixer(Pl, Pc, conv_w, conv_b, w_r, b_r, w_i, b_i, lam, out_norm, ctx_out):
    ul = dwconv(Pl[..., B_X:B_X + GROUP], conv_w, conv_b, LRU_CONV_LEFT)
    uc = dwconv(Pc[..., B_X:B_X + GROUP], conv_w, conv_b, LRU_CONV_LEFT)
    h0 = jnp.zeros((Pc.shape[0], GROUP), F32)
    hl = 0.0
    hc = 0.0
    for d in range(2):
        rev = d == 1
        h_c, s_c = rglru_direction(uc, w_r[d], b_r[d], w_i[d], b_i[d], lam[d], h0, rev)
        h_l, _ = rglru_direction(ul, w_r[d], b_r[d], w_i[d], b_i[d], lam[d], s_c, rev)
        hl = hl + h_l
        hc = hc + h_c
    gl = jax.nn.gelu(Pl[..., B_GATE:B_GATE + GROUP])
    yl = rms_norm(hl * gl, out_norm)
    yc = None
    if ctx_out:
        gc = jax.nn.gelu(Pc[..., B_GATE:B_GATE + GROUP])
        yc = rms_norm(hc * gc, out_norm)
    return yl, yc


def rwkv_prepare(P, mu_prev, mu_next, w0, w_up, a0, a_up, k_k, k_a):
    z = token_shift(P[..., C_OFF:C_OFF + C_COLS], mu_prev, mu_next)
    B, L, _ = z.shape

    def heads(t):
        return t.reshape(B, L, RWKV_HEADS, RWKV_HEAD)

    r = heads(z[..., C_R:C_R + GROUP])
    k = heads(z[..., C_K:C_K + GROUP])
    v = heads(z[..., C_V:C_V + GROUP])
    kk = k * k_k.reshape(RWKV_HEADS, RWKV_HEAD)
    kk = kk * lax.rsqrt(jnp.maximum(jnp.sum(jnp.square(kk), -1, keepdims=True), 1e-24))
    dirs = []
    for d in range(2):
        wd = z[..., C_WD + d * RWKV_DECAY_LORA:C_WD + (d + 1) * RWKV_DECAY_LORA]
        ad = z[..., C_AD + d * RWKV_AAA_LORA:C_AD + (d + 1) * RWKV_AAA_LORA]
        log_w = -jnp.exp(-jax.nn.softplus(-(w0[d] + jnp.tanh(wd) @ w_up[d])) - 0.5)
        a = heads(jax.nn.sigmoid(a0[d] + ad @ a_up[d]))
        k_d = k * (1.0 + (a - 1.0) * k_a.reshape(RWKV_HEADS, RWKV_HEAD))
        dirs.append((heads(jnp.exp(log_w)), kk * a, k_d))
    gd = z[..., C_GD:C_GD + RWKV_GATE_LORA]
    return r, v, kk, dirs, gd


def rwkv_finish(y, r, v, dirs, gd, g_up, r_k, ln_g, ln_b):
    B, L = y.shape[:2]
    mu = jnp.mean(y, -1, keepdims=True)
    var = jnp.mean(jnp.square(y - mu), -1, keepdims=True)
    yn = (y - mu) * lax.rsqrt(var + RWKV_GN_EPS) * ln_g.reshape(RWKV_HEADS, RWKV_HEAD) + ln_b.reshape(RWKV_HEADS, RWKV_HEAD)
    bonus = sum(jnp.sum(r * kd * r_k, -1, keepdims=True) for (_, _, kd) in dirs) * v
    g = jax.nn.sigmoid(gd) @ g_up
    return (yn + bonus).reshape(B, L, GROUP) * g


def _scan_order(tc, tl):
    def one(c, l, rev):
        if rev:
            c, l = jnp.flip(c, 1), jnp.flip(l, 1)
        return jnp.concatenate([c, l], 1)

    both = jnp.stack([one(tc[0], tl[0], False), one(tc[1], tl[1], True)], 0)
    d, b, s, h, n = both.shape
    return jnp.transpose(both, (2, 4, 0, 1, 3)).reshape(s, n, d * b * h)


def rwkv_mixer(Pl, Pc, mu_prev, mu_next, w0, w_up, a0, a_up, g_up, k_k, k_a, r_k, ln_g, ln_b, ctx_out):
    rl, vl, kkl, dl, gdl = rwkv_prepare(Pl, mu_prev, mu_next, w0, w_up, a0, a_up, k_k, k_a)
    rc, vc, kkc, dc, gdc = rwkv_prepare(Pc, mu_prev, mu_next, w0, w_up, a0, a_up, k_k, k_a)
    B, L = Pl.shape[:2]
    Lc = Pc.shape[1]
    y = rwkv_scan_pallas(
        _scan_order((rc, rc), (rl, rl)),
        _scan_order((dc[0][0], dc[1][0]), (dl[0][0], dl[1][0])),
        _scan_order((kkc, kkc), (kkl, kkl)),
        _scan_order((dc[0][1], dc[1][1]), (dl[0][1], dl[1][1])),
        _scan_order((vc, vc), (vl, vl)),
        _scan_order((dc[0][2], dc[1][2]), (dl[0][2], dl[1][2])),
    )
    y = jnp.transpose(y.reshape(Lc + L, RWKV_HEAD, 2, B, RWKV_HEADS), (2, 3, 0, 4, 1))
    yl = y[0, :, Lc:] + jnp.flip(y[1, :, Lc:], 1)
    out_l = rwkv_finish(yl, rl, vl, dl, gdl, g_up, r_k, ln_g, ln_b)
    out_c = None
    if ctx_out:
        yc = y[0, :, :Lc] + jnp.flip(y[1, :, :Lc], 1)
        out_c = rwkv_finish(yc, rc, vc, dc, gdc, g_up, r_k, ln_g, ln_b)
    return out_l, out_c


def hyena_filters(L, w1, b1, w2, b2, w3):
    t01 = jnp.linspace(0.0, 1.0, L, dtype=F32)[:, None]
    bands = jnp.linspace(1e-4, HY_BANDS - 1, HY_BANDS, dtype=F32)[None, :]
    wpos = (2.0 * math.pi / L) * jnp.arange(L, dtype=F32)[:, None]
    z = jnp.concatenate([t01, jnp.cos(bands * wpos), -jnp.sin(bands * wpos)], -1)
    h = jnp.sin(HY_SIN_FREQ * (z @ w1 + b1))
    h = jnp.sin(HY_SIN_FREQ * (h @ w2 + b2))
    h = (h @ w3).reshape(L, HY_ORDER, 2, GROUP)
    deltas = jnp.abs(jnp.linspace(HY_DECAY_MIN, HY_DECAY_MAX, GROUP, dtype=F32))
    window = jnp.exp(-t01 * deltas) + HY_SHIFT
    return h * window[:, None, None, :]


def two_sided_spectrum(h_fwd, h_bwd):
    L, C = h_fwd.shape
    k = jnp.concatenate([h_fwd, jnp.zeros((1, C), h_fwd.dtype), jnp.flip(h_bwd[1:], 0)], 0)
    return jnp.fft.rfft(k, axis=0)


def long_conv(u, spec, d_skip):
    L = u.shape[1]
    U = jnp.fft.rfft(u, n=2 * L, axis=1)
    y = jnp.fft.irfft(U * spec[None], n=2 * L, axis=1)[:, :L]
    return y + u * d_skip


def hyena_sequence(P, conv_w, conv_b, w1, b1, w2, b2, w3, d_skip):
    L = P.shape[1]
    z = dwconv(P[..., D_OFF:D_OFF + D_COLS], conv_w, conv_b, 1)
    v, x1, x2 = z[..., :GROUP], z[..., GROUP:2 * GROUP], z[..., 2 * GROUP:]
    h = hyena_filters(L, w1, b1, w2, b2, w3)
    u = x1 * long_conv(v, two_sided_spectrum(h[:, 0, 0], h[:, 0, 1]), d_skip[0])
    return x2 * long_conv(u, two_sided_spectrum(h[:, 1, 0], h[:, 1, 1]), d_skip[1])


def swiglu(x, wg, wu, wd):
    return (jax.nn.silu(x @ wg) * (x @ wu)) @ wd


def moe_swiglu(x, router, wg, wu, wd):
    N, D = x.shape
    logits = x @ router
    top_v, top_i = lax.top_k(logits, TOP_K)
    gates = jax.nn.softmax(top_v, axis=-1)
    A = N * TOP_K
    e_flat = top_i.reshape(-1)
    tok_flat = jnp.arange(A, dtype=jnp.int32) // TOP_K
    g_flat = gates.reshape(-1)
    order = jnp.argsort(e_flat)
    e_sorted = e_flat[order]
    counts = jnp.bincount(e_flat, length=N_EXPERTS)
    starts = jnp.cumsum(counts) - counts
    padded = (counts + MOE_BLOCK - 1) // MOE_BLOCK * MOE_BLOCK
    pends = jnp.cumsum(padded)
    pstarts = pends - padded
    dest = pstarts[e_sorted] + jnp.arange(A, dtype=jnp.int32) - starts[e_sorted]
    n_blocks = -(-A // MOE_BLOCK) + N_EXPERTS
    n_slots = n_blocks * MOE_BLOCK
    slot_tok = jnp.full((n_slots,), N, jnp.int32).at[dest].set(tok_flat[order])
    slot_gate = jnp.zeros((n_slots,), F32).at[dest].set(g_flat[order])
    block_expert = jnp.clip(jnp.searchsorted(pends, jnp.arange(n_blocks) * MOE_BLOCK, side='right'), 0, N_EXPERTS - 1)
    xp = jnp.concatenate([x, jnp.zeros((1, D), x.dtype)], 0)
    xs = xp[slot_tok].reshape(n_blocks, MOE_BLOCK, D)

    def expert_block(args):
        xb, e = args
        return swiglu(xb, wg[e], wu[e], wd[e])

    ys = lax.map(expert_block, (xs, block_expert)).reshape(n_slots, D)
    out = jnp.zeros((N + 1, D), F32).at[slot_tok].add(ys * slot_gate[:, None])
    return out[:N]


def kernel(x, c, ctx, c_ctx, ada_w, ada_b, w_in, mla_q_norm, mla_kv_norm, mla_w_uq, mla_w_ukv, mla_out_norm, lru_conv_w, lru_conv_b, lru_w_r, lru_b_r, lru_w_i, lru_b_i, lru_lambda, lru_out_norm, rwkv_mu_prev, rwkv_mu_next, rwkv_w0, rwkv_w_up, rwkv_a0, rwkv_a_up, rwkv_g_up, rwkv_k_k, rwkv_k_a, rwkv_r_k, rwkv_ln_g, rwkv_ln_b, hy_conv_w, hy_conv_b, hy_f_w1, hy_f_b1, hy_f_w2, hy_f_b2, hy_f_w3, hy_d, hy_out_norm, w_out, ln1_g, ln1_b, ln2_g, ln2_b, ffn_w_gate, ffn_w_up, ffn_w_down, moe_router, moe_w_gate, moe_w_up, moe_w_down):
    B, L, D = x.shape
    Lc = ctx.shape[1]
    rows = L // GRID_W
    cos, sin = axial_rope(rows)
    s_lat = jax.nn.silu(c)
    s_ctx = jax.nn.silu(c_ctx)
    xl, xc = x, ctx
    for li in range(DEPTH):
        ctx_out = li < DEPTH - 1
        mod_l = (s_lat @ ada_w[li] + ada_b[li]).reshape(B, 6, 1, D)
        mod_c = (s_ctx @ ada_w[li] + ada_b[li]).reshape(6, 1, D)

        hl = modulate(xl, mod_l[:, 0], mod_l[:, 1])
        hc = modulate(xc, mod_c[0], mod_c[1])
        w_in_l = w_in[li]
        Pl = hl @ w_in_l
        Pc = hc @ (w_in_l if ctx_out else w_in_l[:, :D_OFF])
        a_l, a_c = mla_mixer(Pl, Pc, mla_q_norm[li], mla_kv_norm[li], mla_w_uq[li], mla_w_ukv[li],
                             mla_out_norm[li], cos, sin, ctx_out)
        b_l, b_c = rglru_mixer(Pl, Pc, lru_conv_w[li], lru_conv_b[li], lru_w_r[li], lru_b_r[li], lru_w_i[li],
                               lru_b_i[li], lru_lambda[li], lru_out_norm[li], ctx_out)
        c_l, c_c = rwkv_mixer(Pl, Pc, rwkv_mu_prev[li], rwkv_mu_next[li], rwkv_w0[li], rwkv_w_up[li], rwkv_a0[li],
                              rwkv_a_up[li], rwkv_g_up[li], rwkv_k_k[li], rwkv_k_a[li], rwkv_r_k[li],
                              rwkv_ln_g[li], rwkv_ln_b[li], ctx_out)
        d_l = rms_norm(hyena_sequence(Pl, hy_conv_w[li], hy_conv_b[li], hy_f_w1[li], hy_f_b1[li], hy_f_w2[li],
                                      hy_f_b2[li], hy_f_w3[li], hy_d[li]), hy_out_norm[li])
        yl = jnp.concatenate([a_l, b_l, c_l, d_l], -1) @ w_out[li]
        xl = layer_norm(ALPHA * xl + mod_l[:, 2] * yl, ln1_g[li], ln1_b[li])
        if ctx_out:
            d_c = rms_norm(hyena_sequence(Pc, hy_conv_w[li], hy_conv_b[li], hy_f_w1[li], hy_f_b1[li], hy_f_w2[li],
                                          hy_f_b2[li], hy_f_w3[li], hy_d[li]), hy_out_norm[li])
            yc = jnp.concatenate([a_c, b_c, c_c, d_c], -1) @ w_out[li]
            xc = layer_norm(ALPHA * xc + mod_c[2] * yc, ln1_g[li], ln1_b[li])

        fl = modulate(xl, mod_l[:, 3], mod_l[:, 4]).reshape(B * L, D)
        if ctx_out:
            fc = modulate(xc, mod_c[3], mod_c[4]).reshape(B * Lc, D)
            tokens = jnp.concatenate([fl, fc], 0)
        else:
            tokens = fl
        j = li // 2
        if li % 2 == 0:
            out = swiglu(tokens, ffn_w_gate[j], ffn_w_up[j], ffn_w_down[j])
        else:
            out = moe_swiglu(tokens, moe_router[j], moe_w_gate[j], moe_w_up[j], moe_w_down[j])
        xl = layer_norm(ALPHA * xl + mod_l[:, 5] * out[:B * L].reshape(B, L, D), ln2_g[li], ln2_b[li])
        if ctx_out:
            xc = layer_norm(ALPHA * xc + mod_c[5] * out[B * L:].reshape(B, Lc, D), ln2_g[li], ln2_b[li])
    return xl
```

```python
import math
from functools import partial

import jax
import jax.numpy as jnp
from jax import lax
from jax.experimental import pallas as pl
from jax.experimental.pallas import tpu as pltpu

F32 = jnp.float32
BF16 = jnp.bfloat16

SUBLANES = 8
LANES = 128
VMEM_LIMIT_BYTES = 48 * 1024 * 1024

D_MODEL = 1024
DEPTH = 4
GRID_W = 64
GROUP = D_MODEL // 4

MLA_HEADS = 4
MLA_NOPE = 64
MLA_ROPE = 32
MLA_V = 64
MLA_Q_RANK = 192
MLA_KV_RANK = 128
ROPE_BASE = 10000.0
Q_BLOCK = 128

LRU_BLOCKS = 4
LRU_CONV = 4
LRU_CONV_LEFT = 2
LRU_C = 8.0

RWKV_HEADS = 4
RWKV_HEAD = GROUP // RWKV_HEADS
RWKV_DECAY_LORA = 32
RWKV_AAA_LORA = 32
RWKV_GATE_LORA = 64
RWKV_GN_EPS = 64e-5

HY_ORDER = 2
HY_SHORT = 3
HY_BANDS = 16
HY_EMB = 1 + 2 * HY_BANDS
HY_HIDDEN = 64
HY_SIN_FREQ = 1.0
HY_DECAY_MIN = math.log(1e-2) / 1.5
HY_DECAY_MAX = math.log(1e-2) / 0.3
HY_SHIFT = 0.05

N_EXPERTS = 8
TOP_K = 2
MOE_BLOCK = 512

ALPHA = (2.0 * DEPTH) ** 0.25

A_CQ = 0
A_CKV = A_CQ + MLA_Q_RANK
A_KR = A_CKV + MLA_KV_RANK
B_X = A_KR + MLA_ROPE
B_GATE = B_X + GROUP
C_OFF = B_GATE + GROUP
C_R = 0
C_K = GROUP
C_V = 2 * GROUP
C_WD = 3 * GROUP
C_AD = C_WD + 2 * RWKV_DECAY_LORA
C_GD = C_AD + 2 * RWKV_AAA_LORA
C_COLS = C_GD + RWKV_GATE_LORA
D_OFF = C_OFF + C_COLS
D_COLS = (HY_ORDER + 1) * GROUP


RWKV_TIME_BLOCK = 16
V_TILES = RWKV_HEAD // SUBLANES


def _rwkv_scan_kernel(r_ref, w_ref, kk_ref, kka_ref, v_ref, kd_ref, y_ref, s_ref):
    @pl.when(pl.program_id(0) == 0)
    def _():
        s_ref[...] = jnp.zeros_like(s_ref)

    n_t = r_ref.shape[0]
    p = r_ref.shape[2]

    def step(t, carry):
        def row(ref, k):
            return jnp.broadcast_to(ref[t, pl.ds(k, 1), :], (SUBLANES, p))[None]

        sa = jnp.zeros((V_TILES, SUBLANES, p), F32)
        for k in range(RWKV_HEAD):
            sa = sa + s_ref[k] * row(kk_ref, k)
        vt = v_ref[t].reshape(V_TILES, SUBLANES, p)
        y = jnp.zeros((V_TILES, SUBLANES, p), F32)
        for k in range(RWKV_HEAD):
            sn = s_ref[k] * row(w_ref, k) - sa * row(kka_ref, k) + vt * row(kd_ref, k)
            s_ref[k] = sn
            y = y + sn * row(r_ref, k)
        y_ref[t] = y.reshape(RWKV_HEAD, p)
        return carry

    lax.fori_loop(0, n_t, step, 0)


def rwkv_scan_pallas(r, w, kk, kka, v, kd):
    n_steps, n, p = r.shape
    tb = RWKV_TIME_BLOCK
    assert n == RWKV_HEAD and p == LANES and n_steps % tb == 0
    spec = pl.BlockSpec((tb, n, p), lambda i: (i, 0, 0))
    return pl.pallas_call(
        _rwkv_scan_kernel,
        grid=(n_steps // tb,),
        in_specs=[spec] * 6,
        out_specs=spec,
        out_shape=jax.ShapeDtypeStruct((n_steps, n, p), F32),
        scratch_shapes=[pltpu.VMEM((n, V_TILES, SUBLANES, p), F32)],
        compiler_params=pltpu.CompilerParams(dimension_semantics=("arbitrary",)),
    )(r, w, kk, kka, v, kd)


ATTN_TQ = 256
ATTN_TK = 256


def _attn_kernel(q_ref, k_ref, v_ref, g_ref, o_ref, s_ref):
    tq = q_ref.shape[1]
    n_chunks = k_ref.shape[1]
    n_tiles = ATTN_TK // LANES
    pair_out = []
    for hp in range(MLA_HEADS // 2):
        o_pair = jnp.zeros((tq, LANES), F32)
        for h in (2 * hp, 2 * hp + 1):
            qh = q_ref[0, :, pl.ds(LANES * h, LANES)]

            m_acc = jnp.full((tq, LANES), -jnp.inf, F32)
            for c in range(n_chunks):
                s = jnp.dot(qh, k_ref[0, c, pl.ds(LANES * h, LANES), :], preferred_element_type=F32)
                s_ref[c] = s
                for j in range(n_tiles):
                    m_acc = jnp.maximum(m_acc, s[:, LANES * j:LANES * (j + 1)])
            m_full = jnp.broadcast_to(jnp.max(m_acc, -1, keepdims=True), (tq, LANES))

            l_acc = jnp.zeros((tq, LANES), F32)
            acc = jnp.zeros((tq, LANES), F32)
            for c in range(n_chunks):
                s = s_ref[c]
                ps = []
                for j in range(n_tiles):
                    p = jnp.exp2(s[:, LANES * j:LANES * (j + 1)] - m_full)
                    l_acc = l_acc + p
                    ps.append(p.astype(BF16))
                vh = v_ref[0, pl.ds(c * ATTN_TK, ATTN_TK), pl.ds(LANES * h, LANES)]
                acc = acc + jnp.dot(jnp.concatenate(ps, -1), vh, preferred_element_type=F32)
            o_pair = o_pair + acc / jnp.sum(l_acc, -1, keepdims=True)
        pair_out.append(o_pair)
    o = jnp.concatenate(pair_out, -1)
    o_ref[0] = o * lax.rsqrt(jnp.mean(jnp.square(o), -1, keepdims=True) + 1e-6) * g_ref[...]


def mla_attention_pallas(q, k, v, out_norm):
    B, Lq, W = q.shape
    Lk = k.shape[1]
    assert Lq % ATTN_TQ == 0 and Lk % ATTN_TK == 0
    kt = jnp.transpose(k.reshape(B, Lk // ATTN_TK, ATTN_TK, W), (0, 1, 3, 2))
    return pl.pallas_call(
        _attn_kernel,
        grid=(B, Lq // ATTN_TQ),
        in_specs=[
            pl.BlockSpec((1, ATTN_TQ, W), lambda b, i: (b, i, 0)),
            pl.BlockSpec((1, Lk // ATTN_TK, W, ATTN_TK), lambda b, i: (b, 0, 0, 0)),
            pl.BlockSpec((1, Lk, W), lambda b, i: (b, 0, 0)),
            pl.BlockSpec((1, GROUP), lambda b, i: (0, 0)),
        ],
        out_specs=pl.BlockSpec((1, ATTN_TQ, GROUP), lambda b, i: (b, i, 0)),
        out_shape=jax.ShapeDtypeStruct((B, Lq, GROUP), F32),
        scratch_shapes=[pltpu.VMEM((Lk // ATTN_TK, ATTN_TQ, ATTN_TK), F32)],
        compiler_params=pltpu.CompilerParams(dimension_semantics=("arbitrary", "arbitrary"),
                                             vmem_limit_bytes=VMEM_LIMIT_BYTES),
        name="mla_attention",
    )(q, kt, v, out_norm[None])


LRU_CHUNK = 256
LRU_HALO = SUBLANES


def _lru_kernel(xl_ref, xc_ref, wbd_ref, bias_ref, c8_ref, cw_ref, cb_ref, gn_ref,
                yl_ref, yc_ref, xs_l, xs_c, hf_l, hf_c, a_s, b_s, hb_s):
    C = GROUP
    CH = LRU_CHUNK
    L = xl_ref.shape[1]
    Lc = xc_ref.shape[1]

    def stage(x_ref, xs, n):
        xs[pl.ds(0, LRU_HALO), :] = jnp.zeros((LRU_HALO, C), F32)
        xs[pl.ds(LRU_HALO + n, LRU_HALO), :] = jnp.zeros((LRU_HALO, C), F32)

        def cp(i, c):
            r0 = pl.multiple_of(i * CH, CH)
            xs[pl.ds(LRU_HALO + r0, CH), :] = x_ref[0, pl.ds(r0, CH), pl.ds(0, C)]
            return c

        lax.fori_loop(0, n // CH, cp, 0)

    stage(xl_ref, xs_l, L)
    stage(xc_ref, xs_c, Lc)

    def coeffs(xs, base, d):
        xv = xs[pl.ds(base, CH + 2 * LRU_HALO), :]
        u = cb_ref[...]
        for j in range(LRU_CONV):
            o = LRU_HALO - LRU_CONV_LEFT + j
            u = u + xv[o:o + CH] * cw_ref[pl.ds(j, 1), :]
        z = jnp.dot(u.astype(BF16), wbd_ref[:, pl.ds(d * 2 * C, 2 * C)], preferred_element_type=F32)
        z = z + bias_ref[:, pl.ds(d * 2 * C, 2 * C)]
        r = jax.nn.sigmoid(z[:, :C])
        i = jax.nn.sigmoid(z[:, C:])
        log_a = r * c8_ref[pl.ds(d, 1), :]
        a = jnp.exp(log_a)
        a_s[...] = a
        b_s[...] = jnp.sqrt(-jnp.tanh(log_a) * (a * a + 1.0)) * (i * u)

    def row_scan(h, out_ref, out_base, reverse):
        def body(t, h):
            tt = CH - 1 - t if reverse else t
            a_t = jnp.broadcast_to(a_s[pl.ds(tt, 1), :], (SUBLANES, C))
            b_t = jnp.broadcast_to(b_s[pl.ds(tt, 1), :], (SUBLANES, C))
            h = a_t * h + b_t
            out_ref[pl.ds(out_base + tt, 1), :] = h[0:1, :]
            return h

        return lax.fori_loop(0, CH, body, h, unroll=8)

    h0 = jnp.zeros((SUBLANES, C), F32)

    h = h0
    for ci in range(Lc // CH):
        coeffs(xs_c, ci * CH, 0)
        h = row_scan(h, hf_c, ci * CH, False)

    def fwd_chunk(ci, h):
        base = pl.multiple_of(ci * CH, CH)
        coeffs(xs_l, base, 0)
        return row_scan(h, hf_l, base, False)

    lax.fori_loop(0, L // CH, fwd_chunk, h)

    def combine(x_ref, hf, base, y_ref):
        hl = hf[pl.ds(base, CH), :] + hb_s[...]
        g = jax.nn.gelu(x_ref[0, pl.ds(base, CH), pl.ds(C, C)])
        v = hl * g
        y = v * lax.rsqrt(jnp.mean(jnp.square(v), -1, keepdims=True) + 1e-6) * gn_ref[...]
        y_ref[0, pl.ds(base, CH), :] = y

    h = h0
    for ci in reversed(range(Lc // CH)):
        coeffs(xs_c, ci * CH, 1)
        h = row_scan(h, hb_s, 0, True)
        combine(xc_ref, hf_c, ci * CH, yc_ref)

    def bwd_chunk(k, h):
        base = pl.multiple_of((L // CH - 1 - k) * CH, CH)
        coeffs(xs_l, base, 1)
        h = row_scan(h, hb_s, 0, True)
        combine(xl_ref, hf_l, base, yl_ref)
        return h

    lax.fori_loop(0, L // CH, bwd_chunk, h)


def rglru_pallas(xg_l, xg_c, conv_w, conv_b, w_r, b_r, w_i, b_i, lam, out_norm):
    B, L, _ = xg_l.shape
    Lc = xg_c.shape[1]
    C = GROUP
    assert L % LRU_CHUNK == 0 and Lc % LRU_CHUNK == 0

    def bd(w):
        return jax.scipy.linalg.block_diag(*[w[n] for n in range(LRU_BLOCKS)])

    wbd = jnp.concatenate([bd(w_r[0]), bd(w_i[0]), bd(w_r[1]), bd(w_i[1])], 1).astype(BF16)
    bias = jnp.concatenate([b_r[0], b_i[0], b_r[1], b_i[1]])[None]
    c8 = -LRU_C * jax.nn.softplus(-lam)

    def full(shape):
        return pl.BlockSpec(shape, lambda b: (0,) * len(shape))

    return pl.pallas_call(
        _lru_kernel,
        grid=(B,),
        in_specs=[
            pl.BlockSpec((1, L, 2 * C), lambda b: (b, 0, 0)),
            pl.BlockSpec((1, Lc, 2 * C), lambda b: (b, 0, 0)),
            full((C, 4 * C)), full((1, 4 * C)), full((2, C)), full((LRU_CONV, C)), full((1, C)), full((1, C)),
        ],
        out_specs=[
            pl.BlockSpec((1, L, C), lambda b: (b, 0, 0)),
            pl.BlockSpec((1, Lc, C), lambda b: (b, 0, 0)),
        ],
        out_shape=[jax.ShapeDtypeStruct((B, L, C), F32), jax.ShapeDtypeStruct((B, Lc, C), F32)],
        scratch_shapes=[
            pltpu.VMEM((L + 2 * LRU_HALO, C), F32),
            pltpu.VMEM((Lc + 2 * LRU_HALO, C), F32),
            pltpu.VMEM((L, C), F32),
            pltpu.VMEM((Lc, C), F32),
            pltpu.VMEM((LRU_CHUNK, C), F32),
            pltpu.VMEM((LRU_CHUNK, C), F32),
            pltpu.VMEM((LRU_CHUNK, C), F32),
        ],
        compiler_params=pltpu.CompilerParams(dimension_semantics=("arbitrary",), vmem_limit_bytes=VMEM_LIMIT_BYTES),
        name="rglru",
    )(xg_l, xg_c, wbd, bias, c8, conv_w, conv_b[None], out_norm[None])


def modulate(x, shift, scale):
    return x * (1.0 + scale) + shift


def layer_norm(x, g, b, eps=1e-5):
    mu = jnp.mean(x, -1, keepdims=True)
    var = jnp.mean(jnp.square(x - mu), -1, keepdims=True)
    return (x - mu) * lax.rsqrt(var + eps) * g + b


def rms_norm(x, g, eps=1e-6):
    return x * lax.rsqrt(jnp.mean(jnp.square(x), -1, keepdims=True) + eps) * g


def dwconv(x, w, b, left):
    K = w.shape[0]
    L = x.shape[1]
    xp = jnp.pad(x, ((0, 0), (left, K - 1 - left), (0, 0)))
    return sum(xp[:, j:j + L] * w[j] for j in range(K)) + b


def token_shift(z, mu_prev, mu_next):
    zp = jnp.pad(z, ((0, 0), (1, 1), (0, 0)))
    return z + mu_prev * (zp[:, :-2] - z) + mu_next * (zp[:, 2:] - z)


def axial_rope(rows):
    r, col = jnp.meshgrid(jnp.arange(rows, dtype=F32), jnp.arange(GRID_W, dtype=F32), indexing='ij')
    half = MLA_ROPE // 2
    inv = 1.0 / (ROPE_BASE ** (jnp.arange(0, half, 2, dtype=F32) / half))
    ang = jnp.concatenate([r.reshape(-1, 1) * inv, col.reshape(-1, 1) * inv], -1)
    return jnp.cos(ang), jnp.sin(ang)


def apply_rope(x, cos, sin):
    h = x.shape[-1] // 2
    x1, x2 = x[..., :h], x[..., h:]
    return jnp.concatenate([x1 * cos - x2 * sin, x1 * sin + x2 * cos], -1)


def mla_padded_qkv(Pa, q_norm, kv_norm, w_uq, w_ukv, cos, sin, want_q=True):
    B, L, _ = Pa.shape
    H, DN, DR, DV = MLA_HEADS, MLA_NOPE, MLA_ROPE, MLA_V
    ckv = rms_norm(Pa[..., A_CKV:A_CKV + MLA_KV_RANK], kv_norm)
    kv = (ckv @ w_ukv).reshape(B, L, H, DN + DV)
    k_rope = Pa[..., A_KR:A_KR + DR]
    if cos is not None:
        k_rope = apply_rope(k_rope, cos[None], sin[None])
    k_rope = jnp.broadcast_to(k_rope[:, :, None, :], (B, L, H, DR))
    zk = jnp.zeros((B, L, H, LANES - DN - DR), F32)
    k = jnp.concatenate([kv[..., :DN], k_rope, zk], -1).reshape(B, L, H * LANES).astype(BF16)
    vv = kv[..., DN:]
    zv = jnp.zeros_like(vv)
    v = jnp.stack([jnp.concatenate([vv[:, :, h], zv[:, :, h]] if h % 2 == 0 else [zv[:, :, h], vv[:, :, h]], -1)
                   for h in range(H)], 2).reshape(B, L, H * LANES).astype(BF16)
    q = None
    if want_q:
        cq = rms_norm(Pa[..., A_CQ:A_CQ + MLA_Q_RANK], q_norm)
        qq = (cq @ w_uq).reshape(B, L, H, DN + DR)
        q_rope = qq[..., DN:]
        if cos is not None:
            q_rope = apply_rope(q_rope, cos[None, :, None, :], sin[None, :, None, :])
        scale = (DN + DR) ** -0.5 * math.log2(math.e)
        q = jnp.concatenate([qq[..., :DN], q_rope, zk], -1) * scale
        q = q.reshape(B, L, H * LANES).astype(BF16)
    return q, k, v


def mla_mixer(Pal, Pac, q_norm, kv_norm, w_uq, w_ukv, out_norm, cos, sin, ctx_out):
    ql, kl, vl = mla_padded_qkv(Pal, q_norm, kv_norm, w_uq, w_ukv, cos, sin)
    qc, kc, vc = mla_padded_qkv(Pac, q_norm, kv_norm, w_uq, w_ukv, None, None, want_q=ctx_out)
    yl = mla_attention_pallas(ql, jnp.concatenate([kc, kl], 1), jnp.concatenate([vc, vl], 1), out_norm)
    yc = mla_attention_pallas(qc, kc, vc, out_norm) if ctx_out else None
    return yl, yc


def rwkv_prepare(Pc_sec, mu_prev, mu_next, w0, w_up, a0, a_up, k_k, k_a):
    z = token_shift(Pc_sec, mu_prev, mu_next)
    B, L, _ = z.shape

    def heads(t):
        return t.reshape(B, L, RWKV_HEADS, RWKV_HEAD)

    r = heads(z[..., C_R:C_R + GROUP])
    k = heads(z[..., C_K:C_K + GROUP])
    v = heads(z[..., C_V:C_V + GROUP])
    kk = k * k_k.reshape(RWKV_HEADS, RWKV_HEAD)
    kk = kk * lax.rsqrt(jnp.maximum(jnp.sum(jnp.square(kk), -1, keepdims=True), 1e-24))
    dirs = []
    for d in range(2):
        wd = z[..., C_WD + d * RWKV_DECAY_LORA:C_WD + (d + 1) * RWKV_DECAY_LORA]
        ad = z[..., C_AD + d * RWKV_AAA_LORA:C_AD + (d + 1) * RWKV_AAA_LORA]
        log_w = -jnp.exp(-jax.nn.softplus(-(w0[d] + jnp.tanh(wd) @ w_up[d])) - 0.5)
        a = heads(jax.nn.sigmoid(a0[d] + ad @ a_up[d]))
        k_d = k * (1.0 + (a - 1.0) * k_a.reshape(RWKV_HEADS, RWKV_HEAD))
        dirs.append((heads(jnp.exp(log_w)), kk * a, k_d))
    gd = z[..., C_GD:C_GD + RWKV_GATE_LORA]
    return r, v, kk, dirs, gd


def rwkv_finish(y, r, v, dirs, gd, g_up, r_k, ln_g, ln_b):
    B, L = y.shape[:2]
    mu = jnp.mean(y, -1, keepdims=True)
    var = jnp.mean(jnp.square(y - mu), -1, keepdims=True)
    yn = (y - mu) * lax.rsqrt(var + RWKV_GN_EPS) * ln_g.reshape(RWKV_HEADS, RWKV_HEAD) + ln_b.reshape(RWKV_HEADS, RWKV_HEAD)
    bonus = sum(jnp.sum(r * kd * r_k, -1, keepdims=True) for (_, _, kd) in dirs) * v
    g = jax.nn.sigmoid(gd) @ g_up
    return (yn + bonus).reshape(B, L, GROUP) * g


def _scan_order(tc, tl):
    def one(c, l, rev):
        if rev:
            c, l = jnp.flip(c, 1), jnp.flip(l, 1)
        return jnp.concatenate([c, l], 1)

    both = jnp.stack([one(tc[0], tl[0], False), one(tc[1], tl[1], True)], 0)
    d, b, s, h, n = both.shape
    return jnp.transpose(both, (2, 4, 0, 1, 3)).reshape(s, n, d * b * h)


def rwkv_mixer(Pl, Pc, mu_prev, mu_next, w0, w_up, a0, a_up, g_up, k_k, k_a, r_k, ln_g, ln_b, ctx_out):
    rl, vl, kkl, dl, gdl = rwkv_prepare(Pl, mu_prev, mu_next, w0, w_up, a0, a_up, k_k, k_a)
    rc, vc, kkc, dc, gdc = rwkv_prepare(Pc, mu_prev, mu_next, w0, w_up, a0, a_up, k_k, k_a)
    B, L = Pl.shape[:2]
    Lc = Pc.shape[1]
    y = rwkv_scan_pallas(
        _scan_order((rc, rc), (rl, rl)),
        _scan_order((dc[0][0], dc[1][0]), (dl[0][0], dl[1][0])),
        _scan_order((kkc, kkc), (kkl, kkl)),
        _scan_order((dc[0][1], dc[1][1]), (dl[0][1], dl[1][1])),
        _scan_order((vc, vc), (vl, vl)),
        _scan_order((dc[0][2], dc[1][2]), (dl[0][2], dl[1][2])),
    )
    y = jnp.transpose(y.reshape(Lc + L, RWKV_HEAD, 2, B, RWKV_HEADS), (2, 3, 0, 4, 1))
    yl = y[0, :, Lc:] + jnp.flip(y[1, :, Lc:], 1)
    out_l = rwkv_finish(yl, rl, vl, dl, gdl, g_up, r_k, ln_g, ln_b)
    out_c = None
    if ctx_out:
        yc = y[0, :, :Lc] + jnp.flip(y[1, :, :Lc], 1)
        out_c = rwkv_finish(yc, rc, vc, dc, gdc, g_up, r_k, ln_g, ln_b)
    return out_l, out_c


def hyena_filters(L, w1, b1, w2, b2, w3):
    t01 = jnp.linspace(0.0, 1.0, L, dtype=F32)[:, None]
    bands = jnp.linspace(1e-4, HY_BANDS - 1, HY_BANDS, dtype=F32)[None, :]
    wpos = (2.0 * math.pi / L) * jnp.arange(L, dtype=F32)[:, None]
    z = jnp.concatenate([t01, jnp.cos(bands * wpos), -jnp.sin(bands * wpos)], -1)
    h = jnp.sin(HY_SIN_FREQ * (z @ w1 + b1))
    h = jnp.sin(HY_SIN_FREQ * (h @ w2 + b2))
    h = (h @ w3).reshape(L, HY_ORDER, 2, GROUP)
    deltas = jnp.abs(jnp.linspace(HY_DECAY_MIN, HY_DECAY_MAX, GROUP, dtype=F32))
    window = jnp.exp(-t01 * deltas) + HY_SHIFT
    return h * window[:, None, None, :]


def two_sided_spectrum(h_fwd, h_bwd):
    L, C = h_fwd.shape
    k = jnp.concatenate([h_fwd, jnp.zeros((1, C), h_fwd.dtype), jnp.flip(h_bwd[1:], 0)], 0)
    return jnp.fft.rfft(k, axis=0)


def long_conv(u, spec, d_skip):
    L = u.shape[1]
    U = jnp.fft.rfft(u, n=2 * L, axis=1)
    y = jnp.fft.irfft(U * spec[None], n=2 * L, axis=1)[:, :L]
    return y + u * d_skip


def hyena_sequence(Pd, conv_w, conv_b, w1, b1, w2, b2, w3, d_skip):
    L = Pd.shape[1]
    z = dwconv(Pd, conv_w, conv_b, 1)
    v, x1, x2 = z[..., :GROUP], z[..., GROUP:2 * GROUP], z[..., 2 * GROUP:]
    h = hyena_filters(L, w1, b1, w2, b2, w3)
    u = x1 * long_conv(v, two_sided_spectrum(h[:, 0, 0], h[:, 0, 1]), d_skip[0])
    return x2 * long_conv(u, two_sided_spectrum(h[:, 1, 0], h[:, 1, 1]), d_skip[1])


def swiglu(x, wg, wu, wd):
    return (jax.nn.silu(x @ wg) * (x @ wu)) @ wd


def moe_swiglu(x, router, wg, wu, wd):
    N, D = x.shape
    logits = x @ router
    top_v, top_i = lax.top_k(logits, TOP_K)
    gates = jax.nn.softmax(top_v, axis=-1)
    A = N * TOP_K
    e_flat = top_i.reshape(-1)
    tok_flat = jnp.arange(A, dtype=jnp.int32) // TOP_K
    g_flat = gates.reshape(-1)
    order = jnp.argsort(e_flat)
    e_sorted = e_flat[order]
    counts = jnp.bincount(e_flat, length=N_EXPERTS)
    starts = jnp.cumsum(counts) - counts
    padded = (counts + MOE_BLOCK - 1) // MOE_BLOCK * MOE_BLOCK
    pends = jnp.cumsum(padded)
    pstarts = pends - padded
    dest = pstarts[e_sorted] + jnp.arange(A, dtype=jnp.int32) - starts[e_sorted]
    n_blocks = -(-A // MOE_BLOCK) + N_EXPERTS
    n_slots = n_blocks * MOE_BLOCK
    slot_tok = jnp.full((n_slots,), N, jnp.int32).at[dest].set(tok_flat[order])
    slot_gate = jnp.zeros((n_slots,), F32).at[dest].set(g_flat[order])
    block_expert = jnp.clip(jnp.searchsorted(pends, jnp.arange(n_blocks) * MOE_BLOCK, side='right'), 0, N_EXPERTS - 1)
    xp = jnp.concatenate([x, jnp.zeros((1, D), x.dtype)], 0)
    xs = xp[slot_tok].reshape(n_blocks, MOE_BLOCK, D)

    def expert_block(args):
        xb, e = args
        return swiglu(xb, wg[e], wu[e], wd[e])

    ys = lax.map(expert_block, (xs, block_expert)).reshape(n_slots, D)
    out = jnp.zeros((N + 1, D), F32).at[slot_tok].add(ys * slot_gate[:, None])
    return out[:N]


def kernel(x, c, ctx, c_ctx, ada_w, ada_b, w_in, mla_q_norm, mla_kv_norm, mla_w_uq, mla_w_ukv, mla_out_norm, lru_conv_w, lru_conv_b, lru_w_r, lru_b_r, lru_w_i, lru_b_i, lru_lambda, lru_out_norm, rwkv_mu_prev, rwkv_mu_next, rwkv_w0, rwkv_w_up, rwkv_a0, rwkv_a_up, rwkv_g_up, rwkv_k_k, rwkv_k_a, rwkv_r_k, rwkv_ln_g, rwkv_ln_b, hy_conv_w, hy_conv_b, hy_f_w1, hy_f_b1, hy_f_w2, hy_f_b2, hy_f_w3, hy_d, hy_out_norm, w_out, ln1_g, ln1_b, ln2_g, ln2_b, ffn_w_gate, ffn_w_up, ffn_w_down, moe_router, moe_w_gate, moe_w_up, moe_w_down):
    B, L, D = x.shape
    Lc = ctx.shape[1]
    rows = L // GRID_W
    cos, sin = axial_rope(rows)
    s_lat = jax.nn.silu(c)
    s_ctx = jax.nn.silu(c_ctx)
    xl, xc = x, ctx
    for li in range(DEPTH):
        ctx_out = li < DEPTH - 1
        mod_l = (s_lat @ ada_w[li] + ada_b[li]).reshape(B, 6, 1, D)
        mod_c = (s_ctx @ ada_w[li] + ada_b[li]).reshape(6, 1, D)

        hl = modulate(xl, mod_l[:, 0], mod_l[:, 1])
        hc = modulate(xc, mod_c[0], mod_c[1])
        w_in_l = w_in[li]
        w_secs = (w_in_l[:, :B_X], w_in_l[:, B_X:C_OFF], w_in_l[:, C_OFF:D_OFF], w_in_l[:, D_OFF:])
        Pal, Pbl, Pcl, Pdl = (hl @ w for w in w_secs)
        Pac, Pbc, Pcc = (hc @ w for w in w_secs[:3])
        a_l, a_c = mla_mixer(Pal, Pac, mla_q_norm[li], mla_kv_norm[li], mla_w_uq[li], mla_w_ukv[li],
                             mla_out_norm[li], cos, sin, ctx_out)
        b_l, b_c = rglru_pallas(Pbl, Pbc, lru_conv_w[li], lru_conv_b[li], lru_w_r[li], lru_b_r[li], lru_w_i[li],
                                lru_b_i[li], lru_lambda[li], lru_out_norm[li])
        c_l, c_c = rwkv_mixer(Pcl, Pcc, rwkv_mu_prev[li], rwkv_mu_next[li], rwkv_w0[li], rwkv_w_up[li], rwkv_a0[li],
                              rwkv_a_up[li], rwkv_g_up[li], rwkv_k_k[li], rwkv_k_a[li], rwkv_r_k[li],
                              rwkv_ln_g[li], rwkv_ln_b[li], ctx_out)
        d_l = rms_norm(hyena_sequence(Pdl, hy_conv_w[li], hy_conv_b[li], hy_f_w1[li], hy_f_b1[li], hy_f_w2[li],
                                      hy_f_b2[li], hy_f_w3[li], hy_d[li]), hy_out_norm[li])
        yl = jnp.concatenate([a_l, b_l, c_l, d_l], -1) @ w_out[li]
        xl = layer_norm(ALPHA * xl + mod_l[:, 2] * yl, ln1_g[li], ln1_b[li])
        if ctx_out:
            Pdc = hc @ w_secs[3]
            d_c = rms_norm(hyena_sequence(Pdc, hy_conv_w[li], hy_conv_b[li], hy_f_w1[li], hy_f_b1[li], hy_f_w2[li],
                                          hy_f_b2[li], hy_f_w3[li], hy_d[li]), hy_out_norm[li])
            yc = jnp.concatenate([a_c, b_c, c_c, d_c], -1) @ w_out[li]
            xc = layer_norm(ALPHA * xc + mod_c[2] * yc, ln1_g[li], ln1_b[li])

        fl = modulate(xl, mod_l[:, 3], mod_l[:, 4]).reshape(B * L, D)
        if ctx_out:
            fc = modulate(xc, mod_c[3], mod_c[4]).reshape(B * Lc, D)
            tokens = jnp.concatenate([fl, fc], 0)
        else:
            tokens = fl
        j = li // 2
        if li % 2 == 0:
            out = swiglu(tokens, ffn_w_gate[j], ffn_w_up[j], ffn_w_down[j])
        else:
            out = moe_swiglu(tokens, moe_router[j], moe_w_gate[j], moe_w_up[j], moe_w_down[j])
        xl = layer_norm(ALPHA * xl + mod_l[:, 5] * out[:B * L].reshape(B, L, D), ln2_g[li], ln2_b[li])
        if ctx_out:
            xc = layer_norm(ALPHA * xc + mod_c[5] * out[B * L:].reshape(B, Lc, D), ln2_g[li], ln2_b[li])
    return xl
```

```python
import math
from functools import partial

import jax
import jax.numpy as jnp
from jax import lax
from jax.experimental import pallas as pl
from jax.experimental.pallas import tpu as pltpu

F32 = jnp.float32
BF16 = jnp.bfloat16

SUBLANES = 8
LANES = 128
VMEM_LIMIT_BYTES = 48 * 1024 * 1024

D_MODEL = 1024
DEPTH = 4
GRID_W = 64
GROUP = D_MODEL // 4

MLA_HEADS = 4
MLA_NOPE = 64
MLA_ROPE = 32
MLA_V = 64
MLA_Q_RANK = 192
MLA_KV_RANK = 128
ROPE_BASE = 10000.0
Q_BLOCK = 128

LRU_BLOCKS = 4
LRU_CONV = 4
LRU_CONV_LEFT = 2
LRU_C = 8.0

RWKV_HEADS = 4
RWKV_HEAD = GROUP // RWKV_HEADS
RWKV_DECAY_LORA = 32
RWKV_AAA_LORA = 32
RWKV_GATE_LORA = 64
RWKV_GN_EPS = 64e-5

HY_ORDER = 2
HY_SHORT = 3
HY_BANDS = 16
HY_EMB = 1 + 2 * HY_BANDS
HY_HIDDEN = 64
HY_SIN_FREQ = 1.0
HY_DECAY_MIN = math.log(1e-2) / 1.5
HY_DECAY_MAX = math.log(1e-2) / 0.3
HY_SHIFT = 0.05

N_EXPERTS = 8
TOP_K = 2
MOE_BLOCK = 512

ALPHA = (2.0 * DEPTH) ** 0.25

A_CQ = 0
A_CKV = A_CQ + MLA_Q_RANK
A_KR = A_CKV + MLA_KV_RANK
B_X = A_KR + MLA_ROPE
B_GATE = B_X + GROUP
C_OFF = B_GATE + GROUP
C_R = 0
C_K = GROUP
C_V = 2 * GROUP
C_WD = 3 * GROUP
C_AD = C_WD + 2 * RWKV_DECAY_LORA
C_GD = C_AD + 2 * RWKV_AAA_LORA
C_COLS = C_GD + RWKV_GATE_LORA
D_OFF = C_OFF + C_COLS
D_COLS = (HY_ORDER + 1) * GROUP


RWKV_TIME_BLOCK = 16
V_TILES = RWKV_HEAD // SUBLANES


def _rwkv_scan_kernel(r_ref, w_ref, kk_ref, kka_ref, v_ref, kd_ref, y_ref, s_ref):
    @pl.when(pl.program_id(0) == 0)
    def _():
        s_ref[...] = jnp.zeros_like(s_ref)

    n_t = r_ref.shape[0]
    p = r_ref.shape[2]

    def step(t, carry):
        def row(ref, k):
            return jnp.broadcast_to(ref[t, pl.ds(k, 1), :], (SUBLANES, p))[None]

        sa = jnp.zeros((V_TILES, SUBLANES, p), F32)
        for k in range(RWKV_HEAD):
            sa = sa + s_ref[k] * row(kk_ref, k)
        vt = v_ref[t].reshape(V_TILES, SUBLANES, p)
        y = jnp.zeros((V_TILES, SUBLANES, p), F32)
        for k in range(RWKV_HEAD):
            sn = s_ref[k] * row(w_ref, k) - sa * row(kka_ref, k) + vt * row(kd_ref, k)
            s_ref[k] = sn
            y = y + sn * row(r_ref, k)
        y_ref[t] = y.reshape(RWKV_HEAD, p)
        return carry

    lax.fori_loop(0, n_t, step, 0)


def rwkv_scan_pallas(r, w, kk, kka, v, kd):
    n_steps, n, p = r.shape
    tb = RWKV_TIME_BLOCK
    assert n == RWKV_HEAD and p == LANES and n_steps % tb == 0
    spec = pl.BlockSpec((tb, n, p), lambda i: (i, 0, 0))
    return pl.pallas_call(
        _rwkv_scan_kernel,
        grid=(n_steps // tb,),
        in_specs=[spec] * 6,
        out_specs=spec,
        out_shape=jax.ShapeDtypeStruct((n_steps, n, p), F32),
        scratch_shapes=[pltpu.VMEM((n, V_TILES, SUBLANES, p), F32)],
        compiler_params=pltpu.CompilerParams(dimension_semantics=("arbitrary",)),
    )(r, w, kk, kka, v, kd)


ATTN_TQ = 256
ATTN_TK = 256


def _attn_kernel(q_ref, k_ref, v_ref, g_ref, o_ref, s_ref):
    tq = q_ref.shape[1]
    n_chunks = k_ref.shape[1]
    n_tiles = ATTN_TK // LANES
    pair_out = []
    for hp in range(MLA_HEADS // 2):
        o_pair = jnp.zeros((tq, LANES), F32)
        for h in (2 * hp, 2 * hp + 1):
            qh = q_ref[0, :, pl.ds(LANES * h, LANES)]

            m_acc = jnp.full((tq, LANES), -jnp.inf, F32)
            for c in range(n_chunks):
                s = jnp.dot(qh, k_ref[0, c, pl.ds(LANES * h, LANES), :], preferred_element_type=F32)
                s_ref[c] = s
                for j in range(n_tiles):
                    m_acc = jnp.maximum(m_acc, s[:, LANES * j:LANES * (j + 1)])
            m_full = jnp.broadcast_to(jnp.max(m_acc, -1, keepdims=True), (tq, LANES))

            l_acc = jnp.zeros((tq, LANES), F32)
            acc = jnp.zeros((tq, LANES), F32)
            for c in range(n_chunks):
                s = s_ref[c]
                ps = []
                for j in range(n_tiles):
                    p = jnp.exp2(s[:, LANES * j:LANES * (j + 1)] - m_full)
                    l_acc = l_acc + p
                    ps.append(p.astype(BF16))
                vh = v_ref[0, pl.ds(c * ATTN_TK, ATTN_TK), pl.ds(LANES * h, LANES)]
                acc = acc + jnp.dot(jnp.concatenate(ps, -1), vh, preferred_element_type=F32)
            o_pair = o_pair + acc / jnp.sum(l_acc, -1, keepdims=True)
        pair_out.append(o_pair)
    o = jnp.concatenate(pair_out, -1)
    o_ref[0] = o * lax.rsqrt(jnp.mean(jnp.square(o), -1, keepdims=True) + 1e-6) * g_ref[...]


def mla_attention_pallas(q, k, v, out_norm):
    B, Lq, W = q.shape
    Lk = k.shape[1]
    assert Lq % ATTN_TQ == 0 and Lk % ATTN_TK == 0
    kt = jnp.transpose(k.reshape(B, Lk // ATTN_TK, ATTN_TK, W), (0, 1, 3, 2))
    return pl.pallas_call(
        _attn_kernel,
        grid=(B, Lq // ATTN_TQ),
        in_specs=[
            pl.BlockSpec((1, ATTN_TQ, W), lambda b, i: (b, i, 0)),
            pl.BlockSpec((1, Lk // ATTN_TK, W, ATTN_TK), lambda b, i: (b, 0, 0, 0)),
            pl.BlockSpec((1, Lk, W), lambda b, i: (b, 0, 0)),
            pl.BlockSpec((1, GROUP), lambda b, i: (0, 0)),
        ],
        out_specs=pl.BlockSpec((1, ATTN_TQ, GROUP), lambda b, i: (b, i, 0)),
        out_shape=jax.ShapeDtypeStruct((B, Lq, GROUP), F32),
        scratch_shapes=[pltpu.VMEM((Lk // ATTN_TK, ATTN_TQ, ATTN_TK), F32)],
        compiler_params=pltpu.CompilerParams(dimension_semantics=("arbitrary", "arbitrary"),
                                             vmem_limit_bytes=VMEM_LIMIT_BYTES),
        name="mla_attention",
    )(q, kt, v, out_norm[None])


LRU_CHUNK = 256
LRU_HALO = SUBLANES


def _lru_kernel(xl_ref, xc_ref, wbd_ref, bias_ref, c8_ref, cw_ref, cb_ref, gn_ref,
                yl_ref, yc_ref, xs_l, xs_c, hf_l, hf_c, a_s, b_s, hb_s):
    C = GROUP
    CH = LRU_CHUNK
    L = xl_ref.shape[1]
    Lc = xc_ref.shape[1]

    def stage(x_ref, xs, n):
        xs[pl.ds(0, LRU_HALO), :] = jnp.zeros((LRU_HALO, C), F32)
        xs[pl.ds(LRU_HALO + n, LRU_HALO), :] = jnp.zeros((LRU_HALO, C), F32)

        def cp(i, c):
            r0 = pl.multiple_of(i * CH, CH)
            xs[pl.ds(LRU_HALO + r0, CH), :] = x_ref[0, pl.ds(r0, CH), pl.ds(0, C)]
            return c

        lax.fori_loop(0, n // CH, cp, 0)

    stage(xl_ref, xs_l, L)
    stage(xc_ref, xs_c, Lc)

    def coeffs(xs, base, d):
        xv = xs[pl.ds(base, CH + 2 * LRU_HALO), :]
        u = cb_ref[...]
        for j in range(LRU_CONV):
            o = LRU_HALO - LRU_CONV_LEFT + j
            u = u + xv[o:o + CH] * cw_ref[pl.ds(j, 1), :]
        z = jnp.dot(u.astype(BF16), wbd_ref[:, pl.ds(d * 2 * C, 2 * C)], preferred_element_type=F32)
        z = z + bias_ref[:, pl.ds(d * 2 * C, 2 * C)]
        r = jax.nn.sigmoid(z[:, :C])
        i = jax.nn.sigmoid(z[:, C:])
        log_a = r * c8_ref[pl.ds(d, 1), :]
        a = jnp.exp(log_a)
        a_s[...] = a
        b_s[...] = jnp.sqrt(-jnp.tanh(log_a) * (a * a + 1.0)) * (i * u)

    def row_scan(h, out_ref, out_base, reverse):
        def body(t, h):
            tt = CH - 1 - t if reverse else t
            a_t = jnp.broadcast_to(a_s[pl.ds(tt, 1), :], (SUBLANES, C))
            b_t = jnp.broadcast_to(b_s[pl.ds(tt, 1), :], (SUBLANES, C))
            h = a_t * h + b_t
            out_ref[pl.ds(out_base + tt, 1), :] = h[0:1, :]
            return h

        return lax.fori_loop(0, CH, body, h, unroll=8)

    h0 = jnp.zeros((SUBLANES, C), F32)

    h = h0
    for ci in range(Lc // CH):
        coeffs(xs_c, ci * CH, 0)
        h = row_scan(h, hf_c, ci * CH, False)

    def fwd_chunk(ci, h):
        base = pl.multiple_of(ci * CH, CH)
        coeffs(xs_l, base, 0)
        return row_scan(h, hf_l, base, False)

    lax.fori_loop(0, L // CH, fwd_chunk, h)

    def combine(x_ref, hf, base, y_ref):
        hl = hf[pl.ds(base, CH), :] + hb_s[...]
        g = jax.nn.gelu(x_ref[0, pl.ds(base, CH), pl.ds(C, C)])
        v = hl * g
        y = v * lax.rsqrt(jnp.mean(jnp.square(v), -1, keepdims=True) + 1e-6) * gn_ref[...]
        y_ref[0, pl.ds(base, CH), :] = y

    h = h0
    for ci in reversed(range(Lc // CH)):
        coeffs(xs_c, ci * CH, 1)
        h = row_scan(h, hb_s, 0, True)
        combine(xc_ref, hf_c, ci * CH, yc_ref)

    def bwd_chunk(k, h):
        base = pl.multiple_of((L // CH - 1 - k) * CH, CH)
        coeffs(xs_l, base, 1)
        h = row_scan(h, hb_s, 0, True)
        combine(xl_ref, hf_l, base, yl_ref)
        return h

    lax.fori_loop(0, L // CH, bwd_chunk, h)


def rglru_pallas(xg_l, xg_c, conv_w, conv_b, w_r, b_r, w_i, b_i, lam, out_norm):
    B, L, _ = xg_l.shape
    Lc = xg_c.shape[1]
    C = GROUP
    assert L % LRU_CHUNK == 0 and Lc % LRU_CHUNK == 0

    def bd(w):
        return jax.scipy.linalg.block_diag(*[w[n] for n in range(LRU_BLOCKS)])

    wbd = jnp.concatenate([bd(w_r[0]), bd(w_i[0]), bd(w_r[1]), bd(w_i[1])], 1).astype(BF16)
    bias = jnp.concatenate([b_r[0], b_i[0], b_r[1], b_i[1]])[None]
    c8 = -LRU_C * jax.nn.softplus(-lam)

    def full(shape):
        return pl.BlockSpec(shape, lambda b: (0,) * len(shape))

    return pl.pallas_call(
        _lru_kernel,
        grid=(B,),
        in_specs=[
            pl.BlockSpec((1, L, 2 * C), lambda b: (b, 0, 0)),
            pl.BlockSpec((1, Lc, 2 * C), lambda b: (b, 0, 0)),
            full((C, 4 * C)), full((1, 4 * C)), full((2, C)), full((LRU_CONV, C)), full((1, C)), full((1, C)),
        ],
        out_specs=[
            pl.BlockSpec((1, L, C), lambda b: (b, 0, 0)),
            pl.BlockSpec((1, Lc, C), lambda b: (b, 0, 0)),
        ],
        out_shape=[jax.ShapeDtypeStruct((B, L, C), F32), jax.ShapeDtypeStruct((B, Lc, C), F32)],
        scratch_shapes=[
            pltpu.VMEM((L + 2 * LRU_HALO, C), F32),
            pltpu.VMEM((Lc + 2 * LRU_HALO, C), F32),
            pltpu.VMEM((L, C), F32),
            pltpu.VMEM((Lc, C), F32),
            pltpu.VMEM((LRU_CHUNK, C), F32),
            pltpu.VMEM((LRU_CHUNK, C), F32),
            pltpu.VMEM((LRU_CHUNK, C), F32),
        ],
        compiler_params=pltpu.CompilerParams(dimension_semantics=("arbitrary",), vmem_limit_bytes=VMEM_LIMIT_BYTES),
        name="rglru",
    )(xg_l, xg_c, wbd, bias, c8, conv_w, conv_b[None], out_norm[None])


HY_T = 256
HY_CB = 8


def _hyena_kernel(k_ref, d_ref, v_ref, x1_ref, x2_ref, o_ref, u_s, acc_s):
    n_rows = v_ref.shape[1]
    T = HY_T
    nb = k_ref.shape[2] // (2 * T)
    bsz = n_rows // nb

    def conv(ci, order):
        acc_s[...] = jnp.zeros_like(acc_s)
        for dd in range(-(nb - 1), nb):
            w2 = k_ref[order, pl.ds(ci, 1), pl.ds(T * (dd + nb - 1), 2 * T)]
            x = jnp.broadcast_to(w2, (T, 2 * T))
            r = pltpu.roll(x, 0, 1, stride=1, stride_axis=0)
            tb = r[:, T:].astype(BF16)
            j0, j1 = max(0, -dd), min(nb, nb - dd)
            lhs = u_s[pl.ds(bsz * j0, bsz * (j1 - j0)), :]
            dst = pl.ds(bsz * (j0 + dd), bsz * (j1 - j0))
            acc_s[dst, :] = acc_s[dst, :] + jnp.dot(lhs, tb, preferred_element_type=F32)

    def channel(ci, carry):
        v = v_ref[ci]
        u_s[...] = v.astype(BF16)
        conv(ci, 0)
        u = x1_ref[ci] * (acc_s[...] + v * d_ref[pl.ds(ci, 1), pl.ds(0, 1)])
        u_s[...] = u.astype(BF16)
        conv(ci, 1)
        o_ref[ci] = x2_ref[ci] * (acc_s[...] + u * d_ref[pl.ds(ci, 1), pl.ds(1, 1)])
        return carry

    lax.fori_loop(0, HY_CB, channel, 0)


def hyena_conv_pallas(kfull, d_skip, vT, x1T, x2T):
    C, R, T = vT.shape
    two_l = kfull.shape[2]
    assert T == HY_T and C % HY_CB == 0
    blk = pl.BlockSpec((HY_CB, R, T), lambda c: (c, 0, 0))
    return pl.pallas_call(
        _hyena_kernel,
        grid=(C // HY_CB,),
        in_specs=[
            pl.BlockSpec((HY_ORDER, HY_CB, two_l), lambda c: (0, c, 0)),
            pl.BlockSpec((HY_CB, HY_ORDER), lambda c: (c, 0)),
            blk, blk, blk,
        ],
        out_specs=blk,
        out_shape=jax.ShapeDtypeStruct((C, R, T), F32),
        scratch_shapes=[pltpu.VMEM((R, T), BF16), pltpu.VMEM((R, T), F32)],
        compiler_params=pltpu.CompilerParams(dimension_semantics=("arbitrary",), vmem_limit_bytes=VMEM_LIMIT_BYTES),
        name="hyena_conv",
    )(kfull, d_skip.T, vT, x1T, x2T)


def modulate(x, shift, scale):
    return x * (1.0 + scale) + shift


def layer_norm(x, g, b, eps=1e-5):
    mu = jnp.mean(x, -1, keepdims=True)
    var = jnp.mean(jnp.square(x - mu), -1, keepdims=True)
    return (x - mu) * lax.rsqrt(var + eps) * g + b


def rms_norm(x, g, eps=1e-6):
    return x * lax.rsqrt(jnp.mean(jnp.square(x), -1, keepdims=True) + eps) * g


def dwconv(x, w, b, left):
    K = w.shape[0]
    L = x.shape[1]
    xp = jnp.pad(x, ((0, 0), (left, K - 1 - left), (0, 0)))
    return sum(xp[:, j:j + L] * w[j] for j in range(K)) + b


def token_shift(z, mu_prev, mu_next):
    zp = jnp.pad(z, ((0, 0), (1, 1), (0, 0)))
    return z + mu_prev * (zp[:, :-2] - z) + mu_next * (zp[:, 2:] - z)


def axial_rope(rows):
    r, col = jnp.meshgrid(jnp.arange(rows, dtype=F32), jnp.arange(GRID_W, dtype=F32), indexing='ij')
    half = MLA_ROPE // 2
    inv = 1.0 / (ROPE_BASE ** (jnp.arange(0, half, 2, dtype=F32) / half))
    ang = jnp.concatenate([r.reshape(-1, 1) * inv, col.reshape(-1, 1) * inv], -1)
    return jnp.cos(ang), jnp.sin(ang)


def apply_rope(x, cos, sin):
    h = x.shape[-1] // 2
    x1, x2 = x[..., :h], x[..., h:]
    return jnp.concatenate([x1 * cos - x2 * sin, x1 * sin + x2 * cos], -1)


def mla_padded_qkv(Pa, q_norm, kv_norm, w_uq, w_ukv, cos, sin, want_q=True):
    B, L, _ = Pa.shape
    H, DN, DR, DV = MLA_HEADS, MLA_NOPE, MLA_ROPE, MLA_V
    ckv = rms_norm(Pa[..., A_CKV:A_CKV + MLA_KV_RANK], kv_norm)
    kv = (ckv @ w_ukv).reshape(B, L, H, DN + DV)
    k_rope = Pa[..., A_KR:A_KR + DR]
    if cos is not None:
        k_rope = apply_rope(k_rope, cos[None], sin[None])
    k_rope = jnp.broadcast_to(k_rope[:, :, None, :], (B, L, H, DR))
    zk = jnp.zeros((B, L, H, LANES - DN - DR), F32)
    k = jnp.concatenate([kv[..., :DN], k_rope, zk], -1).reshape(B, L, H * LANES).astype(BF16)
    vv = kv[..., DN:]
    zv = jnp.zeros_like(vv)
    v = jnp.stack([jnp.concatenate([vv[:, :, h], zv[:, :, h]] if h % 2 == 0 else [zv[:, :, h], vv[:, :, h]], -1)
                   for h in range(H)], 2).reshape(B, L, H * LANES).astype(BF16)
    q = None
    if want_q:
        cq = rms_norm(Pa[..., A_CQ:A_CQ + MLA_Q_RANK], q_norm)
        qq = (cq @ w_uq).reshape(B, L, H, DN + DR)
        q_rope = qq[..., DN:]
        if cos is not None:
            q_rope = apply_rope(q_rope, cos[None, :, None, :], sin[None, :, None, :])
        scale = (DN + DR) ** -0.5 * math.log2(math.e)
        q = jnp.concatenate([qq[..., :DN], q_rope, zk], -1) * scale
        q = q.reshape(B, L, H * LANES).astype(BF16)
    return q, k, v


def mla_mixer(Pal, Pac, q_norm, kv_norm, w_uq, w_ukv, out_norm, cos, sin, ctx_out):
    ql, kl, vl = mla_padded_qkv(Pal, q_norm, kv_norm, w_uq, w_ukv, cos, sin)
    qc, kc, vc = mla_padded_qkv(Pac, q_norm, kv_norm, w_uq, w_ukv, None, None, want_q=ctx_out)
    yl = mla_attention_pallas(ql, jnp.concatenate([kc, kl], 1), jnp.concatenate([vc, vl], 1), out_norm)
    yc = mla_attention_pallas(qc, kc, vc, out_norm) if ctx_out else None
    return yl, yc


def rwkv_prepare(Pc_sec, mu_prev, mu_next, w0, w_up, a0, a_up, k_k, k_a):
    z = token_shift(Pc_sec, mu_prev, mu_next)
    B, L, _ = z.shape

    def heads(t):
        return t.reshape(B, L, RWKV_HEADS, RWKV_HEAD)

    r = heads(z[..., C_R:C_R + GROUP])
    k = heads(z[..., C_K:C_K + GROUP])
    v = heads(z[..., C_V:C_V + GROUP])
    kk = k * k_k.reshape(RWKV_HEADS, RWKV_HEAD)
    kk = kk * lax.rsqrt(jnp.maximum(jnp.sum(jnp.square(kk), -1, keepdims=True), 1e-24))
    dirs = []
    for d in range(2):
        wd = z[..., C_WD + d * RWKV_DECAY_LORA:C_WD + (d + 1) * RWKV_DECAY_LORA]
        ad = z[..., C_AD + d * RWKV_AAA_LORA:C_AD + (d + 1) * RWKV_AAA_LORA]
        log_w = -jnp.exp(-jax.nn.softplus(-(w0[d] + jnp.tanh(wd) @ w_up[d])) - 0.5)
        a = heads(jax.nn.sigmoid(a0[d] + ad @ a_up[d]))
        k_d = k * (1.0 + (a - 1.0) * k_a.reshape(RWKV_HEADS, RWKV_HEAD))
        dirs.append((heads(jnp.exp(log_w)), kk * a, k_d))
    gd = z[..., C_GD:C_GD + RWKV_GATE_LORA]
    return r, v, kk, dirs, gd


def rwkv_finish(y, r, v, dirs, gd, g_up, r_k, ln_g, ln_b):
    B, L = y.shape[:2]
    mu = jnp.mean(y, -1, keepdims=True)
    var = jnp.mean(jnp.square(y - mu), -1, keepdims=True)
    yn = (y - mu) * lax.rsqrt(var + RWKV_GN_EPS) * ln_g.reshape(RWKV_HEADS, RWKV_HEAD) + ln_b.reshape(RWKV_HEADS, RWKV_HEAD)
    bonus = sum(jnp.sum(r * kd * r_k, -1, keepdims=True) for (_, _, kd) in dirs) * v
    g = jax.nn.sigmoid(gd) @ g_up
    return (yn + bonus).reshape(B, L, GROUP) * g


def _scan_order(tc, tl):
    def one(c, l, rev):
        if rev:
            c, l = jnp.flip(c, 1), jnp.flip(l, 1)
        return jnp.concatenate([c, l], 1)

    both = jnp.stack([one(tc[0], tl[0], False), one(tc[1], tl[1], True)], 0)
    d, b, s, h, n = both.shape
    return jnp.transpose(both, (2, 4, 0, 1, 3)).reshape(s, n, d * b * h)


def rwkv_mixer(Pl, Pc, mu_prev, mu_next, w0, w_up, a0, a_up, g_up, k_k, k_a, r_k, ln_g, ln_b, ctx_out):
    rl, vl, kkl, dl, gdl = rwkv_prepare(Pl, mu_prev, mu_next, w0, w_up, a0, a_up, k_k, k_a)
    rc, vc, kkc, dc, gdc = rwkv_prepare(Pc, mu_prev, mu_next, w0, w_up, a0, a_up, k_k, k_a)
    B, L = Pl.shape[:2]
    Lc = Pc.shape[1]
    y = rwkv_scan_pallas(
        _scan_order((rc, rc), (rl, rl)),
        _scan_order((dc[0][0], dc[1][0]), (dl[0][0], dl[1][0])),
        _scan_order((kkc, kkc), (kkl, kkl)),
        _scan_order((dc[0][1], dc[1][1]), (dl[0][1], dl[1][1])),
        _scan_order((vc, vc), (vl, vl)),
        _scan_order((dc[0][2], dc[1][2]), (dl[0][2], dl[1][2])),
    )
    y = jnp.transpose(y.reshape(Lc + L, RWKV_HEAD, 2, B, RWKV_HEADS), (2, 3, 0, 4, 1))
    yl = y[0, :, Lc:] + jnp.flip(y[1, :, Lc:], 1)
    out_l = rwkv_finish(yl, rl, vl, dl, gdl, g_up, r_k, ln_g, ln_b)
    out_c = None
    if ctx_out:
        yc = y[0, :, :Lc] + jnp.flip(y[1, :, :Lc], 1)
        out_c = rwkv_finish(yc, rc, vc, dc, gdc, g_up, r_k, ln_g, ln_b)
    return out_l, out_c


def hyena_filters(L, w1, b1, w2, b2, w3):
    t01 = jnp.linspace(0.0, 1.0, L, dtype=F32)[:, None]
    bands = jnp.linspace(1e-4, HY_BANDS - 1, HY_BANDS, dtype=F32)[None, :]
    wpos = (2.0 * math.pi / L) * jnp.arange(L, dtype=F32)[:, None]
    z = jnp.concatenate([t01, jnp.cos(bands * wpos), -jnp.sin(bands * wpos)], -1)
    h = jnp.sin(HY_SIN_FREQ * (z @ w1 + b1))
    h = jnp.sin(HY_SIN_FREQ * (h @ w2 + b2))
    h = (h @ w3).reshape(L, HY_ORDER, 2, GROUP)
    deltas = jnp.abs(jnp.linspace(HY_DECAY_MIN, HY_DECAY_MAX, GROUP, dtype=F32))
    window = jnp.exp(-t01 * deltas) + HY_SHIFT
    return h * window[:, None, None, :]


def hyena_sequence(Pd, conv_w, conv_b, w1, b1, w2, b2, w3, d_skip):
    B, L, _ = Pd.shape
    C = GROUP
    T = HY_T
    nb = L // T
    z = dwconv(Pd, conv_w, conv_b, 1)
    zT = jnp.transpose(z.reshape(B, nb, T, 3 * C), (3, 1, 0, 2)).reshape(3 * C, nb * B, T)
    h = hyena_filters(L, w1, b1, w2, b2, w3)
    hf = jnp.transpose(h[:, :, 0], (1, 2, 0))
    hb = jnp.transpose(h[:, :, 1], (1, 2, 0))
    kfull = jnp.concatenate([jnp.zeros((HY_ORDER, C, 1), F32), jnp.flip(hb[..., 1:], -1), hf], -1)
    oT = hyena_conv_pallas(kfull, d_skip, zT[:C], zT[C:2 * C], zT[2 * C:])
    return jnp.transpose(oT.reshape(C, nb, B, T), (2, 1, 3, 0)).reshape(B, L, C)


def swiglu(x, wg, wu, wd):
    return (jax.nn.silu(x @ wg) * (x @ wu)) @ wd


def moe_swiglu(x, router, wg, wu, wd):
    N, D = x.shape
    logits = x @ router
    top_v, top_i = lax.top_k(logits, TOP_K)
    gates = jax.nn.softmax(top_v, axis=-1)
    A = N * TOP_K
    e_flat = top_i.reshape(-1)
    tok_flat = jnp.arange(A, dtype=jnp.int32) // TOP_K
    g_flat = gates.reshape(-1)
    order = jnp.argsort(e_flat)
    e_sorted = e_flat[order]
    counts = jnp.bincount(e_flat, length=N_EXPERTS)
    starts = jnp.cumsum(counts) - counts
    padded = (counts + MOE_BLOCK - 1) // MOE_BLOCK * MOE_BLOCK
    pends = jnp.cumsum(padded)
    pstarts = pends - padded
    dest = pstarts[e_sorted] + jnp.arange(A, dtype=jnp.int32) - starts[e_sorted]
    n_blocks = -(-A // MOE_BLOCK) + N_EXPERTS
    n_slots = n_blocks * MOE_BLOCK
    slot_tok = jnp.full((n_slots,), N, jnp.int32).at[dest].set(tok_flat[order])
    slot_gate = jnp.zeros((n_slots,), F32).at[dest].set(g_flat[order])
    block_expert = jnp.clip(jnp.searchsorted(pends, jnp.arange(n_blocks) * MOE_BLOCK, side='right'), 0, N_EXPERTS - 1)
    xp = jnp.concatenate([x, jnp.zeros((1, D), x.dtype)], 0)
    xs = xp[slot_tok].reshape(n_blocks, MOE_BLOCK, D)

    def expert_block(args):
        xb, e = args
        return swiglu(xb, wg[e], wu[e], wd[e])

    ys = lax.map(expert_block, (xs, block_expert)).reshape(n_slots, D)
    out = jnp.zeros((N + 1, D), F32).at[slot_tok].add(ys * slot_gate[:, None])
    return out[:N]


def kernel(x, c, ctx, c_ctx, ada_w, ada_b, w_in, mla_q_norm, mla_kv_norm, mla_w_uq, mla_w_ukv, mla_out_norm, lru_conv_w, lru_conv_b, lru_w_r, lru_b_r, lru_w_i, lru_b_i, lru_lambda, lru_out_norm, rwkv_mu_prev, rwkv_mu_next, rwkv_w0, rwkv_w_up, rwkv_a0, rwkv_a_up, rwkv_g_up, rwkv_k_k, rwkv_k_a, rwkv_r_k, rwkv_ln_g, rwkv_ln_b, hy_conv_w, hy_conv_b, hy_f_w1, hy_f_b1, hy_f_w2, hy_f_b2, hy_f_w3, hy_d, hy_out_norm, w_out, ln1_g, ln1_b, ln2_g, ln2_b, ffn_w_gate, ffn_w_up, ffn_w_down, moe_router, moe_w_gate, moe_w_up, moe_w_down):
    B, L, D = x.shape
    Lc = ctx.shape[1]
    rows = L // GRID_W
    cos, sin = axial_rope(rows)
    s_lat = jax.nn.silu(c)
    s_ctx = jax.nn.silu(c_ctx)
    xl, xc = x, ctx
    for li in range(DEPTH):
        ctx_out = li < DEPTH - 1
        mod_l = (s_lat @ ada_w[li] + ada_b[li]).reshape(B, 6, 1, D)
        mod_c = (s_ctx @ ada_w[li] + ada_b[li]).reshape(6, 1, D)

        hl = modulate(xl, mod_l[:, 0], mod_l[:, 1])
        hc = modulate(xc, mod_c[0], mod_c[1])
        w_in_l = w_in[li]
        w_secs = (w_in_l[:, :B_X], w_in_l[:, B_X:C_OFF], w_in_l[:, C_OFF:D_OFF], w_in_l[:, D_OFF:])
        Pal, Pbl, Pcl, Pdl = (hl @ w for w in w_secs)
        Pac, Pbc, Pcc = (hc @ w for w in w_secs[:3])
        a_l, a_c = mla_mixer(Pal, Pac, mla_q_norm[li], mla_kv_norm[li], mla_w_uq[li], mla_w_ukv[li],
                             mla_out_norm[li], cos, sin, ctx_out)
        b_l, b_c = rglru_pallas(Pbl, Pbc, lru_conv_w[li], lru_conv_b[li], lru_w_r[li], lru_b_r[li], lru_w_i[li],
                                lru_b_i[li], lru_lambda[li], lru_out_norm[li])
        c_l, c_c = rwkv_mixer(Pcl, Pcc, rwkv_mu_prev[li], rwkv_mu_next[li], rwkv_w0[li], rwkv_w_up[li], rwkv_a0[li],
                              rwkv_a_up[li], rwkv_g_up[li], rwkv_k_k[li], rwkv_k_a[li], rwkv_r_k[li],
                              rwkv_ln_g[li], rwkv_ln_b[li], ctx_out)
        d_l = rms_norm(hyena_sequence(Pdl, hy_conv_w[li], hy_conv_b[li], hy_f_w1[li], hy_f_b1[li], hy_f_w2[li],
                                      hy_f_b2[li], hy_f_w3[li], hy_d[li]), hy_out_norm[li])
        yl = jnp.concatenate([a_l, b_l, c_l, d_l], -1) @ w_out[li]
        xl = layer_norm(ALPHA * xl + mod_l[:, 2] * yl, ln1_g[li], ln1_b[li])
        if ctx_out:
            Pdc = hc @ w_secs[3]
            d_c = rms_norm(hyena_sequence(Pdc, hy_conv_w[li], hy_conv_b[li], hy_f_w1[li], hy_f_b1[li], hy_f_w2[li],
                                          hy_f_b2[li], hy_f_w3[li], hy_d[li]), hy_out_norm[li])
            yc = jnp.concatenate([a_c, b_c, c_c, d_c], -1) @ w_out[li]
            xc = layer_norm(ALPHA * xc + mod_c[2] * yc, ln1_g[li], ln1_b[li])

        fl = modulate(xl, mod_l[:, 3], mod_l[:, 4]).reshape(B * L, D)
        if ctx_out:
            fc = modulate(xc, mod_c[3], mod_c[4]).reshape(B * Lc, D)
            tokens = jnp.concatenate([fl, fc], 0)
        else:
            tokens = fl
        j = li // 2
        if li % 2 == 0:
            out = swiglu(tokens, ffn_w_gate[j], ffn_w_up[j], ffn_w_down[j])
        else:
            out = moe_swiglu(tokens, moe_router[j], moe_w_gate[j], moe_w_up[j], moe_w_down[j])
        xl = layer_norm(ALPHA * xl + mod_l[:, 5] * out[:B * L].reshape(B, L, D), ln2_g[li], ln2_b[li])
        if ctx_out:
            xc = layer_norm(ALPHA * xc + mod_c[5] * out[B * L:].reshape(B, Lc, D), ln2_g[li], ln2_b[li])
    return xl
```

```python
import math
from functools import partial

import jax
import jax.numpy as jnp
from jax import lax
from jax.experimental import pallas as pl
from jax.experimental.pallas import tpu as pltpu

F32 = jnp.float32
BF16 = jnp.bfloat16

SUBLANES = 8
LANES = 128
VMEM_LIMIT_BYTES = 48 * 1024 * 1024

D_MODEL = 1024
DEPTH = 4
GRID_W = 64
GROUP = D_MODEL // 4

MLA_HEADS = 4
MLA_NOPE = 64
MLA_ROPE = 32
MLA_V = 64
MLA_Q_RANK = 192
MLA_KV_RANK = 128
ROPE_BASE = 10000.0
Q_BLOCK = 128

LRU_BLOCKS = 4
LRU_CONV = 4
LRU_CONV_LEFT = 2
LRU_C = 8.0

RWKV_HEADS = 4
RWKV_HEAD = GROUP // RWKV_HEADS
RWKV_DECAY_LORA = 32
RWKV_AAA_LORA = 32
RWKV_GATE_LORA = 64
RWKV_GN_EPS = 64e-5

HY_ORDER = 2
HY_SHORT = 3
HY_BANDS = 16
HY_EMB = 1 + 2 * HY_BANDS
HY_HIDDEN = 64
HY_SIN_FREQ = 1.0
HY_DECAY_MIN = math.log(1e-2) / 1.5
HY_DECAY_MAX = math.log(1e-2) / 0.3
HY_SHIFT = 0.05

N_EXPERTS = 8
TOP_K = 2
MOE_BLOCK = 512

ALPHA = (2.0 * DEPTH) ** 0.25

A_CQ = 0
A_CKV = A_CQ + MLA_Q_RANK
A_KR = A_CKV + MLA_KV_RANK
B_X = A_KR + MLA_ROPE
B_GATE = B_X + GROUP
C_OFF = B_GATE + GROUP
C_R = 0
C_K = GROUP
C_V = 2 * GROUP
C_WD = 3 * GROUP
C_AD = C_WD + 2 * RWKV_DECAY_LORA
C_GD = C_AD + 2 * RWKV_AAA_LORA
C_COLS = C_GD + RWKV_GATE_LORA
D_OFF = C_OFF + C_COLS
D_COLS = (HY_ORDER + 1) * GROUP


RWKV_TIME_BLOCK = 16
V_TILES = RWKV_HEAD // SUBLANES


def _rwkv_scan_kernel(r_ref, w_ref, kk_ref, kka_ref, v_ref, kd_ref, y_ref, s_ref):
    @pl.when(pl.program_id(0) == 0)
    def _():
        s_ref[...] = jnp.zeros_like(s_ref)

    n_t = r_ref.shape[0]
    p = r_ref.shape[2]

    def step(t, carry):
        def row(ref, k):
            return jnp.broadcast_to(ref[t, pl.ds(k, 1), :], (SUBLANES, p))[None]

        sa = jnp.zeros((V_TILES, SUBLANES, p), F32)
        for k in range(RWKV_HEAD):
            sa = sa + s_ref[k] * row(kk_ref, k)
        vt = v_ref[t].reshape(V_TILES, SUBLANES, p)
        y = jnp.zeros((V_TILES, SUBLANES, p), F32)
        for k in range(RWKV_HEAD):
            sn = s_ref[k] * row(w_ref, k) - sa * row(kka_ref, k) + vt * row(kd_ref, k)
            s_ref[k] = sn
            y = y + sn * row(r_ref, k)
        y_ref[t] = y.reshape(RWKV_HEAD, p)
        return carry

    lax.fori_loop(0, n_t, step, 0)


def rwkv_scan_pallas(r, w, kk, kka, v, kd):
    n_steps, n, p = r.shape
    tb = RWKV_TIME_BLOCK
    assert n == RWKV_HEAD and p == LANES and n_steps % tb == 0
    spec = pl.BlockSpec((tb, n, p), lambda i: (i, 0, 0))
    return pl.pallas_call(
        _rwkv_scan_kernel,
        grid=(n_steps // tb,),
        in_specs=[spec] * 6,
        out_specs=spec,
        out_shape=jax.ShapeDtypeStruct((n_steps, n, p), F32),
        scratch_shapes=[pltpu.VMEM((n, V_TILES, SUBLANES, p), F32)],
        compiler_params=pltpu.CompilerParams(dimension_semantics=("arbitrary",)),
    )(r, w, kk, kka, v, kd)


ATTN_TQ = 256
ATTN_TK = 256


def _attn_kernel(q_ref, k_ref, v_ref, g_ref, o_ref, s_ref):
    tq = q_ref.shape[1]
    n_chunks = k_ref.shape[1]
    n_tiles = ATTN_TK // LANES
    pair_out = []
    for hp in range(MLA_HEADS // 2):
        o_pair = jnp.zeros((tq, LANES), F32)
        for h in (2 * hp, 2 * hp + 1):
            qh = q_ref[0, :, pl.ds(LANES * h, LANES)]

            m_acc = jnp.full((tq, LANES), -jnp.inf, F32)
            for c in range(n_chunks):
                s = jnp.dot(qh, k_ref[0, c, pl.ds(LANES * h, LANES), :], preferred_element_type=F32)
                s_ref[c] = s
                for j in range(n_tiles):
                    m_acc = jnp.maximum(m_acc, s[:, LANES * j:LANES * (j + 1)])
            m_full = jnp.broadcast_to(jnp.max(m_acc, -1, keepdims=True), (tq, LANES))

            l_acc = jnp.zeros((tq, LANES), F32)
            acc = jnp.zeros((tq, LANES), F32)
            for c in range(n_chunks):
                s = s_ref[c]
                ps = []
                for j in range(n_tiles):
                    p = jnp.exp2(s[:, LANES * j:LANES * (j + 1)] - m_full)
                    l_acc = l_acc + p
                    ps.append(p.astype(BF16))
                vh = v_ref[0, pl.ds(c * ATTN_TK, ATTN_TK), pl.ds(LANES * h, LANES)]
                acc = acc + jnp.dot(jnp.concatenate(ps, -1), vh, preferred_element_type=F32)
            o_pair = o_pair + acc / jnp.sum(l_acc, -1, keepdims=True)
        pair_out.append(o_pair)
    o = jnp.concatenate(pair_out, -1)
    o_ref[0] = o * lax.rsqrt(jnp.mean(jnp.square(o), -1, keepdims=True) + 1e-6) * g_ref[...]


def mla_attention_pallas(q, k, v, out_norm):
    B, Lq, W = q.shape
    Lk = k.shape[1]
    assert Lq % ATTN_TQ == 0 and Lk % ATTN_TK == 0
    kt = jnp.transpose(k.reshape(B, Lk // ATTN_TK, ATTN_TK, W), (0, 1, 3, 2))
    return pl.pallas_call(
        _attn_kernel,
        grid=(B, Lq // ATTN_TQ),
        in_specs=[
            pl.BlockSpec((1, ATTN_TQ, W), lambda b, i: (b, i, 0)),
            pl.BlockSpec((1, Lk // ATTN_TK, W, ATTN_TK), lambda b, i: (b, 0, 0, 0)),
            pl.BlockSpec((1, Lk, W), lambda b, i: (b, 0, 0)),
            pl.BlockSpec((1, GROUP), lambda b, i: (0, 0)),
        ],
        out_specs=pl.BlockSpec((1, ATTN_TQ, GROUP), lambda b, i: (b, i, 0)),
        out_shape=jax.ShapeDtypeStruct((B, Lq, GROUP), F32),
        scratch_shapes=[pltpu.VMEM((Lk // ATTN_TK, ATTN_TQ, ATTN_TK), F32)],
        compiler_params=pltpu.CompilerParams(dimension_semantics=("arbitrary", "arbitrary"),
                                             vmem_limit_bytes=VMEM_LIMIT_BYTES),
        name="mla_attention",
    )(q, kt, v, out_norm[None])


LRU_CHUNK = 256
LRU_HALO = SUBLANES


def _lru_kernel(xl_ref, xc_ref, wbd_ref, bias_ref, c8_ref, cw_ref, cb_ref, gn_ref,
                yl_ref, yc_ref, xs_l, xs_c, hf_l, hf_c, a_s, b_s, hb_s):
    C = GROUP
    CH = LRU_CHUNK
    L = xl_ref.shape[1]
    Lc = xc_ref.shape[1]

    def stage(x_ref, xs, n):
        xs[pl.ds(0, LRU_HALO), :] = jnp.zeros((LRU_HALO, C), F32)
        xs[pl.ds(LRU_HALO + n, LRU_HALO), :] = jnp.zeros((LRU_HALO, C), F32)

        def cp(i, c):
            r0 = pl.multiple_of(i * CH, CH)
            xs[pl.ds(LRU_HALO + r0, CH), :] = x_ref[0, pl.ds(r0, CH), pl.ds(0, C)]
            return c

        lax.fori_loop(0, n // CH, cp, 0)

    stage(xl_ref, xs_l, L)
    stage(xc_ref, xs_c, Lc)

    def coeffs(xs, base, d):
        xv = xs[pl.ds(base, CH + 2 * LRU_HALO), :]
        u = cb_ref[...]
        for j in range(LRU_CONV):
            o = LRU_HALO - LRU_CONV_LEFT + j
            u = u + xv[o:o + CH] * cw_ref[pl.ds(j, 1), :]
        z = jnp.dot(u.astype(BF16), wbd_ref[:, pl.ds(d * 2 * C, 2 * C)], preferred_element_type=F32)
        z = z + bias_ref[:, pl.ds(d * 2 * C, 2 * C)]
        r = jax.nn.sigmoid(z[:, :C])
        i = jax.nn.sigmoid(z[:, C:])
        log_a = r * c8_ref[pl.ds(d, 1), :]
        a = jnp.exp(log_a)
        a_s[...] = a
        b_s[...] = jnp.sqrt(-jnp.tanh(log_a) * (a * a + 1.0)) * (i * u)

    def row_scan(h, out_ref, out_base, reverse):
        def body(t, h):
            tt = CH - 1 - t if reverse else t
            a_t = jnp.broadcast_to(a_s[pl.ds(tt, 1), :], (SUBLANES, C))
            b_t = jnp.broadcast_to(b_s[pl.ds(tt, 1), :], (SUBLANES, C))
            h = a_t * h + b_t
            out_ref[pl.ds(out_base + tt, 1), :] = h[0:1, :]
            return h

        return lax.fori_loop(0, CH, body, h, unroll=8)

    h0 = jnp.zeros((SUBLANES, C), F32)

    h = h0
    for ci in range(Lc // CH):
        coeffs(xs_c, ci * CH, 0)
        h = row_scan(h, hf_c, ci * CH, False)

    def fwd_chunk(ci, h):
        base = pl.multiple_of(ci * CH, CH)
        coeffs(xs_l, base, 0)
        return row_scan(h, hf_l, base, False)

    lax.fori_loop(0, L // CH, fwd_chunk, h)

    def combine(x_ref, hf, base, y_ref):
        hl = hf[pl.ds(base, CH), :] + hb_s[...]
        g = jax.nn.gelu(x_ref[0, pl.ds(base, CH), pl.ds(C, C)])
        v = hl * g
        y = v * lax.rsqrt(jnp.mean(jnp.square(v), -1, keepdims=True) + 1e-6) * gn_ref[...]
        y_ref[0, pl.ds(base, CH), :] = y

    h = h0
    for ci in reversed(range(Lc // CH)):
        coeffs(xs_c, ci * CH, 1)
        h = row_scan(h, hb_s, 0, True)
        combine(xc_ref, hf_c, ci * CH, yc_ref)

    def bwd_chunk(k, h):
        base = pl.multiple_of((L // CH - 1 - k) * CH, CH)
        coeffs(xs_l, base, 1)
        h = row_scan(h, hb_s, 0, True)
        combine(xl_ref, hf_l, base, yl_ref)
        return h

    lax.fori_loop(0, L // CH, bwd_chunk, h)


def rglru_pallas(xg_l, xg_c, conv_w, conv_b, w_r, b_r, w_i, b_i, lam, out_norm):
    B, L, _ = xg_l.shape
    Lc = xg_c.shape[1]
    C = GROUP
    assert L % LRU_CHUNK == 0 and Lc % LRU_CHUNK == 0

    def bd(w):
        return jax.scipy.linalg.block_diag(*[w[n] for n in range(LRU_BLOCKS)])

    wbd = jnp.concatenate([bd(w_r[0]), bd(w_i[0]), bd(w_r[1]), bd(w_i[1])], 1).astype(BF16)
    bias = jnp.concatenate([b_r[0], b_i[0], b_r[1], b_i[1]])[None]
    c8 = -LRU_C * jax.nn.softplus(-lam)

    def full(shape):
        return pl.BlockSpec(shape, lambda b: (0,) * len(shape))

    return pl.pallas_call(
        _lru_kernel,
        grid=(B,),
        in_specs=[
            pl.BlockSpec((1, L, 2 * C), lambda b: (b, 0, 0)),
            pl.BlockSpec((1, Lc, 2 * C), lambda b: (b, 0, 0)),
            full((C, 4 * C)), full((1, 4 * C)), full((2, C)), full((LRU_CONV, C)), full((1, C)), full((1, C)),
        ],
        out_specs=[
            pl.BlockSpec((1, L, C), lambda b: (b, 0, 0)),
            pl.BlockSpec((1, Lc, C), lambda b: (b, 0, 0)),
        ],
        out_shape=[jax.ShapeDtypeStruct((B, L, C), F32), jax.ShapeDtypeStruct((B, Lc, C), F32)],
        scratch_shapes=[
            pltpu.VMEM((L + 2 * LRU_HALO, C), F32),
            pltpu.VMEM((Lc + 2 * LRU_HALO, C), F32),
            pltpu.VMEM((L, C), F32),
            pltpu.VMEM((Lc, C), F32),
            pltpu.VMEM((LRU_CHUNK, C), F32),
            pltpu.VMEM((LRU_CHUNK, C), F32),
            pltpu.VMEM((LRU_CHUNK, C), F32),
        ],
        compiler_params=pltpu.CompilerParams(dimension_semantics=("arbitrary",), vmem_limit_bytes=VMEM_LIMIT_BYTES),
        name="rglru",
    )(xg_l, xg_c, wbd, bias, c8, conv_w, conv_b[None], out_norm[None])


HY_T = 256
HY_CB = 8


def _hyena_kernel(k_ref, d_ref, v_ref, x1_ref, x2_ref, o_ref, u_s, acc_s):
    n_rows = v_ref.shape[1]
    T = HY_T
    nb = k_ref.shape[2] // (2 * T)
    bsz = n_rows // nb

    def conv(ci, order):
        acc_s[...] = jnp.zeros_like(acc_s)
        for dd in range(-(nb - 1), nb):
            w2 = k_ref[order, pl.ds(ci, 1), pl.ds(T * (dd + nb - 1), 2 * T)]
            x = jnp.broadcast_to(w2, (T, 2 * T))
            r = pltpu.roll(x, 0, 1, stride=1, stride_axis=0)
            tb = r[:, T:].astype(BF16)
            j0, j1 = max(0, -dd), min(nb, nb - dd)
            lhs = u_s[pl.ds(bsz * j0, bsz * (j1 - j0)), :]
            dst = pl.ds(bsz * (j0 + dd), bsz * (j1 - j0))
            acc_s[dst, :] = acc_s[dst, :] + jnp.dot(lhs, tb, preferred_element_type=F32)

    def channel(ci, carry):
        v = v_ref[ci]
        u_s[...] = v.astype(BF16)
        conv(ci, 0)
        u = x1_ref[ci] * (acc_s[...] + v * d_ref[pl.ds(ci, 1), pl.ds(0, 1)])
        u_s[...] = u.astype(BF16)
        conv(ci, 1)
        o_ref[ci] = x2_ref[ci] * (acc_s[...] + u * d_ref[pl.ds(ci, 1), pl.ds(1, 1)])
        return carry

    lax.fori_loop(0, HY_CB, channel, 0)


def hyena_conv_pallas(kfull, d_skip, vT, x1T, x2T):
    C, R, T = vT.shape
    two_l = kfull.shape[2]
    assert T == HY_T and C % HY_CB == 0
    blk = pl.BlockSpec((HY_CB, R, T), lambda c: (c, 0, 0))
    return pl.pallas_call(
        _hyena_kernel,
        grid=(C // HY_CB,),
        in_specs=[
            pl.BlockSpec((HY_ORDER, HY_CB, two_l), lambda c: (0, c, 0)),
            pl.BlockSpec((HY_CB, HY_ORDER), lambda c: (c, 0)),
            blk, blk, blk,
        ],
        out_specs=blk,
        out_shape=jax.ShapeDtypeStruct((C, R, T), F32),
        scratch_shapes=[pltpu.VMEM((R, T), BF16), pltpu.VMEM((R, T), F32)],
        compiler_params=pltpu.CompilerParams(dimension_semantics=("arbitrary",), vmem_limit_bytes=VMEM_LIMIT_BYTES),
        name="hyena_conv",
    )(kfull, d_skip.T, vT, x1T, x2T)


FFN_TF_MAX = 1408


def _ffn_tile(hidden):
    return max(t for t in range(LANES, FFN_TF_MAX + 1, LANES) if hidden % t == 0)


def _swiglu_kernel(be_ref, nu_ref, x_ref, wg_ref, wu_ref, wd_ref, o_ref, acc_ref):
    i = pl.program_id(0)
    f = pl.program_id(1)

    @pl.when(i < nu_ref[0])
    def _():
        x = x_ref[...]
        g = jnp.dot(x, wg_ref[0], preferred_element_type=F32)
        u = jnp.dot(x, wu_ref[0], preferred_element_type=F32)
        h = (jax.nn.silu(g) * u).astype(BF16)
        part = jnp.dot(h, wd_ref[0], preferred_element_type=F32)

        @pl.when(f == 0)
        def _():
            acc_ref[...] = part

        @pl.when(f > 0)
        def _():
            acc_ref[...] = acc_ref[...] + part

    @pl.when(f == pl.num_programs(1) - 1)
    def _():
        o_ref[...] = jnp.where(i < nu_ref[0], acc_ref[...], 0.0)


def grouped_swiglu_pallas(xs, block_expert, n_used, wg, wu, wd):
    n_rows, D = xs.shape
    F = wg.shape[2]
    TM = MOE_BLOCK
    assert n_rows % TM == 0
    tf = _ffn_tile(F)
    n_blocks = n_rows // TM
    grid_spec = pltpu.PrefetchScalarGridSpec(
        num_scalar_prefetch=2,
        grid=(n_blocks, F // tf),
        in_specs=[
            pl.BlockSpec((TM, D), lambda i, f, be, nu: (i, 0)),
            pl.BlockSpec((1, D, tf), lambda i, f, be, nu: (be[i], 0, f)),
            pl.BlockSpec((1, D, tf), lambda i, f, be, nu: (be[i], 0, f)),
            pl.BlockSpec((1, tf, D), lambda i, f, be, nu: (be[i], f, 0)),
        ],
        out_specs=pl.BlockSpec((TM, D), lambda i, f, be, nu: (i, 0)),
        scratch_shapes=[pltpu.VMEM((TM, D), F32)],
    )
    return pl.pallas_call(
        _swiglu_kernel,
        grid_spec=grid_spec,
        out_shape=jax.ShapeDtypeStruct((n_rows, D), F32),
        compiler_params=pltpu.CompilerParams(dimension_semantics=("arbitrary", "arbitrary"),
                                             vmem_limit_bytes=VMEM_LIMIT_BYTES),
        name="grouped_swiglu",
    )(block_expert.astype(jnp.int32), jnp.reshape(n_used, (1,)).astype(jnp.int32), xs, wg, wu, wd)


def modulate(x, shift, scale):
    return x * (1.0 + scale) + shift


def layer_norm(x, g, b, eps=1e-5):
    mu = jnp.mean(x, -1, keepdims=True)
    var = jnp.mean(jnp.square(x - mu), -1, keepdims=True)
    return (x - mu) * lax.rsqrt(var + eps) * g + b


def rms_norm(x, g, eps=1e-6):
    return x * lax.rsqrt(jnp.mean(jnp.square(x), -1, keepdims=True) + eps) * g


def dwconv(x, w, b, left):
    K = w.shape[0]
    L = x.shape[1]
    xp = jnp.pad(x, ((0, 0), (left, K - 1 - left), (0, 0)))
    return sum(xp[:, j:j + L] * w[j] for j in range(K)) + b


def token_shift(z, mu_prev, mu_next):
    zp = jnp.pad(z, ((0, 0), (1, 1), (0, 0)))
    return z + mu_prev * (zp[:, :-2] - z) + mu_next * (zp[:, 2:] - z)


def axial_rope(rows):
    r, col = jnp.meshgrid(jnp.arange(rows, dtype=F32), jnp.arange(GRID_W, dtype=F32), indexing='ij')
    half = MLA_ROPE // 2
    inv = 1.0 / (ROPE_BASE ** (jnp.arange(0, half, 2, dtype=F32) / half))
    ang = jnp.concatenate([r.reshape(-1, 1) * inv, col.reshape(-1, 1) * inv], -1)
    return jnp.cos(ang), jnp.sin(ang)


def apply_rope(x, cos, sin):
    h = x.shape[-1] // 2
    x1, x2 = x[..., :h], x[..., h:]
    return jnp.concatenate([x1 * cos - x2 * sin, x1 * sin + x2 * cos], -1)


def mla_padded_qkv(Pa, q_norm, kv_norm, w_uq, w_ukv, cos, sin, want_q=True):
    B, L, _ = Pa.shape
    H, DN, DR, DV = MLA_HEADS, MLA_NOPE, MLA_ROPE, MLA_V
    ckv = rms_norm(Pa[..., A_CKV:A_CKV + MLA_KV_RANK], kv_norm)
    kv = (ckv @ w_ukv).reshape(B, L, H, DN + DV)
    k_rope = Pa[..., A_KR:A_KR + DR]
    if cos is not None:
        k_rope = apply_rope(k_rope, cos[None], sin[None])
    k_rope = jnp.broadcast_to(k_rope[:, :, None, :], (B, L, H, DR))
    zk = jnp.zeros((B, L, H, LANES - DN - DR), F32)
    k = jnp.concatenate([kv[..., :DN], k_rope, zk], -1).reshape(B, L, H * LANES).astype(BF16)
    vv = kv[..., DN:]
    zv = jnp.zeros_like(vv)
    v = jnp.stack([jnp.concatenate([vv[:, :, h], zv[:, :, h]] if h % 2 == 0 else [zv[:, :, h], vv[:, :, h]], -1)
                   for h in range(H)], 2).reshape(B, L, H * LANES).astype(BF16)
    q = None
    if want_q:
        cq = rms_norm(Pa[..., A_CQ:A_CQ + MLA_Q_RANK], q_norm)
        qq = (cq @ w_uq).reshape(B, L, H, DN + DR)
        q_rope = qq[..., DN:]
        if cos is not None:
            q_rope = apply_rope(q_rope, cos[None, :, None, :], sin[None, :, None, :])
        scale = (DN + DR) ** -0.5 * math.log2(math.e)
        q = jnp.concatenate([qq[..., :DN], q_rope, zk], -1) * scale
        q = q.reshape(B, L, H * LANES).astype(BF16)
    return q, k, v


def mla_mixer(Pal, Pac, q_norm, kv_norm, w_uq, w_ukv, out_norm, cos, sin, ctx_out):
    ql, kl, vl = mla_padded_qkv(Pal, q_norm, kv_norm, w_uq, w_ukv, cos, sin)
    qc, kc, vc = mla_padded_qkv(Pac, q_norm, kv_norm, w_uq, w_ukv, None, None, want_q=ctx_out)
    yl = mla_attention_pallas(ql, jnp.concatenate([kc, kl], 1), jnp.concatenate([vc, vl], 1), out_norm)
    yc = mla_attention_pallas(qc, kc, vc, out_norm) if ctx_out else None
    return yl, yc


def rwkv_prepare(Pc_sec, mu_prev, mu_next, w0, w_up, a0, a_up, k_k, k_a):
    z = token_shift(Pc_sec, mu_prev, mu_next)
    B, L, _ = z.shape

    def heads(t):
        return t.reshape(B, L, RWKV_HEADS, RWKV_HEAD)

    r = heads(z[..., C_R:C_R + GROUP])
    k = heads(z[..., C_K:C_K + GROUP])
    v = heads(z[..., C_V:C_V + GROUP])
    kk = k * k_k.reshape(RWKV_HEADS, RWKV_HEAD)
    kk = kk * lax.rsqrt(jnp.maximum(jnp.sum(jnp.square(kk), -1, keepdims=True), 1e-24))
    dirs = []
    for d in range(2):
        wd = z[..., C_WD + d * RWKV_DECAY_LORA:C_WD + (d + 1) * RWKV_DECAY_LORA]
        ad = z[..., C_AD + d * RWKV_AAA_LORA:C_AD + (d + 1) * RWKV_AAA_LORA]
        log_w = -jnp.exp(-jax.nn.softplus(-(w0[d] + jnp.tanh(wd) @ w_up[d])) - 0.5)
        a = heads(jax.nn.sigmoid(a0[d] + ad @ a_up[d]))
        k_d = k * (1.0 + (a - 1.0) * k_a.reshape(RWKV_HEADS, RWKV_HEAD))
        dirs.append((heads(jnp.exp(log_w)), kk * a, k_d))
    gd = z[..., C_GD:C_GD + RWKV_GATE_LORA]
    return r, v, kk, dirs, gd


def rwkv_finish(y, r, v, dirs, gd, g_up, r_k, ln_g, ln_b):
    B, L = y.shape[:2]
    mu = jnp.mean(y, -1, keepdims=True)
    var = jnp.mean(jnp.square(y - mu), -1, keepdims=True)
    yn = (y - mu) * lax.rsqrt(var + RWKV_GN_EPS) * ln_g.reshape(RWKV_HEADS, RWKV_HEAD) + ln_b.reshape(RWKV_HEADS, RWKV_HEAD)
    bonus = sum(jnp.sum(r * kd * r_k, -1, keepdims=True) for (_, _, kd) in dirs) * v
    g = jax.nn.sigmoid(gd) @ g_up
    return (yn + bonus).reshape(B, L, GROUP) * g


def _scan_order(tc, tl):
    def one(c, l, rev):
        if rev:
            c, l = jnp.flip(c, 1), jnp.flip(l, 1)
        return jnp.concatenate([c, l], 1)

    both = jnp.stack([one(tc[0], tl[0], False), one(tc[1], tl[1], True)], 0)
    d, b, s, h, n = both.shape
    return jnp.transpose(both, (2, 4, 0, 1, 3)).reshape(s, n, d * b * h)


def rwkv_mixer(Pl, Pc, mu_prev, mu_next, w0, w_up, a0, a_up, g_up, k_k, k_a, r_k, ln_g, ln_b, ctx_out):
    rl, vl, kkl, dl, gdl = rwkv_prepare(Pl, mu_prev, mu_next, w0, w_up, a0, a_up, k_k, k_a)
    rc, vc, kkc, dc, gdc = rwkv_prepare(Pc, mu_prev, mu_next, w0, w_up, a0, a_up, k_k, k_a)
    B, L = Pl.shape[:2]
    Lc = Pc.shape[1]
    y = rwkv_scan_pallas(
        _scan_order((rc, rc), (rl, rl)),
        _scan_order((dc[0][0], dc[1][0]), (dl[0][0], dl[1][0])),
        _scan_order((kkc, kkc), (kkl, kkl)),
        _scan_order((dc[0][1], dc[1][1]), (dl[0][1], dl[1][1])),
        _scan_order((vc, vc), (vl, vl)),
        _scan_order((dc[0][2], dc[1][2]), (dl[0][2], dl[1][2])),
    )
    y = jnp.transpose(y.reshape(Lc + L, RWKV_HEAD, 2, B, RWKV_HEADS), (2, 3, 0, 4, 1))
    yl = y[0, :, Lc:] + jnp.flip(y[1, :, Lc:], 1)
    out_l = rwkv_finish(yl, rl, vl, dl, gdl, g_up, r_k, ln_g, ln_b)
    out_c = None
    if ctx_out:
        yc = y[0, :, :Lc] + jnp.flip(y[1, :, :Lc], 1)
        out_c = rwkv_finish(yc, rc, vc, dc, gdc, g_up, r_k, ln_g, ln_b)
    return out_l, out_c


def hyena_filters(L, w1, b1, w2, b2, w3):
    t01 = jnp.linspace(0.0, 1.0, L, dtype=F32)[:, None]
    bands = jnp.linspace(1e-4, HY_BANDS - 1, HY_BANDS, dtype=F32)[None, :]
    wpos = (2.0 * math.pi / L) * jnp.arange(L, dtype=F32)[:, None]
    z = jnp.concatenate([t01, jnp.cos(bands * wpos), -jnp.sin(bands * wpos)], -1)
    h = jnp.sin(HY_SIN_FREQ * (z @ w1 + b1))
    h = jnp.sin(HY_SIN_FREQ * (h @ w2 + b2))
    h = (h @ w3).reshape(L, HY_ORDER, 2, GROUP)
    deltas = jnp.abs(jnp.linspace(HY_DECAY_MIN, HY_DECAY_MAX, GROUP, dtype=F32))
    window = jnp.exp(-t01 * deltas) + HY_SHIFT
    return h * window[:, None, None, :]


def hyena_sequence(Pd, conv_w, conv_b, w1, b1, w2, b2, w3, d_skip):
    B, L, _ = Pd.shape
    C = GROUP
    T = HY_T
    nb = L // T
    z = dwconv(Pd, conv_w, conv_b, 1)
    zT = jnp.transpose(z.reshape(B, nb, T, 3 * C), (3, 1, 0, 2)).reshape(3 * C, nb * B, T)
    h = hyena_filters(L, w1, b1, w2, b2, w3)
    hf = jnp.transpose(h[:, :, 0], (1, 2, 0))
    hb = jnp.transpose(h[:, :, 1], (1, 2, 0))
    kfull = jnp.concatenate([jnp.zeros((HY_ORDER, C, 1), F32), jnp.flip(hb[..., 1:], -1), hf], -1)
    oT = hyena_conv_pallas(kfull, d_skip, zT[:C], zT[C:2 * C], zT[2 * C:])
    return jnp.transpose(oT.reshape(C, nb, B, T), (2, 1, 3, 0)).reshape(B, L, C)


def swiglu(x, wg, wu, wd):
    n_blocks = x.shape[0] // MOE_BLOCK
    return grouped_swiglu_pallas(x.astype(BF16), jnp.zeros((n_blocks,), jnp.int32), jnp.int32(n_blocks),
                                 wg[None].astype(BF16), wu[None].astype(BF16), wd[None].astype(BF16))


def moe_swiglu(x, router, wg, wu, wd):
    N, D = x.shape
    logits = x @ router
    top_v, top_i = lax.top_k(logits, TOP_K)
    gates = jax.nn.softmax(top_v, axis=-1)
    A = N * TOP_K
    e_flat = top_i.reshape(-1)
    tok_flat = jnp.arange(A, dtype=jnp.int32) // TOP_K
    order = jnp.argsort(e_flat)
    e_sorted = e_flat[order]
    counts = jnp.bincount(e_flat, length=N_EXPERTS)
    starts = jnp.cumsum(counts) - counts
    padded = (counts + MOE_BLOCK - 1) // MOE_BLOCK * MOE_BLOCK
    pends = jnp.cumsum(padded)
    pstarts = pends - padded
    dest = (pstarts[e_sorted] + jnp.arange(A, dtype=jnp.int32) - starts[e_sorted]).astype(jnp.int32)
    n_blocks = -(-A // MOE_BLOCK) + N_EXPERTS
    n_slots = n_blocks * MOE_BLOCK
    slot_tok = jnp.full((n_slots,), N, jnp.int32).at[dest].set(tok_flat[order])
    block_expert = jnp.clip(jnp.searchsorted(pends, jnp.arange(n_blocks) * MOE_BLOCK, side='right'), 0, N_EXPERTS - 1)
    xp = jnp.concatenate([x.astype(BF16), jnp.zeros((1, D), BF16)], 0)
    ys = grouped_swiglu_pallas(xp[slot_tok], block_expert, pends[-1] // MOE_BLOCK,
                               wg.astype(BF16), wu.astype(BF16), wd.astype(BF16))
    slot_of = jnp.zeros((A,), jnp.int32).at[order].set(dest).reshape(N, TOP_K)
    return ys[slot_of[:, 0]] * gates[:, 0:1] + ys[slot_of[:, 1]] * gates[:, 1:2]


def kernel(x, c, ctx, c_ctx, ada_w, ada_b, w_in, mla_q_norm, mla_kv_norm, mla_w_uq, mla_w_ukv, mla_out_norm, lru_conv_w, lru_conv_b, lru_w_r, lru_b_r, lru_w_i, lru_b_i, lru_lambda, lru_out_norm, rwkv_mu_prev, rwkv_mu_next, rwkv_w0, rwkv_w_up, rwkv_a0, rwkv_a_up, rwkv_g_up, rwkv_k_k, rwkv_k_a, rwkv_r_k, rwkv_ln_g, rwkv_ln_b, hy_conv_w, hy_conv_b, hy_f_w1, hy_f_b1, hy_f_w2, hy_f_b2, hy_f_w3, hy_d, hy_out_norm, w_out, ln1_g, ln1_b, ln2_g, ln2_b, ffn_w_gate, ffn_w_up, ffn_w_down, moe_router, moe_w_gate, moe_w_up, moe_w_down):
    B, L, D = x.shape
    Lc = ctx.shape[1]
    rows = L // GRID_W
    cos, sin = axial_rope(rows)
    s_lat = jax.nn.silu(c)
    s_ctx = jax.nn.silu(c_ctx)
    xl, xc = x, ctx
    for li in range(DEPTH):
        ctx_out = li < DEPTH - 1
        mod_l = (s_lat @ ada_w[li] + ada_b[li]).reshape(B, 6, 1, D)
        mod_c = (s_ctx @ ada_w[li] + ada_b[li]).reshape(6, 1, D)

        hl = modulate(xl, mod_l[:, 0], mod_l[:, 1])
        hc = modulate(xc, mod_c[0], mod_c[1])
        w_in_l = w_in[li]
        w_secs = (w_in_l[:, :B_X], w_in_l[:, B_X:C_OFF], w_in_l[:, C_OFF:D_OFF], w_in_l[:, D_OFF:])
        Pal, Pbl, Pcl, Pdl = (hl @ w for w in w_secs)
        Pac, Pbc, Pcc = (hc @ w for w in w_secs[:3])
        a_l, a_c = mla_mixer(Pal, Pac, mla_q_norm[li], mla_kv_norm[li], mla_w_uq[li], mla_w_ukv[li],
                             mla_out_norm[li], cos, sin, ctx_out)
        b_l, b_c = rglru_pallas(Pbl, Pbc, lru_conv_w[li], lru_conv_b[li], lru_w_r[li], lru_b_r[li], lru_w_i[li],
                                lru_b_i[li], lru_lambda[li], lru_out_norm[li])
        c_l, c_c = rwkv_mixer(Pcl, Pcc, rwkv_mu_prev[li], rwkv_mu_next[li], rwkv_w0[li], rwkv_w_up[li], rwkv_a0[li],
                              rwkv_a_up[li], rwkv_g_up[li], rwkv_k_k[li], rwkv_k_a[li], rwkv_r_k[li],
                              rwkv_ln_g[li], rwkv_ln_b[li], ctx_out)
        d_l = rms_norm(hyena_sequence(Pdl, hy_conv_w[li], hy_conv_b[li], hy_f_w1[li], hy_f_b1[li], hy_f_w2[li],
                                      hy_f_b2[li], hy_f_w3[li], hy_d[li]), hy_out_norm[li])
        yl = jnp.concatenate([a_l, b_l, c_l, d_l], -1) @ w_out[li]
        xl = layer_norm(ALPHA * xl + mod_l[:, 2] * yl, ln1_g[li], ln1_b[li])
        if ctx_out:
            Pdc = hc @ w_secs[3]
            d_c = rms_norm(hyena_sequence(Pdc, hy_conv_w[li], hy_conv_b[li], hy_f_w1[li], hy_f_b1[li], hy_f_w2[li],
                                          hy_f_b2[li], hy_f_w3[li], hy_d[li]), hy_out_norm[li])
            yc = jnp.concatenate([a_c, b_c, c_c, d_c], -1) @ w_out[li]
            xc = layer_norm(ALPHA * xc + mod_c[2] * yc, ln1_g[li], ln1_b[li])

        fl = modulate(xl, mod_l[:, 3], mod_l[:, 4]).reshape(B * L, D)
        if ctx_out:
            fc = modulate(xc, mod_c[3], mod_c[4]).reshape(B * Lc, D)
            tokens = jnp.concatenate([fl, fc], 0)
        else:
            tokens = fl
        j = li // 2
        if li % 2 == 0:
            out = swiglu(tokens, ffn_w_gate[j], ffn_w_up[j], ffn_w_down[j])
        else:
            out = moe_swiglu(tokens, moe_router[j], moe_w_gate[j], moe_w_up[j], moe_w_down[j])
        xl = layer_norm(ALPHA * xl + mod_l[:, 5] * out[:B * L].reshape(B, L, D), ln2_g[li], ln2_b[li])
        if ctx_out:
            xc = layer_norm(ALPHA * xc + mod_c[5] * out[B * L:].reshape(B, Lc, D), ln2_g[li], ln2_b[li])
    return xl
```

```python
import math
from functools import partial

import jax
import jax.numpy as jnp
from jax import lax
from jax.experimental import pallas as pl
from jax.experimental.pallas import tpu as pltpu

F32 = jnp.float32
BF16 = jnp.bfloat16

SUBLANES = 8
LANES = 128
VMEM_LIMIT_BYTES = 48 * 1024 * 1024

D_MODEL = 1024
DEPTH = 4
GRID_W = 64
GROUP = D_MODEL // 4

MLA_HEADS = 4
MLA_NOPE = 64
MLA_ROPE = 32
MLA_V = 64
MLA_Q_RANK = 192
MLA_KV_RANK = 128
ROPE_BASE = 10000.0
Q_BLOCK = 128

LRU_BLOCKS = 4
LRU_CONV = 4
LRU_CONV_LEFT = 2
LRU_C = 8.0

RWKV_HEADS = 4
RWKV_HEAD = GROUP // RWKV_HEADS
RWKV_DECAY_LORA = 32
RWKV_AAA_LORA = 32
RWKV_GATE_LORA = 64
RWKV_GN_EPS = 64e-5

HY_ORDER = 2
HY_SHORT = 3
HY_BANDS = 16
HY_EMB = 1 + 2 * HY_BANDS
HY_HIDDEN = 64
HY_SIN_FREQ = 1.0
HY_DECAY_MIN = math.log(1e-2) / 1.5
HY_DECAY_MAX = math.log(1e-2) / 0.3
HY_SHIFT = 0.05

N_EXPERTS = 8
TOP_K = 2
MOE_BLOCK = 512

ALPHA = (2.0 * DEPTH) ** 0.25

A_CQ = 0
A_CKV = A_CQ + MLA_Q_RANK
A_KR = A_CKV + MLA_KV_RANK
B_X = A_KR + MLA_ROPE
B_GATE = B_X + GROUP
C_OFF = B_GATE + GROUP
C_R = 0
C_K = GROUP
C_V = 2 * GROUP
C_WD = 3 * GROUP
C_AD = C_WD + 2 * RWKV_DECAY_LORA
C_GD = C_AD + 2 * RWKV_AAA_LORA
C_COLS = C_GD + RWKV_GATE_LORA
D_OFF = C_OFF + C_COLS
D_COLS = (HY_ORDER + 1) * GROUP


RWKV_TIME_BLOCK = 16
V_TILES = RWKV_HEAD // SUBLANES
RWKV_KEY_GROUP = 16


def _rwkv_scan_kernel(r_ref, w_ref, kk_ref, kka_ref, v_ref, kd_ref, y_ref, s_ref):
    @pl.when(pl.program_id(0) == 0)
    def _():
        s_ref[...] = jnp.zeros_like(s_ref)

    n_t = r_ref.shape[0]
    p = r_ref.shape[2]

    def step(t, carry):
        def row(ref, k):
            return jnp.broadcast_to(ref[t, pl.ds(k, 1), :], (SUBLANES, p))[None]

        def sa_group(g, sa):
            k0 = pl.multiple_of(g * RWKV_KEY_GROUP, RWKV_KEY_GROUP)
            for j in range(RWKV_KEY_GROUP):
                sa = sa + s_ref[k0 + j] * row(kk_ref, k0 + j)
            return sa

        zero = jnp.zeros((V_TILES, SUBLANES, p), F32)
        sa = lax.fori_loop(0, RWKV_HEAD // RWKV_KEY_GROUP, sa_group, zero)
        vt = v_ref[t].reshape(V_TILES, SUBLANES, p)

        def update_group(g, y):
            k0 = pl.multiple_of(g * RWKV_KEY_GROUP, RWKV_KEY_GROUP)
            for j in range(RWKV_KEY_GROUP):
                k = k0 + j
                sn = s_ref[k] * row(w_ref, k) - sa * row(kka_ref, k) + vt * row(kd_ref, k)
                s_ref[k] = sn
                y = y + sn * row(r_ref, k)
            return y

        y = lax.fori_loop(0, RWKV_HEAD // RWKV_KEY_GROUP, update_group, zero)
        y_ref[t] = y.reshape(RWKV_HEAD, p)
        return carry

    lax.fori_loop(0, n_t, step, 0)


def rwkv_scan_pallas(r, w, kk, kka, v, kd):
    n_steps, n, p = r.shape
    tb = RWKV_TIME_BLOCK
    assert n == RWKV_HEAD and p == LANES and n_steps % tb == 0
    spec = pl.BlockSpec((tb, n, p), lambda i: (i, 0, 0))
    return pl.pallas_call(
        _rwkv_scan_kernel,
        grid=(n_steps // tb,),
        in_specs=[spec] * 6,
        out_specs=spec,
        out_shape=jax.ShapeDtypeStruct((n_steps, n, p), F32),
        scratch_shapes=[pltpu.VMEM((n, V_TILES, SUBLANES, p), F32)],
        compiler_params=pltpu.CompilerParams(dimension_semantics=("arbitrary",)),
    )(r, w, kk, kka, v, kd)


ATTN_TQ = 256
ATTN_TK = 256


def _attn_kernel(q_ref, k_ref, v_ref, g_ref, o_ref, s_ref):
    tq = q_ref.shape[1]
    n_chunks = k_ref.shape[1]
    n_tiles = ATTN_TK // LANES
    pair_out = []
    for hp in range(MLA_HEADS // 2):
        o_pair = jnp.zeros((tq, LANES), F32)
        for h in (2 * hp, 2 * hp + 1):
            qh = q_ref[0, :, pl.ds(LANES * h, LANES)]

            m_acc = jnp.full((tq, LANES), -jnp.inf, F32)
            for c in range(n_chunks):
                s = jnp.dot(qh, k_ref[0, c, pl.ds(LANES * h, LANES), :], preferred_element_type=F32)
                s_ref[c] = s
                for j in range(n_tiles):
                    m_acc = jnp.maximum(m_acc, s[:, LANES * j:LANES * (j + 1)])
            m_full = jnp.broadcast_to(jnp.max(m_acc, -1, keepdims=True), (tq, LANES))

            l_acc = jnp.zeros((tq, LANES), F32)
            acc = jnp.zeros((tq, LANES), F32)
            for c in range(n_chunks):
                s = s_ref[c]
                ps = []
                for j in range(n_tiles):
                    p = jnp.exp2(s[:, LANES * j:LANES * (j + 1)] - m_full)
                    l_acc = l_acc + p
                    ps.append(p.astype(BF16))
                vh = v_ref[0, pl.ds(c * ATTN_TK, ATTN_TK), pl.ds(LANES * h, LANES)]
                acc = acc + jnp.dot(jnp.concatenate(ps, -1), vh, preferred_element_type=F32)
            o_pair = o_pair + acc / jnp.sum(l_acc, -1, keepdims=True)
        pair_out.append(o_pair)
    o = jnp.concatenate(pair_out, -1)
    o_ref[0] = o * lax.rsqrt(jnp.mean(jnp.square(o), -1, keepdims=True) + 1e-6) * g_ref[...]


def mla_attention_pallas(q, k, v, out_norm):
    B, Lq, W = q.shape
    Lk = k.shape[1]
    assert Lq % ATTN_TQ == 0 and Lk % ATTN_TK == 0
    kt = jnp.transpose(k.reshape(B, Lk // ATTN_TK, ATTN_TK, W), (0, 1, 3, 2))
    return pl.pallas_call(
        _attn_kernel,
        grid=(B, Lq // ATTN_TQ),
        in_specs=[
            pl.BlockSpec((1, ATTN_TQ, W), lambda b, i: (b, i, 0)),
            pl.BlockSpec((1, Lk // ATTN_TK, W, ATTN_TK), lambda b, i: (b, 0, 0, 0)),
            pl.BlockSpec((1, Lk, W), lambda b, i: (b, 0, 0)),
            pl.BlockSpec((1, GROUP), lambda b, i: (0, 0)),
        ],
        out_specs=pl.BlockSpec((1, ATTN_TQ, GROUP), lambda b, i: (b, i, 0)),
        out_shape=jax.ShapeDtypeStruct((B, Lq, GROUP), F32),
        scratch_shapes=[pltpu.VMEM((Lk // ATTN_TK, ATTN_TQ, ATTN_TK), F32)],
        compiler_params=pltpu.CompilerParams(dimension_semantics=("arbitrary", "arbitrary"),
                                             vmem_limit_bytes=VMEM_LIMIT_BYTES),
        name="mla_attention",
    )(q, kt, v, out_norm[None])


LRU_CHUNK = 256
LRU_HALO = SUBLANES


def _lru_kernel(xl_ref, xc_ref, wbd_ref, bias_ref, c8_ref, cw_ref, cb_ref, gn_ref,
                yl_ref, yc_ref, xs_l, xs_c, hf_l, hf_c, a_s, b_s, hb_s):
    C = GROUP
    CH = LRU_CHUNK
    L = xl_ref.shape[1]
    Lc = xc_ref.shape[1]

    def stage(x_ref, xs, n):
        xs[pl.ds(0, LRU_HALO), :] = jnp.zeros((LRU_HALO, C), F32)
        xs[pl.ds(LRU_HALO + n, LRU_HALO), :] = jnp.zeros((LRU_HALO, C), F32)

        def cp(i, c):
            r0 = pl.multiple_of(i * CH, CH)
            xs[pl.ds(LRU_HALO + r0, CH), :] = x_ref[0, pl.ds(r0, CH), pl.ds(0, C)]
            return c

        lax.fori_loop(0, n // CH, cp, 0)

    stage(xl_ref, xs_l, L)
    stage(xc_ref, xs_c, Lc)

    def coeffs(xs, base, d):
        xv = xs[pl.ds(base, CH + 2 * LRU_HALO), :]
        u = cb_ref[...]
        for j in range(LRU_CONV):
            o = LRU_HALO - LRU_CONV_LEFT + j
            u = u + xv[o:o + CH] * cw_ref[pl.ds(j, 1), :]
        z = jnp.dot(u.astype(BF16), wbd_ref[:, pl.ds(d * 2 * C, 2 * C)], preferred_element_type=F32)
        z = z + bias_ref[:, pl.ds(d * 2 * C, 2 * C)]
        r = jax.nn.sigmoid(z[:, :C])
        i = jax.nn.sigmoid(z[:, C:])
        log_a = r * c8_ref[pl.ds(d, 1), :]
        a = jnp.exp(log_a)
        a_s[...] = a
        b_s[...] = jnp.sqrt(-jnp.tanh(log_a) * (a * a + 1.0)) * (i * u)

    def row_scan(h, out_ref, out_base, reverse):
        def body(t, h):
            tt = CH - 1 - t if reverse else t
            a_t = jnp.broadcast_to(a_s[pl.ds(tt, 1), :], (SUBLANES, C))
            b_t = jnp.broadcast_to(b_s[pl.ds(tt, 1), :], (SUBLANES, C))
            h = a_t * h + b_t
            out_ref[pl.ds(out_base + tt, 1), :] = h[0:1, :]
            return h

        return lax.fori_loop(0, CH, body, h, unroll=8)

    h0 = jnp.zeros((SUBLANES, C), F32)

    h = h0
    for ci in range(Lc // CH):
        coeffs(xs_c, ci * CH, 0)
        h = row_scan(h, hf_c, ci * CH, False)

    def fwd_chunk(ci, h):
        base = pl.multiple_of(ci * CH, CH)
        coeffs(xs_l, base, 0)
        return row_scan(h, hf_l, base, False)

    lax.fori_loop(0, L // CH, fwd_chunk, h)

    def combine(x_ref, hf, base, y_ref):
        hl = hf[pl.ds(base, CH), :] + hb_s[...]
        g = jax.nn.gelu(x_ref[0, pl.ds(base, CH), pl.ds(C, C)])
        v = hl * g
        y = v * lax.rsqrt(jnp.mean(jnp.square(v), -1, keepdims=True) + 1e-6) * gn_ref[...]
        y_ref[0, pl.ds(base, CH), :] = y

    h = h0
    for ci in reversed(range(Lc // CH)):
        coeffs(xs_c, ci * CH, 1)
        h = row_scan(h, hb_s, 0, True)
        combine(xc_ref, hf_c, ci * CH, yc_ref)

    def bwd_chunk(k, h):
        base = pl.multiple_of((L // CH - 1 - k) * CH, CH)
        coeffs(xs_l, base, 1)
        h = row_scan(h, hb_s, 0, True)
        combine(xl_ref, hf_l, base, yl_ref)
        return h

    lax.fori_loop(0, L // CH, bwd_chunk, h)


def rglru_pallas(xg_l, xg_c, conv_w, conv_b, w_r, b_r, w_i, b_i, lam, out_norm):
    B, L, _ = xg_l.shape
    Lc = xg_c.shape[1]
    C = GROUP
    assert L % LRU_CHUNK == 0 and Lc % LRU_CHUNK == 0

    def bd(w):
        return jax.scipy.linalg.block_diag(*[w[n] for n in range(LRU_BLOCKS)])

    wbd = jnp.concatenate([bd(w_r[0]), bd(w_i[0]), bd(w_r[1]), bd(w_i[1])], 1).astype(BF16)
    bias = jnp.concatenate([b_r[0], b_i[0], b_r[1], b_i[1]])[None]
    c8 = -LRU_C * jax.nn.softplus(-lam)

    def full(shape):
        return pl.BlockSpec(shape, lambda b: (0,) * len(shape))

    return pl.pallas_call(
        _lru_kernel,
        grid=(B,),
        in_specs=[
            pl.BlockSpec((1, L, 2 * C), lambda b: (b, 0, 0)),
            pl.BlockSpec((1, Lc, 2 * C), lambda b: (b, 0, 0)),
            full((C, 4 * C)), full((1, 4 * C)), full((2, C)), full((LRU_CONV, C)), full((1, C)), full((1, C)),
        ],
        out_specs=[
            pl.BlockSpec((1, L, C), lambda b: (b, 0, 0)),
            pl.BlockSpec((1, Lc, C), lambda b: (b, 0, 0)),
        ],
        out_shape=[jax.ShapeDtypeStruct((B, L, C), F32), jax.ShapeDtypeStruct((B, Lc, C), F32)],
        scratch_shapes=[
            pltpu.VMEM((L + 2 * LRU_HALO, C), F32),
            pltpu.VMEM((Lc + 2 * LRU_HALO, C), F32),
            pltpu.VMEM((L, C), F32),
            pltpu.VMEM((Lc, C), F32),
            pltpu.VMEM((LRU_CHUNK, C), F32),
            pltpu.VMEM((LRU_CHUNK, C), F32),
            pltpu.VMEM((LRU_CHUNK, C), F32),
        ],
        compiler_params=pltpu.CompilerParams(dimension_semantics=("arbitrary",), vmem_limit_bytes=VMEM_LIMIT_BYTES),
        name="rglru",
    )(xg_l, xg_c, wbd, bias, c8, conv_w, conv_b[None], out_norm[None])


HY_T = 256
HY_CB = 8


def _hyena_kernel(k_ref, d_ref, v_ref, x1_ref, x2_ref, o_ref, u_s, acc_s):
    n_rows = v_ref.shape[1]
    T = HY_T
    nb = k_ref.shape[2] // (2 * T)
    bsz = n_rows // nb

    def conv(ci, order):
        acc_s[...] = jnp.zeros_like(acc_s)
        for dd in range(-(nb - 1), nb):
            w2 = k_ref[order, pl.ds(ci, 1), pl.ds(T * (dd + nb - 1), 2 * T)]
            x = jnp.broadcast_to(w2, (T, 2 * T))
            r = pltpu.roll(x, 0, 1, stride=1, stride_axis=0)
            tb = r[:, T:].astype(BF16)
            j0, j1 = max(0, -dd), min(nb, nb - dd)
            lhs = u_s[pl.ds(bsz * j0, bsz * (j1 - j0)), :]
            dst = pl.ds(bsz * (j0 + dd), bsz * (j1 - j0))
            acc_s[dst, :] = acc_s[dst, :] + jnp.dot(lhs, tb, preferred_element_type=F32)

    def channel(ci, carry):
        v = v_ref[ci]
        u_s[...] = v.astype(BF16)
        conv(ci, 0)
        u = x1_ref[ci] * (acc_s[...] + v * d_ref[pl.ds(ci, 1), pl.ds(0, 1)])
        u_s[...] = u.astype(BF16)
        conv(ci, 1)
        o_ref[ci] = x2_ref[ci] * (acc_s[...] + u * d_ref[pl.ds(ci, 1), pl.ds(1, 1)])
        return carry

    lax.fori_loop(0, HY_CB, channel, 0)


def hyena_conv_pallas(kfull, d_skip, vT, x1T, x2T):
    C, R, T = vT.shape
    two_l = kfull.shape[2]
    assert T == HY_T and C % HY_CB == 0
    blk = pl.BlockSpec((HY_CB, R, T), lambda c: (c, 0, 0))
    return pl.pallas_call(
        _hyena_kernel,
        grid=(C // HY_CB,),
        in_specs=[
            pl.BlockSpec((HY_ORDER, HY_CB, two_l), lambda c: (0, c, 0)),
            pl.BlockSpec((HY_CB, HY_ORDER), lambda c: (c, 0)),
            blk, blk, blk,
        ],
        out_specs=blk,
        out_shape=jax.ShapeDtypeStruct((C, R, T), F32),
        scratch_shapes=[pltpu.VMEM((R, T), BF16), pltpu.VMEM((R, T), F32)],
        compiler_params=pltpu.CompilerParams(dimension_semantics=("arbitrary",), vmem_limit_bytes=VMEM_LIMIT_BYTES),
        name="hyena_conv",
    )(kfull, d_skip.T, vT, x1T, x2T)


FFN_TF_MAX = 1408


def _ffn_tile(hidden):
    return max(t for t in range(LANES, FFN_TF_MAX + 1, LANES) if hidden % t == 0)


def _swiglu_kernel(be_ref, nu_ref, x_ref, wg_ref, wu_ref, wd_ref, o_ref, acc_ref):
    i = pl.program_id(0)
    f = pl.program_id(1)

    @pl.when(i < nu_ref[0])
    def _():
        x = x_ref[...]
        g = jnp.dot(x, wg_ref[0], preferred_element_type=F32)
        u = jnp.dot(x, wu_ref[0], preferred_element_type=F32)
        h = (jax.nn.silu(g) * u).astype(BF16)
        part = jnp.dot(h, wd_ref[0], preferred_element_type=F32)

        @pl.when(f == 0)
        def _():
            acc_ref[...] = part

        @pl.when(f > 0)
        def _():
            acc_ref[...] = acc_ref[...] + part

    @pl.when(f == pl.num_programs(1) - 1)
    def _():
        o_ref[...] = jnp.where(i < nu_ref[0], acc_ref[...], 0.0)


def grouped_swiglu_pallas(xs, block_expert, n_used, wg, wu, wd):
    n_rows, D = xs.shape
    F = wg.shape[2]
    TM = MOE_BLOCK
    assert n_rows % TM == 0
    tf = _ffn_tile(F)
    n_blocks = n_rows // TM
    grid_spec = pltpu.PrefetchScalarGridSpec(
        num_scalar_prefetch=2,
        grid=(n_blocks, F // tf),
        in_specs=[
            pl.BlockSpec((TM, D), lambda i, f, be, nu: (i, 0)),
            pl.BlockSpec((1, D, tf), lambda i, f, be, nu: (be[i], 0, f)),
            pl.BlockSpec((1, D, tf), lambda i, f, be, nu: (be[i], 0, f)),
            pl.BlockSpec((1, tf, D), lambda i, f, be, nu: (be[i], f, 0)),
        ],
        out_specs=pl.BlockSpec((TM, D), lambda i, f, be, nu: (i, 0)),
        scratch_shapes=[pltpu.VMEM((TM, D), F32)],
    )
    return pl.pallas_call(
        _swiglu_kernel,
        grid_spec=grid_spec,
        out_shape=jax.ShapeDtypeStruct((n_rows, D), F32),
        compiler_params=pltpu.CompilerParams(dimension_semantics=("arbitrary", "arbitrary"),
                                             vmem_limit_bytes=VMEM_LIMIT_BYTES),
        name="grouped_swiglu",
    )(block_expert.astype(jnp.int32), jnp.reshape(n_used, (1,)).astype(jnp.int32), xs, wg, wu, wd)


def modulate(x, shift, scale):
    return x * (1.0 + scale) + shift


def layer_norm(x, g, b, eps=1e-5):
    mu = jnp.mean(x, -1, keepdims=True)
    var = jnp.mean(jnp.square(x - mu), -1, keepdims=True)
    return (x - mu) * lax.rsqrt(var + eps) * g + b


def rms_norm(x, g, eps=1e-6):
    return x * lax.rsqrt(jnp.mean(jnp.square(x), -1, keepdims=True) + eps) * g


def dwconv(x, w, b, left):
    K = w.shape[0]
    L = x.shape[1]
    xp = jnp.pad(x, ((0, 0), (left, K - 1 - left), (0, 0)))
    return sum(xp[:, j:j + L] * w[j] for j in range(K)) + b


def token_shift(z, mu_prev, mu_next):
    zp = jnp.pad(z, ((0, 0), (1, 1), (0, 0)))
    return z + mu_prev * (zp[:, :-2] - z) + mu_next * (zp[:, 2:] - z)


def axial_rope(rows):
    r, col = jnp.meshgrid(jnp.arange(rows, dtype=F32), jnp.arange(GRID_W, dtype=F32), indexing='ij')
    half = MLA_ROPE // 2
    inv = 1.0 / (ROPE_BASE ** (jnp.arange(0, half, 2, dtype=F32) / half))
    ang = jnp.concatenate([r.reshape(-1, 1) * inv, col.reshape(-1, 1) * inv], -1)
    return jnp.cos(ang), jnp.sin(ang)


def apply_rope(x, cos, sin):
    h = x.shape[-1] // 2
    x1, x2 = x[..., :h], x[..., h:]
    return jnp.concatenate([x1 * cos - x2 * sin, x1 * sin + x2 * cos], -1)


def mla_padded_qkv(Pa, q_norm, kv_norm, w_uq, w_ukv, cos, sin, want_q=True):
    B, L, _ = Pa.shape
    H, DN, DR, DV = MLA_HEADS, MLA_NOPE, MLA_ROPE, MLA_V
    ckv = rms_norm(Pa[..., A_CKV:A_CKV + MLA_KV_RANK], kv_norm)
    kv = (ckv @ w_ukv).reshape(B, L, H, DN + DV)
    k_rope = Pa[..., A_KR:A_KR + DR]
    if cos is not None:
        k_rope = apply_rope(k_rope, cos[None], sin[None])
    k_rope = jnp.broadcast_to(k_rope[:, :, None, :], (B, L, H, DR))
    zk = jnp.zeros((B, L, H, LANES - DN - DR), F32)
    k = jnp.concatenate([kv[..., :DN], k_rope, zk], -1).reshape(B, L, H * LANES).astype(BF16)
    vv = kv[..., DN:]
    zv = jnp.zeros_like(vv)
    v = jnp.stack([jnp.concatenate([vv[:, :, h], zv[:, :, h]] if h % 2 == 0 else [zv[:, :, h], vv[:, :, h]], -1)
                   for h in range(H)], 2).reshape(B, L, H * LANES).astype(BF16)
    q = None
    if want_q:
        cq = rms_norm(Pa[..., A_CQ:A_CQ + MLA_Q_RANK], q_norm)
        qq = (cq @ w_uq).reshape(B, L, H, DN + DR)
        q_rope = qq[..., DN:]
        if cos is not None:
            q_rope = apply_rope(q_rope, cos[None, :, None, :], sin[None, :, None, :])
        scale = (DN + DR) ** -0.5 * math.log2(math.e)
        q = jnp.concatenate([qq[..., :DN], q_rope, zk], -1) * scale
        q = q.reshape(B, L, H * LANES).astype(BF16)
    return q, k, v


def mla_mixer(Pal, Pac, q_norm, kv_norm, w_uq, w_ukv, out_norm, cos, sin, ctx_out):
    ql, kl, vl = mla_padded_qkv(Pal, q_norm, kv_norm, w_uq, w_ukv, cos, sin)
    qc, kc, vc = mla_padded_qkv(Pac, q_norm, kv_norm, w_uq, w_ukv, None, None, want_q=ctx_out)
    yl = mla_attention_pallas(ql, jnp.concatenate([kc, kl], 1), jnp.concatenate([vc, vl], 1), out_norm)
    yc = mla_attention_pallas(qc, kc, vc, out_norm) if ctx_out else None
    return yl, yc


def rwkv_finish(y, bonus, v, gd, g_up, ln_g, ln_b):
    B, L = y.shape[:2]
    mu = jnp.mean(y, -1, keepdims=True)
    var = jnp.mean(jnp.square(y - mu), -1, keepdims=True)
    yn = (y - mu) * lax.rsqrt(var + RWKV_GN_EPS) * ln_g.reshape(RWKV_HEADS, RWKV_HEAD) + ln_b.reshape(RWKV_HEADS, RWKV_HEAD)
    g = jax.nn.sigmoid(gd) @ g_up
    return (yn + bonus * v).reshape(B, L, GROUP) * g


def rwkv_mixer(Pl, Pc, mu_prev, mu_next, w0, w_up, a0, a_up, g_up, k_k, k_a, r_k, ln_g, ln_b, ctx_out):
    B, L = Pl.shape[:2]
    Lc = Pc.shape[1]
    S = Lc + L
    H, N = RWKV_HEADS, RWKV_HEAD
    zl = token_shift(Pl, mu_prev, mu_next)
    zc = token_shift(Pc, mu_prev, mu_next)
    z2 = jnp.stack([jnp.concatenate([zc, zl], 1),
                    jnp.concatenate([jnp.flip(zc, 1), jnp.flip(zl, 1)], 1)], 0)

    def heads(t):
        return t.reshape(2, B, S, H, N)

    r = heads(z2[..., C_R:C_R + GROUP])
    k = heads(z2[..., C_K:C_K + GROUP])
    v = heads(z2[..., C_V:C_V + GROUP])
    kk = k * k_k.reshape(H, N)
    kk = kk * lax.rsqrt(jnp.maximum(jnp.sum(jnp.square(kk), -1, keepdims=True), 1e-24))
    wd = jnp.stack([z2[d, ..., C_WD + d * RWKV_DECAY_LORA:C_WD + (d + 1) * RWKV_DECAY_LORA] for d in range(2)], 0)
    ad = jnp.stack([z2[d, ..., C_AD + d * RWKV_AAA_LORA:C_AD + (d + 1) * RWKV_AAA_LORA] for d in range(2)], 0)
    lw = jnp.einsum('dbsl,dlg->dbsg', jnp.tanh(wd), w_up)
    log_w = -jnp.exp(-jax.nn.softplus(-(w0[:, None, None, :] + lw)) - 0.5)
    a = heads(jax.nn.sigmoid(a0[:, None, None, :] + jnp.einsum('dbsl,dlg->dbsg', ad, a_up)))
    kd = k * (1.0 + (a - 1.0) * k_a.reshape(H, N))
    w = heads(jnp.exp(log_w))

    def to_scan(t):
        return jnp.transpose(t, (2, 4, 0, 1, 3)).reshape(S, N, 2 * B * H)

    y = rwkv_scan_pallas(to_scan(r), to_scan(w), to_scan(kk), to_scan(kk * a), to_scan(v), to_scan(kd))
    y = jnp.transpose(y.reshape(S, N, 2, B, H), (2, 3, 0, 4, 1))
    bonus = jnp.sum(r * kd * r_k, -1, keepdims=True)

    def natural(t, lo, hi):
        return t[0, :, lo:hi] + jnp.flip(t[1, :, lo:hi], 1)

    out_l = rwkv_finish(natural(y, Lc, S), natural(bonus, Lc, S), v[0, :, Lc:], zl[..., C_GD:C_GD + RWKV_GATE_LORA],
                        g_up, ln_g, ln_b)
    out_c = None
    if ctx_out:
        out_c = rwkv_finish(natural(y, 0, Lc), natural(bonus, 0, Lc), v[0, :, :Lc], zc[..., C_GD:C_GD + RWKV_GATE_LORA],
                            g_up, ln_g, ln_b)
    return out_l, out_c


def hyena_filters(L, w1, b1, w2, b2, w3):
    t01 = jnp.linspace(0.0, 1.0, L, dtype=F32)[:, None]
    bands = jnp.linspace(1e-4, HY_BANDS - 1, HY_BANDS, dtype=F32)[None, :]
    wpos = (2.0 * math.pi / L) * jnp.arange(L, dtype=F32)[:, None]
    z = jnp.concatenate([t01, jnp.cos(bands * wpos), -jnp.sin(bands * wpos)], -1)
    h = jnp.sin(HY_SIN_FREQ * (z @ w1 + b1))
    h = jnp.sin(HY_SIN_FREQ * (h @ w2 + b2))
    h = (h @ w3).reshape(L, HY_ORDER, 2, GROUP)
    deltas = jnp.abs(jnp.linspace(HY_DECAY_MIN, HY_DECAY_MAX, GROUP, dtype=F32))
    window = jnp.exp(-t01 * deltas) + HY_SHIFT
    return h * window[:, None, None, :]


def hyena_sequence(Pd, conv_w, conv_b, w1, b1, w2, b2, w3, d_skip):
    B, L, _ = Pd.shape
    C = GROUP
    T = HY_T
    nb = L // T
    z = dwconv(Pd, conv_w, conv_b, 1)
    zT = jnp.transpose(z.reshape(B, nb, T, 3 * C), (3, 1, 0, 2)).reshape(3 * C, nb * B, T)
    h = hyena_filters(L, w1, b1, w2, b2, w3)
    hf = jnp.transpose(h[:, :, 0], (1, 2, 0))
    hb = jnp.transpose(h[:, :, 1], (1, 2, 0))
    kfull = jnp.concatenate([jnp.zeros((HY_ORDER, C, 1), F32), jnp.flip(hb[..., 1:], -1), hf], -1)
    oT = hyena_conv_pallas(kfull, d_skip, zT[:C], zT[C:2 * C], zT[2 * C:])
    return jnp.transpose(oT.reshape(C, nb, B, T), (2, 1, 3, 0)).reshape(B, L, C)


def swiglu(x, wg, wu, wd):
    n_blocks = x.shape[0] // MOE_BLOCK
    return grouped_swiglu_pallas(x.astype(BF16), jnp.zeros((n_blocks,), jnp.int32), jnp.int32(n_blocks),
                                 wg[None].astype(BF16), wu[None].astype(BF16), wd[None].astype(BF16))


def moe_swiglu(x, router, wg, wu, wd):
    N, D = x.shape
    logits = x @ router
    top_v, top_i = lax.top_k(logits, TOP_K)
    gates = jax.nn.softmax(top_v, axis=-1)
    A = N * TOP_K
    e_flat = top_i.reshape(-1)
    tok_flat = jnp.arange(A, dtype=jnp.int32) // TOP_K
    order = jnp.argsort(e_flat)
    e_sorted = e_flat[order]
    counts = jnp.bincount(e_flat, length=N_EXPERTS)
    starts = jnp.cumsum(counts) - counts
    padded = (counts + MOE_BLOCK - 1) // MOE_BLOCK * MOE_BLOCK
    pends = jnp.cumsum(padded)
    pstarts = pends - padded
    dest = (pstarts[e_sorted] + jnp.arange(A, dtype=jnp.int32) - starts[e_sorted]).astype(jnp.int32)
    n_blocks = -(-A // MOE_BLOCK) + N_EXPERTS
    n_slots = n_blocks * MOE_BLOCK
    slot_tok = jnp.full((n_slots,), N, jnp.int32).at[dest].set(tok_flat[order])
    block_expert = jnp.clip(jnp.searchsorted(pends, jnp.arange(n_blocks) * MOE_BLOCK, side='right'), 0, N_EXPERTS - 1)
    xp = jnp.concatenate([x.astype(BF16), jnp.zeros((1, D), BF16)], 0)
    ys = grouped_swiglu_pallas(xp[slot_tok], block_expert, pends[-1] // MOE_BLOCK,
                               wg.astype(BF16), wu.astype(BF16), wd.astype(BF16))
    slot_of = jnp.zeros((A,), jnp.int32).at[order].set(dest).reshape(N, TOP_K)
    return ys[slot_of[:, 0]] * gates[:, 0:1] + ys[slot_of[:, 1]] * gates[:, 1:2]


def kernel(x, c, ctx, c_ctx, ada_w, ada_b, w_in, mla_q_norm, mla_kv_norm, mla_w_uq, mla_w_ukv, mla_out_norm, lru_conv_w, lru_conv_b, lru_w_r, lru_b_r, lru_w_i, lru_b_i, lru_lambda, lru_out_norm, rwkv_mu_prev, rwkv_mu_next, rwkv_w0, rwkv_w_up, rwkv_a0, rwkv_a_up, rwkv_g_up, rwkv_k_k, rwkv_k_a, rwkv_r_k, rwkv_ln_g, rwkv_ln_b, hy_conv_w, hy_conv_b, hy_f_w1, hy_f_b1, hy_f_w2, hy_f_b2, hy_f_w3, hy_d, hy_out_norm, w_out, ln1_g, ln1_b, ln2_g, ln2_b, ffn_w_gate, ffn_w_up, ffn_w_down, moe_router, moe_w_gate, moe_w_up, moe_w_down):
    B, L, D = x.shape
    Lc = ctx.shape[1]
    rows = L // GRID_W
    cos, sin = axial_rope(rows)
    s_lat = jax.nn.silu(c)
    s_ctx = jax.nn.silu(c_ctx)
    xl, xc = x, ctx
    for li in range(DEPTH):
        ctx_out = li < DEPTH - 1
        mod_l = (s_lat @ ada_w[li] + ada_b[li]).reshape(B, 6, 1, D)
        mod_c = (s_ctx @ ada_w[li] + ada_b[li]).reshape(6, 1, D)

        hl = modulate(xl, mod_l[:, 0], mod_l[:, 1])
        hc = modulate(xc, mod_c[0], mod_c[1])
        w_in_l = w_in[li]
        w_secs = (w_in_l[:, :B_X], w_in_l[:, B_X:C_OFF], w_in_l[:, C_OFF:D_OFF], w_in_l[:, D_OFF:])
        Pal, Pbl, Pcl, Pdl = (hl @ w for w in w_secs)
        Pac, Pbc, Pcc = (hc @ w for w in w_secs[:3])
        a_l, a_c = mla_mixer(Pal, Pac, mla_q_norm[li], mla_kv_norm[li], mla_w_uq[li], mla_w_ukv[li],
                             mla_out_norm[li], cos, sin, ctx_out)
        b_l, b_c = rglru_pallas(Pbl, Pbc, lru_conv_w[li], lru_conv_b[li], lru_w_r[li], lru_b_r[li], lru_w_i[li],
                                lru_b_i[li], lru_lambda[li], lru_out_norm[li])
        c_l, c_c = rwkv_mixer(Pcl, Pcc, rwkv_mu_prev[li], rwkv_mu_next[li], rwkv_w0[li], rwkv_w_up[li], rwkv_a0[li],
                              rwkv_a_up[li], rwkv_g_up[li], rwkv_k_k[li], rwkv_k_a[li], rwkv_r_k[li],
                              rwkv_ln_g[li], rwkv_ln_b[li], ctx_out)
        d_l = rms_norm(hyena_sequence(Pdl, hy_conv_w[li], hy_conv_b[li], hy_f_w1[li], hy_f_b1[li], hy_f_w2[li],
                                      hy_f_b2[li], hy_f_w3[li], hy_d[li]), hy_out_norm[li])
        yl = jnp.concatenate([a_l, b_l, c_l, d_l], -1) @ w_out[li]
        xl = layer_norm(ALPHA * xl + mod_l[:, 2] * yl, ln1_g[li], ln1_b[li])
        if ctx_out:
            Pdc = hc @ w_secs[3]
            d_c = rms_norm(hyena_sequence(Pdc, hy_conv_w[li], hy_conv_b[li], hy_f_w1[li], hy_f_b1[li], hy_f_w2[li],
                                          hy_f_b2[li], hy_f_w3[li], hy_d[li]), hy_out_norm[li])
            yc = jnp.concatenate([a_c, b_c, c_c, d_c], -1) @ w_out[li]
            xc = layer_norm(ALPHA * xc + mod_c[2] * yc, ln1_g[li], ln1_b[li])

        fl = modulate(xl, mod_l[:, 3], mod_l[:, 4]).reshape(B * L, D)
        if ctx_out:
            fc = modulate(xc, mod_c[3], mod_c[4]).reshape(B * Lc, D)
            tokens = jnp.concatenate([fl, fc], 0)
        else:
            tokens = fl
        j = li // 2
        if li % 2 == 0:
            out = swiglu(tokens, ffn_w_gate[j], ffn_w_up[j], ffn_w_down[j])
        else:
            out = moe_swiglu(tokens, moe_router[j], moe_w_gate[j], moe_w_up[j], moe_w_down[j])
        xl = layer_norm(ALPHA * xl + mod_l[:, 5] * out[:B * L].reshape(B, L, D), ln2_g[li], ln2_b[li])
        if ctx_out:
            xc = layer_norm(ALPHA * xc + mod_c[5] * out[B * L:].reshape(B, Lc, D), ln2_g[li], ln2_b[li])
    return xl
```

```python
import math
from functools import partial

import jax
import jax.numpy as jnp
from jax import lax
from jax.experimental import pallas as pl
from jax.experimental.pallas import tpu as pltpu

F32 = jnp.float32
BF16 = jnp.bfloat16

SUBLANES = 8
LANES = 128
VMEM_LIMIT_BYTES = 48 * 1024 * 1024

D_MODEL = 1024
DEPTH = 4
GRID_W = 64
GROUP = D_MODEL // 4

MLA_HEADS = 4
MLA_NOPE = 64
MLA_ROPE = 32
MLA_V = 64
MLA_Q_RANK = 192
MLA_KV_RANK = 128
ROPE_BASE = 10000.0
Q_BLOCK = 128

LRU_BLOCKS = 4
LRU_CONV = 4
LRU_CONV_LEFT = 2
LRU_C = 8.0

RWKV_HEADS = 4
RWKV_HEAD = GROUP // RWKV_HEADS
RWKV_DECAY_LORA = 32
RWKV_AAA_LORA = 32
RWKV_GATE_LORA = 64
RWKV_GN_EPS = 64e-5

HY_ORDER = 2
HY_SHORT = 3
HY_BANDS = 16
HY_EMB = 1 + 2 * HY_BANDS
HY_HIDDEN = 64
HY_SIN_FREQ = 1.0
HY_DECAY_MIN = math.log(1e-2) / 1.5
HY_DECAY_MAX = math.log(1e-2) / 0.3
HY_SHIFT = 0.05

N_EXPERTS = 8
TOP_K = 2
MOE_BLOCK = 512

ALPHA = (2.0 * DEPTH) ** 0.25

A_CQ = 0
A_CKV = A_CQ + MLA_Q_RANK
A_KR = A_CKV + MLA_KV_RANK
B_X = A_KR + MLA_ROPE
B_GATE = B_X + GROUP
C_OFF = B_GATE + GROUP
C_R = 0
C_K = GROUP
C_V = 2 * GROUP
C_WD = 3 * GROUP
C_AD = C_WD + 2 * RWKV_DECAY_LORA
C_GD = C_AD + 2 * RWKV_AAA_LORA
C_COLS = C_GD + RWKV_GATE_LORA
D_OFF = C_OFF + C_COLS
D_COLS = (HY_ORDER + 1) * GROUP


RWKV_TIME_BLOCK = 16
V_TILES = RWKV_HEAD // SUBLANES
RWKV_KEY_GROUP = 16


def _rwkv_scan_kernel(rf, rb, wf, wb, kkf, kkb, kaf, kab, vf, vb, kdf, kdb, yf_ref, yb_ref, s_ref, m_ref):
    @pl.when(pl.program_id(0) == 0)
    def _():
        s_ref[...] = jnp.zeros_like(s_ref)

    n_t = rf.shape[0]
    p = rf.shape[2]
    fwd_lanes = lax.broadcasted_iota(jnp.int32, (RWKV_HEAD, p), 1) < p // 2

    for q, (f_ref, b_ref) in enumerate(((rf, rb), (wf, wb), (kkf, kkb), (kaf, kab), (vf, vb), (kdf, kdb))):
        for j in range(n_t):
            m_ref[q, j] = jnp.where(fwd_lanes, f_ref[j], b_ref[n_t - 1 - j])

    r_ref, w_ref, kk_ref, kka_ref, v_ref, kd_ref = (m_ref.at[q] for q in range(6))

    def step(t, carry):
        def row(ref, k):
            return jnp.broadcast_to(ref[t, pl.ds(k, 1), :], (SUBLANES, p))[None]

        def sa_group(g, sa):
            k0 = pl.multiple_of(g * RWKV_KEY_GROUP, RWKV_KEY_GROUP)
            for j in range(RWKV_KEY_GROUP):
                sa = sa + s_ref[k0 + j] * row(kk_ref, k0 + j)
            return sa

        zero = jnp.zeros((V_TILES, SUBLANES, p), F32)
        sa = lax.fori_loop(0, RWKV_HEAD // RWKV_KEY_GROUP, sa_group, zero)
        vt = v_ref[t].reshape(V_TILES, SUBLANES, p)

        def update_group(g, y):
            k0 = pl.multiple_of(g * RWKV_KEY_GROUP, RWKV_KEY_GROUP)
            for j in range(RWKV_KEY_GROUP):
                k = k0 + j
                sn = s_ref[k] * row(w_ref, k) - sa * row(kka_ref, k) + vt * row(kd_ref, k)
                s_ref[k] = sn
                y = y + sn * row(r_ref, k)
            return y

        y = lax.fori_loop(0, RWKV_HEAD // RWKV_KEY_GROUP, update_group, zero).reshape(RWKV_HEAD, p)
        yf_ref[t] = y
        yb_ref[n_t - 1 - t] = y
        return carry

    lax.fori_loop(0, n_t, step, 0)


def rwkv_scan_pallas(r, kk, v, w, kka, kd, n_ctx):
    n_steps, n, p = r.shape
    tb = RWKV_TIME_BLOCK
    assert n == RWKV_HEAD and p == LANES and n_steps % tb == 0 and n_ctx % tb == 0
    nb, nc = n_steps // tb, n_ctx // tb
    fwd = pl.BlockSpec((tb, n, p), lambda g: (g, 0, 0))
    bwd = pl.BlockSpec((tb, n, p), lambda g: (jnp.where(g < nc, nc - 1 - g, nb + nc - 1 - g), 0, 0))
    out = jax.ShapeDtypeStruct((n_steps, n, p), F32)
    return pl.pallas_call(
        _rwkv_scan_kernel,
        grid=(nb,),
        in_specs=[fwd, bwd] * 6,
        out_specs=[fwd, bwd],
        out_shape=[out, out],
        scratch_shapes=[pltpu.VMEM((n, V_TILES, SUBLANES, LANES), F32), pltpu.VMEM((6, tb, n, LANES), F32)],
        compiler_params=pltpu.CompilerParams(dimension_semantics=("arbitrary",), vmem_limit_bytes=VMEM_LIMIT_BYTES),
        name="rwkv_scan",
    )(r, r, w[0], w[1], kk, kk, kka[0], kka[1], v, v, kd[0], kd[1])


ATTN_TQ = 256
ATTN_TK = 256


def _attn_kernel(q_ref, k_ref, v_ref, g_ref, o_ref, s_ref):
    tq = q_ref.shape[1]
    n_chunks = k_ref.shape[1]
    n_tiles = ATTN_TK // LANES
    pair_out = []
    for hp in range(MLA_HEADS // 2):
        o_pair = jnp.zeros((tq, LANES), F32)
        for h in (2 * hp, 2 * hp + 1):
            qh = q_ref[0, :, pl.ds(LANES * h, LANES)]

            m_acc = jnp.full((tq, LANES), -jnp.inf, F32)
            for c in range(n_chunks):
                s = jnp.dot(qh, k_ref[0, c, pl.ds(LANES * h, LANES), :], preferred_element_type=F32)
                s_ref[c] = s
                for j in range(n_tiles):
                    m_acc = jnp.maximum(m_acc, s[:, LANES * j:LANES * (j + 1)])
            m_full = jnp.broadcast_to(jnp.max(m_acc, -1, keepdims=True), (tq, LANES))

            l_acc = jnp.zeros((tq, LANES), F32)
            acc = jnp.zeros((tq, LANES), F32)
            for c in range(n_chunks):
                s = s_ref[c]
                ps = []
                for j in range(n_tiles):
                    p = jnp.exp2(s[:, LANES * j:LANES * (j + 1)] - m_full)
                    l_acc = l_acc + p
                    ps.append(p.astype(BF16))
                vh = v_ref[0, pl.ds(c * ATTN_TK, ATTN_TK), pl.ds(LANES * h, LANES)]
                acc = acc + jnp.dot(jnp.concatenate(ps, -1), vh, preferred_element_type=F32)
            o_pair = o_pair + acc / jnp.sum(l_acc, -1, keepdims=True)
        pair_out.append(o_pair)
    o = jnp.concatenate(pair_out, -1)
    o_ref[0] = o * lax.rsqrt(jnp.mean(jnp.square(o), -1, keepdims=True) + 1e-6) * g_ref[...]


def mla_attention_pallas(q, k, v, out_norm):
    B, Lq, W = q.shape
    Lk = k.shape[1]
    assert Lq % ATTN_TQ == 0 and Lk % ATTN_TK == 0
    kt = jnp.transpose(k.reshape(B, Lk // ATTN_TK, ATTN_TK, W), (0, 1, 3, 2))
    return pl.pallas_call(
        _attn_kernel,
        grid=(B, Lq // ATTN_TQ),
        in_specs=[
            pl.BlockSpec((1, ATTN_TQ, W), lambda b, i: (b, i, 0)),
            pl.BlockSpec((1, Lk // ATTN_TK, W, ATTN_TK), lambda b, i: (b, 0, 0, 0)),
            pl.BlockSpec((1, Lk, W), lambda b, i: (b, 0, 0)),
            pl.BlockSpec((1, GROUP), lambda b, i: (0, 0)),
        ],
        out_specs=pl.BlockSpec((1, ATTN_TQ, GROUP), lambda b, i: (b, i, 0)),
        out_shape=jax.ShapeDtypeStruct((B, Lq, GROUP), F32),
        scratch_shapes=[pltpu.VMEM((Lk // ATTN_TK, ATTN_TQ, ATTN_TK), F32)],
        compiler_params=pltpu.CompilerParams(dimension_semantics=("arbitrary", "arbitrary"),
                                             vmem_limit_bytes=VMEM_LIMIT_BYTES),
        name="mla_attention",
    )(q, kt, v, out_norm[None])


LRU_CHUNK = 256
LRU_HALO = SUBLANES


def _lru_kernel(xl_ref, xc_ref, wbd_ref, bias_ref, c8_ref, cw_ref, cb_ref, gn_ref,
                yl_ref, yc_ref, xs_l, xs_c, hf_l, hf_c, a_s, b_s, hb_s):
    C = GROUP
    CH = LRU_CHUNK
    L = xl_ref.shape[1]
    Lc = xc_ref.shape[1]

    def stage(x_ref, xs, n):
        xs[pl.ds(0, LRU_HALO), :] = jnp.zeros((LRU_HALO, C), F32)
        xs[pl.ds(LRU_HALO + n, LRU_HALO), :] = jnp.zeros((LRU_HALO, C), F32)

        def cp(i, c):
            r0 = pl.multiple_of(i * CH, CH)
            xs[pl.ds(LRU_HALO + r0, CH), :] = x_ref[0, pl.ds(r0, CH), pl.ds(0, C)]
            return c

        lax.fori_loop(0, n // CH, cp, 0)

    stage(xl_ref, xs_l, L)
    stage(xc_ref, xs_c, Lc)

    def coeffs(xs, base, d):
        xv = xs[pl.ds(base, CH + 2 * LRU_HALO), :]
        u = cb_ref[...]
        for j in range(LRU_CONV):
            o = LRU_HALO - LRU_CONV_LEFT + j
            u = u + xv[o:o + CH] * cw_ref[pl.ds(j, 1), :]
        z = jnp.dot(u.astype(BF16), wbd_ref[:, pl.ds(d * 2 * C, 2 * C)], preferred_element_type=F32)
        z = z + bias_ref[:, pl.ds(d * 2 * C, 2 * C)]
        r = jax.nn.sigmoid(z[:, :C])
        i = jax.nn.sigmoid(z[:, C:])
        log_a = r * c8_ref[pl.ds(d, 1), :]
        a = jnp.exp(log_a)
        a_s[...] = a
        b_s[...] = jnp.sqrt(-jnp.tanh(log_a) * (a * a + 1.0)) * (i * u)

    def row_scan(h, out_ref, out_base, reverse):
        def body(t, h):
            tt = CH - 1 - t if reverse else t
            a_t = jnp.broadcast_to(a_s[pl.ds(tt, 1), :], (SUBLANES, C))
            b_t = jnp.broadcast_to(b_s[pl.ds(tt, 1), :], (SUBLANES, C))
            h = a_t * h + b_t
            out_ref[pl.ds(out_base + tt, 1), :] = h[0:1, :]
            return h

        return lax.fori_loop(0, CH, body, h, unroll=8)

    h0 = jnp.zeros((SUBLANES, C), F32)

    h = h0
    for ci in range(Lc // CH):
        coeffs(xs_c, ci * CH, 0)
        h = row_scan(h, hf_c, ci * CH, False)

    def fwd_chunk(ci, h):
        base = pl.multiple_of(ci * CH, CH)
        coeffs(xs_l, base, 0)
        return row_scan(h, hf_l, base, False)

    lax.fori_loop(0, L // CH, fwd_chunk, h)

    def combine(x_ref, hf, base, y_ref):
        hl = hf[pl.ds(base, CH), :] + hb_s[...]
        g = jax.nn.gelu(x_ref[0, pl.ds(base, CH), pl.ds(C, C)])
        v = hl * g
        y = v * lax.rsqrt(jnp.mean(jnp.square(v), -1, keepdims=True) + 1e-6) * gn_ref[...]
        y_ref[0, pl.ds(base, CH), :] = y

    h = h0
    for ci in reversed(range(Lc // CH)):
        coeffs(xs_c, ci * CH, 1)
        h = row_scan(h, hb_s, 0, True)
        combine(xc_ref, hf_c, ci * CH, yc_ref)

    def bwd_chunk(k, h):
        base = pl.multiple_of((L // CH - 1 - k) * CH, CH)
        coeffs(xs_l, base, 1)
        h = row_scan(h, hb_s, 0, True)
        combine(xl_ref, hf_l, base, yl_ref)
        return h

    lax.fori_loop(0, L // CH, bwd_chunk, h)


def rglru_pallas(xg_l, xg_c, conv_w, conv_b, w_r, b_r, w_i, b_i, lam, out_norm):
    B, L, _ = xg_l.shape
    Lc = xg_c.shape[1]
    C = GROUP
    assert L % LRU_CHUNK == 0 and Lc % LRU_CHUNK == 0

    def bd(w):
        return jax.scipy.linalg.block_diag(*[w[n] for n in range(LRU_BLOCKS)])

    wbd = jnp.concatenate([bd(w_r[0]), bd(w_i[0]), bd(w_r[1]), bd(w_i[1])], 1).astype(BF16)
    bias = jnp.concatenate([b_r[0], b_i[0], b_r[1], b_i[1]])[None]
    c8 = -LRU_C * jax.nn.softplus(-lam)

    def full(shape):
        return pl.BlockSpec(shape, lambda b: (0,) * len(shape))

    return pl.pallas_call(
        _lru_kernel,
        grid=(B,),
        in_specs=[
            pl.BlockSpec((1, L, 2 * C), lambda b: (b, 0, 0)),
            pl.BlockSpec((1, Lc, 2 * C), lambda b: (b, 0, 0)),
            full((C, 4 * C)), full((1, 4 * C)), full((2, C)), full((LRU_CONV, C)), full((1, C)), full((1, C)),
        ],
        out_specs=[
            pl.BlockSpec((1, L, C), lambda b: (b, 0, 0)),
            pl.BlockSpec((1, Lc, C), lambda b: (b, 0, 0)),
        ],
        out_shape=[jax.ShapeDtypeStruct((B, L, C), F32), jax.ShapeDtypeStruct((B, Lc, C), F32)],
        scratch_shapes=[
            pltpu.VMEM((L + 2 * LRU_HALO, C), F32),
            pltpu.VMEM((Lc + 2 * LRU_HALO, C), F32),
            pltpu.VMEM((L, C), F32),
            pltpu.VMEM((Lc, C), F32),
            pltpu.VMEM((LRU_CHUNK, C), F32),
            pltpu.VMEM((LRU_CHUNK, C), F32),
            pltpu.VMEM((LRU_CHUNK, C), F32),
        ],
        compiler_params=pltpu.CompilerParams(dimension_semantics=("arbitrary",), vmem_limit_bytes=VMEM_LIMIT_BYTES),
        name="rglru",
    )(xg_l, xg_c, wbd, bias, c8, conv_w, conv_b[None], out_norm[None])


HY_T = 256
HY_CB = 8


def _hyena_kernel(k_ref, d_ref, v_ref, x1_ref, x2_ref, o_ref, u_s, acc_s):
    n_rows = v_ref.shape[1]
    T = HY_T
    nb = k_ref.shape[2] // (2 * T)
    bsz = n_rows // nb

    def conv(ci, order):
        acc_s[...] = jnp.zeros_like(acc_s)
        for dd in range(-(nb - 1), nb):
            w2 = k_ref[order, pl.ds(ci, 1), pl.ds(T * (dd + nb - 1), 2 * T)]
            x = jnp.broadcast_to(w2, (T, 2 * T))
            r = pltpu.roll(x, 0, 1, stride=1, stride_axis=0)
            tb = r[:, T:].astype(BF16)
            j0, j1 = max(0, -dd), min(nb, nb - dd)
            lhs = u_s[pl.ds(bsz * j0, bsz * (j1 - j0)), :]
            dst = pl.ds(bsz * (j0 + dd), bsz * (j1 - j0))
            acc_s[dst, :] = acc_s[dst, :] + jnp.dot(lhs, tb, preferred_element_type=F32)

    def channel(ci, carry):
        v = v_ref[ci]
        u_s[...] = v.astype(BF16)
        conv(ci, 0)
        u = x1_ref[ci] * (acc_s[...] + v * d_ref[pl.ds(ci, 1), pl.ds(0, 1)])
        u_s[...] = u.astype(BF16)
        conv(ci, 1)
        o_ref[ci] = x2_ref[ci] * (acc_s[...] + u * d_ref[pl.ds(ci, 1), pl.ds(1, 1)])
        return carry

    lax.fori_loop(0, HY_CB, channel, 0)


def hyena_conv_pallas(kfull, d_skip, vT, x1T, x2T):
    C, R, T = vT.shape
    two_l = kfull.shape[2]
    assert T == HY_T and C % HY_CB == 0
    blk = pl.BlockSpec((HY_CB, R, T), lambda c: (c, 0, 0))
    return pl.pallas_call(
        _hyena_kernel,
        grid=(C // HY_CB,),
        in_specs=[
            pl.BlockSpec((HY_ORDER, HY_CB, two_l), lambda c: (0, c, 0)),
            pl.BlockSpec((HY_CB, HY_ORDER), lambda c: (c, 0)),
            blk, blk, blk,
        ],
        out_specs=blk,
        out_shape=jax.ShapeDtypeStruct((C, R, T), F32),
        scratch_shapes=[pltpu.VMEM((R, T), BF16), pltpu.VMEM((R, T), F32)],
        compiler_params=pltpu.CompilerParams(dimension_semantics=("arbitrary",), vmem_limit_bytes=VMEM_LIMIT_BYTES),
        name="hyena_conv",
    )(kfull, d_skip.T, vT, x1T, x2T)


FFN_TF_MAX = 1408


def _ffn_tile(hidden):
    return max(t for t in range(LANES, FFN_TF_MAX + 1, LANES) if hidden % t == 0)


def _swiglu_kernel(be_ref, nu_ref, x_ref, wg_ref, wu_ref, wd_ref, o_ref, acc_ref):
    i = pl.program_id(0)
    f = pl.program_id(1)

    @pl.when(i < nu_ref[0])
    def _():
        x = x_ref[...]
        g = jnp.dot(x, wg_ref[0], preferred_element_type=F32)
        u = jnp.dot(x, wu_ref[0], preferred_element_type=F32)
        h = (jax.nn.silu(g) * u).astype(BF16)
        part = jnp.dot(h, wd_ref[0], preferred_element_type=F32)

        @pl.when(f == 0)
        def _():
            acc_ref[...] = part

        @pl.when(f > 0)
        def _():
            acc_ref[...] = acc_ref[...] + part

    @pl.when(f == pl.num_programs(1) - 1)
    def _():
        o_ref[...] = jnp.where(i < nu_ref[0], acc_ref[...], 0.0)


def grouped_swiglu_pallas(xs, block_expert, n_used, wg, wu, wd):
    n_rows, D = xs.shape
    F = wg.shape[2]
    TM = MOE_BLOCK
    assert n_rows % TM == 0
    tf = _ffn_tile(F)
    n_blocks = n_rows // TM
    grid_spec = pltpu.PrefetchScalarGridSpec(
        num_scalar_prefetch=2,
        grid=(n_blocks, F // tf),
        in_specs=[
            pl.BlockSpec((TM, D), lambda i, f, be, nu: (i, 0)),
            pl.BlockSpec((1, D, tf), lambda i, f, be, nu: (be[i], 0, f)),
            pl.BlockSpec((1, D, tf), lambda i, f, be, nu: (be[i], 0, f)),
            pl.BlockSpec((1, tf, D), lambda i, f, be, nu: (be[i], f, 0)),
        ],
        out_specs=pl.BlockSpec((TM, D), lambda i, f, be, nu: (i, 0)),
        scratch_shapes=[pltpu.VMEM((TM, D), F32)],
    )
    return pl.pallas_call(
        _swiglu_kernel,
        grid_spec=grid_spec,
        out_shape=jax.ShapeDtypeStruct((n_rows, D), F32),
        compiler_params=pltpu.CompilerParams(dimension_semantics=("arbitrary", "arbitrary"),
                                             vmem_limit_bytes=VMEM_LIMIT_BYTES),
        name="grouped_swiglu",
    )(block_expert.astype(jnp.int32), jnp.reshape(n_used, (1,)).astype(jnp.int32), xs, wg, wu, wd)


def modulate(x, shift, scale):
    return x * (1.0 + scale) + shift


def layer_norm(x, g, b, eps=1e-5):
    mu = jnp.mean(x, -1, keepdims=True)
    var = jnp.mean(jnp.square(x - mu), -1, keepdims=True)
    return (x - mu) * lax.rsqrt(var + eps) * g + b


def rms_norm(x, g, eps=1e-6):
    return x * lax.rsqrt(jnp.mean(jnp.square(x), -1, keepdims=True) + eps) * g


def dwconv(x, w, b, left):
    K = w.shape[0]
    L = x.shape[1]
    xp = jnp.pad(x, ((0, 0), (left, K - 1 - left), (0, 0)))
    return sum(xp[:, j:j + L] * w[j] for j in range(K)) + b


def token_shift(z, mu_prev, mu_next):
    zp = jnp.pad(z, ((0, 0), (1, 1), (0, 0)))
    return z + mu_prev * (zp[:, :-2] - z) + mu_next * (zp[:, 2:] - z)


def axial_rope(rows):
    r, col = jnp.meshgrid(jnp.arange(rows, dtype=F32), jnp.arange(GRID_W, dtype=F32), indexing='ij')
    half = MLA_ROPE // 2
    inv = 1.0 / (ROPE_BASE ** (jnp.arange(0, half, 2, dtype=F32) / half))
    ang = jnp.concatenate([r.reshape(-1, 1) * inv, col.reshape(-1, 1) * inv], -1)
    return jnp.cos(ang), jnp.sin(ang)


def apply_rope(x, cos, sin):
    h = x.shape[-1] // 2
    x1, x2 = x[..., :h], x[..., h:]
    return jnp.concatenate([x1 * cos - x2 * sin, x1 * sin + x2 * cos], -1)


def mla_padded_qkv(Pa, q_norm, kv_norm, w_uq, w_ukv, cos, sin, want_q=True):
    B, L, _ = Pa.shape
    H, DN, DR, DV = MLA_HEADS, MLA_NOPE, MLA_ROPE, MLA_V
    ckv = rms_norm(Pa[..., A_CKV:A_CKV + MLA_KV_RANK], kv_norm)
    kv = (ckv @ w_ukv).reshape(B, L, H, DN + DV)
    k_rope = Pa[..., A_KR:A_KR + DR]
    if cos is not None:
        k_rope = apply_rope(k_rope, cos[None], sin[None])
    k_rope = jnp.broadcast_to(k_rope[:, :, None, :], (B, L, H, DR))
    zk = jnp.zeros((B, L, H, LANES - DN - DR), F32)
    k = jnp.concatenate([kv[..., :DN], k_rope, zk], -1).reshape(B, L, H * LANES).astype(BF16)
    vv = kv[..., DN:]
    zv = jnp.zeros_like(vv)
    v = jnp.stack([jnp.concatenate([vv[:, :, h], zv[:, :, h]] if h % 2 == 0 else [zv[:, :, h], vv[:, :, h]], -1)
                   for h in range(H)], 2).reshape(B, L, H * LANES).astype(BF16)
    q = None
    if want_q:
        cq = rms_norm(Pa[..., A_CQ:A_CQ + MLA_Q_RANK], q_norm)
        qq = (cq @ w_uq).reshape(B, L, H, DN + DR)
        q_rope = qq[..., DN:]
        if cos is not None:
            q_rope = apply_rope(q_rope, cos[None, :, None, :], sin[None, :, None, :])
        scale = (DN + DR) ** -0.5 * math.log2(math.e)
        q = jnp.concatenate([qq[..., :DN], q_rope, zk], -1) * scale
        q = q.reshape(B, L, H * LANES).astype(BF16)
    return q, k, v


def mla_mixer(Pal, Pac, q_norm, kv_norm, w_uq, w_ukv, out_norm, cos, sin, ctx_out):
    ql, kl, vl = mla_padded_qkv(Pal, q_norm, kv_norm, w_uq, w_ukv, cos, sin)
    qc, kc, vc = mla_padded_qkv(Pac, q_norm, kv_norm, w_uq, w_ukv, None, None, want_q=ctx_out)
    yl = mla_attention_pallas(ql, jnp.concatenate([kc, kl], 1), jnp.concatenate([vc, vl], 1), out_norm)
    yc = mla_attention_pallas(qc, kc, vc, out_norm) if ctx_out else None
    return yl, yc


def rwkv_finish(y, bonus, v, gd, g_up, ln_g, ln_b):
    B, L = y.shape[:2]
    mu = jnp.mean(y, -1, keepdims=True)
    var = jnp.mean(jnp.square(y - mu), -1, keepdims=True)
    yn = (y - mu) * lax.rsqrt(var + RWKV_GN_EPS) * ln_g.reshape(RWKV_HEADS, RWKV_HEAD) + ln_b.reshape(RWKV_HEADS, RWKV_HEAD)
    g = jax.nn.sigmoid(gd) @ g_up
    return (yn + bonus * v).reshape(B, L, GROUP) * g


def rwkv_mixer(Pl, Pc, mu_prev, mu_next, w0, w_up, a0, a_up, g_up, k_k, k_a, r_k, ln_g, ln_b, ctx_out):
    B, L = Pl.shape[:2]
    Lc = Pc.shape[1]
    S = Lc + L
    H, N = RWKV_HEADS, RWKV_HEAD
    z = jnp.concatenate([token_shift(Pc, mu_prev, mu_next), token_shift(Pl, mu_prev, mu_next)], 1)

    def heads(t):
        return t.reshape(B, S, H, N)

    r = heads(z[..., C_R:C_R + GROUP])
    k = heads(z[..., C_K:C_K + GROUP])
    v = heads(z[..., C_V:C_V + GROUP])
    kk = k * k_k.reshape(H, N)
    kk = kk * lax.rsqrt(jnp.maximum(jnp.sum(jnp.square(kk), -1, keepdims=True), 1e-24))
    w, kka, kd = [], [], []
    for d in range(2):
        wd = z[..., C_WD + d * RWKV_DECAY_LORA:C_WD + (d + 1) * RWKV_DECAY_LORA]
        ad = z[..., C_AD + d * RWKV_AAA_LORA:C_AD + (d + 1) * RWKV_AAA_LORA]
        log_w = -jnp.exp(-jax.nn.softplus(-(w0[d] + jnp.tanh(wd) @ w_up[d])) - 0.5)
        a = heads(jax.nn.sigmoid(a0[d] + ad @ a_up[d]))
        w.append(heads(jnp.exp(log_w)))
        kka.append(kk * a)
        kd.append(k * (1.0 + (a - 1.0) * k_a.reshape(H, N)))

    def to_scan(t):
        t = jnp.transpose(t, (1, 3, 0, 2)).reshape(S, N, B * H)
        return jnp.concatenate([t, t], -1)

    yf, yb = rwkv_scan_pallas(to_scan(r), to_scan(kk), to_scan(v), [to_scan(t) for t in w],
                              [to_scan(t) for t in kka], [to_scan(t) for t in kd], Lc)
    y = yf[..., :B * H] + yb[..., B * H:]
    y = jnp.transpose(y.reshape(S, N, B, H), (2, 0, 3, 1))
    bonus = sum(jnp.sum(r * kd_d * r_k, -1, keepdims=True) for kd_d in kd)
    gd = z[..., C_GD:C_GD + RWKV_GATE_LORA]
    out_l = rwkv_finish(y[:, Lc:], bonus[:, Lc:], v[:, Lc:], gd[:, Lc:], g_up, ln_g, ln_b)
    out_c = None
    if ctx_out:
        out_c = rwkv_finish(y[:, :Lc], bonus[:, :Lc], v[:, :Lc], gd[:, :Lc], g_up, ln_g, ln_b)
    return out_l, out_c


def hyena_filters(L, w1, b1, w2, b2, w3):
    t01 = jnp.linspace(0.0, 1.0, L, dtype=F32)[:, None]
    bands = jnp.linspace(1e-4, HY_BANDS - 1, HY_BANDS, dtype=F32)[None, :]
    wpos = (2.0 * math.pi / L) * jnp.arange(L, dtype=F32)[:, None]
    z = jnp.concatenate([t01, jnp.cos(bands * wpos), -jnp.sin(bands * wpos)], -1)
    h = jnp.sin(HY_SIN_FREQ * (z @ w1 + b1))
    h = jnp.sin(HY_SIN_FREQ * (h @ w2 + b2))
    h = (h @ w3).reshape(L, HY_ORDER, 2, GROUP)
    deltas = jnp.abs(jnp.linspace(HY_DECAY_MIN, HY_DECAY_MAX, GROUP, dtype=F32))
    window = jnp.exp(-t01 * deltas) + HY_SHIFT
    return h * window[:, None, None, :]


def hyena_sequence(Pd, conv_w, conv_b, w1, b1, w2, b2, w3, d_skip):
    B, L, _ = Pd.shape
    C = GROUP
    T = HY_T
    nb = L // T
    z = dwconv(Pd, conv_w, conv_b, 1)
    zT = jnp.transpose(z.reshape(B, nb, T, 3 * C), (3, 1, 0, 2)).reshape(3 * C, nb * B, T)
    h = hyena_filters(L, w1, b1, w2, b2, w3)
    hf = jnp.transpose(h[:, :, 0], (1, 2, 0))
    hb = jnp.transpose(h[:, :, 1], (1, 2, 0))
    kfull = jnp.concatenate([jnp.zeros((HY_ORDER, C, 1), F32), jnp.flip(hb[..., 1:], -1), hf], -1)
    oT = hyena_conv_pallas(kfull, d_skip, zT[:C], zT[C:2 * C], zT[2 * C:])
    return jnp.transpose(oT.reshape(C, nb, B, T), (2, 1, 3, 0)).reshape(B, L, C)


def swiglu(x, wg, wu, wd):
    n_blocks = x.shape[0] // MOE_BLOCK
    return grouped_swiglu_pallas(x.astype(BF16), jnp.zeros((n_blocks,), jnp.int32), jnp.int32(n_blocks),
                                 wg[None].astype(BF16), wu[None].astype(BF16), wd[None].astype(BF16))


def moe_swiglu(x, router, wg, wu, wd):
    N, D = x.shape
    logits = x @ router
    top_v, top_i = lax.top_k(logits, TOP_K)
    gates = jax.nn.softmax(top_v, axis=-1)
    A = N * TOP_K
    e_flat = top_i.reshape(-1)
    tok_flat = jnp.arange(A, dtype=jnp.int32) // TOP_K
    order = jnp.argsort(e_flat)
    e_sorted = e_flat[order]
    counts = jnp.bincount(e_flat, length=N_EXPERTS)
    starts = jnp.cumsum(counts) - counts
    padded = (counts + MOE_BLOCK - 1) // MOE_BLOCK * MOE_BLOCK
    pends = jnp.cumsum(padded)
    pstarts = pends - padded
    dest = (pstarts[e_sorted] + jnp.arange(A, dtype=jnp.int32) - starts[e_sorted]).astype(jnp.int32)
    n_blocks = -(-A // MOE_BLOCK) + N_EXPERTS
    n_slots = n_blocks * MOE_BLOCK
    slot_tok = jnp.full((n_slots,), N, jnp.int32).at[dest].set(tok_flat[order])
    block_expert = jnp.clip(jnp.searchsorted(pends, jnp.arange(n_blocks) * MOE_BLOCK, side='right'), 0, N_EXPERTS - 1)
    xp = jnp.concatenate([x.astype(BF16), jnp.zeros((1, D), BF16)], 0)
    ys = grouped_swiglu_pallas(xp[slot_tok], block_expert, pends[-1] // MOE_BLOCK,
                               wg.astype(BF16), wu.astype(BF16), wd.astype(BF16))
    slot_of = jnp.zeros((A,), jnp.int32).at[order].set(dest).reshape(N, TOP_K)
    return ys[slot_of[:, 0]] * gates[:, 0:1] + ys[slot_of[:, 1]] * gates[:, 1:2]


def kernel(x, c, ctx, c_ctx, ada_w, ada_b, w_in, mla_q_norm, mla_kv_norm, mla_w_uq, mla_w_ukv, mla_out_norm, lru_conv_w, lru_conv_b, lru_w_r, lru_b_r, lru_w_i, lru_b_i, lru_lambda, lru_out_norm, rwkv_mu_prev, rwkv_mu_next, rwkv_w0, rwkv_w_up, rwkv_a0, rwkv_a_up, rwkv_g_up, rwkv_k_k, rwkv_k_a, rwkv_r_k, rwkv_ln_g, rwkv_ln_b, hy_conv_w, hy_conv_b, hy_f_w1, hy_f_b1, hy_f_w2, hy_f_b2, hy_f_w3, hy_d, hy_out_norm, w_out, ln1_g, ln1_b, ln2_g, ln2_b, ffn_w_gate, ffn_w_up, ffn_w_down, moe_router, moe_w_gate, moe_w_up, moe_w_down):
    B, L, D = x.shape
    Lc = ctx.shape[1]
    rows = L // GRID_W
    cos, sin = axial_rope(rows)
    s_lat = jax.nn.silu(c)
    s_ctx = jax.nn.silu(c_ctx)
    xl, xc = x, ctx
    for li in range(DEPTH):
        ctx_out = li < DEPTH - 1
        mod_l = (s_lat @ ada_w[li] + ada_b[li]).reshape(B, 6, 1, D)
        mod_c = (s_ctx @ ada_w[li] + ada_b[li]).reshape(6, 1, D)

        hl = modulate(xl, mod_l[:, 0], mod_l[:, 1])
        hc = modulate(xc, mod_c[0], mod_c[1])
        w_in_l = w_in[li]
        w_secs = (w_in_l[:, :B_X], w_in_l[:, B_X:C_OFF], w_in_l[:, C_OFF:D_OFF], w_in_l[:, D_OFF:])
        Pal, Pbl, Pcl, Pdl = (hl @ w for w in w_secs)
        Pac, Pbc, Pcc = (hc @ w for w in w_secs[:3])
        a_l, a_c = mla_mixer(Pal, Pac, mla_q_norm[li], mla_kv_norm[li], mla_w_uq[li], mla_w_ukv[li],
                             mla_out_norm[li], cos, sin, ctx_out)
        b_l, b_c = rglru_pallas(Pbl, Pbc, lru_conv_w[li], lru_conv_b[li], lru_w_r[li], lru_b_r[li], lru_w_i[li],
                                lru_b_i[li], lru_lambda[li], lru_out_norm[li])
        c_l, c_c = rwkv_mixer(Pcl, Pcc, rwkv_mu_prev[li], rwkv_mu_next[li], rwkv_w0[li], rwkv_w_up[li], rwkv_a0[li],
                              rwkv_a_up[li], rwkv_g_up[li], rwkv_k_k[li], rwkv_k_a[li], rwkv_r_k[li],
                              rwkv_ln_g[li], rwkv_ln_b[li], ctx_out)
        d_l = rms_norm(hyena_sequence(Pdl, hy_conv_w[li], hy_conv_b[li], hy_f_w1[li], hy_f_b1[li], hy_f_w2[li],
                                      hy_f_b2[li], hy_f_w3[li], hy_d[li]), hy_out_norm[li])
        yl = jnp.concatenate([a_l, b_l, c_l, d_l], -1) @ w_out[li]
        xl = layer_norm(ALPHA * xl + mod_l[:, 2] * yl, ln1_g[li], ln1_b[li])
        if ctx_out:
            Pdc = hc @ w_secs[3]
            d_c = rms_norm(hyena_sequence(Pdc, hy_conv_w[li], hy_conv_b[li], hy_f_w1[li], hy_f_b1[li], hy_f_w2[li],
                                          hy_f_b2[li], hy_f_w3[li], hy_d[li]), hy_out_norm[li])
            yc = jnp.concatenate([a_c, b_c, c_c, d_c], -1) @ w_out[li]
            xc = layer_norm(ALPHA * xc + mod_c[2] * yc, ln1_g[li], ln1_b[li])

        fl = modulate(xl, mod_l[:, 3], mod_l[:, 4]).reshape(B * L, D)
        if ctx_out:
            fc = modulate(xc, mod_c[3], mod_c[4]).reshape(B * Lc, D)
            tokens = jnp.concatenate([fl, fc], 0)
        else:
            tokens = fl
        j = li // 2
        if li % 2 == 0:
            out = swiglu(tokens, ffn_w_gate[j], ffn_w_up[j], ffn_w_down[j])
        else:
            out = moe_swiglu(tokens, moe_router[j], moe_w_gate[j], moe_w_up[j], moe_w_down[j])
        xl = layer_norm(ALPHA * xl + mod_l[:, 5] * out[:B * L].reshape(B, L, D), ln2_g[li], ln2_b[li])
        if ctx_out:
            xc = layer_norm(ALPHA * xc + mod_c[5] * out[B * L:].reshape(B, Lc, D), ln2_g[li], ln2_b[li])
    return xl
```

```python
import math
from functools import partial

import jax
import jax.numpy as jnp
from jax import lax
from jax.experimental import pallas as pl
from jax.experimental.pallas import tpu as pltpu

F32 = jnp.float32
BF16 = jnp.bfloat16

SUBLANES = 8
LANES = 128
VMEM_LIMIT_BYTES = 48 * 1024 * 1024

D_MODEL = 1024
DEPTH = 4
GRID_W = 64
GROUP = D_MODEL // 4

MLA_HEADS = 4
MLA_NOPE = 64
MLA_ROPE = 32
MLA_V = 64
MLA_Q_RANK = 192
MLA_KV_RANK = 128
ROPE_BASE = 10000.0
Q_BLOCK = 128

LRU_BLOCKS = 4
LRU_CONV = 4
LRU_CONV_LEFT = 2
LRU_C = 8.0

RWKV_HEADS = 4
RWKV_HEAD = GROUP // RWKV_HEADS
RWKV_DECAY_LORA = 32
RWKV_AAA_LORA = 32
RWKV_GATE_LORA = 64
RWKV_GN_EPS = 64e-5

HY_ORDER = 2
HY_SHORT = 3
HY_BANDS = 16
HY_EMB = 1 + 2 * HY_BANDS
HY_HIDDEN = 64
HY_SIN_FREQ = 1.0
HY_DECAY_MIN = math.log(1e-2) / 1.5
HY_DECAY_MAX = math.log(1e-2) / 0.3
HY_SHIFT = 0.05

N_EXPERTS = 8
TOP_K = 2
MOE_BLOCK = 512

ALPHA = (2.0 * DEPTH) ** 0.25

A_CQ = 0
A_CKV = A_CQ + MLA_Q_RANK
A_KR = A_CKV + MLA_KV_RANK
B_X = A_KR + MLA_ROPE
B_GATE = B_X + GROUP
C_OFF = B_GATE + GROUP
C_R = 0
C_K = GROUP
C_V = 2 * GROUP
C_WD = 3 * GROUP
C_AD = C_WD + 2 * RWKV_DECAY_LORA
C_GD = C_AD + 2 * RWKV_AAA_LORA
C_COLS = C_GD + RWKV_GATE_LORA
D_OFF = C_OFF + C_COLS
D_COLS = (HY_ORDER + 1) * GROUP


RWKV_TIME_BLOCK = 16
V_TILES = RWKV_HEAD // SUBLANES
RWKV_KEY_GROUP = 16


def _rwkv_scan_kernel(rf, rb, wf, wb, kkf, kkb, kaf, kab, vf, vb, kdf, kdb, yf_ref, yb_ref, s_ref, m_ref):
    @pl.when(pl.program_id(0) == 0)
    def _():
        s_ref[...] = jnp.zeros_like(s_ref)

    n_t = rf.shape[0]
    p = rf.shape[2]
    fwd_lanes = lax.broadcasted_iota(jnp.int32, (RWKV_HEAD, p), 1) < p // 2

    for q, (f_ref, b_ref) in enumerate(((rf, rb), (wf, wb), (kkf, kkb), (kaf, kab), (vf, vb), (kdf, kdb))):
        for j in range(n_t):
            m_ref[q, j] = jnp.where(fwd_lanes, f_ref[j], b_ref[n_t - 1 - j])

    r_ref, w_ref, kk_ref, kka_ref, v_ref, kd_ref = (m_ref.at[q] for q in range(6))

    def step(t, carry):
        def row(ref, k):
            return jnp.broadcast_to(ref[t, pl.ds(k, 1), :], (SUBLANES, p))[None]

        def sa_group(g, sa):
            k0 = pl.multiple_of(g * RWKV_KEY_GROUP, RWKV_KEY_GROUP)
            for j in range(RWKV_KEY_GROUP):
                sa = sa + s_ref[k0 + j] * row(kk_ref, k0 + j)
            return sa

        zero = jnp.zeros((V_TILES, SUBLANES, p), F32)
        sa = lax.fori_loop(0, RWKV_HEAD // RWKV_KEY_GROUP, sa_group, zero)
        vt = v_ref[t].reshape(V_TILES, SUBLANES, p)

        def update_group(g, y):
            k0 = pl.multiple_of(g * RWKV_KEY_GROUP, RWKV_KEY_GROUP)
            for j in range(RWKV_KEY_GROUP):
                k = k0 + j
                sn = s_ref[k] * row(w_ref, k) - sa * row(kka_ref, k) + vt * row(kd_ref, k)
                s_ref[k] = sn
                y = y + sn * row(r_ref, k)
            return y

        y = lax.fori_loop(0, RWKV_HEAD // RWKV_KEY_GROUP, update_group, zero).reshape(RWKV_HEAD, p)
        yf_ref[t] = y
        yb_ref[n_t - 1 - t] = y
        return carry

    lax.fori_loop(0, n_t, step, 0)


def rwkv_scan_pallas(r, kk, v, w, kka, kd, n_ctx):
    n_steps, n, p = r.shape
    tb = RWKV_TIME_BLOCK
    assert n == RWKV_HEAD and p == LANES and n_steps % tb == 0 and n_ctx % tb == 0
    nb, nc = n_steps // tb, n_ctx // tb
    fwd = pl.BlockSpec((tb, n, p), lambda g: (g, 0, 0))
    bwd = pl.BlockSpec((tb, n, p), lambda g: (jnp.where(g < nc, nc - 1 - g, nb + nc - 1 - g), 0, 0))
    out = jax.ShapeDtypeStruct((n_steps, n, p), F32)
    return pl.pallas_call(
        _rwkv_scan_kernel,
        grid=(nb,),
        in_specs=[fwd, bwd] * 6,
        out_specs=[fwd, bwd],
        out_shape=[out, out],
        scratch_shapes=[pltpu.VMEM((n, V_TILES, SUBLANES, LANES), F32), pltpu.VMEM((6, tb, n, LANES), F32)],
        compiler_params=pltpu.CompilerParams(dimension_semantics=("arbitrary",), vmem_limit_bytes=VMEM_LIMIT_BYTES),
        name="rwkv_scan",
    )(r, r, w[0], w[1], kk, kk, kka[0], kka[1], v, v, kd[0], kd[1])


ATTN_TQ = 256
ATTN_TK = 256


def _attn_kernel(q_ref, k_ref, v_ref, g_ref, o_ref, s_ref):
    tq = q_ref.shape[1]
    n_chunks = k_ref.shape[1]
    n_tiles = ATTN_TK // LANES
    pair_out = []
    for hp in range(MLA_HEADS // 2):
        o_pair = jnp.zeros((tq, LANES), F32)
        for h in (2 * hp, 2 * hp + 1):
            qh = q_ref[0, :, pl.ds(LANES * h, LANES)]

            m_acc = jnp.full((tq, LANES), -jnp.inf, F32)
            for c in range(n_chunks):
                s = jnp.dot(qh, k_ref[0, c, pl.ds(LANES * h, LANES), :], preferred_element_type=F32)
                s_ref[c] = s
                for j in range(n_tiles):
                    m_acc = jnp.maximum(m_acc, s[:, LANES * j:LANES * (j + 1)])
            m_full = jnp.broadcast_to(jnp.max(m_acc, -1, keepdims=True), (tq, LANES))

            l_acc = jnp.zeros((tq, LANES), F32)
            acc = jnp.zeros((tq, LANES), F32)
            for c in range(n_chunks):
                s = s_ref[c]
                ps = []
                for j in range(n_tiles):
                    p = jnp.exp2(s[:, LANES * j:LANES * (j + 1)] - m_full)
                    l_acc = l_acc + p
                    ps.append(p.astype(BF16))
                vh = v_ref[0, pl.ds(c * ATTN_TK, ATTN_TK), pl.ds(LANES * h, LANES)]
                acc = acc + jnp.dot(jnp.concatenate(ps, -1), vh, preferred_element_type=F32)
            o_pair = o_pair + acc / jnp.sum(l_acc, -1, keepdims=True)
        pair_out.append(o_pair)
    o = jnp.concatenate(pair_out, -1)
    o_ref[0] = o * lax.rsqrt(jnp.mean(jnp.square(o), -1, keepdims=True) + 1e-6) * g_ref[...]


def mla_attention_pallas(q, kt, v, out_norm, q_start, n_q, n_kv):
    B, _, W = q.shape
    return pl.pallas_call(
        _attn_kernel,
        grid=(B, n_q),
        in_specs=[
            pl.BlockSpec((1, ATTN_TQ, W), lambda b, i: (b, i + q_start, 0)),
            pl.BlockSpec((1, n_kv, W, ATTN_TK), lambda b, i: (b, 0, 0, 0)),
            pl.BlockSpec((1, n_kv * ATTN_TK, W), lambda b, i: (b, 0, 0)),
            pl.BlockSpec((1, GROUP), lambda b, i: (0, 0)),
        ],
        out_specs=pl.BlockSpec((1, ATTN_TQ, GROUP), lambda b, i: (b, i, 0)),
        out_shape=jax.ShapeDtypeStruct((B, n_q * ATTN_TQ, GROUP), F32),
        scratch_shapes=[pltpu.VMEM((n_kv, ATTN_TQ, ATTN_TK), F32)],
        compiler_params=pltpu.CompilerParams(dimension_semantics=("arbitrary", "arbitrary"),
                                             vmem_limit_bytes=VMEM_LIMIT_BYTES),
        name="mla_attention",
    )(q, kt, v, out_norm[None])


MLA_PA_COLS = 4 * LANES
MLA_W = MLA_HEADS * LANES


def _rot_cols(w):
    h = w.shape[1] // 2
    return jnp.concatenate([-w[:, h:], w[:, :h]], 1)


def mla_section_weight(w_a):
    z = jnp.zeros((w_a.shape[0], LANES // 2), w_a.dtype)
    kr = w_a[:, A_KR:A_KR + MLA_ROPE]
    return jnp.concatenate([w_a[:, A_CKV:A_CKV + MLA_KV_RANK], kr, _rot_cols(kr), z,
                            w_a[:, A_CQ:A_CQ + MLA_Q_RANK], z], 1)


def mla_prep_weights(w_uq, w_ukv):
    H, DN, DR, DV = MLA_HEADS, MLA_NOPE, MLA_ROPE, MLA_V
    wq = jnp.zeros((2 * LANES, MLA_W), F32)
    wq_rot = jnp.zeros((2 * LANES, MLA_W), F32)
    wk = jnp.zeros((2 * LANES, MLA_W), F32)
    wv = jnp.zeros((LANES, MLA_W), F32)
    place = jnp.eye(DR, dtype=F32)
    for h in range(H):
        q_h = w_uq[:, h * (DN + DR):(h + 1) * (DN + DR)]
        wq = wq.at[:MLA_Q_RANK, LANES * h:LANES * h + DN + DR].set(q_h)
        wq_rot = wq_rot.at[:MLA_Q_RANK, LANES * h + DN:LANES * h + DN + DR].set(_rot_cols(q_h[:, DN:]))
        kv_h = w_ukv[:, h * (DN + DV):(h + 1) * (DN + DV)]
        wk = wk.at[:MLA_KV_RANK, LANES * h:LANES * h + DN].set(kv_h[:, :DN])
        wk = wk.at[LANES:LANES + DR, LANES * h + DN:LANES * h + DN + DR].set(place)
        v0 = LANES * h + DV * (h % 2)
        wv = wv.at[:, v0:v0 + DV].set(kv_h[:, DN:])
    return wq.astype(BF16), wq_rot.astype(BF16), wk.astype(BF16), wv.astype(BF16)


def mla_tables(cos, sin, n_ctx):
    H, DN, DR = MLA_HEADS, MLA_NOPE, MLA_ROPE
    L = cos.shape[0]
    cf = jnp.concatenate([jnp.ones((n_ctx, DR), F32), jnp.concatenate([cos, cos], -1)], 0)
    sf = jnp.concatenate([jnp.zeros((n_ctx, DR), F32), jnp.concatenate([sin, sin], -1)], 0)
    S = n_ctx + L
    tab_k = jnp.concatenate([cf, sf, jnp.zeros((S, LANES - 2 * DR), F32)], -1)
    scale = (DN + DR) ** -0.5 * math.log2(math.e)
    zpad = jnp.zeros((S, LANES - DN - DR), F32)
    qc_h = jnp.concatenate([jnp.full((S, DN), scale, F32), cf * scale, zpad], -1)
    qs_h = jnp.concatenate([jnp.zeros((S, DN), F32), sf * scale, zpad], -1)
    return tab_k, jnp.tile(qc_h, (1, H)), jnp.tile(qs_h, (1, H))


def _mla_prep_kernel(pa_ref, tk_ref, tqc_ref, tqs_ref, kvn_ref, qn_ref, wq_ref, wqr_ref, wk_ref, wv_ref,
                     q_ref, kt_ref, v_ref):
    pa = pa_ref[0]
    ckv = pa[:, :LANES]
    ckv = ckv * lax.rsqrt(jnp.mean(jnp.square(ckv), -1, keepdims=True) + 1e-6) * kvn_ref[...]
    t = pa[:, LANES:2 * LANES] * tk_ref[...]
    kr = t + pltpu.roll(t, LANES - MLA_ROPE, 1)
    kr = jnp.where(lax.broadcasted_iota(jnp.int32, kr.shape, 1) < MLA_ROPE, kr, 0.0)
    k = jnp.dot(jnp.concatenate([ckv, kr], -1).astype(BF16), wk_ref[...], preferred_element_type=F32)
    kt_ref[0, 0] = k.T.astype(BF16)
    v_ref[0] = jnp.dot(ckv.astype(BF16), wv_ref[...], preferred_element_type=F32).astype(BF16)
    cq = pa[:, 2 * LANES:]
    ms = jnp.sum(jnp.square(cq), -1, keepdims=True) * (1.0 / MLA_Q_RANK)
    cq = (cq * lax.rsqrt(ms + 1e-6) * qn_ref[...]).astype(BF16)
    qa = jnp.dot(cq, wq_ref[...], preferred_element_type=F32)
    qb = jnp.dot(cq, wqr_ref[...], preferred_element_type=F32)
    q_ref[0] = (qa * tqc_ref[...] + qb * tqs_ref[...]).astype(BF16)


def mla_prep_pallas(pa, tables, q_norm, kv_norm, weights):
    B, S, _ = pa.shape
    T = ATTN_TK
    assert S % T == 0
    tab_k, tab_qc, tab_qs = tables
    wq, wq_rot, wk, wv = weights
    qn = jnp.concatenate([q_norm, jnp.zeros((2 * LANES - MLA_Q_RANK,), F32)])[None]

    def full(a):
        return pl.BlockSpec(a.shape, lambda b, i: (0,) * a.ndim)

    def rows(width):
        return pl.BlockSpec((T, width), lambda b, i: (i, 0))

    return pl.pallas_call(
        _mla_prep_kernel,
        grid=(B, S // T),
        in_specs=[pl.BlockSpec((1, T, MLA_PA_COLS), lambda b, i: (b, i, 0)),
                  rows(LANES), rows(MLA_W), rows(MLA_W),
                  pl.BlockSpec((1, LANES), lambda b, i: (0, 0)), full(qn), full(wq), full(wq_rot), full(wk), full(wv)],
        out_specs=[pl.BlockSpec((1, T, MLA_W), lambda b, i: (b, i, 0)),
                   pl.BlockSpec((1, 1, MLA_W, T), lambda b, i: (b, i, 0, 0)),
                   pl.BlockSpec((1, T, MLA_W), lambda b, i: (b, i, 0))],
        out_shape=[jax.ShapeDtypeStruct((B, S, MLA_W), BF16),
                   jax.ShapeDtypeStruct((B, S // T, MLA_W, T), BF16),
                   jax.ShapeDtypeStruct((B, S, MLA_W), BF16)],
        compiler_params=pltpu.CompilerParams(dimension_semantics=("arbitrary", "arbitrary")),
        name="mla_prep",
    )(pa, tab_k, tab_qc, tab_qs, kv_norm[None], qn, wq, wq_rot, wk, wv)


LRU_CHUNK = 256
LRU_HALO = SUBLANES


def _lru_kernel(xl_ref, xc_ref, wbd_ref, bias_ref, c8_ref, cw_ref, cb_ref, gn_ref,
                yl_ref, yc_ref, xs_l, xs_c, hf_l, hf_c, a_s, b_s, hb_s):
    C = GROUP
    CH = LRU_CHUNK
    L = xl_ref.shape[1]
    Lc = xc_ref.shape[1]

    def stage(x_ref, xs, n):
        xs[pl.ds(0, LRU_HALO), :] = jnp.zeros((LRU_HALO, C), F32)
        xs[pl.ds(LRU_HALO + n, LRU_HALO), :] = jnp.zeros((LRU_HALO, C), F32)

        def cp(i, c):
            r0 = pl.multiple_of(i * CH, CH)
            xs[pl.ds(LRU_HALO + r0, CH), :] = x_ref[0, pl.ds(r0, CH), pl.ds(0, C)]
            return c

        lax.fori_loop(0, n // CH, cp, 0)

    stage(xl_ref, xs_l, L)
    stage(xc_ref, xs_c, Lc)

    def coeffs(xs, base, d):
        xv = xs[pl.ds(base, CH + 2 * LRU_HALO), :]
        u = cb_ref[...]
        for j in range(LRU_CONV):
            o = LRU_HALO - LRU_CONV_LEFT + j
            u = u + xv[o:o + CH] * cw_ref[pl.ds(j, 1), :]
        z = jnp.dot(u.astype(BF16), wbd_ref[:, pl.ds(d * 2 * C, 2 * C)], preferred_element_type=F32)
        z = z + bias_ref[:, pl.ds(d * 2 * C, 2 * C)]
        r = jax.nn.sigmoid(z[:, :C])
        i = jax.nn.sigmoid(z[:, C:])
        log_a = r * c8_ref[pl.ds(d, 1), :]
        a = jnp.exp(log_a)
        a_s[...] = a
        b_s[...] = jnp.sqrt(-jnp.tanh(log_a) * (a * a + 1.0)) * (i * u)

    def row_scan(h, out_ref, out_base, reverse):
        def body(t, h):
            tt = CH - 1 - t if reverse else t
            a_t = jnp.broadcast_to(a_s[pl.ds(tt, 1), :], (SUBLANES, C))
            b_t = jnp.broadcast_to(b_s[pl.ds(tt, 1), :], (SUBLANES, C))
            h = a_t * h + b_t
            out_ref[pl.ds(out_base + tt, 1), :] = h[0:1, :]
            return h

        return lax.fori_loop(0, CH, body, h, unroll=8)

    h0 = jnp.zeros((SUBLANES, C), F32)

    h = h0
    for ci in range(Lc // CH):
        coeffs(xs_c, ci * CH, 0)
        h = row_scan(h, hf_c, ci * CH, False)

    def fwd_chunk(ci, h):
        base = pl.multiple_of(ci * CH, CH)
        coeffs(xs_l, base, 0)
        return row_scan(h, hf_l, base, False)

    lax.fori_loop(0, L // CH, fwd_chunk, h)

    def combine(x_ref, hf, base, y_ref):
        hl = hf[pl.ds(base, CH), :] + hb_s[...]
        g = jax.nn.gelu(x_ref[0, pl.ds(base, CH), pl.ds(C, C)])
        v = hl * g
        y = v * lax.rsqrt(jnp.mean(jnp.square(v), -1, keepdims=True) + 1e-6) * gn_ref[...]
        y_ref[0, pl.ds(base, CH), :] = y

    h = h0
    for ci in reversed(range(Lc // CH)):
        coeffs(xs_c, ci * CH, 1)
        h = row_scan(h, hb_s, 0, True)
        combine(xc_ref, hf_c, ci * CH, yc_ref)

    def bwd_chunk(k, h):
        base = pl.multiple_of((L // CH - 1 - k) * CH, CH)
        coeffs(xs_l, base, 1)
        h = row_scan(h, hb_s, 0, True)
        combine(xl_ref, hf_l, base, yl_ref)
        return h

    lax.fori_loop(0, L // CH, bwd_chunk, h)


def rglru_pallas(xg_l, xg_c, conv_w, conv_b, w_r, b_r, w_i, b_i, lam, out_norm):
    B, L, _ = xg_l.shape
    Lc = xg_c.shape[1]
    C = GROUP
    assert L % LRU_CHUNK == 0 and Lc % LRU_CHUNK == 0

    def bd(w):
        return jax.scipy.linalg.block_diag(*[w[n] for n in range(LRU_BLOCKS)])

    wbd = jnp.concatenate([bd(w_r[0]), bd(w_i[0]), bd(w_r[1]), bd(w_i[1])], 1).astype(BF16)
    bias = jnp.concatenate([b_r[0], b_i[0], b_r[1], b_i[1]])[None]
    c8 = -LRU_C * jax.nn.softplus(-lam)

    def full(shape):
        return pl.BlockSpec(shape, lambda b: (0,) * len(shape))

    return pl.pallas_call(
        _lru_kernel,
        grid=(B,),
        in_specs=[
            pl.BlockSpec((1, L, 2 * C), lambda b: (b, 0, 0)),
            pl.BlockSpec((1, Lc, 2 * C), lambda b: (b, 0, 0)),
            full((C, 4 * C)), full((1, 4 * C)), full((2, C)), full((LRU_CONV, C)), full((1, C)), full((1, C)),
        ],
        out_specs=[
            pl.BlockSpec((1, L, C), lambda b: (b, 0, 0)),
            pl.BlockSpec((1, Lc, C), lambda b: (b, 0, 0)),
        ],
        out_shape=[jax.ShapeDtypeStruct((B, L, C), F32), jax.ShapeDtypeStruct((B, Lc, C), F32)],
        scratch_shapes=[
            pltpu.VMEM((L + 2 * LRU_HALO, C), F32),
            pltpu.VMEM((Lc + 2 * LRU_HALO, C), F32),
            pltpu.VMEM((L, C), F32),
            pltpu.VMEM((Lc, C), F32),
            pltpu.VMEM((LRU_CHUNK, C), F32),
            pltpu.VMEM((LRU_CHUNK, C), F32),
            pltpu.VMEM((LRU_CHUNK, C), F32),
        ],
        compiler_params=pltpu.CompilerParams(dimension_semantics=("arbitrary",), vmem_limit_bytes=VMEM_LIMIT_BYTES),
        name="rglru",
    )(xg_l, xg_c, wbd, bias, c8, conv_w, conv_b[None], out_norm[None])


HY_T = 256
HY_CB = 8


def _hyena_kernel(k_ref, d_ref, v_ref, x1_ref, x2_ref, o_ref, u_s, acc_s):
    n_rows = v_ref.shape[1]
    T = HY_T
    nb = k_ref.shape[2] // (2 * T)
    bsz = n_rows // nb

    def conv(ci, order):
        acc_s[...] = jnp.zeros_like(acc_s)
        for dd in range(-(nb - 1), nb):
            w2 = k_ref[order, pl.ds(ci, 1), pl.ds(T * (dd + nb - 1), 2 * T)]
            x = jnp.broadcast_to(w2, (T, 2 * T))
            r = pltpu.roll(x, 0, 1, stride=1, stride_axis=0)
            tb = r[:, T:].astype(BF16)
            j0, j1 = max(0, -dd), min(nb, nb - dd)
            lhs = u_s[pl.ds(bsz * j0, bsz * (j1 - j0)), :]
            dst = pl.ds(bsz * (j0 + dd), bsz * (j1 - j0))
            acc_s[dst, :] = acc_s[dst, :] + jnp.dot(lhs, tb, preferred_element_type=F32)

    def channel(ci, carry):
        v = v_ref[ci]
        u_s[...] = v.astype(BF16)
        conv(ci, 0)
        u = x1_ref[ci] * (acc_s[...] + v * d_ref[pl.ds(ci, 1), pl.ds(0, 1)])
        u_s[...] = u.astype(BF16)
        conv(ci, 1)
        o_ref[ci] = x2_ref[ci] * (acc_s[...] + u * d_ref[pl.ds(ci, 1), pl.ds(1, 1)])
        return carry

    lax.fori_loop(0, HY_CB, channel, 0)


def hyena_conv_pallas(kfull, d_skip, vT, x1T, x2T):
    C, R, T = vT.shape
    two_l = kfull.shape[2]
    assert T == HY_T and C % HY_CB == 0
    blk = pl.BlockSpec((HY_CB, R, T), lambda c: (c, 0, 0))
    return pl.pallas_call(
        _hyena_kernel,
        grid=(C // HY_CB,),
        in_specs=[
            pl.BlockSpec((HY_ORDER, HY_CB, two_l), lambda c: (0, c, 0)),
            pl.BlockSpec((HY_CB, HY_ORDER), lambda c: (c, 0)),
            blk, blk, blk,
        ],
        out_specs=blk,
        out_shape=jax.ShapeDtypeStruct((C, R, T), F32),
        scratch_shapes=[pltpu.VMEM((R, T), BF16), pltpu.VMEM((R, T), F32)],
        compiler_params=pltpu.CompilerParams(dimension_semantics=("arbitrary",), vmem_limit_bytes=VMEM_LIMIT_BYTES),
        name="hyena_conv",
    )(kfull, d_skip.T, vT, x1T, x2T)


FFN_TF_MAX = 1408


def _ffn_tile(hidden):
    return max(t for t in range(LANES, FFN_TF_MAX + 1, LANES) if hidden % t == 0)


def _swiglu_kernel(be_ref, nu_ref, x_ref, wg_ref, wu_ref, wd_ref, o_ref, acc_ref):
    i = pl.program_id(0)
    f = pl.program_id(1)

    @pl.when(i < nu_ref[0])
    def _():
        x = x_ref[...]
        g = jnp.dot(x, wg_ref[0], preferred_element_type=F32)
        u = jnp.dot(x, wu_ref[0], preferred_element_type=F32)
        h = (jax.nn.silu(g) * u).astype(BF16)
        part = jnp.dot(h, wd_ref[0], preferred_element_type=F32)

        @pl.when(f == 0)
        def _():
            acc_ref[...] = part

        @pl.when(f > 0)
        def _():
            acc_ref[...] = acc_ref[...] + part

    @pl.when(f == pl.num_programs(1) - 1)
    def _():
        o_ref[...] = jnp.where(i < nu_ref[0], acc_ref[...], 0.0)


def grouped_swiglu_pallas(xs, block_expert, n_used, wg, wu, wd):
    n_rows, D = xs.shape
    F = wg.shape[2]
    TM = MOE_BLOCK
    assert n_rows % TM == 0
    tf = _ffn_tile(F)
    n_blocks = n_rows // TM
    grid_spec = pltpu.PrefetchScalarGridSpec(
        num_scalar_prefetch=2,
        grid=(n_blocks, F // tf),
        in_specs=[
            pl.BlockSpec((TM, D), lambda i, f, be, nu: (i, 0)),
            pl.BlockSpec((1, D, tf), lambda i, f, be, nu: (be[i], 0, f)),
            pl.BlockSpec((1, D, tf), lambda i, f, be, nu: (be[i], 0, f)),
            pl.BlockSpec((1, tf, D), lambda i, f, be, nu: (be[i], f, 0)),
        ],
        out_specs=pl.BlockSpec((TM, D), lambda i, f, be, nu: (i, 0)),
        scratch_shapes=[pltpu.VMEM((TM, D), F32)],
    )
    return pl.pallas_call(
        _swiglu_kernel,
        grid_spec=grid_spec,
        out_shape=jax.ShapeDtypeStruct((n_rows, D), F32),
        compiler_params=pltpu.CompilerParams(dimension_semantics=("arbitrary", "arbitrary"),
                                             vmem_limit_bytes=VMEM_LIMIT_BYTES),
        name="grouped_swiglu",
    )(block_expert.astype(jnp.int32), jnp.reshape(n_used, (1,)).astype(jnp.int32), xs, wg, wu, wd)


def modulate(x, shift, scale):
    return x * (1.0 + scale) + shift


def layer_norm(x, g, b, eps=1e-5):
    mu = jnp.mean(x, -1, keepdims=True)
    var = jnp.mean(jnp.square(x - mu), -1, keepdims=True)
    return (x - mu) * lax.rsqrt(var + eps) * g + b


def rms_norm(x, g, eps=1e-6):
    return x * lax.rsqrt(jnp.mean(jnp.square(x), -1, keepdims=True) + eps) * g


def dwconv(x, w, b, left):
    K = w.shape[0]
    L = x.shape[1]
    xp = jnp.pad(x, ((0, 0), (left, K - 1 - left), (0, 0)))
    return sum(xp[:, j:j + L] * w[j] for j in range(K)) + b


def token_shift(z, mu_prev, mu_next):
    zp = jnp.pad(z, ((0, 0), (1, 1), (0, 0)))
    return z + mu_prev * (zp[:, :-2] - z) + mu_next * (zp[:, 2:] - z)


def axial_rope(rows):
    r, col = jnp.meshgrid(jnp.arange(rows, dtype=F32), jnp.arange(GRID_W, dtype=F32), indexing='ij')
    half = MLA_ROPE // 2
    inv = 1.0 / (ROPE_BASE ** (jnp.arange(0, half, 2, dtype=F32) / half))
    ang = jnp.concatenate([r.reshape(-1, 1) * inv, col.reshape(-1, 1) * inv], -1)
    return jnp.cos(ang), jnp.sin(ang)


def mla_mixer(Pac, Pal, tables, q_norm, kv_norm, w_uq, w_ukv, out_norm, ctx_out):
    Lc, L = Pac.shape[1], Pal.shape[1]
    q, kt, v = mla_prep_pallas(jnp.concatenate([Pac, Pal], 1), tables, q_norm, kv_norm, mla_prep_weights(w_uq, w_ukv))
    nc, nl = Lc // ATTN_TQ, L // ATTN_TQ
    yl = mla_attention_pallas(q, kt, v, out_norm, nc, nl, nc + nl)
    yc = mla_attention_pallas(q, kt, v, out_norm, 0, nc, nc) if ctx_out else None
    return yl, yc


def rwkv_finish(y, bonus, v, gd, g_up, ln_g, ln_b):
    B, L = y.shape[:2]
    mu = jnp.mean(y, -1, keepdims=True)
    var = jnp.mean(jnp.square(y - mu), -1, keepdims=True)
    yn = (y - mu) * lax.rsqrt(var + RWKV_GN_EPS) * ln_g.reshape(RWKV_HEADS, RWKV_HEAD) + ln_b.reshape(RWKV_HEADS, RWKV_HEAD)
    g = jax.nn.sigmoid(gd) @ g_up
    return (yn + bonus * v).reshape(B, L, GROUP) * g


def rwkv_mixer(Pl, Pc, mu_prev, mu_next, w0, w_up, a0, a_up, g_up, k_k, k_a, r_k, ln_g, ln_b, ctx_out):
    B, L = Pl.shape[:2]
    Lc = Pc.shape[1]
    S = Lc + L
    H, N = RWKV_HEADS, RWKV_HEAD
    z = jnp.concatenate([token_shift(Pc, mu_prev, mu_next), token_shift(Pl, mu_prev, mu_next)], 1)

    def heads(t):
        return t.reshape(B, S, H, N)

    r = heads(z[..., C_R:C_R + GROUP])
    k = heads(z[..., C_K:C_K + GROUP])
    v = heads(z[..., C_V:C_V + GROUP])
    kk = k * k_k.reshape(H, N)
    kk = kk * lax.rsqrt(jnp.maximum(jnp.sum(jnp.square(kk), -1, keepdims=True), 1e-24))
    w, kka, kd = [], [], []
    for d in range(2):
        wd = z[..., C_WD + d * RWKV_DECAY_LORA:C_WD + (d + 1) * RWKV_DECAY_LORA]
        ad = z[..., C_AD + d * RWKV_AAA_LORA:C_AD + (d + 1) * RWKV_AAA_LORA]
        log_w = -jnp.exp(-jax.nn.softplus(-(w0[d] + jnp.tanh(wd) @ w_up[d])) - 0.5)
        a = heads(jax.nn.sigmoid(a0[d] + ad @ a_up[d]))
        w.append(heads(jnp.exp(log_w)))
        kka.append(kk * a)
        kd.append(k * (1.0 + (a - 1.0) * k_a.reshape(H, N)))

    def to_scan(t):
        t = jnp.transpose(t, (1, 3, 0, 2)).reshape(S, N, B * H)
        return jnp.concatenate([t, t], -1)

    yf, yb = rwkv_scan_pallas(to_scan(r), to_scan(kk), to_scan(v), [to_scan(t) for t in w],
                              [to_scan(t) for t in kka], [to_scan(t) for t in kd], Lc)
    y = yf[..., :B * H] + yb[..., B * H:]
    y = jnp.transpose(y.reshape(S, N, B, H), (2, 0, 3, 1))
    bonus = sum(jnp.sum(r * kd_d * r_k, -1, keepdims=True) for kd_d in kd)
    gd = z[..., C_GD:C_GD + RWKV_GATE_LORA]
    out_l = rwkv_finish(y[:, Lc:], bonus[:, Lc:], v[:, Lc:], gd[:, Lc:], g_up, ln_g, ln_b)
    out_c = None
    if ctx_out:
        out_c = rwkv_finish(y[:, :Lc], bonus[:, :Lc], v[:, :Lc], gd[:, :Lc], g_up, ln_g, ln_b)
    return out_l, out_c


def hyena_filters(L, w1, b1, w2, b2, w3):
    t01 = jnp.linspace(0.0, 1.0, L, dtype=F32)[:, None]
    bands = jnp.linspace(1e-4, HY_BANDS - 1, HY_BANDS, dtype=F32)[None, :]
    wpos = (2.0 * math.pi / L) * jnp.arange(L, dtype=F32)[:, None]
    z = jnp.concatenate([t01, jnp.cos(bands * wpos), -jnp.sin(bands * wpos)], -1)
    h = jnp.sin(HY_SIN_FREQ * (z @ w1 + b1))
    h = jnp.sin(HY_SIN_FREQ * (h @ w2 + b2))
    h = (h @ w3).reshape(L, HY_ORDER, 2, GROUP)
    deltas = jnp.abs(jnp.linspace(HY_DECAY_MIN, HY_DECAY_MAX, GROUP, dtype=F32))
    window = jnp.exp(-t01 * deltas) + HY_SHIFT
    return h * window[:, None, None, :]


def hyena_sequence(Pd, conv_w, conv_b, w1, b1, w2, b2, w3, d_skip):
    B, L, _ = Pd.shape
    C = GROUP
    T = HY_T
    nb = L // T
    z = dwconv(Pd, conv_w, conv_b, 1)
    zT = jnp.transpose(z.reshape(B, nb, T, 3 * C), (3, 1, 0, 2)).reshape(3 * C, nb * B, T)
    h = hyena_filters(L, w1, b1, w2, b2, w3)
    hf = jnp.transpose(h[:, :, 0], (1, 2, 0))
    hb = jnp.transpose(h[:, :, 1], (1, 2, 0))
    kfull = jnp.concatenate([jnp.zeros((HY_ORDER, C, 1), F32), jnp.flip(hb[..., 1:], -1), hf], -1)
    oT = hyena_conv_pallas(kfull, d_skip, zT[:C], zT[C:2 * C], zT[2 * C:])
    return jnp.transpose(oT.reshape(C, nb, B, T), (2, 1, 3, 0)).reshape(B, L, C)


def swiglu(x, wg, wu, wd):
    n_blocks = x.shape[0] // MOE_BLOCK
    return grouped_swiglu_pallas(x.astype(BF16), jnp.zeros((n_blocks,), jnp.int32), jnp.int32(n_blocks),
                                 wg[None].astype(BF16), wu[None].astype(BF16), wd[None].astype(BF16))


def moe_swiglu(x, router, wg, wu, wd):
    N, D = x.shape
    logits = x @ router
    top_v, top_i = lax.top_k(logits, TOP_K)
    gates = jax.nn.softmax(top_v, axis=-1)
    A = N * TOP_K
    e_flat = top_i.reshape(-1)
    tok_flat = jnp.arange(A, dtype=jnp.int32) // TOP_K
    order = jnp.argsort(e_flat)
    e_sorted = e_flat[order]
    counts = jnp.bincount(e_flat, length=N_EXPERTS)
    starts = jnp.cumsum(counts) - counts
    padded = (counts + MOE_BLOCK - 1) // MOE_BLOCK * MOE_BLOCK
    pends = jnp.cumsum(padded)
    pstarts = pends - padded
    dest = (pstarts[e_sorted] + jnp.arange(A, dtype=jnp.int32) - starts[e_sorted]).astype(jnp.int32)
    n_blocks = -(-A // MOE_BLOCK) + N_EXPERTS
    n_slots = n_blocks * MOE_BLOCK
    block_expert = jnp.clip(jnp.searchsorted(pends, jnp.arange(n_blocks) * MOE_BLOCK, side='right'), 0, N_EXPERTS - 1)
    slot_e = jnp.repeat(block_expert, MOE_BLOCK)
    slot_pos = jnp.arange(n_slots, dtype=jnp.int32) - pstarts[slot_e].astype(jnp.int32)
    slot_src = jnp.clip(starts[slot_e].astype(jnp.int32) + slot_pos, 0, A - 1)
    slot_tok = jnp.where(slot_pos < counts[slot_e], tok_flat[order[slot_src]], N)
    xp = jnp.concatenate([x.astype(BF16), jnp.zeros((1, D), BF16)], 0)
    ys = grouped_swiglu_pallas(xp[slot_tok], block_expert, pends[-1] // MOE_BLOCK,
                               wg.astype(BF16), wu.astype(BF16), wd.astype(BF16))
    slot_of = dest[jnp.argsort(order)].reshape(N, TOP_K)
    return ys[slot_of[:, 0]] * gates[:, 0:1] + ys[slot_of[:, 1]] * gates[:, 1:2]


def kernel(x, c, ctx, c_ctx, ada_w, ada_b, w_in, mla_q_norm, mla_kv_norm, mla_w_uq, mla_w_ukv, mla_out_norm, lru_conv_w, lru_conv_b, lru_w_r, lru_b_r, lru_w_i, lru_b_i, lru_lambda, lru_out_norm, rwkv_mu_prev, rwkv_mu_next, rwkv_w0, rwkv_w_up, rwkv_a0, rwkv_a_up, rwkv_g_up, rwkv_k_k, rwkv_k_a, rwkv_r_k, rwkv_ln_g, rwkv_ln_b, hy_conv_w, hy_conv_b, hy_f_w1, hy_f_b1, hy_f_w2, hy_f_b2, hy_f_w3, hy_d, hy_out_norm, w_out, ln1_g, ln1_b, ln2_g, ln2_b, ffn_w_gate, ffn_w_up, ffn_w_down, moe_router, moe_w_gate, moe_w_up, moe_w_down):
    B, L, D = x.shape
    Lc = ctx.shape[1]
    rows = L // GRID_W
    mla_tabs = mla_tables(*axial_rope(rows), Lc)
    s_lat = jax.nn.silu(c)
    s_ctx = jax.nn.silu(c_ctx)
    xl, xc = x, ctx
    for li in range(DEPTH):
        ctx_out = li < DEPTH - 1
        mod_l = (s_lat @ ada_w[li] + ada_b[li]).reshape(B, 6, 1, D)
        mod_c = (s_ctx @ ada_w[li] + ada_b[li]).reshape(6, 1, D)

        hl = modulate(xl, mod_l[:, 0], mod_l[:, 1])
        hc = modulate(xc, mod_c[0], mod_c[1])
        w_in_l = w_in[li]
        w_secs = (mla_section_weight(w_in_l[:, :B_X]), w_in_l[:, B_X:C_OFF], w_in_l[:, C_OFF:D_OFF], w_in_l[:, D_OFF:])
        Pal, Pbl, Pcl, Pdl = (hl @ w for w in w_secs)
        Pac, Pbc, Pcc = (hc @ w for w in w_secs[:3])
        a_l, a_c = mla_mixer(Pac, Pal, mla_tabs, mla_q_norm[li], mla_kv_norm[li], mla_w_uq[li], mla_w_ukv[li],
                             mla_out_norm[li], ctx_out)
        b_l, b_c = rglru_pallas(Pbl, Pbc, lru_conv_w[li], lru_conv_b[li], lru_w_r[li], lru_b_r[li], lru_w_i[li],
                                lru_b_i[li], lru_lambda[li], lru_out_norm[li])
        c_l, c_c = rwkv_mixer(Pcl, Pcc, rwkv_mu_prev[li], rwkv_mu_next[li], rwkv_w0[li], rwkv_w_up[li], rwkv_a0[li],
                              rwkv_a_up[li], rwkv_g_up[li], rwkv_k_k[li], rwkv_k_a[li], rwkv_r_k[li],
                              rwkv_ln_g[li], rwkv_ln_b[li], ctx_out)
        d_l = rms_norm(hyena_sequence(Pdl, hy_conv_w[li], hy_conv_b[li], hy_f_w1[li], hy_f_b1[li], hy_f_w2[li],
                                      hy_f_b2[li], hy_f_w3[li], hy_d[li]), hy_out_norm[li])
        yl = jnp.concatenate([a_l, b_l, c_l, d_l], -1) @ w_out[li]
        xl = layer_norm(ALPHA * xl + mod_l[:, 2] * yl, ln1_g[li], ln1_b[li])
        if ctx_out:
            Pdc = hc @ w_secs[3]
            d_c = rms_norm(hyena_sequence(Pdc, hy_conv_w[li], hy_conv_b[li], hy_f_w1[li], hy_f_b1[li], hy_f_w2[li],
                                          hy_f_b2[li], hy_f_w3[li], hy_d[li]), hy_out_norm[li])
            yc = jnp.concatenate([a_c, b_c, c_c, d_c], -1) @ w_out[li]
            xc = layer_norm(ALPHA * xc + mod_c[2] * yc, ln1_g[li], ln1_b[li])

        fl = modulate(xl, mod_l[:, 3], mod_l[:, 4]).reshape(B * L, D)
        if ctx_out:
            fc = modulate(xc, mod_c[3], mod_c[4]).reshape(B * Lc, D)
            tokens = jnp.concatenate([fl, fc], 0)
        else:
            tokens = fl
        j = li // 2
        if li % 2 == 0:
            out = swiglu(tokens, ffn_w_gate[j], ffn_w_up[j], ffn_w_down[j])
        else:
            out = moe_swiglu(tokens, moe_router[j], moe_w_gate[j], moe_w_up[j], moe_w_down[j])
        xl = layer_norm(ALPHA * xl + mod_l[:, 5] * out[:B * L].reshape(B, L, D), ln2_g[li], ln2_b[li])
        if ctx_out:
            xc = layer_norm(ALPHA * xc + mod_c[5] * out[B * L:].reshape(B, Lc, D), ln2_g[li], ln2_b[li])
    return xl
```

```python
import math
from functools import partial

import jax
import jax.numpy as jnp
from jax import lax
from jax.experimental import pallas as pl
from jax.experimental.pallas import tpu as pltpu

F32 = jnp.float32
BF16 = jnp.bfloat16

SUBLANES = 8
LANES = 128
VMEM_LIMIT_BYTES = 48 * 1024 * 1024

D_MODEL = 1024
DEPTH = 4
GRID_W = 64
GROUP = D_MODEL // 4

MLA_HEADS = 4
MLA_NOPE = 64
MLA_ROPE = 32
MLA_V = 64
MLA_Q_RANK = 192
MLA_KV_RANK = 128
ROPE_BASE = 10000.0
Q_BLOCK = 128

LRU_BLOCKS = 4
LRU_CONV = 4
LRU_CONV_LEFT = 2
LRU_C = 8.0

RWKV_HEADS = 4
RWKV_HEAD = GROUP // RWKV_HEADS
RWKV_DECAY_LORA = 32
RWKV_AAA_LORA = 32
RWKV_GATE_LORA = 64
RWKV_GN_EPS = 64e-5

HY_ORDER = 2
HY_SHORT = 3
HY_BANDS = 16
HY_EMB = 1 + 2 * HY_BANDS
HY_HIDDEN = 64
HY_SIN_FREQ = 1.0
HY_DECAY_MIN = math.log(1e-2) / 1.5
HY_DECAY_MAX = math.log(1e-2) / 0.3
HY_SHIFT = 0.05

N_EXPERTS = 8
TOP_K = 2
MOE_BLOCK = 512

ALPHA = (2.0 * DEPTH) ** 0.25

A_CQ = 0
A_CKV = A_CQ + MLA_Q_RANK
A_KR = A_CKV + MLA_KV_RANK
B_X = A_KR + MLA_ROPE
B_GATE = B_X + GROUP
C_OFF = B_GATE + GROUP
C_R = 0
C_K = GROUP
C_V = 2 * GROUP
C_WD = 3 * GROUP
C_AD = C_WD + 2 * RWKV_DECAY_LORA
C_GD = C_AD + 2 * RWKV_AAA_LORA
C_COLS = C_GD + RWKV_GATE_LORA
D_OFF = C_OFF + C_COLS
D_COLS = (HY_ORDER + 1) * GROUP


RWKV_TIME_BLOCK = 16
V_TILES = RWKV_HEAD // SUBLANES
RWKV_KEY_GROUP = 16


def _rwkv_scan_kernel(rf, rb, wf, wb, kkf, kkb, kaf, kab, vf, vb, kdf, kdb, yf_ref, yb_ref, s_ref, m_ref):
    @pl.when(pl.program_id(0) == 0)
    def _():
        s_ref[...] = jnp.zeros_like(s_ref)

    n_t = rf.shape[0]
    p = rf.shape[2]
    fwd_lanes = lax.broadcasted_iota(jnp.int32, (RWKV_HEAD, p), 1) < p // 2

    for q, (f_ref, b_ref) in enumerate(((rf, rb), (wf, wb), (kkf, kkb), (kaf, kab), (vf, vb), (kdf, kdb))):
        for j in range(n_t):
            m_ref[q, j] = jnp.where(fwd_lanes, f_ref[j], b_ref[n_t - 1 - j])

    r_ref, w_ref, kk_ref, kka_ref, v_ref, kd_ref = (m_ref.at[q] for q in range(6))

    def step(t, carry):
        def row(ref, k):
            return jnp.broadcast_to(ref[t, pl.ds(k, 1), :], (SUBLANES, p))[None]

        def sa_group(g, sa):
            k0 = pl.multiple_of(g * RWKV_KEY_GROUP, RWKV_KEY_GROUP)
            for j in range(RWKV_KEY_GROUP):
                sa = sa + s_ref[k0 + j] * row(kk_ref, k0 + j)
            return sa

        zero = jnp.zeros((V_TILES, SUBLANES, p), F32)
        sa = lax.fori_loop(0, RWKV_HEAD // RWKV_KEY_GROUP, sa_group, zero)
        vt = v_ref[t].reshape(V_TILES, SUBLANES, p)

        def update_group(g, y):
            k0 = pl.multiple_of(g * RWKV_KEY_GROUP, RWKV_KEY_GROUP)
            for j in range(RWKV_KEY_GROUP):
                k = k0 + j
                sn = s_ref[k] * row(w_ref, k) - sa * row(kka_ref, k) + vt * row(kd_ref, k)
                s_ref[k] = sn
                y = y + sn * row(r_ref, k)
            return y

        y = lax.fori_loop(0, RWKV_HEAD // RWKV_KEY_GROUP, update_group, zero).reshape(RWKV_HEAD, p)
        yf_ref[t] = y
        yb_ref[n_t - 1 - t] = y
        return carry

    lax.fori_loop(0, n_t, step, 0)


def rwkv_scan_pallas(r, kk, v, w, kka, kd, n_ctx):
    n_steps, n, p = r.shape
    tb = RWKV_TIME_BLOCK
    assert n == RWKV_HEAD and p == LANES and n_steps % tb == 0 and n_ctx % tb == 0
    nb, nc = n_steps // tb, n_ctx // tb
    fwd = pl.BlockSpec((tb, n, p), lambda g: (g, 0, 0))
    bwd = pl.BlockSpec((tb, n, p), lambda g: (jnp.where(g < nc, nc - 1 - g, nb + nc - 1 - g), 0, 0))
    out = jax.ShapeDtypeStruct((n_steps, n, p), F32)
    return pl.pallas_call(
        _rwkv_scan_kernel,
        grid=(nb,),
        in_specs=[fwd, bwd] * 6,
        out_specs=[fwd, bwd],
        out_shape=[out, out],
        scratch_shapes=[pltpu.VMEM((n, V_TILES, SUBLANES, LANES), F32), pltpu.VMEM((6, tb, n, LANES), F32)],
        compiler_params=pltpu.CompilerParams(dimension_semantics=("arbitrary",), vmem_limit_bytes=VMEM_LIMIT_BYTES),
        name="rwkv_scan",
    )(r, r, w[0], w[1], kk, kk, kka[0], kka[1], v, v, kd[0], kd[1])


ATTN_TQ = 256
ATTN_TK = 256


def _attn_kernel(q_ref, k_ref, v_ref, g_ref, o_ref, s_ref):
    tq = q_ref.shape[1]
    n_chunks = k_ref.shape[1]
    n_tiles = ATTN_TK // LANES
    pair_out = []
    for hp in range(MLA_HEADS // 2):
        o_pair = jnp.zeros((tq, LANES), F32)
        for h in (2 * hp, 2 * hp + 1):
            qh = q_ref[0, :, pl.ds(LANES * h, LANES)]

            m_acc = jnp.full((tq, LANES), -jnp.inf, F32)
            for c in range(n_chunks):
                s = jnp.dot(qh, k_ref[0, c, pl.ds(LANES * h, LANES), :], preferred_element_type=F32)
                s_ref[c] = s
                for j in range(n_tiles):
                    m_acc = jnp.maximum(m_acc, s[:, LANES * j:LANES * (j + 1)])
            m_full = jnp.broadcast_to(jnp.max(m_acc, -1, keepdims=True), (tq, LANES))

            l_acc = jnp.zeros((tq, LANES), F32)
            acc = jnp.zeros((tq, LANES), F32)
            for c in range(n_chunks):
                s = s_ref[c]
                ps = []
                for j in range(n_tiles):
                    p = jnp.exp2(s[:, LANES * j:LANES * (j + 1)] - m_full)
                    l_acc = l_acc + p
                    ps.append(p.astype(BF16))
                vh = v_ref[0, pl.ds(c * ATTN_TK, ATTN_TK), pl.ds(LANES * h, LANES)]
                acc = acc + jnp.dot(jnp.concatenate(ps, -1), vh, preferred_element_type=F32)
            o_pair = o_pair + acc / jnp.sum(l_acc, -1, keepdims=True)
        pair_out.append(o_pair)
    o = jnp.concatenate(pair_out, -1)
    o_ref[0] = o * lax.rsqrt(jnp.mean(jnp.square(o), -1, keepdims=True) + 1e-6) * g_ref[...]


def mla_attention_pallas(q, kt, v, out_norm, q_start, n_q, n_kv):
    B, _, W = q.shape
    return pl.pallas_call(
        _attn_kernel,
        grid=(B, n_q),
        in_specs=[
            pl.BlockSpec((1, ATTN_TQ, W), lambda b, i: (b, i + q_start, 0)),
            pl.BlockSpec((1, n_kv, W, ATTN_TK), lambda b, i: (b, 0, 0, 0)),
            pl.BlockSpec((1, n_kv * ATTN_TK, W), lambda b, i: (b, 0, 0)),
            pl.BlockSpec((1, GROUP), lambda b, i: (0, 0)),
        ],
        out_specs=pl.BlockSpec((1, ATTN_TQ, GROUP), lambda b, i: (b, i, 0)),
        out_shape=jax.ShapeDtypeStruct((B, n_q * ATTN_TQ, GROUP), F32),
        scratch_shapes=[pltpu.VMEM((n_kv, ATTN_TQ, ATTN_TK), F32)],
        compiler_params=pltpu.CompilerParams(dimension_semantics=("arbitrary", "arbitrary"),
                                             vmem_limit_bytes=VMEM_LIMIT_BYTES),
        name="mla_attention",
    )(q, kt, v, out_norm[None])


MLA_PA_COLS = 4 * LANES
MLA_W = MLA_HEADS * LANES


def _rot_cols(w):
    h = w.shape[1] // 2
    return jnp.concatenate([-w[:, h:], w[:, :h]], 1)


def mla_section_weight(w_a):
    z = jnp.zeros((w_a.shape[0], LANES // 2), w_a.dtype)
    kr = w_a[:, A_KR:A_KR + MLA_ROPE]
    return jnp.concatenate([w_a[:, A_CKV:A_CKV + MLA_KV_RANK], kr, _rot_cols(kr), z,
                            w_a[:, A_CQ:A_CQ + MLA_Q_RANK], z], 1)


def mla_prep_weights(w_uq, w_ukv):
    H, DN, DR, DV = MLA_HEADS, MLA_NOPE, MLA_ROPE, MLA_V
    wq = jnp.zeros((2 * LANES, MLA_W), F32)
    wq_rot = jnp.zeros((2 * LANES, MLA_W), F32)
    wk = jnp.zeros((2 * LANES, MLA_W), F32)
    wv = jnp.zeros((LANES, MLA_W), F32)
    place = jnp.eye(DR, dtype=F32)
    for h in range(H):
        q_h = w_uq[:, h * (DN + DR):(h + 1) * (DN + DR)]
        wq = wq.at[:MLA_Q_RANK, LANES * h:LANES * h + DN + DR].set(q_h)
        wq_rot = wq_rot.at[:MLA_Q_RANK, LANES * h + DN:LANES * h + DN + DR].set(_rot_cols(q_h[:, DN:]))
        kv_h = w_ukv[:, h * (DN + DV):(h + 1) * (DN + DV)]
        wk = wk.at[:MLA_KV_RANK, LANES * h:LANES * h + DN].set(kv_h[:, :DN])
        wk = wk.at[LANES:LANES + DR, LANES * h + DN:LANES * h + DN + DR].set(place)
        v0 = LANES * h + DV * (h % 2)
        wv = wv.at[:, v0:v0 + DV].set(kv_h[:, DN:])
    return wq.astype(BF16), wq_rot.astype(BF16), wk.astype(BF16), wv.astype(BF16)


def mla_tables(cos, sin, n_ctx):
    H, DN, DR = MLA_HEADS, MLA_NOPE, MLA_ROPE
    L = cos.shape[0]
    cf = jnp.concatenate([jnp.ones((n_ctx, DR), F32), jnp.concatenate([cos, cos], -1)], 0)
    sf = jnp.concatenate([jnp.zeros((n_ctx, DR), F32), jnp.concatenate([sin, sin], -1)], 0)
    S = n_ctx + L
    tab_k = jnp.concatenate([cf, sf, jnp.zeros((S, LANES - 2 * DR), F32)], -1)
    scale = (DN + DR) ** -0.5 * math.log2(math.e)
    zpad = jnp.zeros((S, LANES - DN - DR), F32)
    qc_h = jnp.concatenate([jnp.full((S, DN), scale, F32), cf * scale, zpad], -1)
    qs_h = jnp.concatenate([jnp.zeros((S, DN), F32), sf * scale, zpad], -1)
    return tab_k, jnp.tile(qc_h, (1, H)), jnp.tile(qs_h, (1, H))


def _mla_prep_kernel(pa_ref, tk_ref, tqc_ref, tqs_ref, kvn_ref, qn_ref, wq_ref, wqr_ref, wk_ref, wv_ref,
                     q_ref, kt_ref, v_ref):
    pa = pa_ref[0]
    ckv = pa[:, :LANES]
    ckv = ckv * lax.rsqrt(jnp.mean(jnp.square(ckv), -1, keepdims=True) + 1e-6) * kvn_ref[...]
    t = pa[:, LANES:2 * LANES] * tk_ref[...]
    kr = t + pltpu.roll(t, LANES - MLA_ROPE, 1)
    kr = jnp.where(lax.broadcasted_iota(jnp.int32, kr.shape, 1) < MLA_ROPE, kr, 0.0)
    k = jnp.dot(jnp.concatenate([ckv, kr], -1).astype(BF16), wk_ref[...], preferred_element_type=F32)
    kt_ref[0, 0] = k.T.astype(BF16)
    v_ref[0] = jnp.dot(ckv.astype(BF16), wv_ref[...], preferred_element_type=F32).astype(BF16)
    cq = pa[:, 2 * LANES:]
    ms = jnp.sum(jnp.square(cq), -1, keepdims=True) * (1.0 / MLA_Q_RANK)
    cq = (cq * lax.rsqrt(ms + 1e-6) * qn_ref[...]).astype(BF16)
    qa = jnp.dot(cq, wq_ref[...], preferred_element_type=F32)
    qb = jnp.dot(cq, wqr_ref[...], preferred_element_type=F32)
    q_ref[0] = (qa * tqc_ref[...] + qb * tqs_ref[...]).astype(BF16)


def mla_prep_pallas(pa, tables, q_norm, kv_norm, weights):
    B, S, _ = pa.shape
    T = ATTN_TK
    assert S % T == 0
    tab_k, tab_qc, tab_qs = tables
    wq, wq_rot, wk, wv = weights
    qn = jnp.concatenate([q_norm, jnp.zeros((2 * LANES - MLA_Q_RANK,), F32)])[None]

    def full(a):
        return pl.BlockSpec(a.shape, lambda b, i: (0,) * a.ndim)

    def rows(width):
        return pl.BlockSpec((T, width), lambda b, i: (i, 0))

    return pl.pallas_call(
        _mla_prep_kernel,
        grid=(B, S // T),
        in_specs=[pl.BlockSpec((1, T, MLA_PA_COLS), lambda b, i: (b, i, 0)),
                  rows(LANES), rows(MLA_W), rows(MLA_W),
                  pl.BlockSpec((1, LANES), lambda b, i: (0, 0)), full(qn), full(wq), full(wq_rot), full(wk), full(wv)],
        out_specs=[pl.BlockSpec((1, T, MLA_W), lambda b, i: (b, i, 0)),
                   pl.BlockSpec((1, 1, MLA_W, T), lambda b, i: (b, i, 0, 0)),
                   pl.BlockSpec((1, T, MLA_W), lambda b, i: (b, i, 0))],
        out_shape=[jax.ShapeDtypeStruct((B, S, MLA_W), BF16),
                   jax.ShapeDtypeStruct((B, S // T, MLA_W, T), BF16),
                   jax.ShapeDtypeStruct((B, S, MLA_W), BF16)],
        compiler_params=pltpu.CompilerParams(dimension_semantics=("arbitrary", "arbitrary")),
        name="mla_prep",
    )(pa, tab_k, tab_qc, tab_qs, kv_norm[None], qn, wq, wq_rot, wk, wv)


LRU_CHUNK = 256
LRU_HALO = SUBLANES


def _lru_kernel(xl_ref, xc_ref, wbd_ref, bias_ref, c8_ref, cw_ref, cb_ref, gn_ref,
                yl_ref, yc_ref, xs_l, xs_c, hf_l, hf_c, a_s, b_s, hb_s):
    C = GROUP
    CH = LRU_CHUNK
    L = xl_ref.shape[1]
    Lc = xc_ref.shape[1]

    def stage(x_ref, xs, n):
        xs[pl.ds(0, LRU_HALO), :] = jnp.zeros((LRU_HALO, C), F32)
        xs[pl.ds(LRU_HALO + n, LRU_HALO), :] = jnp.zeros((LRU_HALO, C), F32)

        def cp(i, c):
            r0 = pl.multiple_of(i * CH, CH)
            xs[pl.ds(LRU_HALO + r0, CH), :] = x_ref[0, pl.ds(r0, CH), pl.ds(0, C)]
            return c

        lax.fori_loop(0, n // CH, cp, 0)

    stage(xl_ref, xs_l, L)
    stage(xc_ref, xs_c, Lc)

    def coeffs(xs, base, d):
        xv = xs[pl.ds(base, CH + 2 * LRU_HALO), :]
        u = cb_ref[...]
        for j in range(LRU_CONV):
            o = LRU_HALO - LRU_CONV_LEFT + j
            u = u + xv[o:o + CH] * cw_ref[pl.ds(j, 1), :]
        z = jnp.dot(u.astype(BF16), wbd_ref[:, pl.ds(d * 2 * C, 2 * C)], preferred_element_type=F32)
        z = z + bias_ref[:, pl.ds(d * 2 * C, 2 * C)]
        r = jax.nn.sigmoid(z[:, :C])
        i = jax.nn.sigmoid(z[:, C:])
        log_a = r * c8_ref[pl.ds(d, 1), :]
        a = jnp.exp(log_a)
        a_s[...] = a
        b_s[...] = jnp.sqrt(-jnp.tanh(log_a) * (a * a + 1.0)) * (i * u)

    def row_scan(h, out_ref, out_base, reverse):
        def body(t, h):
            tt = CH - 1 - t if reverse else t
            a_t = jnp.broadcast_to(a_s[pl.ds(tt, 1), :], (SUBLANES, C))
            b_t = jnp.broadcast_to(b_s[pl.ds(tt, 1), :], (SUBLANES, C))
            h = a_t * h + b_t
            out_ref[pl.ds(out_base + tt, 1), :] = h[0:1, :]
            return h

        return lax.fori_loop(0, CH, body, h, unroll=8)

    h0 = jnp.zeros((SUBLANES, C), F32)

    h = h0
    for ci in range(Lc // CH):
        coeffs(xs_c, ci * CH, 0)
        h = row_scan(h, hf_c, ci * CH, False)

    def fwd_chunk(ci, h):
        base = pl.multiple_of(ci * CH, CH)
        coeffs(xs_l, base, 0)
        return row_scan(h, hf_l, base, False)

    lax.fori_loop(0, L // CH, fwd_chunk, h)

    def combine(x_ref, hf, base, y_ref):
        hl = hf[pl.ds(base, CH), :] + hb_s[...]
        g = jax.nn.gelu(x_ref[0, pl.ds(base, CH), pl.ds(C, C)])
        v = hl * g
        y = v * lax.rsqrt(jnp.mean(jnp.square(v), -1, keepdims=True) + 1e-6) * gn_ref[...]
        y_ref[0, pl.ds(base, CH), :] = y

    h = h0
    for ci in reversed(range(Lc // CH)):
        coeffs(xs_c, ci * CH, 1)
        h = row_scan(h, hb_s, 0, True)
        combine(xc_ref, hf_c, ci * CH, yc_ref)

    def bwd_chunk(k, h):
        base = pl.multiple_of((L // CH - 1 - k) * CH, CH)
        coeffs(xs_l, base, 1)
        h = row_scan(h, hb_s, 0, True)
        combine(xl_ref, hf_l, base, yl_ref)
        return h

    lax.fori_loop(0, L // CH, bwd_chunk, h)


def rglru_pallas(xg_l, xg_c, conv_w, conv_b, w_r, b_r, w_i, b_i, lam, out_norm):
    B, L, _ = xg_l.shape
    Lc = xg_c.shape[1]
    C = GROUP
    assert L % LRU_CHUNK == 0 and Lc % LRU_CHUNK == 0

    def bd(w):
        return jax.scipy.linalg.block_diag(*[w[n] for n in range(LRU_BLOCKS)])

    wbd = jnp.concatenate([bd(w_r[0]), bd(w_i[0]), bd(w_r[1]), bd(w_i[1])], 1).astype(BF16)
    bias = jnp.concatenate([b_r[0], b_i[0], b_r[1], b_i[1]])[None]
    c8 = -LRU_C * jax.nn.softplus(-lam)

    def full(shape):
        return pl.BlockSpec(shape, lambda b: (0,) * len(shape))

    return pl.pallas_call(
        _lru_kernel,
        grid=(B,),
        in_specs=[
            pl.BlockSpec((1, L, 2 * C), lambda b: (b, 0, 0)),
            pl.BlockSpec((1, Lc, 2 * C), lambda b: (b, 0, 0)),
            full((C, 4 * C)), full((1, 4 * C)), full((2, C)), full((LRU_CONV, C)), full((1, C)), full((1, C)),
        ],
        out_specs=[
            pl.BlockSpec((1, L, C), lambda b: (b, 0, 0)),
            pl.BlockSpec((1, Lc, C), lambda b: (b, 0, 0)),
        ],
        out_shape=[jax.ShapeDtypeStruct((B, L, C), F32), jax.ShapeDtypeStruct((B, Lc, C), F32)],
        scratch_shapes=[
            pltpu.VMEM((L + 2 * LRU_HALO, C), F32),
            pltpu.VMEM((Lc + 2 * LRU_HALO, C), F32),
            pltpu.VMEM((L, C), F32),
            pltpu.VMEM((Lc, C), F32),
            pltpu.VMEM((LRU_CHUNK, C), F32),
            pltpu.VMEM((LRU_CHUNK, C), F32),
            pltpu.VMEM((LRU_CHUNK, C), F32),
        ],
        compiler_params=pltpu.CompilerParams(dimension_semantics=("arbitrary",), vmem_limit_bytes=VMEM_LIMIT_BYTES),
        name="rglru",
    )(xg_l, xg_c, wbd, bias, c8, conv_w, conv_b[None], out_norm[None])


HY_T = 256
HY_CB = 8


def _hyena_kernel(k_ref, d_ref, v_ref, x1_ref, x2_ref, o_ref, u_s, acc_s):
    n_rows = v_ref.shape[1]
    T = HY_T
    nb = k_ref.shape[2] // (2 * T)
    bsz = n_rows // nb

    def conv(ci, order):
        acc_s[...] = jnp.zeros_like(acc_s)
        for dd in range(-(nb - 1), nb):
            w2 = k_ref[order, pl.ds(ci, 1), pl.ds(T * (dd + nb - 1), 2 * T)]
            x = jnp.broadcast_to(w2, (T, 2 * T))
            r = pltpu.roll(x, 0, 1, stride=1, stride_axis=0)
            tb = r[:, T:].astype(BF16)
            j0, j1 = max(0, -dd), min(nb, nb - dd)
            lhs = u_s[pl.ds(bsz * j0, bsz * (j1 - j0)), :]
            dst = pl.ds(bsz * (j0 + dd), bsz * (j1 - j0))
            acc_s[dst, :] = acc_s[dst, :] + jnp.dot(lhs, tb, preferred_element_type=F32)

    def channel(ci, carry):
        v = v_ref[ci]
        u_s[...] = v.astype(BF16)
        conv(ci, 0)
        u = x1_ref[ci] * (acc_s[...] + v * d_ref[pl.ds(ci, 1), pl.ds(0, 1)])
        u_s[...] = u.astype(BF16)
        conv(ci, 1)
        o_ref[ci] = x2_ref[ci] * (acc_s[...] + u * d_ref[pl.ds(ci, 1), pl.ds(1, 1)])
        return carry

    lax.fori_loop(0, HY_CB, channel, 0)


def hyena_conv_pallas(kfull, d_skip, vT, x1T, x2T):
    C, R, T = vT.shape
    two_l = kfull.shape[2]
    assert T == HY_T and C % HY_CB == 0
    blk = pl.BlockSpec((HY_CB, R, T), lambda c: (c, 0, 0))
    return pl.pallas_call(
        _hyena_kernel,
        grid=(C // HY_CB,),
        in_specs=[
            pl.BlockSpec((HY_ORDER, HY_CB, two_l), lambda c: (0, c, 0)),
            pl.BlockSpec((HY_CB, HY_ORDER), lambda c: (c, 0)),
            blk, blk, blk,
        ],
        out_specs=blk,
        out_shape=jax.ShapeDtypeStruct((C, R, T), F32),
        scratch_shapes=[pltpu.VMEM((R, T), BF16), pltpu.VMEM((R, T), F32)],
        compiler_params=pltpu.CompilerParams(dimension_semantics=("arbitrary",), vmem_limit_bytes=VMEM_LIMIT_BYTES),
        name="hyena_conv",
    )(kfull, d_skip.T, vT, x1T, x2T)


FFN_TF_MAX = 1408


def _ffn_tile(hidden):
    return max(t for t in range(LANES, FFN_TF_MAX + 1, LANES) if hidden % t == 0)


def _swiglu_kernel(be_ref, nu_ref, x_ref, wg_ref, wu_ref, wd_ref, o_ref, acc_ref):
    i = pl.program_id(0)
    f = pl.program_id(1)

    @pl.when(i < nu_ref[0])
    def _():
        x = x_ref[...]
        g = jnp.dot(x, wg_ref[0], preferred_element_type=F32)
        u = jnp.dot(x, wu_ref[0], preferred_element_type=F32)
        h = (jax.nn.silu(g) * u).astype(BF16)
        part = jnp.dot(h, wd_ref[0], preferred_element_type=F32)

        @pl.when(f == 0)
        def _():
            acc_ref[...] = part

        @pl.when(f > 0)
        def _():
            acc_ref[...] = acc_ref[...] + part

    @pl.when(f == pl.num_programs(1) - 1)
    def _():
        o_ref[...] = jnp.where(i < nu_ref[0], acc_ref[...], 0.0)


def grouped_swiglu_pallas(xs, block_expert, n_used, wg, wu, wd):
    n_rows, D = xs.shape
    F = wg.shape[2]
    TM = MOE_BLOCK
    assert n_rows % TM == 0
    tf = _ffn_tile(F)
    n_blocks = n_rows // TM
    grid_spec = pltpu.PrefetchScalarGridSpec(
        num_scalar_prefetch=2,
        grid=(n_blocks, F // tf),
        in_specs=[
            pl.BlockSpec((TM, D), lambda i, f, be, nu: (i, 0)),
            pl.BlockSpec((1, D, tf), lambda i, f, be, nu: (be[i], 0, f)),
            pl.BlockSpec((1, D, tf), lambda i, f, be, nu: (be[i], 0, f)),
            pl.BlockSpec((1, tf, D), lambda i, f, be, nu: (be[i], f, 0)),
        ],
        out_specs=pl.BlockSpec((TM, D), lambda i, f, be, nu: (i, 0)),
        scratch_shapes=[pltpu.VMEM((TM, D), F32)],
    )
    return pl.pallas_call(
        _swiglu_kernel,
        grid_spec=grid_spec,
        out_shape=jax.ShapeDtypeStruct((n_rows, D), F32),
        compiler_params=pltpu.CompilerParams(dimension_semantics=("arbitrary", "arbitrary"),
                                             vmem_limit_bytes=VMEM_LIMIT_BYTES),
        name="grouped_swiglu",
    )(block_expert.astype(jnp.int32), jnp.reshape(n_used, (1,)).astype(jnp.int32), xs, wg, wu, wd)


PROJ_TM = 512


def _residual_ln(x, branch, gate_ref, g_ref, b_ref):
    s = ALPHA * x + gate_ref[0] * branch
    mu = jnp.mean(s, -1, keepdims=True)
    d = s - mu
    var = jnp.mean(jnp.square(d), -1, keepdims=True)
    return d * lax.rsqrt(var + 1e-5) * g_ref[...] + b_ref[...]


def _in_proj_kernel(*refs, has_ln, n_w):
    x_ref = refs[0]
    pos = 1
    x = x_ref[...]
    if has_ln:
        branch_ref, gate_ref, g_ref, b_ref = refs[1:5]
        pos = 5
        x = _residual_ln(x, branch_ref[...], gate_ref, g_ref, b_ref)
    shift_ref, scale_ref = refs[pos:pos + 2]
    w_refs = refs[pos + 2:pos + 2 + n_w]
    out_refs = refs[pos + 2 + n_w:]
    if has_ln:
        out_refs[0][...] = x
        out_refs = out_refs[1:]
    h = (x * (1.0 + scale_ref[0]) + shift_ref[0]).astype(BF16)
    for w_ref, o_ref in zip(w_refs, out_refs):
        o_ref[...] = jnp.dot(h, w_ref[...], preferred_element_type=F32)


def in_proj_pallas(x, shift, scale, weights, residual=None):
    M, D = x.shape
    tm = PROJ_TM
    G = shift.shape[0]
    tiles_per_group = M // G // tm
    assert M % (G * tm) == 0
    row = pl.BlockSpec((tm, D), lambda i: (i, 0))
    grp = pl.BlockSpec((1, 1, D), lambda i: (i // tiles_per_group, 0, 0))
    vec = pl.BlockSpec((1, D), lambda i: (0, 0))
    args, specs = [x], [row]
    has_ln = residual is not None
    if has_ln:
        branch, first_row, gate, ln_g, ln_b = residual
        assert first_row % tm == 0
        off = first_row // tm
        args += [branch, gate, ln_g[None], ln_b[None]]
        specs += [pl.BlockSpec((tm, D), lambda i: (i + off, 0)), grp, vec, vec]
    args += [shift, scale] + list(weights)
    specs += [grp, grp] + [pl.BlockSpec(w.shape, lambda i: (0, 0)) for w in weights]
    out_shape = [jax.ShapeDtypeStruct((M, w.shape[1]), F32) for w in weights]
    out_specs = [pl.BlockSpec((tm, w.shape[1]), lambda i: (i, 0)) for w in weights]
    if has_ln:
        out_shape = [jax.ShapeDtypeStruct((M, D), F32)] + out_shape
        out_specs = [row] + out_specs
    outs = pl.pallas_call(
        partial(_in_proj_kernel, has_ln=has_ln, n_w=len(weights)),
        grid=(M // tm,),
        in_specs=specs,
        out_specs=out_specs,
        out_shape=out_shape,
        compiler_params=pltpu.CompilerParams(dimension_semantics=("arbitrary",), vmem_limit_bytes=VMEM_LIMIT_BYTES),
        name="in_proj",
    )(*args)
    return (outs[0], outs[1:]) if has_ln else (None, outs)


def _out_proj_kernel(a_ref, b_ref, c_ref, d_ref, x_ref, w_ref, gate_ref, g_ref, bb_ref, shift_ref, scale_ref,
                     xn_ref, f_ref):
    y = jnp.concatenate([a_ref[...], b_ref[...], c_ref[...], d_ref[...]], -1).astype(BF16)
    y = jnp.dot(y, w_ref[...], preferred_element_type=F32)
    xn = _residual_ln(x_ref[...], y, gate_ref, g_ref, bb_ref)
    xn_ref[...] = xn
    f_ref[...] = (xn * (1.0 + scale_ref[0]) + shift_ref[0]).astype(BF16)


def out_proj_pallas(parts, x, w_out, gate, ln_g, ln_b, shift, scale):
    M, D = x.shape
    tm = PROJ_TM
    G = gate.shape[0]
    tiles_per_group = M // G // tm
    assert M % (G * tm) == 0
    row = pl.BlockSpec((tm, D), lambda i: (i, 0))
    part = pl.BlockSpec((tm, GROUP), lambda i: (i, 0))
    grp = pl.BlockSpec((1, 1, D), lambda i: (i // tiles_per_group, 0, 0))
    vec = pl.BlockSpec((1, D), lambda i: (0, 0))
    return pl.pallas_call(
        _out_proj_kernel,
        grid=(M // tm,),
        in_specs=[part] * 4 + [row, pl.BlockSpec(w_out.shape, lambda i: (0, 0)), grp, vec, vec, grp, grp],
        out_specs=[row, row],
        out_shape=[jax.ShapeDtypeStruct((M, D), F32), jax.ShapeDtypeStruct((M, D), BF16)],
        compiler_params=pltpu.CompilerParams(dimension_semantics=("arbitrary",), vmem_limit_bytes=VMEM_LIMIT_BYTES),
        name="out_proj",
    )(*parts, x, w_out, gate, ln_g[None], ln_b[None], shift, scale)


def modulate(x, shift, scale):
    return x * (1.0 + scale) + shift


def layer_norm(x, g, b, eps=1e-5):
    mu = jnp.mean(x, -1, keepdims=True)
    var = jnp.mean(jnp.square(x - mu), -1, keepdims=True)
    return (x - mu) * lax.rsqrt(var + eps) * g + b


def rms_norm(x, g, eps=1e-6):
    return x * lax.rsqrt(jnp.mean(jnp.square(x), -1, keepdims=True) + eps) * g


def dwconv(x, w, b, left):
    K = w.shape[0]
    L = x.shape[1]
    xp = jnp.pad(x, ((0, 0), (left, K - 1 - left), (0, 0)))
    return sum(xp[:, j:j + L] * w[j] for j in range(K)) + b


def token_shift(z, mu_prev, mu_next):
    zp = jnp.pad(z, ((0, 0), (1, 1), (0, 0)))
    return z + mu_prev * (zp[:, :-2] - z) + mu_next * (zp[:, 2:] - z)


def axial_rope(rows):
    r, col = jnp.meshgrid(jnp.arange(rows, dtype=F32), jnp.arange(GRID_W, dtype=F32), indexing='ij')
    half = MLA_ROPE // 2
    inv = 1.0 / (ROPE_BASE ** (jnp.arange(0, half, 2, dtype=F32) / half))
    ang = jnp.concatenate([r.reshape(-1, 1) * inv, col.reshape(-1, 1) * inv], -1)
    return jnp.cos(ang), jnp.sin(ang)


def mla_mixer(Pac, Pal, tables, q_norm, kv_norm, w_uq, w_ukv, out_norm, ctx_out):
    Lc, L = Pac.shape[1], Pal.shape[1]
    q, kt, v = mla_prep_pallas(jnp.concatenate([Pac, Pal], 1), tables, q_norm, kv_norm, mla_prep_weights(w_uq, w_ukv))
    nc, nl = Lc // ATTN_TQ, L // ATTN_TQ
    yl = mla_attention_pallas(q, kt, v, out_norm, nc, nl, nc + nl)
    yc = mla_attention_pallas(q, kt, v, out_norm, 0, nc, nc) if ctx_out else None
    return yl, yc


def rwkv_finish(y, bonus, v, gd, g_up, ln_g, ln_b):
    B, L = y.shape[:2]
    mu = jnp.mean(y, -1, keepdims=True)
    var = jnp.mean(jnp.square(y - mu), -1, keepdims=True)
    yn = (y - mu) * lax.rsqrt(var + RWKV_GN_EPS) * ln_g.reshape(RWKV_HEADS, RWKV_HEAD) + ln_b.reshape(RWKV_HEADS, RWKV_HEAD)
    g = jax.nn.sigmoid(gd) @ g_up
    return (yn + bonus * v).reshape(B, L, GROUP) * g


def rwkv_mixer(Pl, Pc, mu_prev, mu_next, w0, w_up, a0, a_up, g_up, k_k, k_a, r_k, ln_g, ln_b, ctx_out):
    B, L = Pl.shape[:2]
    Lc = Pc.shape[1]
    S = Lc + L
    H, N = RWKV_HEADS, RWKV_HEAD
    z = jnp.concatenate([token_shift(Pc, mu_prev, mu_next), token_shift(Pl, mu_prev, mu_next)], 1)

    def heads(t):
        return t.reshape(B, S, H, N)

    r = heads(z[..., C_R:C_R + GROUP])
    k = heads(z[..., C_K:C_K + GROUP])
    v = heads(z[..., C_V:C_V + GROUP])
    kk = k * k_k.reshape(H, N)
    kk = kk * lax.rsqrt(jnp.maximum(jnp.sum(jnp.square(kk), -1, keepdims=True), 1e-24))
    w, kka, kd = [], [], []
    for d in range(2):
        wd = z[..., C_WD + d * RWKV_DECAY_LORA:C_WD + (d + 1) * RWKV_DECAY_LORA]
        ad = z[..., C_AD + d * RWKV_AAA_LORA:C_AD + (d + 1) * RWKV_AAA_LORA]
        log_w = -jnp.exp(-jax.nn.softplus(-(w0[d] + jnp.tanh(wd) @ w_up[d])) - 0.5)
        a = heads(jax.nn.sigmoid(a0[d] + ad @ a_up[d]))
        w.append(heads(jnp.exp(log_w)))
        kka.append(kk * a)
        kd.append(k * (1.0 + (a - 1.0) * k_a.reshape(H, N)))

    def to_scan(t):
        t = jnp.transpose(t, (1, 3, 0, 2)).reshape(S, N, B * H)
        return jnp.concatenate([t, t], -1)

    yf, yb = rwkv_scan_pallas(to_scan(r), to_scan(kk), to_scan(v), [to_scan(t) for t in w],
                              [to_scan(t) for t in kka], [to_scan(t) for t in kd], Lc)
    y = yf[..., :B * H] + yb[..., B * H:]
    y = jnp.transpose(y.reshape(S, N, B, H), (2, 0, 3, 1))
    bonus = sum(jnp.sum(r * kd_d * r_k, -1, keepdims=True) for kd_d in kd)
    gd = z[..., C_GD:C_GD + RWKV_GATE_LORA]
    out_l = rwkv_finish(y[:, Lc:], bonus[:, Lc:], v[:, Lc:], gd[:, Lc:], g_up, ln_g, ln_b)
    out_c = None
    if ctx_out:
        out_c = rwkv_finish(y[:, :Lc], bonus[:, :Lc], v[:, :Lc], gd[:, :Lc], g_up, ln_g, ln_b)
    return out_l, out_c


def hyena_filters(L, w1, b1, w2, b2, w3):
    t01 = jnp.linspace(0.0, 1.0, L, dtype=F32)[:, None]
    bands = jnp.linspace(1e-4, HY_BANDS - 1, HY_BANDS, dtype=F32)[None, :]
    wpos = (2.0 * math.pi / L) * jnp.arange(L, dtype=F32)[:, None]
    z = jnp.concatenate([t01, jnp.cos(bands * wpos), -jnp.sin(bands * wpos)], -1)
    h = jnp.sin(HY_SIN_FREQ * (z @ w1 + b1))
    h = jnp.sin(HY_SIN_FREQ * (h @ w2 + b2))
    h = (h @ w3).reshape(L, HY_ORDER, 2, GROUP)
    deltas = jnp.abs(jnp.linspace(HY_DECAY_MIN, HY_DECAY_MAX, GROUP, dtype=F32))
    window = jnp.exp(-t01 * deltas) + HY_SHIFT
    return h * window[:, None, None, :]


def hyena_sequence(Pd, conv_w, conv_b, w1, b1, w2, b2, w3, d_skip):
    B, L, _ = Pd.shape
    C = GROUP
    T = HY_T
    nb = L // T
    z = dwconv(Pd, conv_w, conv_b, 1)
    zT = jnp.transpose(z.reshape(B, nb, T, 3 * C), (3, 1, 0, 2)).reshape(3 * C, nb * B, T)
    h = hyena_filters(L, w1, b1, w2, b2, w3)
    hf = jnp.transpose(h[:, :, 0], (1, 2, 0))
    hb = jnp.transpose(h[:, :, 1], (1, 2, 0))
    kfull = jnp.concatenate([jnp.zeros((HY_ORDER, C, 1), F32), jnp.flip(hb[..., 1:], -1), hf], -1)
    oT = hyena_conv_pallas(kfull, d_skip, zT[:C], zT[C:2 * C], zT[2 * C:])
    return jnp.transpose(oT.reshape(C, nb, B, T), (2, 1, 3, 0)).reshape(B, L, C)


def swiglu(x, wg, wu, wd):
    n_blocks = x.shape[0] // MOE_BLOCK
    return grouped_swiglu_pallas(x.astype(BF16), jnp.zeros((n_blocks,), jnp.int32), jnp.int32(n_blocks),
                                 wg[None].astype(BF16), wu[None].astype(BF16), wd[None].astype(BF16))


def moe_swiglu(x, router, wg, wu, wd):
    N, D = x.shape
    logits = x @ router
    top_v, top_i = lax.top_k(logits, TOP_K)
    gates = jax.nn.softmax(top_v, axis=-1)
    A = N * TOP_K
    e_flat = top_i.reshape(-1)
    tok_flat = jnp.arange(A, dtype=jnp.int32) // TOP_K
    order = jnp.argsort(e_flat)
    e_sorted = e_flat[order]
    counts = jnp.bincount(e_flat, length=N_EXPERTS)
    starts = jnp.cumsum(counts) - counts
    padded = (counts + MOE_BLOCK - 1) // MOE_BLOCK * MOE_BLOCK
    pends = jnp.cumsum(padded)
    pstarts = pends - padded
    dest = (pstarts[e_sorted] + jnp.arange(A, dtype=jnp.int32) - starts[e_sorted]).astype(jnp.int32)
    n_blocks = -(-A // MOE_BLOCK) + N_EXPERTS
    n_slots = n_blocks * MOE_BLOCK
    block_expert = jnp.clip(jnp.searchsorted(pends, jnp.arange(n_blocks) * MOE_BLOCK, side='right'), 0, N_EXPERTS - 1)
    slot_e = jnp.repeat(block_expert, MOE_BLOCK)
    slot_pos = jnp.arange(n_slots, dtype=jnp.int32) - pstarts[slot_e].astype(jnp.int32)
    slot_src = jnp.clip(starts[slot_e].astype(jnp.int32) + slot_pos, 0, A - 1)
    slot_tok = jnp.where(slot_pos < counts[slot_e], tok_flat[order[slot_src]], N)
    xp = jnp.concatenate([x.astype(BF16), jnp.zeros((1, D), BF16)], 0)
    ys = grouped_swiglu_pallas(xp[slot_tok], block_expert, pends[-1] // MOE_BLOCK,
                               wg.astype(BF16), wu.astype(BF16), wd.astype(BF16))
    slot_of = dest[jnp.argsort(order)].reshape(N, TOP_K)
    return ys[slot_of[:, 0]] * gates[:, 0:1] + ys[slot_of[:, 1]] * gates[:, 1:2]


def kernel(x, c, ctx, c_ctx, ada_w, ada_b, w_in, mla_q_norm, mla_kv_norm, mla_w_uq, mla_w_ukv, mla_out_norm, lru_conv_w, lru_conv_b, lru_w_r, lru_b_r, lru_w_i, lru_b_i, lru_lambda, lru_out_norm, rwkv_mu_prev, rwkv_mu_next, rwkv_w0, rwkv_w_up, rwkv_a0, rwkv_a_up, rwkv_g_up, rwkv_k_k, rwkv_k_a, rwkv_r_k, rwkv_ln_g, rwkv_ln_b, hy_conv_w, hy_conv_b, hy_f_w1, hy_f_b1, hy_f_w2, hy_f_b2, hy_f_w3, hy_d, hy_out_norm, w_out, ln1_g, ln1_b, ln2_g, ln2_b, ffn_w_gate, ffn_w_up, ffn_w_down, moe_router, moe_w_gate, moe_w_up, moe_w_down):
    B, L, D = x.shape
    Lc = ctx.shape[1]
    rows = L // GRID_W
    mla_tabs = mla_tables(*axial_rope(rows), Lc)
    s_lat = jax.nn.silu(c)
    s_ctx = jax.nn.silu(c_ctx)
    Ml, Mc = B * L, B * Lc
    xl, xc = x.reshape(Ml, D), ctx.reshape(Mc, D)
    prev = None
    for li in range(DEPTH):
        ctx_out = li < DEPTH - 1
        mod_l = (s_lat @ ada_w[li] + ada_b[li]).reshape(B, 6, 1, D)
        mod_c = (s_ctx @ ada_w[li] + ada_b[li]).reshape(6, 1, D)
        ml = [mod_l[:, q] for q in range(6)]
        mc = [mod_c[q][None] for q in range(6)]

        w_in_l = w_in[li]
        w_secs = [w.astype(BF16) for w in (mla_section_weight(w_in_l[:, :B_X]), w_in_l[:, B_X:C_OFF],
                                           w_in_l[:, C_OFF:D_OFF], w_in_l[:, D_OFF:])]
        res_l = res_c = None
        if prev is not None:
            out, gate_l, gate_c, g2, b2 = prev
            res_l, res_c = (out, 0, gate_l, g2, b2), (out, Ml, gate_c, g2, b2)
        xl_new, (Pal, Pbl, Pcl, Pdl) = in_proj_pallas(xl, ml[0], ml[1], w_secs, res_l)
        xc_new, ctx_secs = in_proj_pallas(xc, mc[0], mc[1], w_secs if ctx_out else w_secs[:3], res_c)
        if prev is not None:
            xl, xc = xl_new, xc_new
        Pal, Pbl, Pcl, Pdl = (p.reshape(B, L, -1) for p in (Pal, Pbl, Pcl, Pdl))
        Pac, Pbc, Pcc = (p.reshape(B, Lc, -1) for p in ctx_secs[:3])
        a_l, a_c = mla_mixer(Pac, Pal, mla_tabs, mla_q_norm[li], mla_kv_norm[li], mla_w_uq[li], mla_w_ukv[li],
                             mla_out_norm[li], ctx_out)
        b_l, b_c = rglru_pallas(Pbl, Pbc, lru_conv_w[li], lru_conv_b[li], lru_w_r[li], lru_b_r[li], lru_w_i[li],
                                lru_b_i[li], lru_lambda[li], lru_out_norm[li])
        c_l, c_c = rwkv_mixer(Pcl, Pcc, rwkv_mu_prev[li], rwkv_mu_next[li], rwkv_w0[li], rwkv_w_up[li], rwkv_a0[li],
                              rwkv_a_up[li], rwkv_g_up[li], rwkv_k_k[li], rwkv_k_a[li], rwkv_r_k[li],
                              rwkv_ln_g[li], rwkv_ln_b[li], ctx_out)
        d_l = rms_norm(hyena_sequence(Pdl, hy_conv_w[li], hy_conv_b[li], hy_f_w1[li], hy_f_b1[li], hy_f_w2[li],
                                      hy_f_b2[li], hy_f_w3[li], hy_d[li]), hy_out_norm[li])
        w_out_l = w_out[li].astype(BF16)
        xl, tokens = out_proj_pallas([t.reshape(Ml, GROUP) for t in (a_l, b_l, c_l, d_l)], xl, w_out_l,
                                     ml[2], ln1_g[li], ln1_b[li], ml[3], ml[4])
        if ctx_out:
            Pdc = ctx_secs[3].reshape(B, Lc, -1)
            d_c = rms_norm(hyena_sequence(Pdc, hy_conv_w[li], hy_conv_b[li], hy_f_w1[li], hy_f_b1[li], hy_f_w2[li],
                                          hy_f_b2[li], hy_f_w3[li], hy_d[li]), hy_out_norm[li])
            xc, fc = out_proj_pallas([t.reshape(Mc, GROUP) for t in (a_c, b_c, c_c, d_c)], xc, w_out_l,
                                     mc[2], ln1_g[li], ln1_b[li], mc[3], mc[4])
            tokens = jnp.concatenate([tokens, fc], 0)
        j = li // 2
        if li % 2 == 0:
            out = swiglu(tokens, ffn_w_gate[j], ffn_w_up[j], ffn_w_down[j])
        else:
            out = moe_swiglu(tokens, moe_router[j], moe_w_gate[j], moe_w_up[j], moe_w_down[j])
        prev = (out, ml[5], mc[5], ln2_g[li], ln2_b[li])
    out, gate_l, _, g2, b2 = prev
    xl, _ = in_proj_pallas(xl, ml[0], ml[1], [], (out, 0, gate_l, g2, b2))
    return xl.reshape(B, L, D)
```

```python
import math
from functools import partial

import jax
import jax.numpy as jnp
from jax import lax
from jax.experimental import pallas as pl
from jax.experimental.pallas import tpu as pltpu

F32 = jnp.float32
BF16 = jnp.bfloat16

SUBLANES = 8
LANES = 128
VMEM_LIMIT_BYTES = 48 * 1024 * 1024

D_MODEL = 1024
DEPTH = 4
GRID_W = 64
GROUP = D_MODEL // 4

MLA_HEADS = 4
MLA_NOPE = 64
MLA_ROPE = 32
MLA_V = 64
MLA_Q_RANK = 192
MLA_KV_RANK = 128
ROPE_BASE = 10000.0
Q_BLOCK = 128

LRU_BLOCKS = 4
LRU_CONV = 4
LRU_CONV_LEFT = 2
LRU_C = 8.0

RWKV_HEADS = 4
RWKV_HEAD = GROUP // RWKV_HEADS
RWKV_DECAY_LORA = 32
RWKV_AAA_LORA = 32
RWKV_GATE_LORA = 64
RWKV_GN_EPS = 64e-5

HY_ORDER = 2
HY_SHORT = 3
HY_BANDS = 16
HY_EMB = 1 + 2 * HY_BANDS
HY_HIDDEN = 64
HY_SIN_FREQ = 1.0
HY_DECAY_MIN = math.log(1e-2) / 1.5
HY_DECAY_MAX = math.log(1e-2) / 0.3
HY_SHIFT = 0.05

N_EXPERTS = 8
TOP_K = 2
MOE_BLOCK = 512

ALPHA = (2.0 * DEPTH) ** 0.25

A_CQ = 0
A_CKV = A_CQ + MLA_Q_RANK
A_KR = A_CKV + MLA_KV_RANK
B_X = A_KR + MLA_ROPE
B_GATE = B_X + GROUP
C_OFF = B_GATE + GROUP
C_R = 0
C_K = GROUP
C_V = 2 * GROUP
C_WD = 3 * GROUP
C_AD = C_WD + 2 * RWKV_DECAY_LORA
C_GD = C_AD + 2 * RWKV_AAA_LORA
C_COLS = C_GD + RWKV_GATE_LORA
D_OFF = C_OFF + C_COLS
D_COLS = (HY_ORDER + 1) * GROUP


RWKV_TIME_BLOCK = 16
V_TILES = RWKV_HEAD // SUBLANES
RWKV_KEY_GROUP = 16


def _rwkv_scan_kernel(rkf, rkb, vvf, vvb, wf, wb, kaf, kab, kdf, kdb, yf_ref, yb_ref, s_ref, m_ref):
    @pl.when(pl.program_id(0) == 0)
    def _():
        s_ref[...] = jnp.zeros_like(s_ref)

    n_t = rkf.shape[0]
    p = rkf.shape[2]
    fwd_lanes = lax.broadcasted_iota(jnp.int32, (RWKV_HEAD, p), 1) < p // 2

    for j in range(n_t):
        jb = n_t - 1 - j
        rk_f, rk_b = rkf[j], rkb[jb]
        m_ref[0, j] = jnp.where(fwd_lanes, rk_f, pltpu.roll(rk_b, p // 2, 1))
        m_ref[2, j] = jnp.where(fwd_lanes, pltpu.roll(rk_f, p // 2, 1), rk_b)
        for q, (f_ref, b_ref) in ((1, (wf, wb)), (3, (kaf, kab)), (4, (vvf, vvb)), (5, (kdf, kdb))):
            m_ref[q, j] = jnp.where(fwd_lanes, f_ref[j], b_ref[jb])

    r_ref, w_ref, kk_ref, kka_ref, v_ref, kd_ref = (m_ref.at[q] for q in range(6))

    def step(t, carry):
        def row(ref, k):
            return jnp.broadcast_to(ref[t, pl.ds(k, 1), :], (SUBLANES, p))[None]

        def sa_group(g, sa):
            k0 = pl.multiple_of(g * RWKV_KEY_GROUP, RWKV_KEY_GROUP)
            for j in range(RWKV_KEY_GROUP):
                sa = sa + s_ref[k0 + j] * row(kk_ref, k0 + j)
            return sa

        zero = jnp.zeros((V_TILES, SUBLANES, p), F32)
        sa = lax.fori_loop(0, RWKV_HEAD // RWKV_KEY_GROUP, sa_group, zero)
        vt = v_ref[t].reshape(V_TILES, SUBLANES, p)

        def update_group(g, y):
            k0 = pl.multiple_of(g * RWKV_KEY_GROUP, RWKV_KEY_GROUP)
            for j in range(RWKV_KEY_GROUP):
                k = k0 + j
                sn = s_ref[k] * row(w_ref, k) - sa * row(kka_ref, k) + vt * row(kd_ref, k)
                s_ref[k] = sn
                y = y + sn * row(r_ref, k)
            return y

        y = lax.fori_loop(0, RWKV_HEAD // RWKV_KEY_GROUP, update_group, zero).reshape(RWKV_HEAD, p)
        yf_ref[t] = y
        yb_ref[n_t - 1 - t] = y
        return carry

    lax.fori_loop(0, n_t, step, 0)


def rwkv_scan_pallas(rk, vv, w, kka, kd, n_ctx):
    n_steps, n, p = rk.shape
    tb = RWKV_TIME_BLOCK
    assert n == RWKV_HEAD and p == LANES and n_steps % tb == 0 and n_ctx % tb == 0
    nb, nc = n_steps // tb, n_ctx // tb
    fwd = pl.BlockSpec((tb, n, p), lambda g: (g, 0, 0))
    bwd = pl.BlockSpec((tb, n, p), lambda g: (jnp.where(g < nc, nc - 1 - g, nb + nc - 1 - g), 0, 0))
    out = jax.ShapeDtypeStruct((n_steps, n, p), F32)
    return pl.pallas_call(
        _rwkv_scan_kernel,
        grid=(nb,),
        in_specs=[fwd, bwd] * 5,
        out_specs=[fwd, bwd],
        out_shape=[out, out],
        scratch_shapes=[pltpu.VMEM((n, V_TILES, SUBLANES, LANES), F32), pltpu.VMEM((6, tb, n, LANES), F32)],
        compiler_params=pltpu.CompilerParams(dimension_semantics=("arbitrary",), vmem_limit_bytes=VMEM_LIMIT_BYTES),
        name="rwkv_scan",
    )(rk, rk, vv, vv, w, w, kka, kka, kd, kd)


ATTN_TQ = 256
ATTN_TK = 256


def _attn_kernel(q_ref, k_ref, v_ref, g_ref, o_ref, s_ref):
    tq = q_ref.shape[1]
    n_chunks = k_ref.shape[1]
    n_tiles = ATTN_TK // LANES
    pair_out = []
    for hp in range(MLA_HEADS // 2):
        o_pair = jnp.zeros((tq, LANES), F32)
        for h in (2 * hp, 2 * hp + 1):
            qh = q_ref[0, :, pl.ds(LANES * h, LANES)]

            m_acc = jnp.full((tq, LANES), -jnp.inf, F32)
            for c in range(n_chunks):
                s = jnp.dot(qh, k_ref[0, c, pl.ds(LANES * h, LANES), :], preferred_element_type=F32)
                s_ref[c] = s
                for j in range(n_tiles):
                    m_acc = jnp.maximum(m_acc, s[:, LANES * j:LANES * (j + 1)])
            m_full = jnp.broadcast_to(jnp.max(m_acc, -1, keepdims=True), (tq, LANES))

            l_acc = jnp.zeros((tq, LANES), F32)
            acc = jnp.zeros((tq, LANES), F32)
            for c in range(n_chunks):
                s = s_ref[c]
                ps = []
                for j in range(n_tiles):
                    p = jnp.exp2(s[:, LANES * j:LANES * (j + 1)] - m_full)
                    l_acc = l_acc + p
                    ps.append(p.astype(BF16))
                vh = v_ref[0, pl.ds(c * ATTN_TK, ATTN_TK), pl.ds(LANES * h, LANES)]
                acc = acc + jnp.dot(jnp.concatenate(ps, -1), vh, preferred_element_type=F32)
            o_pair = o_pair + acc / jnp.sum(l_acc, -1, keepdims=True)
        pair_out.append(o_pair)
    o = jnp.concatenate(pair_out, -1)
    o_ref[0] = o * lax.rsqrt(jnp.mean(jnp.square(o), -1, keepdims=True) + 1e-6) * g_ref[...]


def mla_attention_pallas(q, kt, v, out_norm, q_start, n_q, n_kv):
    B, _, W = q.shape
    return pl.pallas_call(
        _attn_kernel,
        grid=(B, n_q),
        in_specs=[
            pl.BlockSpec((1, ATTN_TQ, W), lambda b, i: (b, i + q_start, 0)),
            pl.BlockSpec((1, n_kv, W, ATTN_TK), lambda b, i: (b, 0, 0, 0)),
            pl.BlockSpec((1, n_kv * ATTN_TK, W), lambda b, i: (b, 0, 0)),
            pl.BlockSpec((1, GROUP), lambda b, i: (0, 0)),
        ],
        out_specs=pl.BlockSpec((1, ATTN_TQ, GROUP), lambda b, i: (b, i, 0)),
        out_shape=jax.ShapeDtypeStruct((B, n_q * ATTN_TQ, GROUP), F32),
        scratch_shapes=[pltpu.VMEM((n_kv, ATTN_TQ, ATTN_TK), F32)],
        compiler_params=pltpu.CompilerParams(dimension_semantics=("arbitrary", "arbitrary"),
                                             vmem_limit_bytes=VMEM_LIMIT_BYTES),
        name="mla_attention",
    )(q, kt, v, out_norm[None])


MLA_PA_COLS = 4 * LANES
MLA_W = MLA_HEADS * LANES


def _rot_cols(w):
    h = w.shape[1] // 2
    return jnp.concatenate([-w[:, h:], w[:, :h]], 1)


def mla_section_weight(w_a):
    z = jnp.zeros((w_a.shape[0], LANES // 2), w_a.dtype)
    kr = w_a[:, A_KR:A_KR + MLA_ROPE]
    return jnp.concatenate([w_a[:, A_CKV:A_CKV + MLA_KV_RANK], kr, _rot_cols(kr), z,
                            w_a[:, A_CQ:A_CQ + MLA_Q_RANK], z], 1)


def mla_prep_weights(w_uq, w_ukv):
    H, DN, DR, DV = MLA_HEADS, MLA_NOPE, MLA_ROPE, MLA_V
    wq = jnp.zeros((2 * LANES, MLA_W), F32)
    wq_rot = jnp.zeros((2 * LANES, MLA_W), F32)
    wk = jnp.zeros((2 * LANES, MLA_W), F32)
    wv = jnp.zeros((LANES, MLA_W), F32)
    place = jnp.eye(DR, dtype=F32)
    for h in range(H):
        q_h = w_uq[:, h * (DN + DR):(h + 1) * (DN + DR)]
        wq = wq.at[:MLA_Q_RANK, LANES * h:LANES * h + DN + DR].set(q_h)
        wq_rot = wq_rot.at[:MLA_Q_RANK, LANES * h + DN:LANES * h + DN + DR].set(_rot_cols(q_h[:, DN:]))
        kv_h = w_ukv[:, h * (DN + DV):(h + 1) * (DN + DV)]
        wk = wk.at[:MLA_KV_RANK, LANES * h:LANES * h + DN].set(kv_h[:, :DN])
        wk = wk.at[LANES:LANES + DR, LANES * h + DN:LANES * h + DN + DR].set(place)
        v0 = LANES * h + DV * (h % 2)
        wv = wv.at[:, v0:v0 + DV].set(kv_h[:, DN:])
    return wq.astype(BF16), wq_rot.astype(BF16), wk.astype(BF16), wv.astype(BF16)


def mla_tables(cos, sin, n_ctx):
    H, DN, DR = MLA_HEADS, MLA_NOPE, MLA_ROPE
    L = cos.shape[0]
    cf = jnp.concatenate([jnp.ones((n_ctx, DR), F32), jnp.concatenate([cos, cos], -1)], 0)
    sf = jnp.concatenate([jnp.zeros((n_ctx, DR), F32), jnp.concatenate([sin, sin], -1)], 0)
    S = n_ctx + L
    tab_k = jnp.concatenate([cf, sf, jnp.zeros((S, LANES - 2 * DR), F32)], -1)
    scale = (DN + DR) ** -0.5 * math.log2(math.e)
    zpad = jnp.zeros((S, LANES - DN - DR), F32)
    qc_h = jnp.concatenate([jnp.full((S, DN), scale, F32), cf * scale, zpad], -1)
    qs_h = jnp.concatenate([jnp.zeros((S, DN), F32), sf * scale, zpad], -1)
    return tab_k, jnp.tile(qc_h, (1, H)), jnp.tile(qs_h, (1, H))


def _mla_prep_kernel(pa_ref, tk_ref, tqc_ref, tqs_ref, kvn_ref, qn_ref, wq_ref, wqr_ref, wk_ref, wv_ref,
                     q_ref, kt_ref, v_ref):
    pa = pa_ref[0]
    ckv = pa[:, :LANES]
    ckv = ckv * lax.rsqrt(jnp.mean(jnp.square(ckv), -1, keepdims=True) + 1e-6) * kvn_ref[...]
    t = pa[:, LANES:2 * LANES] * tk_ref[...]
    kr = t + pltpu.roll(t, LANES - MLA_ROPE, 1)
    kr = jnp.where(lax.broadcasted_iota(jnp.int32, kr.shape, 1) < MLA_ROPE, kr, 0.0)
    k = jnp.dot(jnp.concatenate([ckv, kr], -1).astype(BF16), wk_ref[...], preferred_element_type=F32)
    kt_ref[0, 0] = k.T.astype(BF16)
    v_ref[0] = jnp.dot(ckv.astype(BF16), wv_ref[...], preferred_element_type=F32).astype(BF16)
    cq = pa[:, 2 * LANES:]
    ms = jnp.sum(jnp.square(cq), -1, keepdims=True) * (1.0 / MLA_Q_RANK)
    cq = (cq * lax.rsqrt(ms + 1e-6) * qn_ref[...]).astype(BF16)
    qa = jnp.dot(cq, wq_ref[...], preferred_element_type=F32)
    qb = jnp.dot(cq, wqr_ref[...], preferred_element_type=F32)
    q_ref[0] = (qa * tqc_ref[...] + qb * tqs_ref[...]).astype(BF16)


def mla_prep_pallas(pa, tables, q_norm, kv_norm, weights):
    B, S, _ = pa.shape
    T = ATTN_TK
    assert S % T == 0
    tab_k, tab_qc, tab_qs = tables
    wq, wq_rot, wk, wv = weights
    qn = jnp.concatenate([q_norm, jnp.zeros((2 * LANES - MLA_Q_RANK,), F32)])[None]

    def full(a):
        return pl.BlockSpec(a.shape, lambda b, i: (0,) * a.ndim)

    def rows(width):
        return pl.BlockSpec((T, width), lambda b, i: (i, 0))

    return pl.pallas_call(
        _mla_prep_kernel,
        grid=(B, S // T),
        in_specs=[pl.BlockSpec((1, T, MLA_PA_COLS), lambda b, i: (b, i, 0)),
                  rows(LANES), rows(MLA_W), rows(MLA_W),
                  pl.BlockSpec((1, LANES), lambda b, i: (0, 0)), full(qn), full(wq), full(wq_rot), full(wk), full(wv)],
        out_specs=[pl.BlockSpec((1, T, MLA_W), lambda b, i: (b, i, 0)),
                   pl.BlockSpec((1, 1, MLA_W, T), lambda b, i: (b, i, 0, 0)),
                   pl.BlockSpec((1, T, MLA_W), lambda b, i: (b, i, 0))],
        out_shape=[jax.ShapeDtypeStruct((B, S, MLA_W), BF16),
                   jax.ShapeDtypeStruct((B, S // T, MLA_W, T), BF16),
                   jax.ShapeDtypeStruct((B, S, MLA_W), BF16)],
        compiler_params=pltpu.CompilerParams(dimension_semantics=("arbitrary", "arbitrary")),
        name="mla_prep",
    )(pa, tab_k, tab_qc, tab_qs, kv_norm[None], qn, wq, wq_rot, wk, wv)


LRU_CHUNK = 256
LRU_HALO = SUBLANES


def _lru_kernel(xl_ref, xc_ref, wbd_ref, bias_ref, c8_ref, cw_ref, cb_ref, gn_ref,
                yl_ref, yc_ref, xs_l, xs_c, hf_l, hf_c, a_s, b_s, hb_s):
    C = GROUP
    CH = LRU_CHUNK
    L = xl_ref.shape[1]
    Lc = xc_ref.shape[1]

    def stage(x_ref, xs, n):
        xs[pl.ds(0, LRU_HALO), :] = jnp.zeros((LRU_HALO, C), F32)
        xs[pl.ds(LRU_HALO + n, LRU_HALO), :] = jnp.zeros((LRU_HALO, C), F32)

        def cp(i, c):
            r0 = pl.multiple_of(i * CH, CH)
            xs[pl.ds(LRU_HALO + r0, CH), :] = x_ref[0, pl.ds(r0, CH), pl.ds(0, C)]
            return c

        lax.fori_loop(0, n // CH, cp, 0)

    stage(xl_ref, xs_l, L)
    stage(xc_ref, xs_c, Lc)

    def coeffs(xs, base, d):
        xv = xs[pl.ds(base, CH + 2 * LRU_HALO), :]
        u = cb_ref[...]
        for j in range(LRU_CONV):
            o = LRU_HALO - LRU_CONV_LEFT + j
            u = u + xv[o:o + CH] * cw_ref[pl.ds(j, 1), :]
        z = jnp.dot(u.astype(BF16), wbd_ref[:, pl.ds(d * 2 * C, 2 * C)], preferred_element_type=F32)
        z = z + bias_ref[:, pl.ds(d * 2 * C, 2 * C)]
        r = jax.nn.sigmoid(z[:, :C])
        i = jax.nn.sigmoid(z[:, C:])
        log_a = r * c8_ref[pl.ds(d, 1), :]
        a = jnp.exp(log_a)
        a_s[...] = a
        b_s[...] = jnp.sqrt(-jnp.tanh(log_a) * (a * a + 1.0)) * (i * u)

    def row_scan(h, out_ref, out_base, reverse):
        n_groups = CH // SUBLANES

        def group(g, h):
            r0 = pl.multiple_of((n_groups - 1 - g if reverse else g) * SUBLANES, SUBLANES)
            for j in (reversed(range(SUBLANES)) if reverse else range(SUBLANES)):
                a_t = jnp.broadcast_to(a_s[pl.ds(r0 + j, 1), :], (SUBLANES, C))
                b_t = jnp.broadcast_to(b_s[pl.ds(r0 + j, 1), :], (SUBLANES, C))
                h = a_t * h + b_t
                out_ref[pl.ds(out_base + r0 + j, 1), :] = h[0:1, :]
            return h

        return lax.fori_loop(0, n_groups, group, h)

    h0 = jnp.zeros((SUBLANES, C), F32)

    h = h0
    for ci in range(Lc // CH):
        coeffs(xs_c, ci * CH, 0)
        h = row_scan(h, hf_c, ci * CH, False)

    def fwd_chunk(ci, h):
        base = pl.multiple_of(ci * CH, CH)
        coeffs(xs_l, base, 0)
        return row_scan(h, hf_l, base, False)

    lax.fori_loop(0, L // CH, fwd_chunk, h)

    def combine(x_ref, hf, base, y_ref):
        hl = hf[pl.ds(base, CH), :] + hb_s[...]
        g = jax.nn.gelu(x_ref[0, pl.ds(base, CH), pl.ds(C, C)])
        v = hl * g
        y = v * lax.rsqrt(jnp.mean(jnp.square(v), -1, keepdims=True) + 1e-6) * gn_ref[...]
        y_ref[0, pl.ds(base, CH), :] = y

    h = h0
    for ci in reversed(range(Lc // CH)):
        coeffs(xs_c, ci * CH, 1)
        h = row_scan(h, hb_s, 0, True)
        combine(xc_ref, hf_c, ci * CH, yc_ref)

    def bwd_chunk(k, h):
        base = pl.multiple_of((L // CH - 1 - k) * CH, CH)
        coeffs(xs_l, base, 1)
        h = row_scan(h, hb_s, 0, True)
        combine(xl_ref, hf_l, base, yl_ref)
        return h

    lax.fori_loop(0, L // CH, bwd_chunk, h)


def rglru_pallas(xg_l, xg_c, conv_w, conv_b, w_r, b_r, w_i, b_i, lam, out_norm):
    B, L, _ = xg_l.shape
    Lc = xg_c.shape[1]
    C = GROUP
    assert L % LRU_CHUNK == 0 and Lc % LRU_CHUNK == 0

    def bd(w):
        return jax.scipy.linalg.block_diag(*[w[n] for n in range(LRU_BLOCKS)])

    wbd = jnp.concatenate([bd(w_r[0]), bd(w_i[0]), bd(w_r[1]), bd(w_i[1])], 1).astype(BF16)
    bias = jnp.concatenate([b_r[0], b_i[0], b_r[1], b_i[1]])[None]
    c8 = -LRU_C * jax.nn.softplus(-lam)

    def full(shape):
        return pl.BlockSpec(shape, lambda b: (0,) * len(shape))

    return pl.pallas_call(
        _lru_kernel,
        grid=(B,),
        in_specs=[
            pl.BlockSpec((1, L, 2 * C), lambda b: (b, 0, 0)),
            pl.BlockSpec((1, Lc, 2 * C), lambda b: (b, 0, 0)),
            full((C, 4 * C)), full((1, 4 * C)), full((2, C)), full((LRU_CONV, C)), full((1, C)), full((1, C)),
        ],
        out_specs=[
            pl.BlockSpec((1, L, C), lambda b: (b, 0, 0)),
            pl.BlockSpec((1, Lc, C), lambda b: (b, 0, 0)),
        ],
        out_shape=[jax.ShapeDtypeStruct((B, L, C), F32), jax.ShapeDtypeStruct((B, Lc, C), F32)],
        scratch_shapes=[
            pltpu.VMEM((L + 2 * LRU_HALO, C), F32),
            pltpu.VMEM((Lc + 2 * LRU_HALO, C), F32),
            pltpu.VMEM((L, C), F32),
            pltpu.VMEM((Lc, C), F32),
            pltpu.VMEM((LRU_CHUNK, C), F32),
            pltpu.VMEM((LRU_CHUNK, C), F32),
            pltpu.VMEM((LRU_CHUNK, C), F32),
        ],
        compiler_params=pltpu.CompilerParams(dimension_semantics=("arbitrary",), vmem_limit_bytes=VMEM_LIMIT_BYTES),
        name="rglru",
    )(xg_l, xg_c, wbd, bias, c8, conv_w, conv_b[None], out_norm[None])


HY_T = 256
HY_CB = 8


def _hyena_kernel(k_ref, d_ref, v_ref, x1_ref, x2_ref, o_ref, u_s, acc_s):
    n_rows = v_ref.shape[1]
    T = HY_T
    nb = k_ref.shape[2] // (2 * T)
    bsz = n_rows // nb

    def conv(ci, order):
        acc_s[...] = jnp.zeros_like(acc_s)
        for dd in range(-(nb - 1), nb):
            w2 = k_ref[order, pl.ds(ci, 1), pl.ds(T * (dd + nb - 1), 2 * T)]
            x = jnp.broadcast_to(w2, (T, 2 * T))
            r = pltpu.roll(x, 0, 1, stride=1, stride_axis=0)
            tb = r[:, T:].astype(BF16)
            j0, j1 = max(0, -dd), min(nb, nb - dd)
            lhs = u_s[pl.ds(bsz * j0, bsz * (j1 - j0)), :]
            dst = pl.ds(bsz * (j0 + dd), bsz * (j1 - j0))
            acc_s[dst, :] = acc_s[dst, :] + jnp.dot(lhs, tb, preferred_element_type=F32)

    def channel(ci, carry):
        v = v_ref[ci]
        u_s[...] = v.astype(BF16)
        conv(ci, 0)
        u = x1_ref[ci] * (acc_s[...] + v * d_ref[pl.ds(ci, 1), pl.ds(0, 1)])
        u_s[...] = u.astype(BF16)
        conv(ci, 1)
        o_ref[ci] = x2_ref[ci] * (acc_s[...] + u * d_ref[pl.ds(ci, 1), pl.ds(1, 1)])
        return carry

    lax.fori_loop(0, HY_CB, channel, 0)


def hyena_conv_pallas(kfull, d_skip, vT, x1T, x2T):
    C, R, T = vT.shape
    two_l = kfull.shape[2]
    assert T == HY_T and C % HY_CB == 0
    blk = pl.BlockSpec((HY_CB, R, T), lambda c: (c, 0, 0))
    return pl.pallas_call(
        _hyena_kernel,
        grid=(C // HY_CB,),
        in_specs=[
            pl.BlockSpec((HY_ORDER, HY_CB, two_l), lambda c: (0, c, 0)),
            pl.BlockSpec((HY_CB, HY_ORDER), lambda c: (c, 0)),
            blk, blk, blk,
        ],
        out_specs=blk,
        out_shape=jax.ShapeDtypeStruct((C, R, T), F32),
        scratch_shapes=[pltpu.VMEM((R, T), BF16), pltpu.VMEM((R, T), F32)],
        compiler_params=pltpu.CompilerParams(dimension_semantics=("arbitrary",), vmem_limit_bytes=VMEM_LIMIT_BYTES),
        name="hyena_conv",
    )(kfull, d_skip.T, vT, x1T, x2T)


FFN_TF_MAX = 1408


def _ffn_tile(hidden):
    return max(t for t in range(LANES, FFN_TF_MAX + 1, LANES) if hidden % t == 0)


def _swiglu_kernel(be_ref, nu_ref, x_ref, wg_ref, wu_ref, wd_ref, o_ref, acc_ref):
    i = pl.program_id(0)
    f = pl.program_id(1)

    @pl.when(i < nu_ref[0])
    def _():
        x = x_ref[...]
        g = jnp.dot(x, wg_ref[0], preferred_element_type=F32)
        u = jnp.dot(x, wu_ref[0], preferred_element_type=F32)
        h = (jax.nn.silu(g) * u).astype(BF16)
        part = jnp.dot(h, wd_ref[0], preferred_element_type=F32)

        @pl.when(f == 0)
        def _():
            acc_ref[...] = part

        @pl.when(f > 0)
        def _():
            acc_ref[...] = acc_ref[...] + part

    @pl.when(f == pl.num_programs(1) - 1)
    def _():
        o_ref[...] = jnp.where(i < nu_ref[0], acc_ref[...], 0.0)


def grouped_swiglu_pallas(xs, block_expert, n_used, wg, wu, wd):
    n_rows, D = xs.shape
    F = wg.shape[2]
    TM = MOE_BLOCK
    assert n_rows % TM == 0
    tf = _ffn_tile(F)
    n_blocks = n_rows // TM
    grid_spec = pltpu.PrefetchScalarGridSpec(
        num_scalar_prefetch=2,
        grid=(n_blocks, F // tf),
        in_specs=[
            pl.BlockSpec((TM, D), lambda i, f, be, nu: (i, 0)),
            pl.BlockSpec((1, D, tf), lambda i, f, be, nu: (be[i], 0, f)),
            pl.BlockSpec((1, D, tf), lambda i, f, be, nu: (be[i], 0, f)),
            pl.BlockSpec((1, tf, D), lambda i, f, be, nu: (be[i], f, 0)),
        ],
        out_specs=pl.BlockSpec((TM, D), lambda i, f, be, nu: (i, 0)),
        scratch_shapes=[pltpu.VMEM((TM, D), F32)],
    )
    return pl.pallas_call(
        _swiglu_kernel,
        grid_spec=grid_spec,
        out_shape=jax.ShapeDtypeStruct((n_rows, D), F32),
        compiler_params=pltpu.CompilerParams(dimension_semantics=("arbitrary", "arbitrary"),
                                             vmem_limit_bytes=VMEM_LIMIT_BYTES),
        name="grouped_swiglu",
    )(block_expert.astype(jnp.int32), jnp.reshape(n_used, (1,)).astype(jnp.int32), xs, wg, wu, wd)


PROJ_TM = 512


def _residual_ln(x, branch, gate_ref, g_ref, b_ref):
    s = ALPHA * x + gate_ref[0] * branch
    mu = jnp.mean(s, -1, keepdims=True)
    d = s - mu
    var = jnp.mean(jnp.square(d), -1, keepdims=True)
    return d * lax.rsqrt(var + 1e-5) * g_ref[...] + b_ref[...]


def _in_proj_kernel(*refs, has_ln, n_w):
    x_ref = refs[0]
    pos = 1
    x = x_ref[...]
    if has_ln:
        branch_ref, gate_ref, g_ref, b_ref = refs[1:5]
        pos = 5
        x = _residual_ln(x, branch_ref[...], gate_ref, g_ref, b_ref)
    shift_ref, scale_ref = refs[pos:pos + 2]
    w_refs = refs[pos + 2:pos + 2 + n_w]
    out_refs = refs[pos + 2 + n_w:]
    if has_ln:
        out_refs[0][...] = x
        out_refs = out_refs[1:]
    h = (x * (1.0 + scale_ref[0]) + shift_ref[0]).astype(BF16)
    for w_ref, o_ref in zip(w_refs, out_refs):
        o_ref[...] = jnp.dot(h, w_ref[...], preferred_element_type=F32)


def in_proj_pallas(x, shift, scale, weights, residual=None):
    M, D = x.shape
    tm = PROJ_TM
    G = shift.shape[0]
    tiles_per_group = M // G // tm
    assert M % (G * tm) == 0
    row = pl.BlockSpec((tm, D), lambda i: (i, 0))
    grp = pl.BlockSpec((1, 1, D), lambda i: (i // tiles_per_group, 0, 0))
    vec = pl.BlockSpec((1, D), lambda i: (0, 0))
    args, specs = [x], [row]
    has_ln = residual is not None
    if has_ln:
        branch, first_row, gate, ln_g, ln_b = residual
        assert first_row % tm == 0
        off = first_row // tm
        args += [branch, gate, ln_g[None], ln_b[None]]
        specs += [pl.BlockSpec((tm, D), lambda i: (i + off, 0)), grp, vec, vec]
    args += [shift, scale] + list(weights)
    specs += [grp, grp] + [pl.BlockSpec(w.shape, lambda i: (0, 0)) for w in weights]
    out_shape = [jax.ShapeDtypeStruct((M, w.shape[1]), F32) for w in weights]
    out_specs = [pl.BlockSpec((tm, w.shape[1]), lambda i: (i, 0)) for w in weights]
    if has_ln:
        out_shape = [jax.ShapeDtypeStruct((M, D), F32)] + out_shape
        out_specs = [row] + out_specs
    outs = pl.pallas_call(
        partial(_in_proj_kernel, has_ln=has_ln, n_w=len(weights)),
        grid=(M // tm,),
        in_specs=specs,
        out_specs=out_specs,
        out_shape=out_shape,
        compiler_params=pltpu.CompilerParams(dimension_semantics=("arbitrary",), vmem_limit_bytes=VMEM_LIMIT_BYTES),
        name="in_proj",
    )(*args)
    return (outs[0], outs[1:]) if has_ln else (None, outs)


def _out_proj_kernel(a_ref, b_ref, c_ref, d_ref, x_ref, w_ref, gate_ref, g_ref, bb_ref, shift_ref, scale_ref,
                     xn_ref, f_ref):
    y = jnp.concatenate([a_ref[...], b_ref[...], c_ref[...], d_ref[...]], -1).astype(BF16)
    y = jnp.dot(y, w_ref[...], preferred_element_type=F32)
    xn = _residual_ln(x_ref[...], y, gate_ref, g_ref, bb_ref)
    xn_ref[...] = xn
    f_ref[...] = (xn * (1.0 + scale_ref[0]) + shift_ref[0]).astype(BF16)


def out_proj_pallas(parts, x, w_out, gate, ln_g, ln_b, shift, scale):
    M, D = x.shape
    tm = PROJ_TM
    G = gate.shape[0]
    tiles_per_group = M // G // tm
    assert M % (G * tm) == 0
    row = pl.BlockSpec((tm, D), lambda i: (i, 0))
    part = pl.BlockSpec((tm, GROUP), lambda i: (i, 0))
    grp = pl.BlockSpec((1, 1, D), lambda i: (i // tiles_per_group, 0, 0))
    vec = pl.BlockSpec((1, D), lambda i: (0, 0))
    return pl.pallas_call(
        _out_proj_kernel,
        grid=(M // tm,),
        in_specs=[part] * 4 + [row, pl.BlockSpec(w_out.shape, lambda i: (0, 0)), grp, vec, vec, grp, grp],
        out_specs=[row, row],
        out_shape=[jax.ShapeDtypeStruct((M, D), F32), jax.ShapeDtypeStruct((M, D), BF16)],
        compiler_params=pltpu.CompilerParams(dimension_semantics=("arbitrary",), vmem_limit_bytes=VMEM_LIMIT_BYTES),
        name="out_proj",
    )(*parts, x, w_out, gate, ln_g[None], ln_b[None], shift, scale)


def modulate(x, shift, scale):
    return x * (1.0 + scale) + shift


def layer_norm(x, g, b, eps=1e-5):
    mu = jnp.mean(x, -1, keepdims=True)
    var = jnp.mean(jnp.square(x - mu), -1, keepdims=True)
    return (x - mu) * lax.rsqrt(var + eps) * g + b


def rms_norm(x, g, eps=1e-6):
    return x * lax.rsqrt(jnp.mean(jnp.square(x), -1, keepdims=True) + eps) * g


def dwconv(x, w, b, left):
    K = w.shape[0]
    L = x.shape[1]
    xp = jnp.pad(x, ((0, 0), (left, K - 1 - left), (0, 0)))
    return sum(xp[:, j:j + L] * w[j] for j in range(K)) + b


def token_shift(z, mu_prev, mu_next):
    zp = jnp.pad(z, ((0, 0), (1, 1), (0, 0)))
    return z + mu_prev * (zp[:, :-2] - z) + mu_next * (zp[:, 2:] - z)


def axial_rope(rows):
    r, col = jnp.meshgrid(jnp.arange(rows, dtype=F32), jnp.arange(GRID_W, dtype=F32), indexing='ij')
    half = MLA_ROPE // 2
    inv = 1.0 / (ROPE_BASE ** (jnp.arange(0, half, 2, dtype=F32) / half))
    ang = jnp.concatenate([r.reshape(-1, 1) * inv, col.reshape(-1, 1) * inv], -1)
    return jnp.cos(ang), jnp.sin(ang)


def mla_mixer(Pac, Pal, tables, q_norm, kv_norm, w_uq, w_ukv, out_norm, ctx_out):
    Lc, L = Pac.shape[1], Pal.shape[1]
    q, kt, v = mla_prep_pallas(jnp.concatenate([Pac, Pal], 1), tables, q_norm, kv_norm, mla_prep_weights(w_uq, w_ukv))
    nc, nl = Lc // ATTN_TQ, L // ATTN_TQ
    yl = mla_attention_pallas(q, kt, v, out_norm, nc, nl, nc + nl)
    yc = mla_attention_pallas(q, kt, v, out_norm, 0, nc, nc) if ctx_out else None
    return yl, yc


def rwkv_finish(y, bonus, v, gd, g_up, ln_g, ln_b):
    B, L = y.shape[:2]
    mu = jnp.mean(y, -1, keepdims=True)
    var = jnp.mean(jnp.square(y - mu), -1, keepdims=True)
    yn = (y - mu) * lax.rsqrt(var + RWKV_GN_EPS) * ln_g.reshape(RWKV_HEADS, RWKV_HEAD) + ln_b.reshape(RWKV_HEADS, RWKV_HEAD)
    g = jax.nn.sigmoid(gd) @ g_up
    return (yn + bonus * v).reshape(B, L, GROUP) * g


def rwkv_mixer(Pl, Pc, mu_prev, mu_next, w0, w_up, a0, a_up, g_up, k_k, k_a, r_k, ln_g, ln_b, ctx_out):
    B, L = Pl.shape[:2]
    Lc = Pc.shape[1]
    S = Lc + L
    H, N = RWKV_HEADS, RWKV_HEAD
    z = jnp.concatenate([token_shift(Pc, mu_prev, mu_next), token_shift(Pl, mu_prev, mu_next)], 1)

    def heads(t):
        return t.reshape(B, S, H, N)

    r = heads(z[..., C_R:C_R + GROUP])
    k = heads(z[..., C_K:C_K + GROUP])
    v = heads(z[..., C_V:C_V + GROUP])
    kk = k * k_k.reshape(H, N)
    kk = kk * lax.rsqrt(jnp.maximum(jnp.sum(jnp.square(kk), -1, keepdims=True), 1e-24))
    w, kka, kd = [], [], []
    for d in range(2):
        wd = z[..., C_WD + d * RWKV_DECAY_LORA:C_WD + (d + 1) * RWKV_DECAY_LORA]
        ad = z[..., C_AD + d * RWKV_AAA_LORA:C_AD + (d + 1) * RWKV_AAA_LORA]
        log_w = -jnp.exp(-jax.nn.softplus(-(w0[d] + jnp.tanh(wd) @ w_up[d])) - 0.5)
        a = heads(jax.nn.sigmoid(a0[d] + ad @ a_up[d]))
        w.append(heads(jnp.exp(log_w)))
        kka.append(kk * a)
        kd.append(k * (1.0 + (a - 1.0) * k_a.reshape(H, N)))

    def to_scan(lo, hi):
        return jnp.concatenate([jnp.transpose(t, (1, 3, 0, 2)).reshape(S, N, B * H) for t in (lo, hi)], -1)

    yf, yb = rwkv_scan_pallas(to_scan(r, kk), to_scan(v, v), to_scan(*w), to_scan(*kka), to_scan(*kd), Lc)
    y = yf[..., :B * H] + yb[..., B * H:]
    y = jnp.transpose(y.reshape(S, N, B, H), (2, 0, 3, 1))
    bonus = sum(jnp.sum(r * kd_d * r_k, -1, keepdims=True) for kd_d in kd)
    gd = z[..., C_GD:C_GD + RWKV_GATE_LORA]
    out_l = rwkv_finish(y[:, Lc:], bonus[:, Lc:], v[:, Lc:], gd[:, Lc:], g_up, ln_g, ln_b)
    out_c = None
    if ctx_out:
        out_c = rwkv_finish(y[:, :Lc], bonus[:, :Lc], v[:, :Lc], gd[:, :Lc], g_up, ln_g, ln_b)
    return out_l, out_c


def hyena_filters(L, w1, b1, w2, b2, w3):
    t01 = jnp.linspace(0.0, 1.0, L, dtype=F32)[:, None]
    bands = jnp.linspace(1e-4, HY_BANDS - 1, HY_BANDS, dtype=F32)[None, :]
    wpos = (2.0 * math.pi / L) * jnp.arange(L, dtype=F32)[:, None]
    z = jnp.concatenate([t01, jnp.cos(bands * wpos), -jnp.sin(bands * wpos)], -1)
    h = jnp.sin(HY_SIN_FREQ * (z @ w1 + b1))
    h = jnp.sin(HY_SIN_FREQ * (h @ w2 + b2))
    h = (h @ w3).reshape(L, HY_ORDER, 2, GROUP)
    deltas = jnp.abs(jnp.linspace(HY_DECAY_MIN, HY_DECAY_MAX, GROUP, dtype=F32))
    window = jnp.exp(-t01 * deltas) + HY_SHIFT
    return h * window[:, None, None, :]


def hyena_sequence(Pd, conv_w, conv_b, w1, b1, w2, b2, w3, d_skip):
    B, L, _ = Pd.shape
    C = GROUP
    T = HY_T
    nb = L // T
    z = dwconv(Pd, conv_w, conv_b, 1)
    zT = jnp.transpose(z.reshape(B, nb, T, 3 * C), (3, 1, 0, 2)).reshape(3 * C, nb * B, T)
    h = hyena_filters(L, w1, b1, w2, b2, w3)
    hf = jnp.transpose(h[:, :, 0], (1, 2, 0))
    hb = jnp.transpose(h[:, :, 1], (1, 2, 0))
    kfull = jnp.concatenate([jnp.zeros((HY_ORDER, C, 1), F32), jnp.flip(hb[..., 1:], -1), hf], -1)
    oT = hyena_conv_pallas(kfull, d_skip, zT[:C], zT[C:2 * C], zT[2 * C:])
    return jnp.transpose(oT.reshape(C, nb, B, T), (2, 1, 3, 0)).reshape(B, L, C)


def swiglu(x, wg, wu, wd):
    n_blocks = x.shape[0] // MOE_BLOCK
    return grouped_swiglu_pallas(x.astype(BF16), jnp.zeros((n_blocks,), jnp.int32), jnp.int32(n_blocks),
                                 wg[None].astype(BF16), wu[None].astype(BF16), wd[None].astype(BF16))


def moe_swiglu(x, router, wg, wu, wd):
    N, D = x.shape
    logits = x @ router
    top_v, top_i = lax.top_k(logits, TOP_K)
    gates = jax.nn.softmax(top_v, axis=-1)
    A = N * TOP_K
    e_flat = top_i.reshape(-1)
    tok_flat = jnp.arange(A, dtype=jnp.int32) // TOP_K
    order = jnp.argsort(e_flat)
    e_sorted = e_flat[order]
    counts = jnp.bincount(e_flat, length=N_EXPERTS)
    starts = jnp.cumsum(counts) - counts
    padded = (counts + MOE_BLOCK - 1) // MOE_BLOCK * MOE_BLOCK
    pends = jnp.cumsum(padded)
    pstarts = pends - padded
    dest = (pstarts[e_sorted] + jnp.arange(A, dtype=jnp.int32) - starts[e_sorted]).astype(jnp.int32)
    n_blocks = -(-A // MOE_BLOCK) + N_EXPERTS
    n_slots = n_blocks * MOE_BLOCK
    block_expert = jnp.clip(jnp.searchsorted(pends, jnp.arange(n_blocks) * MOE_BLOCK, side='right'), 0, N_EXPERTS - 1)
    slot_e = jnp.repeat(block_expert, MOE_BLOCK)
    slot_pos = jnp.arange(n_slots, dtype=jnp.int32) - pstarts[slot_e].astype(jnp.int32)
    slot_src = jnp.clip(starts[slot_e].astype(jnp.int32) + slot_pos, 0, A - 1)
    slot_tok = jnp.where(slot_pos < counts[slot_e], tok_flat[order[slot_src]], N)
    xp = jnp.concatenate([x.astype(BF16), jnp.zeros((1, D), BF16)], 0)
    ys = grouped_swiglu_pallas(xp[slot_tok], block_expert, pends[-1] // MOE_BLOCK,
                               wg.astype(BF16), wu.astype(BF16), wd.astype(BF16))
    slot_of = dest[jnp.argsort(order)].reshape(N, TOP_K)
    return ys[slot_of[:, 0]] * gates[:, 0:1] + ys[slot_of[:, 1]] * gates[:, 1:2]


def kernel(x, c, ctx, c_ctx, ada_w, ada_b, w_in, mla_q_norm, mla_kv_norm, mla_w_uq, mla_w_ukv, mla_out_norm, lru_conv_w, lru_conv_b, lru_w_r, lru_b_r, lru_w_i, lru_b_i, lru_lambda, lru_out_norm, rwkv_mu_prev, rwkv_mu_next, rwkv_w0, rwkv_w_up, rwkv_a0, rwkv_a_up, rwkv_g_up, rwkv_k_k, rwkv_k_a, rwkv_r_k, rwkv_ln_g, rwkv_ln_b, hy_conv_w, hy_conv_b, hy_f_w1, hy_f_b1, hy_f_w2, hy_f_b2, hy_f_w3, hy_d, hy_out_norm, w_out, ln1_g, ln1_b, ln2_g, ln2_b, ffn_w_gate, ffn_w_up, ffn_w_down, moe_router, moe_w_gate, moe_w_up, moe_w_down):
    B, L, D = x.shape
    Lc = ctx.shape[1]
    rows = L // GRID_W
    mla_tabs = mla_tables(*axial_rope(rows), Lc)
    s_lat = jax.nn.silu(c)
    s_ctx = jax.nn.silu(c_ctx)
    Ml, Mc = B * L, B * Lc
    xl, xc = x.reshape(Ml, D), ctx.reshape(Mc, D)
    prev = None
    for li in range(DEPTH):
        ctx_out = li < DEPTH - 1
        mod_l = (s_lat @ ada_w[li] + ada_b[li]).reshape(B, 6, 1, D)
        mod_c = (s_ctx @ ada_w[li] + ada_b[li]).reshape(6, 1, D)
        ml = [mod_l[:, q] for q in range(6)]
        mc = [mod_c[q][None] for q in range(6)]

        w_in_l = w_in[li]
        w_secs = [w.astype(BF16) for w in (mla_section_weight(w_in_l[:, :B_X]), w_in_l[:, B_X:C_OFF],
                                           w_in_l[:, C_OFF:D_OFF], w_in_l[:, D_OFF:])]
        res_l = res_c = None
        if prev is not None:
            out, gate_l, gate_c, g2, b2 = prev
            res_l, res_c = (out, 0, gate_l, g2, b2), (out, Ml, gate_c, g2, b2)
        xl_new, (Pal, Pbl, Pcl, Pdl) = in_proj_pallas(xl, ml[0], ml[1], w_secs, res_l)
        xc_new, ctx_secs = in_proj_pallas(xc, mc[0], mc[1], w_secs if ctx_out else w_secs[:3], res_c)
        if prev is not None:
            xl, xc = xl_new, xc_new
        Pal, Pbl, Pcl, Pdl = (p.reshape(B, L, -1) for p in (Pal, Pbl, Pcl, Pdl))
        Pac, Pbc, Pcc = (p.reshape(B, Lc, -1) for p in ctx_secs[:3])
        a_l, a_c = mla_mixer(Pac, Pal, mla_tabs, mla_q_norm[li], mla_kv_norm[li], mla_w_uq[li], mla_w_ukv[li],
                             mla_out_norm[li], ctx_out)
        b_l, b_c = rglru_pallas(Pbl, Pbc, lru_conv_w[li], lru_conv_b[li], lru_w_r[li], lru_b_r[li], lru_w_i[li],
                                lru_b_i[li], lru_lambda[li], lru_out_norm[li])
        c_l, c_c = rwkv_mixer(Pcl, Pcc, rwkv_mu_prev[li], rwkv_mu_next[li], rwkv_w0[li], rwkv_w_up[li], rwkv_a0[li],
                              rwkv_a_up[li], rwkv_g_up[li], rwkv_k_k[li], rwkv_k_a[li], rwkv_r_k[li],
                              rwkv_ln_g[li], rwkv_ln_b[li], ctx_out)
        d_l = rms_norm(hyena_sequence(Pdl, hy_conv_w[li], hy_conv_b[li], hy_f_w1[li], hy_f_b1[li], hy_f_w2[li],
                                      hy_f_b2[li], hy_f_w3[li], hy_d[li]), hy_out_norm[li])
        w_out_l = w_out[li].astype(BF16)
        xl, tokens = out_proj_pallas([t.reshape(Ml, GROUP) for t in (a_l, b_l, c_l, d_l)], xl, w_out_l,
                                     ml[2], ln1_g[li], ln1_b[li], ml[3], ml[4])
        if ctx_out:
            Pdc = ctx_secs[3].reshape(B, Lc, -1)
            d_c = rms_norm(hyena_sequence(Pdc, hy_conv_w[li], hy_conv_b[li], hy_f_w1[li], hy_f_b1[li], hy_f_w2[li],
                                          hy_f_b2[li], hy_f_w3[li], hy_d[li]), hy_out_norm[li])
            xc, fc = out_proj_pallas([t.reshape(Mc, GROUP) for t in (a_c, b_c, c_c, d_c)], xc, w_out_l,
                                     mc[2], ln1_g[li], ln1_b[li], mc[3], mc[4])
            tokens = jnp.concatenate([tokens, fc], 0)
        j = li // 2
        if li % 2 == 0:
            out = swiglu(tokens, ffn_w_gate[j], ffn_w_up[j], ffn_w_down[j])
        else:
            out = moe_swiglu(tokens, moe_router[j], moe_w_gate[j], moe_w_up[j], moe_w_down[j])
        prev = (out, ml[5], mc[5], ln2_g[li], ln2_b[li])
    out, gate_l, _, g2, b2 = prev
    xl, _ = in_proj_pallas(xl, ml[0], ml[1], [], (out, 0, gate_l, g2, b2))
    return xl.reshape(B, L, D)
```

```python
import math
from functools import partial

import jax
import jax.numpy as jnp
from jax import lax
from jax.experimental import pallas as pl
from jax.experimental.pallas import tpu as pltpu

F32 = jnp.float32
BF16 = jnp.bfloat16

SUBLANES = 8
LANES = 128
VMEM_LIMIT_BYTES = 48 * 1024 * 1024

D_MODEL = 1024
DEPTH = 4
GRID_W = 64
GROUP = D_MODEL // 4

MLA_HEADS = 4
MLA_NOPE = 64
MLA_ROPE = 32
MLA_V = 64
MLA_Q_RANK = 192
MLA_KV_RANK = 128
ROPE_BASE = 10000.0
Q_BLOCK = 128

LRU_BLOCKS = 4
LRU_CONV = 4
LRU_CONV_LEFT = 2
LRU_C = 8.0

RWKV_HEADS = 4
RWKV_HEAD = GROUP // RWKV_HEADS
RWKV_DECAY_LORA = 32
RWKV_AAA_LORA = 32
RWKV_GATE_LORA = 64
RWKV_GN_EPS = 64e-5

HY_ORDER = 2
HY_SHORT = 3
HY_BANDS = 16
HY_EMB = 1 + 2 * HY_BANDS
HY_HIDDEN = 64
HY_SIN_FREQ = 1.0
HY_DECAY_MIN = math.log(1e-2) / 1.5
HY_DECAY_MAX = math.log(1e-2) / 0.3
HY_SHIFT = 0.05

N_EXPERTS = 8
TOP_K = 2
MOE_BLOCK = 512

ALPHA = (2.0 * DEPTH) ** 0.25

A_CQ = 0
A_CKV = A_CQ + MLA_Q_RANK
A_KR = A_CKV + MLA_KV_RANK
B_X = A_KR + MLA_ROPE
B_GATE = B_X + GROUP
C_OFF = B_GATE + GROUP
C_R = 0
C_K = GROUP
C_V = 2 * GROUP
C_WD = 3 * GROUP
C_AD = C_WD + 2 * RWKV_DECAY_LORA
C_GD = C_AD + 2 * RWKV_AAA_LORA
C_COLS = C_GD + RWKV_GATE_LORA
D_OFF = C_OFF + C_COLS
D_COLS = (HY_ORDER + 1) * GROUP


RWKV_TIME_BLOCK = 16
V_TILES = RWKV_HEAD // SUBLANES
RWKV_KEY_GROUP = 16


def _rwkv_scan_kernel(rkf, rkb, vvf, vvb, wf, wb, kaf, kab, kdf, kdb, yf_ref, yb_ref, s_ref, m_ref):
    @pl.when(pl.program_id(0) == 0)
    def _():
        s_ref[...] = jnp.zeros_like(s_ref)

    n_t = rkf.shape[0]
    p = rkf.shape[2]
    fwd_lanes = lax.broadcasted_iota(jnp.int32, (RWKV_HEAD, p), 1) < p // 2

    for j in range(n_t):
        jb = n_t - 1 - j
        rk_f, rk_b = rkf[j], rkb[jb]
        m_ref[0, j] = jnp.where(fwd_lanes, rk_f, pltpu.roll(rk_b, p // 2, 1))
        m_ref[2, j] = jnp.where(fwd_lanes, pltpu.roll(rk_f, p // 2, 1), rk_b)
        for q, (f_ref, b_ref) in ((1, (wf, wb)), (3, (kaf, kab)), (4, (vvf, vvb)), (5, (kdf, kdb))):
            m_ref[q, j] = jnp.where(fwd_lanes, f_ref[j], b_ref[jb])

    r_ref, w_ref, kk_ref, kka_ref, v_ref, kd_ref = (m_ref.at[q] for q in range(6))

    def step(t, carry):
        def row(ref, k):
            return jnp.broadcast_to(ref[t, pl.ds(k, 1), :], (SUBLANES, p))[None]

        def sa_group(g, sa):
            k0 = pl.multiple_of(g * RWKV_KEY_GROUP, RWKV_KEY_GROUP)
            for j in range(RWKV_KEY_GROUP):
                sa = sa + s_ref[k0 + j] * row(kk_ref, k0 + j)
            return sa

        zero = jnp.zeros((V_TILES, SUBLANES, p), F32)
        sa = lax.fori_loop(0, RWKV_HEAD // RWKV_KEY_GROUP, sa_group, zero)
        vt = v_ref[t].reshape(V_TILES, SUBLANES, p)

        def update_group(g, y):
            k0 = pl.multiple_of(g * RWKV_KEY_GROUP, RWKV_KEY_GROUP)
            for j in range(RWKV_KEY_GROUP):
                k = k0 + j
                sn = s_ref[k] * row(w_ref, k) - sa * row(kka_ref, k) + vt * row(kd_ref, k)
                s_ref[k] = sn
                y = y + sn * row(r_ref, k)
            return y

        y = lax.fori_loop(0, RWKV_HEAD // RWKV_KEY_GROUP, update_group, zero).reshape(RWKV_HEAD, p)
        yf_ref[t] = y
        yb_ref[n_t - 1 - t] = y
        return carry

    lax.fori_loop(0, n_t, step, 0)


def rwkv_scan_pallas(rk, vv, w, kka, kd, n_ctx):
    n_steps, n, p = rk.shape
    tb = RWKV_TIME_BLOCK
    assert n == RWKV_HEAD and p == LANES and n_steps % tb == 0 and n_ctx % tb == 0
    nb, nc = n_steps // tb, n_ctx // tb
    fwd = pl.BlockSpec((tb, n, p), lambda g: (g, 0, 0))
    bwd = pl.BlockSpec((tb, n, p), lambda g: (jnp.where(g < nc, nc - 1 - g, nb + nc - 1 - g), 0, 0))
    out = jax.ShapeDtypeStruct((n_steps, n, p), F32)
    return pl.pallas_call(
        _rwkv_scan_kernel,
        grid=(nb,),
        in_specs=[fwd, bwd] * 5,
        out_specs=[fwd, bwd],
        out_shape=[out, out],
        scratch_shapes=[pltpu.VMEM((n, V_TILES, SUBLANES, LANES), F32), pltpu.VMEM((6, tb, n, LANES), F32)],
        compiler_params=pltpu.CompilerParams(dimension_semantics=("arbitrary",), vmem_limit_bytes=VMEM_LIMIT_BYTES),
        name="rwkv_scan",
    )(rk, rk, vv, vv, w, w, kka, kka, kd, kd)


ATTN_TQ = 256
ATTN_TK = 256


def _attn_kernel(q_ref, k_ref, v_ref, g_ref, o_ref, s_ref):
    tq = q_ref.shape[1]
    n_chunks = k_ref.shape[1]
    n_tiles = ATTN_TK // LANES
    pair_out = []
    for hp in range(MLA_HEADS // 2):
        o_pair = jnp.zeros((tq, LANES), F32)
        for h in (2 * hp, 2 * hp + 1):
            qh = q_ref[0, :, pl.ds(LANES * h, LANES)]

            m_acc = jnp.full((tq, LANES), -jnp.inf, F32)
            for c in range(n_chunks):
                s = jnp.dot(qh, k_ref[0, c, pl.ds(LANES * h, LANES), :], preferred_element_type=F32)
                s_ref[c] = s
                for j in range(n_tiles):
                    m_acc = jnp.maximum(m_acc, s[:, LANES * j:LANES * (j + 1)])
            m_full = jnp.broadcast_to(jnp.max(m_acc, -1, keepdims=True), (tq, LANES))

            l_acc = jnp.zeros((tq, LANES), F32)
            acc = jnp.zeros((tq, LANES), F32)
            for c in range(n_chunks):
                s = s_ref[c]
                ps = []
                for j in range(n_tiles):
                    p = jnp.exp2(s[:, LANES * j:LANES * (j + 1)] - m_full)
                    l_acc = l_acc + p
                    ps.append(p.astype(BF16))
                vh = v_ref[0, pl.ds(c * ATTN_TK, ATTN_TK), pl.ds(LANES * h, LANES)]
                acc = acc + jnp.dot(jnp.concatenate(ps, -1), vh, preferred_element_type=F32)
            o_pair = o_pair + acc / jnp.sum(l_acc, -1, keepdims=True)
        pair_out.append(o_pair)
    o = jnp.concatenate(pair_out, -1)
    o_ref[0] = o * lax.rsqrt(jnp.mean(jnp.square(o), -1, keepdims=True) + 1e-6) * g_ref[...]


def mla_attention_pallas(q, kt, v, out_norm, q_start, n_q, n_kv):
    B, _, W = q.shape
    return pl.pallas_call(
        _attn_kernel,
        grid=(B, n_q),
        in_specs=[
            pl.BlockSpec((1, ATTN_TQ, W), lambda b, i: (b, i + q_start, 0)),
            pl.BlockSpec((1, n_kv, W, ATTN_TK), lambda b, i: (b, 0, 0, 0)),
            pl.BlockSpec((1, n_kv * ATTN_TK, W), lambda b, i: (b, 0, 0)),
            pl.BlockSpec((1, GROUP), lambda b, i: (0, 0)),
        ],
        out_specs=pl.BlockSpec((1, ATTN_TQ, GROUP), lambda b, i: (b, i, 0)),
        out_shape=jax.ShapeDtypeStruct((B, n_q * ATTN_TQ, GROUP), F32),
        scratch_shapes=[pltpu.VMEM((n_kv, ATTN_TQ, ATTN_TK), F32)],
        compiler_params=pltpu.CompilerParams(dimension_semantics=("arbitrary", "arbitrary"),
                                             vmem_limit_bytes=VMEM_LIMIT_BYTES),
        name="mla_attention",
    )(q, kt, v, out_norm[None])


MLA_PA_COLS = 4 * LANES
MLA_W = MLA_HEADS * LANES


def _rot_cols(w):
    h = w.shape[1] // 2
    return jnp.concatenate([-w[:, h:], w[:, :h]], 1)


def mla_section_weight(w_a):
    z = jnp.zeros((w_a.shape[0], LANES // 2), w_a.dtype)
    kr = w_a[:, A_KR:A_KR + MLA_ROPE]
    return jnp.concatenate([w_a[:, A_CKV:A_CKV + MLA_KV_RANK], kr, _rot_cols(kr), z,
                            w_a[:, A_CQ:A_CQ + MLA_Q_RANK], z], 1)


def mla_prep_weights(w_uq, w_ukv):
    H, DN, DR, DV = MLA_HEADS, MLA_NOPE, MLA_ROPE, MLA_V
    wq = jnp.zeros((2 * LANES, MLA_W), F32)
    wq_rot = jnp.zeros((2 * LANES, MLA_W), F32)
    wk = jnp.zeros((2 * LANES, MLA_W), F32)
    wv = jnp.zeros((LANES, MLA_W), F32)
    place = jnp.eye(DR, dtype=F32)
    for h in range(H):
        q_h = w_uq[:, h * (DN + DR):(h + 1) * (DN + DR)]
        wq = wq.at[:MLA_Q_RANK, LANES * h:LANES * h + DN + DR].set(q_h)
        wq_rot = wq_rot.at[:MLA_Q_RANK, LANES * h + DN:LANES * h + DN + DR].set(_rot_cols(q_h[:, DN:]))
        kv_h = w_ukv[:, h * (DN + DV):(h + 1) * (DN + DV)]
        wk = wk.at[:MLA_KV_RANK, LANES * h:LANES * h + DN].set(kv_h[:, :DN])
        wk = wk.at[LANES:LANES + DR, LANES * h + DN:LANES * h + DN + DR].set(place)
        v0 = LANES * h + DV * (h % 2)
        wv = wv.at[:, v0:v0 + DV].set(kv_h[:, DN:])
    return wq.astype(BF16), wq_rot.astype(BF16), wk.astype(BF16), wv.astype(BF16)


def mla_tables(cos, sin, n_ctx):
    H, DN, DR = MLA_HEADS, MLA_NOPE, MLA_ROPE
    L = cos.shape[0]
    cf = jnp.concatenate([jnp.ones((n_ctx, DR), F32), jnp.concatenate([cos, cos], -1)], 0)
    sf = jnp.concatenate([jnp.zeros((n_ctx, DR), F32), jnp.concatenate([sin, sin], -1)], 0)
    S = n_ctx + L
    tab_k = jnp.concatenate([cf, sf, jnp.zeros((S, LANES - 2 * DR), F32)], -1)
    scale = (DN + DR) ** -0.5 * math.log2(math.e)
    zpad = jnp.zeros((S, LANES - DN - DR), F32)
    qc_h = jnp.concatenate([jnp.full((S, DN), scale, F32), cf * scale, zpad], -1)
    qs_h = jnp.concatenate([jnp.zeros((S, DN), F32), sf * scale, zpad], -1)
    return tab_k, jnp.tile(qc_h, (1, H)), jnp.tile(qs_h, (1, H))


def _mla_prep_kernel(pa_ref, tk_ref, tqc_ref, tqs_ref, kvn_ref, qn_ref, wq_ref, wqr_ref, wk_ref, wv_ref,
                     q_ref, kt_ref, v_ref):
    pa = pa_ref[0]
    ckv = pa[:, :LANES]
    ckv = ckv * lax.rsqrt(jnp.mean(jnp.square(ckv), -1, keepdims=True) + 1e-6) * kvn_ref[...]
    t = pa[:, LANES:2 * LANES] * tk_ref[...]
    kr = t + pltpu.roll(t, LANES - MLA_ROPE, 1)
    kr = jnp.where(lax.broadcasted_iota(jnp.int32, kr.shape, 1) < MLA_ROPE, kr, 0.0)
    k = jnp.dot(jnp.concatenate([ckv, kr], -1).astype(BF16), wk_ref[...], preferred_element_type=F32)
    kt_ref[0, 0] = k.T.astype(BF16)
    v_ref[0] = jnp.dot(ckv.astype(BF16), wv_ref[...], preferred_element_type=F32).astype(BF16)
    cq = pa[:, 2 * LANES:]
    ms = jnp.sum(jnp.square(cq), -1, keepdims=True) * (1.0 / MLA_Q_RANK)
    cq = (cq * lax.rsqrt(ms + 1e-6) * qn_ref[...]).astype(BF16)
    qa = jnp.dot(cq, wq_ref[...], preferred_element_type=F32)
    qb = jnp.dot(cq, wqr_ref[...], preferred_element_type=F32)
    q_ref[0] = (qa * tqc_ref[...] + qb * tqs_ref[...]).astype(BF16)


def mla_prep_pallas(pa, tables, q_norm, kv_norm, weights):
    B, S, _ = pa.shape
    T = ATTN_TK
    assert S % T == 0
    tab_k, tab_qc, tab_qs = tables
    wq, wq_rot, wk, wv = weights
    qn = jnp.concatenate([q_norm, jnp.zeros((2 * LANES - MLA_Q_RANK,), F32)])[None]

    def full(a):
        return pl.BlockSpec(a.shape, lambda b, i: (0,) * a.ndim)

    def rows(width):
        return pl.BlockSpec((T, width), lambda b, i: (i, 0))

    return pl.pallas_call(
        _mla_prep_kernel,
        grid=(B, S // T),
        in_specs=[pl.BlockSpec((1, T, MLA_PA_COLS), lambda b, i: (b, i, 0)),
                  rows(LANES), rows(MLA_W), rows(MLA_W),
                  pl.BlockSpec((1, LANES), lambda b, i: (0, 0)), full(qn), full(wq), full(wq_rot), full(wk), full(wv)],
        out_specs=[pl.BlockSpec((1, T, MLA_W), lambda b, i: (b, i, 0)),
                   pl.BlockSpec((1, 1, MLA_W, T), lambda b, i: (b, i, 0, 0)),
                   pl.BlockSpec((1, T, MLA_W), lambda b, i: (b, i, 0))],
        out_shape=[jax.ShapeDtypeStruct((B, S, MLA_W), BF16),
                   jax.ShapeDtypeStruct((B, S // T, MLA_W, T), BF16),
                   jax.ShapeDtypeStruct((B, S, MLA_W), BF16)],
        compiler_params=pltpu.CompilerParams(dimension_semantics=("arbitrary", "arbitrary")),
        name="mla_prep",
    )(pa, tab_k, tab_qc, tab_qs, kv_norm[None], qn, wq, wq_rot, wk, wv)


LRU_CHUNK = 256
LRU_HALO = SUBLANES


def _lru_kernel(xl_ref, xc_ref, wbd_ref, bias_ref, c8_ref, cw_ref, cb_ref, gn_ref,
                yl_ref, yc_ref, xs_l, xs_c, hf_l, hf_c, a_s, b_s, hb_s):
    C = GROUP
    CH = LRU_CHUNK
    L = xl_ref.shape[1]
    Lc = xc_ref.shape[1]

    def stage(x_ref, xs, n):
        xs[pl.ds(0, LRU_HALO), :] = jnp.zeros((LRU_HALO, C), F32)
        xs[pl.ds(LRU_HALO + n, LRU_HALO), :] = jnp.zeros((LRU_HALO, C), F32)

        def cp(i, c):
            r0 = pl.multiple_of(i * CH, CH)
            xs[pl.ds(LRU_HALO + r0, CH), :] = x_ref[0, pl.ds(r0, CH), pl.ds(0, C)]
            return c

        lax.fori_loop(0, n // CH, cp, 0)

    stage(xl_ref, xs_l, L)
    stage(xc_ref, xs_c, Lc)

    def coeffs(xs, base, d):
        xv = xs[pl.ds(base, CH + 2 * LRU_HALO), :]
        u = cb_ref[...]
        for j in range(LRU_CONV):
            o = LRU_HALO - LRU_CONV_LEFT + j
            u = u + xv[o:o + CH] * cw_ref[pl.ds(j, 1), :]
        z = jnp.dot(u.astype(BF16), wbd_ref[:, pl.ds(d * 2 * C, 2 * C)], preferred_element_type=F32)
        z = z + bias_ref[:, pl.ds(d * 2 * C, 2 * C)]
        r = jax.nn.sigmoid(z[:, :C])
        i = jax.nn.sigmoid(z[:, C:])
        log_a = r * c8_ref[pl.ds(d, 1), :]
        a = jnp.exp(log_a)
        a_s[...] = a
        b_s[...] = jnp.sqrt(-jnp.tanh(log_a) * (a * a + 1.0)) * (i * u)

    def row_scan(h, out_ref, out_base, reverse):
        n_groups = CH // SUBLANES

        def group(g, h):
            r0 = pl.multiple_of((n_groups - 1 - g if reverse else g) * SUBLANES, SUBLANES)
            for j in (reversed(range(SUBLANES)) if reverse else range(SUBLANES)):
                a_t = jnp.broadcast_to(a_s[pl.ds(r0 + j, 1), :], (SUBLANES, C))
                b_t = jnp.broadcast_to(b_s[pl.ds(r0 + j, 1), :], (SUBLANES, C))
                h = a_t * h + b_t
                out_ref[pl.ds(out_base + r0 + j, 1), :] = h[0:1, :]
            return h

        return lax.fori_loop(0, n_groups, group, h)

    h0 = jnp.zeros((SUBLANES, C), F32)

    h = h0
    for ci in range(Lc // CH):
        coeffs(xs_c, ci * CH, 0)
        h = row_scan(h, hf_c, ci * CH, False)

    def fwd_chunk(ci, h):
        base = pl.multiple_of(ci * CH, CH)
        coeffs(xs_l, base, 0)
        return row_scan(h, hf_l, base, False)

    lax.fori_loop(0, L // CH, fwd_chunk, h)

    def combine(x_ref, hf, base, y_ref):
        hl = hf[pl.ds(base, CH), :] + hb_s[...]
        g = jax.nn.gelu(x_ref[0, pl.ds(base, CH), pl.ds(C, C)])
        v = hl * g
        y = v * lax.rsqrt(jnp.mean(jnp.square(v), -1, keepdims=True) + 1e-6) * gn_ref[...]
        y_ref[0, pl.ds(base, CH), :] = y

    h = h0
    for ci in reversed(range(Lc // CH)):
        coeffs(xs_c, ci * CH, 1)
        h = row_scan(h, hb_s, 0, True)
        combine(xc_ref, hf_c, ci * CH, yc_ref)

    def bwd_chunk(k, h):
        base = pl.multiple_of((L // CH - 1 - k) * CH, CH)
        coeffs(xs_l, base, 1)
        h = row_scan(h, hb_s, 0, True)
        combine(xl_ref, hf_l, base, yl_ref)
        return h

    lax.fori_loop(0, L // CH, bwd_chunk, h)


def rglru_pallas(xg_l, xg_c, conv_w, conv_b, w_r, b_r, w_i, b_i, lam, out_norm):
    B, L, _ = xg_l.shape
    Lc = xg_c.shape[1]
    C = GROUP
    assert L % LRU_CHUNK == 0 and Lc % LRU_CHUNK == 0

    def bd(w):
        return jax.scipy.linalg.block_diag(*[w[n] for n in range(LRU_BLOCKS)])

    wbd = jnp.concatenate([bd(w_r[0]), bd(w_i[0]), bd(w_r[1]), bd(w_i[1])], 1).astype(BF16)
    bias = jnp.concatenate([b_r[0], b_i[0], b_r[1], b_i[1]])[None]
    c8 = -LRU_C * jax.nn.softplus(-lam)

    def full(shape):
        return pl.BlockSpec(shape, lambda b: (0,) * len(shape))

    return pl.pallas_call(
        _lru_kernel,
        grid=(B,),
        in_specs=[
            pl.BlockSpec((1, L, 2 * C), lambda b: (b, 0, 0)),
            pl.BlockSpec((1, Lc, 2 * C), lambda b: (b, 0, 0)),
            full((C, 4 * C)), full((1, 4 * C)), full((2, C)), full((LRU_CONV, C)), full((1, C)), full((1, C)),
        ],
        out_specs=[
            pl.BlockSpec((1, L, C), lambda b: (b, 0, 0)),
            pl.BlockSpec((1, Lc, C), lambda b: (b, 0, 0)),
        ],
        out_shape=[jax.ShapeDtypeStruct((B, L, C), F32), jax.ShapeDtypeStruct((B, Lc, C), F32)],
        scratch_shapes=[
            pltpu.VMEM((L + 2 * LRU_HALO, C), F32),
            pltpu.VMEM((Lc + 2 * LRU_HALO, C), F32),
            pltpu.VMEM((L, C), F32),
            pltpu.VMEM((Lc, C), F32),
            pltpu.VMEM((LRU_CHUNK, C), F32),
            pltpu.VMEM((LRU_CHUNK, C), F32),
            pltpu.VMEM((LRU_CHUNK, C), F32),
        ],
        compiler_params=pltpu.CompilerParams(dimension_semantics=("arbitrary",), vmem_limit_bytes=VMEM_LIMIT_BYTES),
        name="rglru",
    )(xg_l, xg_c, wbd, bias, c8, conv_w, conv_b[None], out_norm[None])


HY_T = 256
HY_CB = 8


def _hyena_kernel(k_ref, d_ref, v_ref, x1_ref, x2_ref, o_ref, u_s, acc_s):
    n_rows = v_ref.shape[1]
    T = HY_T
    nb = k_ref.shape[2] // (2 * T)
    bsz = n_rows // nb

    def conv(ci, order):
        acc_s[...] = jnp.zeros_like(acc_s)
        for dd in range(-(nb - 1), nb):
            w2 = k_ref[order, pl.ds(ci, 1), pl.ds(T * (dd + nb - 1), 2 * T)]
            x = jnp.broadcast_to(w2, (T, 2 * T))
            r = pltpu.roll(x, 0, 1, stride=1, stride_axis=0)
            tb = r[:, T:].astype(BF16)
            j0, j1 = max(0, -dd), min(nb, nb - dd)
            lhs = u_s[pl.ds(bsz * j0, bsz * (j1 - j0)), :]
            dst = pl.ds(bsz * (j0 + dd), bsz * (j1 - j0))
            acc_s[dst, :] = acc_s[dst, :] + jnp.dot(lhs, tb, preferred_element_type=F32)

    def channel(ci, carry):
        v = v_ref[ci]
        u_s[...] = v.astype(BF16)
        conv(ci, 0)
        u = x1_ref[ci] * (acc_s[...] + v * d_ref[pl.ds(ci, 1), pl.ds(0, 1)])
        u_s[...] = u.astype(BF16)
        conv(ci, 1)
        o_ref[ci] = x2_ref[ci] * (acc_s[...] + u * d_ref[pl.ds(ci, 1), pl.ds(1, 1)])
        return carry

    lax.fori_loop(0, HY_CB, channel, 0)


def hyena_conv_pallas(kfull, d_skip, zT):
    C3, R, T = zT.shape
    C = C3 // (HY_ORDER + 1)
    two_l = kfull.shape[2]
    assert T == HY_T and C % HY_CB == 0
    n_cb = C // HY_CB
    blk = pl.BlockSpec((HY_CB, R, T), lambda c: (c, 0, 0))
    return pl.pallas_call(
        _hyena_kernel,
        grid=(n_cb,),
        in_specs=[
            pl.BlockSpec((HY_ORDER, HY_CB, two_l), lambda c: (0, c, 0)),
            pl.BlockSpec((HY_CB, HY_ORDER), lambda c: (c, 0)),
            blk,
            pl.BlockSpec((HY_CB, R, T), lambda c: (c + n_cb, 0, 0)),
            pl.BlockSpec((HY_CB, R, T), lambda c: (c + 2 * n_cb, 0, 0)),
        ],
        out_specs=blk,
        out_shape=jax.ShapeDtypeStruct((C, R, T), F32),
        scratch_shapes=[pltpu.VMEM((R, T), BF16), pltpu.VMEM((R, T), F32)],
        compiler_params=pltpu.CompilerParams(dimension_semantics=("arbitrary",), vmem_limit_bytes=VMEM_LIMIT_BYTES),
        name="hyena_conv",
    )(kfull, d_skip.T, zT, zT, zT)


FFN_TF_MAX = 1408


def _ffn_tile(hidden):
    return max(t for t in range(LANES, FFN_TF_MAX + 1, LANES) if hidden % t == 0)


def _swiglu_kernel(be_ref, nu_ref, x_ref, wg_ref, wu_ref, wd_ref, o_ref, acc_ref):
    i = pl.program_id(0)
    f = pl.program_id(1)

    @pl.when(i < nu_ref[0])
    def _():
        x = x_ref[...]
        g = jnp.dot(x, wg_ref[0], preferred_element_type=F32)
        u = jnp.dot(x, wu_ref[0], preferred_element_type=F32)
        h = (jax.nn.silu(g) * u).astype(BF16)
        part = jnp.dot(h, wd_ref[0], preferred_element_type=F32)

        @pl.when(f == 0)
        def _():
            acc_ref[...] = part

        @pl.when(f > 0)
        def _():
            acc_ref[...] = acc_ref[...] + part

    @pl.when(f == pl.num_programs(1) - 1)
    def _():
        o_ref[...] = jnp.where(i < nu_ref[0], acc_ref[...], 0.0)


def grouped_swiglu_pallas(xs, block_expert, n_used, wg, wu, wd):
    n_rows, D = xs.shape
    F = wg.shape[2]
    TM = MOE_BLOCK
    assert n_rows % TM == 0
    tf = _ffn_tile(F)
    n_blocks = n_rows // TM
    grid_spec = pltpu.PrefetchScalarGridSpec(
        num_scalar_prefetch=2,
        grid=(n_blocks, F // tf),
        in_specs=[
            pl.BlockSpec((TM, D), lambda i, f, be, nu: (i, 0)),
            pl.BlockSpec((1, D, tf), lambda i, f, be, nu: (be[i], 0, f)),
            pl.BlockSpec((1, D, tf), lambda i, f, be, nu: (be[i], 0, f)),
            pl.BlockSpec((1, tf, D), lambda i, f, be, nu: (be[i], f, 0)),
        ],
        out_specs=pl.BlockSpec((TM, D), lambda i, f, be, nu: (i, 0)),
        scratch_shapes=[pltpu.VMEM((TM, D), F32)],
    )
    return pl.pallas_call(
        _swiglu_kernel,
        grid_spec=grid_spec,
        out_shape=jax.ShapeDtypeStruct((n_rows, D), F32),
        compiler_params=pltpu.CompilerParams(dimension_semantics=("arbitrary", "arbitrary"),
                                             vmem_limit_bytes=VMEM_LIMIT_BYTES),
        name="grouped_swiglu",
    )(block_expert.astype(jnp.int32), jnp.reshape(n_used, (1,)).astype(jnp.int32), xs, wg, wu, wd)


PROJ_TM = 512


def _residual_ln(x, branch, gate_ref, g_ref, b_ref):
    s = ALPHA * x + gate_ref[0] * branch
    mu = jnp.mean(s, -1, keepdims=True)
    d = s - mu
    var = jnp.mean(jnp.square(d), -1, keepdims=True)
    return d * lax.rsqrt(var + 1e-5) * g_ref[...] + b_ref[...]


def _in_proj_kernel(*refs, has_ln, n_w):
    x_ref = refs[0]
    pos = 1
    x = x_ref[...]
    if has_ln:
        branch_ref, gate_ref, g_ref, b_ref = refs[1:5]
        pos = 5
        x = _residual_ln(x, branch_ref[...], gate_ref, g_ref, b_ref)
    shift_ref, scale_ref = refs[pos:pos + 2]
    w_refs = refs[pos + 2:pos + 2 + n_w]
    out_refs = refs[pos + 2 + n_w:]
    if has_ln:
        out_refs[0][...] = x
        out_refs = out_refs[1:]
    h = (x * (1.0 + scale_ref[0]) + shift_ref[0]).astype(BF16)
    for w_ref, o_ref in zip(w_refs, out_refs):
        o_ref[...] = jnp.dot(h, w_ref[...], preferred_element_type=F32)


def in_proj_pallas(x, shift, scale, weights, residual=None):
    M, D = x.shape
    tm = PROJ_TM
    G = shift.shape[0]
    tiles_per_group = M // G // tm
    assert M % (G * tm) == 0
    row = pl.BlockSpec((tm, D), lambda i: (i, 0))
    grp = pl.BlockSpec((1, 1, D), lambda i: (i // tiles_per_group, 0, 0))
    vec = pl.BlockSpec((1, D), lambda i: (0, 0))
    args, specs = [x], [row]
    has_ln = residual is not None
    if has_ln:
        branch, first_row, gate, ln_g, ln_b = residual
        assert first_row % tm == 0
        off = first_row // tm
        args += [branch, gate, ln_g[None], ln_b[None]]
        specs += [pl.BlockSpec((tm, D), lambda i: (i + off, 0)), grp, vec, vec]
    args += [shift, scale] + list(weights)
    specs += [grp, grp] + [pl.BlockSpec(w.shape, lambda i: (0, 0)) for w in weights]
    out_shape = [jax.ShapeDtypeStruct((M, w.shape[1]), F32) for w in weights]
    out_specs = [pl.BlockSpec((tm, w.shape[1]), lambda i: (i, 0)) for w in weights]
    if has_ln:
        out_shape = [jax.ShapeDtypeStruct((M, D), F32)] + out_shape
        out_specs = [row] + out_specs
    outs = pl.pallas_call(
        partial(_in_proj_kernel, has_ln=has_ln, n_w=len(weights)),
        grid=(M // tm,),
        in_specs=specs,
        out_specs=out_specs,
        out_shape=out_shape,
        compiler_params=pltpu.CompilerParams(dimension_semantics=("arbitrary",), vmem_limit_bytes=VMEM_LIMIT_BYTES),
        name="in_proj",
    )(*args)
    return (outs[0], outs[1:]) if has_ln else (None, outs)


def _out_proj_kernel(a_ref, b_ref, c_ref, d_ref, x_ref, w_ref, gate_ref, g_ref, bb_ref, shift_ref, scale_ref,
                     xn_ref, f_ref):
    y = jnp.concatenate([a_ref[...], b_ref[...], c_ref[...], d_ref[...]], -1).astype(BF16)
    y = jnp.dot(y, w_ref[...], preferred_element_type=F32)
    xn = _residual_ln(x_ref[...], y, gate_ref, g_ref, bb_ref)
    xn_ref[...] = xn
    f_ref[...] = (xn * (1.0 + scale_ref[0]) + shift_ref[0]).astype(BF16)


def out_proj_pallas(parts, x, w_out, gate, ln_g, ln_b, shift, scale):
    M, D = x.shape
    tm = PROJ_TM
    G = gate.shape[0]
    tiles_per_group = M // G // tm
    assert M % (G * tm) == 0
    row = pl.BlockSpec((tm, D), lambda i: (i, 0))
    part = pl.BlockSpec((tm, GROUP), lambda i: (i, 0))
    grp = pl.BlockSpec((1, 1, D), lambda i: (i // tiles_per_group, 0, 0))
    vec = pl.BlockSpec((1, D), lambda i: (0, 0))
    return pl.pallas_call(
        _out_proj_kernel,
        grid=(M // tm,),
        in_specs=[part] * 4 + [row, pl.BlockSpec(w_out.shape, lambda i: (0, 0)), grp, vec, vec, grp, grp],
        out_specs=[row, row],
        out_shape=[jax.ShapeDtypeStruct((M, D), F32), jax.ShapeDtypeStruct((M, D), BF16)],
        compiler_params=pltpu.CompilerParams(dimension_semantics=("arbitrary",), vmem_limit_bytes=VMEM_LIMIT_BYTES),
        name="out_proj",
    )(*parts, x, w_out, gate, ln_g[None], ln_b[None], shift, scale)


def modulate(x, shift, scale):
    return x * (1.0 + scale) + shift


def layer_norm(x, g, b, eps=1e-5):
    mu = jnp.mean(x, -1, keepdims=True)
    var = jnp.mean(jnp.square(x - mu), -1, keepdims=True)
    return (x - mu) * lax.rsqrt(var + eps) * g + b


def rms_norm(x, g, eps=1e-6):
    return x * lax.rsqrt(jnp.mean(jnp.square(x), -1, keepdims=True) + eps) * g


def dwconv(x, w, b, left):
    K = w.shape[0]
    L = x.shape[1]
    xp = jnp.pad(x, ((0, 0), (left, K - 1 - left), (0, 0)))
    return sum(xp[:, j:j + L] * w[j] for j in range(K)) + b


def token_shift(z, mu_prev, mu_next):
    zp = jnp.pad(z, ((0, 0), (1, 1), (0, 0)))
    return z + mu_prev * (zp[:, :-2] - z) + mu_next * (zp[:, 2:] - z)


def axial_rope(rows):
    r, col = jnp.meshgrid(jnp.arange(rows, dtype=F32), jnp.arange(GRID_W, dtype=F32), indexing='ij')
    half = MLA_ROPE // 2
    inv = 1.0 / (ROPE_BASE ** (jnp.arange(0, half, 2, dtype=F32) / half))
    ang = jnp.concatenate([r.reshape(-1, 1) * inv, col.reshape(-1, 1) * inv], -1)
    return jnp.cos(ang), jnp.sin(ang)


def mla_mixer(Pac, Pal, tables, q_norm, kv_norm, w_uq, w_ukv, out_norm, ctx_out):
    Lc, L = Pac.shape[1], Pal.shape[1]
    q, kt, v = mla_prep_pallas(jnp.concatenate([Pac, Pal], 1), tables, q_norm, kv_norm, mla_prep_weights(w_uq, w_ukv))
    nc, nl = Lc // ATTN_TQ, L // ATTN_TQ
    yl = mla_attention_pallas(q, kt, v, out_norm, nc, nl, nc + nl)
    yc = mla_attention_pallas(q, kt, v, out_norm, 0, nc, nc) if ctx_out else None
    return yl, yc


def rwkv_finish(y, bonus, v, gd, g_up, ln_g, ln_b):
    B, L = y.shape[:2]
    mu = jnp.mean(y, -1, keepdims=True)
    var = jnp.mean(jnp.square(y - mu), -1, keepdims=True)
    yn = (y - mu) * lax.rsqrt(var + RWKV_GN_EPS) * ln_g.reshape(RWKV_HEADS, RWKV_HEAD) + ln_b.reshape(RWKV_HEADS, RWKV_HEAD)
    g = jax.nn.sigmoid(gd) @ g_up
    return (yn + bonus * v).reshape(B, L, GROUP) * g


def rwkv_mixer(Pl, Pc, mu_prev, mu_next, w0, w_up, a0, a_up, g_up, k_k, k_a, r_k, ln_g, ln_b, ctx_out):
    B, L = Pl.shape[:2]
    Lc = Pc.shape[1]
    S = Lc + L
    H, N = RWKV_HEADS, RWKV_HEAD
    z = jnp.concatenate([token_shift(Pc, mu_prev, mu_next), token_shift(Pl, mu_prev, mu_next)], 1)

    def heads(t):
        return t.reshape(B, S, H, N)

    r = heads(z[..., C_R:C_R + GROUP])
    k = heads(z[..., C_K:C_K + GROUP])
    v = heads(z[..., C_V:C_V + GROUP])
    kk = k * k_k.reshape(H, N)
    kk = kk * lax.rsqrt(jnp.maximum(jnp.sum(jnp.square(kk), -1, keepdims=True), 1e-24))
    w, kka, kd = [], [], []
    for d in range(2):
        wd = z[..., C_WD + d * RWKV_DECAY_LORA:C_WD + (d + 1) * RWKV_DECAY_LORA]
        ad = z[..., C_AD + d * RWKV_AAA_LORA:C_AD + (d + 1) * RWKV_AAA_LORA]
        log_w = -jnp.exp(-jax.nn.softplus(-(w0[d] + jnp.tanh(wd) @ w_up[d])) - 0.5)
        a = heads(jax.nn.sigmoid(a0[d] + ad @ a_up[d]))
        w.append(heads(jnp.exp(log_w)))
        kka.append(kk * a)
        kd.append(k * (1.0 + (a - 1.0) * k_a.reshape(H, N)))

    def to_scan(lo, hi):
        return jnp.concatenate([jnp.transpose(t, (1, 3, 0, 2)).reshape(S, N, B * H) for t in (lo, hi)], -1)

    yf, yb = rwkv_scan_pallas(to_scan(r, kk), to_scan(v, v), to_scan(*w), to_scan(*kka), to_scan(*kd), Lc)
    y = yf[..., :B * H] + yb[..., B * H:]
    y = jnp.transpose(y.reshape(S, N, B, H), (2, 0, 3, 1))
    bonus = sum(jnp.sum(r * kd_d * r_k, -1, keepdims=True) for kd_d in kd)
    gd = z[..., C_GD:C_GD + RWKV_GATE_LORA]
    out_l = rwkv_finish(y[:, Lc:], bonus[:, Lc:], v[:, Lc:], gd[:, Lc:], g_up, ln_g, ln_b)
    out_c = None
    if ctx_out:
        out_c = rwkv_finish(y[:, :Lc], bonus[:, :Lc], v[:, :Lc], gd[:, :Lc], g_up, ln_g, ln_b)
    return out_l, out_c


def hyena_filters(L, w1, b1, w2, b2, w3):
    t01 = jnp.linspace(0.0, 1.0, L, dtype=F32)[:, None]
    bands = jnp.linspace(1e-4, HY_BANDS - 1, HY_BANDS, dtype=F32)[None, :]
    wpos = (2.0 * math.pi / L) * jnp.arange(L, dtype=F32)[:, None]
    z = jnp.concatenate([t01, jnp.cos(bands * wpos), -jnp.sin(bands * wpos)], -1)
    h = jnp.sin(HY_SIN_FREQ * (z @ w1 + b1))
    h = jnp.sin(HY_SIN_FREQ * (h @ w2 + b2))
    h = (h @ w3).reshape(L, HY_ORDER, 2, GROUP)
    deltas = jnp.abs(jnp.linspace(HY_DECAY_MIN, HY_DECAY_MAX, GROUP, dtype=F32))
    window = jnp.exp(-t01 * deltas) + HY_SHIFT
    return h * window[:, None, None, :]


def hyena_sequence(Pd, conv_w, conv_b, w1, b1, w2, b2, w3, d_skip):
    B, L, _ = Pd.shape
    C = GROUP
    T = HY_T
    nb = L // T
    z = dwconv(Pd, conv_w, conv_b, 1)
    zT = jnp.transpose(z.reshape(B, nb, T, 3 * C), (3, 1, 0, 2)).reshape(3 * C, nb * B, T)
    h = hyena_filters(L, w1, b1, w2, b2, w3)
    hf = jnp.transpose(h[:, :, 0], (1, 2, 0))
    hb = jnp.transpose(h[:, :, 1], (1, 2, 0))
    kfull = jnp.concatenate([jnp.zeros((HY_ORDER, C, 1), F32), jnp.flip(hb[..., 1:], -1), hf], -1)
    oT = hyena_conv_pallas(kfull, d_skip, zT)
    return jnp.transpose(oT.reshape(C, nb, B, T), (2, 1, 3, 0)).reshape(B, L, C)


def swiglu(x, wg, wu, wd):
    n_blocks = x.shape[0] // MOE_BLOCK
    return grouped_swiglu_pallas(x.astype(BF16), jnp.zeros((n_blocks,), jnp.int32), jnp.int32(n_blocks),
                                 wg[None].astype(BF16), wu[None].astype(BF16), wd[None].astype(BF16))


def moe_swiglu(x, router, wg, wu, wd):
    N, D = x.shape
    logits = x @ router
    top_v, top_i = lax.top_k(logits, TOP_K)
    gates = jax.nn.softmax(top_v, axis=-1)
    A = N * TOP_K
    e_flat = top_i.reshape(-1)
    tok_flat = jnp.arange(A, dtype=jnp.int32) // TOP_K
    order = jnp.argsort(e_flat)
    e_sorted = e_flat[order]
    counts = jnp.bincount(e_flat, length=N_EXPERTS)
    starts = jnp.cumsum(counts) - counts
    padded = (counts + MOE_BLOCK - 1) // MOE_BLOCK * MOE_BLOCK
    pends = jnp.cumsum(padded)
    pstarts = pends - padded
    dest = (pstarts[e_sorted] + jnp.arange(A, dtype=jnp.int32) - starts[e_sorted]).astype(jnp.int32)
    n_blocks = -(-A // MOE_BLOCK) + N_EXPERTS
    n_slots = n_blocks * MOE_BLOCK
    block_expert = jnp.clip(jnp.searchsorted(pends, jnp.arange(n_blocks) * MOE_BLOCK, side='right'), 0, N_EXPERTS - 1)
    slot_e = jnp.repeat(block_expert, MOE_BLOCK)
    slot_pos = jnp.arange(n_slots, dtype=jnp.int32) - pstarts[slot_e].astype(jnp.int32)
    slot_src = jnp.clip(starts[slot_e].astype(jnp.int32) + slot_pos, 0, A - 1)
    slot_tok = jnp.where(slot_pos < counts[slot_e], tok_flat[order[slot_src]], N)
    xp = jnp.concatenate([x.astype(BF16), jnp.zeros((1, D), BF16)], 0)
    ys = grouped_swiglu_pallas(xp[slot_tok], block_expert, pends[-1] // MOE_BLOCK,
                               wg.astype(BF16), wu.astype(BF16), wd.astype(BF16))
    slot_of = dest[jnp.argsort(order)].reshape(N, TOP_K)
    return ys[slot_of[:, 0]] * gates[:, 0:1] + ys[slot_of[:, 1]] * gates[:, 1:2]


def kernel(x, c, ctx, c_ctx, ada_w, ada_b, w_in, mla_q_norm, mla_kv_norm, mla_w_uq, mla_w_ukv, mla_out_norm, lru_conv_w, lru_conv_b, lru_w_r, lru_b_r, lru_w_i, lru_b_i, lru_lambda, lru_out_norm, rwkv_mu_prev, rwkv_mu_next, rwkv_w0, rwkv_w_up, rwkv_a0, rwkv_a_up, rwkv_g_up, rwkv_k_k, rwkv_k_a, rwkv_r_k, rwkv_ln_g, rwkv_ln_b, hy_conv_w, hy_conv_b, hy_f_w1, hy_f_b1, hy_f_w2, hy_f_b2, hy_f_w3, hy_d, hy_out_norm, w_out, ln1_g, ln1_b, ln2_g, ln2_b, ffn_w_gate, ffn_w_up, ffn_w_down, moe_router, moe_w_gate, moe_w_up, moe_w_down):
    B, L, D = x.shape
    Lc = ctx.shape[1]
    rows = L // GRID_W
    mla_tabs = mla_tables(*axial_rope(rows), Lc)
    s_lat = jax.nn.silu(c)
    s_ctx = jax.nn.silu(c_ctx)
    Ml, Mc = B * L, B * Lc
    xl, xc = x.reshape(Ml, D), ctx.reshape(Mc, D)
    prev = None
    for li in range(DEPTH):
        ctx_out = li < DEPTH - 1
        mod_l = (s_lat @ ada_w[li] + ada_b[li]).reshape(B, 6, 1, D)
        mod_c = (s_ctx @ ada_w[li] + ada_b[li]).reshape(6, 1, D)
        ml = [mod_l[:, q] for q in range(6)]
        mc = [mod_c[q][None] for q in range(6)]

        w_in_l = w_in[li]
        w_secs = [w.astype(BF16) for w in (mla_section_weight(w_in_l[:, :B_X]), w_in_l[:, B_X:C_OFF],
                                           w_in_l[:, C_OFF:D_OFF], w_in_l[:, D_OFF:])]
        res_l = res_c = None
        if prev is not None:
            out, gate_l, gate_c, g2, b2 = prev
            res_l, res_c = (out, 0, gate_l, g2, b2), (out, Ml, gate_c, g2, b2)
        xl_new, (Pal, Pbl, Pcl, Pdl) = in_proj_pallas(xl, ml[0], ml[1], w_secs, res_l)
        xc_new, ctx_secs = in_proj_pallas(xc, mc[0], mc[1], w_secs if ctx_out else w_secs[:3], res_c)
        if prev is not None:
            xl, xc = xl_new, xc_new
        Pal, Pbl, Pcl, Pdl = (p.reshape(B, L, -1) for p in (Pal, Pbl, Pcl, Pdl))
        Pac, Pbc, Pcc = (p.reshape(B, Lc, -1) for p in ctx_secs[:3])
        a_l, a_c = mla_mixer(Pac, Pal, mla_tabs, mla_q_norm[li], mla_kv_norm[li], mla_w_uq[li], mla_w_ukv[li],
                             mla_out_norm[li], ctx_out)
        b_l, b_c = rglru_pallas(Pbl, Pbc, lru_conv_w[li], lru_conv_b[li], lru_w_r[li], lru_b_r[li], lru_w_i[li],
                                lru_b_i[li], lru_lambda[li], lru_out_norm[li])
        c_l, c_c = rwkv_mixer(Pcl, Pcc, rwkv_mu_prev[li], rwkv_mu_next[li], rwkv_w0[li], rwkv_w_up[li], rwkv_a0[li],
                              rwkv_a_up[li], rwkv_g_up[li], rwkv_k_k[li], rwkv_k_a[li], rwkv_r_k[li],
                              rwkv_ln_g[li], rwkv_ln_b[li], ctx_out)
        d_l = rms_norm(hyena_sequence(Pdl, hy_conv_w[li], hy_conv_b[li], hy_f_w1[li], hy_f_b1[li], hy_f_w2[li],
                                      hy_f_b2[li], hy_f_w3[li], hy_d[li]), hy_out_norm[li])
        w_out_l = w_out[li].astype(BF16)
        xl, tokens = out_proj_pallas([t.reshape(Ml, GROUP) for t in (a_l, b_l, c_l, d_l)], xl, w_out_l,
                                     ml[2], ln1_g[li], ln1_b[li], ml[3], ml[4])
        if ctx_out:
            Pdc = ctx_secs[3].reshape(B, Lc, -1)
            d_c = rms_norm(hyena_sequence(Pdc, hy_conv_w[li], hy_conv_b[li], hy_f_w1[li], hy_f_b1[li], hy_f_w2[li],
                                          hy_f_b2[li], hy_f_w3[li], hy_d[li]), hy_out_norm[li])
            xc, fc = out_proj_pallas([t.reshape(Mc, GROUP) for t in (a_c, b_c, c_c, d_c)], xc, w_out_l,
                                     mc[2], ln1_g[li], ln1_b[li], mc[3], mc[4])
            tokens = jnp.concatenate([tokens, fc], 0)
        j = li // 2
        if li % 2 == 0:
            out = swiglu(tokens, ffn_w_gate[j], ffn_w_up[j], ffn_w_down[j])
        else:
            out = moe_swiglu(tokens, moe_router[j], moe_w_gate[j], moe_w_up[j], moe_w_down[j])
        prev = (out, ml[5], mc[5], ln2_g[li], ln2_b[li])
    out, gate_l, _, g2, b2 = prev
    xl, _ = in_proj_pallas(xl, ml[0], ml[1], [], (out, 0, gate_l, g2, b2))
    return xl.reshape(B, L, D)
```

```python
import math
from functools import partial

import jax
import jax.numpy as jnp
from jax import lax
from jax.experimental import pallas as pl
from jax.experimental.pallas import tpu as pltpu

F32 = jnp.float32
BF16 = jnp.bfloat16

SUBLANES = 8
LANES = 128
VMEM_LIMIT_BYTES = 48 * 1024 * 1024

D_MODEL = 1024
DEPTH = 4
GRID_W = 64
GROUP = D_MODEL // 4

MLA_HEADS = 4
MLA_NOPE = 64
MLA_ROPE = 32
MLA_V = 64
MLA_Q_RANK = 192
MLA_KV_RANK = 128
ROPE_BASE = 10000.0
Q_BLOCK = 128

LRU_BLOCKS = 4
LRU_CONV = 4
LRU_CONV_LEFT = 2
LRU_C = 8.0

RWKV_HEADS = 4
RWKV_HEAD = GROUP // RWKV_HEADS
RWKV_DECAY_LORA = 32
RWKV_AAA_LORA = 32
RWKV_GATE_LORA = 64
RWKV_GN_EPS = 64e-5

HY_ORDER = 2
HY_SHORT = 3
HY_BANDS = 16
HY_EMB = 1 + 2 * HY_BANDS
HY_HIDDEN = 64
HY_SIN_FREQ = 1.0
HY_DECAY_MIN = math.log(1e-2) / 1.5
HY_DECAY_MAX = math.log(1e-2) / 0.3
HY_SHIFT = 0.05

N_EXPERTS = 8
TOP_K = 2
MOE_BLOCK = 512

ALPHA = (2.0 * DEPTH) ** 0.25

A_CQ = 0
A_CKV = A_CQ + MLA_Q_RANK
A_KR = A_CKV + MLA_KV_RANK
B_X = A_KR + MLA_ROPE
B_GATE = B_X + GROUP
C_OFF = B_GATE + GROUP
C_R = 0
C_K = GROUP
C_V = 2 * GROUP
C_WD = 3 * GROUP
C_AD = C_WD + 2 * RWKV_DECAY_LORA
C_GD = C_AD + 2 * RWKV_AAA_LORA
C_COLS = C_GD + RWKV_GATE_LORA
D_OFF = C_OFF + C_COLS
D_COLS = (HY_ORDER + 1) * GROUP


RWKV_TIME_BLOCK = 16
V_TILES = RWKV_HEAD // SUBLANES
RWKV_KEY_GROUP = 16


def _rwkv_scan_kernel(rkf, rkb, vvf, vvb, wf, wb, kaf, kab, kdf, kdb, yf_ref, yb_ref, s_ref, m_ref):
    @pl.when(pl.program_id(0) == 0)
    def _():
        s_ref[...] = jnp.zeros_like(s_ref)

    n_t = rkf.shape[0]
    p = rkf.shape[2]
    fwd_lanes = lax.broadcasted_iota(jnp.int32, (RWKV_HEAD, p), 1) < p // 2

    for j in range(n_t):
        jb = n_t - 1 - j
        rk_f, rk_b = rkf[j], rkb[jb]
        m_ref[0, j] = jnp.where(fwd_lanes, rk_f, pltpu.roll(rk_b, p // 2, 1))
        m_ref[2, j] = jnp.where(fwd_lanes, pltpu.roll(rk_f, p // 2, 1), rk_b)
        for q, (f_ref, b_ref) in ((1, (wf, wb)), (3, (kaf, kab)), (4, (vvf, vvb)), (5, (kdf, kdb))):
            m_ref[q, j] = jnp.where(fwd_lanes, f_ref[j], b_ref[jb])

    r_ref, w_ref, kk_ref, kka_ref, v_ref, kd_ref = (m_ref.at[q] for q in range(6))

    def step(t, carry):
        def row(ref, k):
            return jnp.broadcast_to(ref[t, pl.ds(k, 1), :], (SUBLANES, p))[None]

        def sa_group(g, sa):
            k0 = pl.multiple_of(g * RWKV_KEY_GROUP, RWKV_KEY_GROUP)
            for j in range(RWKV_KEY_GROUP):
                sa = sa + s_ref[k0 + j] * row(kk_ref, k0 + j)
            return sa

        zero = jnp.zeros((V_TILES, SUBLANES, p), F32)
        sa = lax.fori_loop(0, RWKV_HEAD // RWKV_KEY_GROUP, sa_group, zero)
        vt = v_ref[t].reshape(V_TILES, SUBLANES, p)

        def update_group(g, y):
            k0 = pl.multiple_of(g * RWKV_KEY_GROUP, RWKV_KEY_GROUP)
            for j in range(RWKV_KEY_GROUP):
                k = k0 + j
                sn = s_ref[k] * row(w_ref, k) - sa * row(kka_ref, k) + vt * row(kd_ref, k)
                s_ref[k] = sn
                y = y + sn * row(r_ref, k)
            return y

        y = lax.fori_loop(0, RWKV_HEAD // RWKV_KEY_GROUP, update_group, zero).reshape(RWKV_HEAD, p)
        yf_ref[t] = y
        yb_ref[n_t - 1 - t] = y
        return carry

    lax.fori_loop(0, n_t, step, 0)


def rwkv_scan_pallas(rk, vv, w, kka, kd, n_ctx):
    n_steps, n, p = rk.shape
    tb = RWKV_TIME_BLOCK
    assert n == RWKV_HEAD and p == LANES and n_steps % tb == 0 and n_ctx % tb == 0
    nb, nc = n_steps // tb, n_ctx // tb
    fwd = pl.BlockSpec((tb, n, p), lambda g: (g, 0, 0))
    bwd = pl.BlockSpec((tb, n, p), lambda g: (jnp.where(g < nc, nc - 1 - g, nb + nc - 1 - g), 0, 0))
    out = jax.ShapeDtypeStruct((n_steps, n, p), F32)
    return pl.pallas_call(
        _rwkv_scan_kernel,
        grid=(nb,),
        in_specs=[fwd, bwd] * 5,
        out_specs=[fwd, bwd],
        out_shape=[out, out],
        scratch_shapes=[pltpu.VMEM((n, V_TILES, SUBLANES, LANES), F32), pltpu.VMEM((6, tb, n, LANES), F32)],
        compiler_params=pltpu.CompilerParams(dimension_semantics=("arbitrary",), vmem_limit_bytes=VMEM_LIMIT_BYTES),
        name="rwkv_scan",
    )(rk, rk, vv, vv, w, w, kka, kka, kd, kd)


ATTN_TQ = 256
ATTN_TK = 256


def _attn_kernel(q_ref, k_ref, v_ref, g_ref, o_ref, s_ref):
    tq = q_ref.shape[1]
    n_chunks = k_ref.shape[1]
    n_tiles = ATTN_TK // LANES
    pair_out = []
    for hp in range(MLA_HEADS // 2):
        o_pair = jnp.zeros((tq, LANES), F32)
        for h in (2 * hp, 2 * hp + 1):
            qh = q_ref[0, :, pl.ds(LANES * h, LANES)]

            m_acc = jnp.full((tq, LANES), -jnp.inf, F32)
            for c in range(n_chunks):
                s = jnp.dot(qh, k_ref[0, c, pl.ds(LANES * h, LANES), :], preferred_element_type=F32)
                s_ref[c] = s
                for j in range(n_tiles):
                    m_acc = jnp.maximum(m_acc, s[:, LANES * j:LANES * (j + 1)])
            m_full = jnp.broadcast_to(jnp.max(m_acc, -1, keepdims=True), (tq, LANES))

            l_acc = jnp.zeros((tq, LANES), F32)
            acc = jnp.zeros((tq, LANES), F32)
            for c in range(n_chunks):
                s = s_ref[c]
                ps = []
                for j in range(n_tiles):
                    p = jnp.exp2(s[:, LANES * j:LANES * (j + 1)] - m_full)
                    l_acc = l_acc + p
                    ps.append(p.astype(BF16))
                vh = v_ref[0, pl.ds(c * ATTN_TK, ATTN_TK), pl.ds(LANES * h, LANES)]
                acc = acc + jnp.dot(jnp.concatenate(ps, -1), vh, preferred_element_type=F32)
            o_pair = o_pair + acc / jnp.sum(l_acc, -1, keepdims=True)
        pair_out.append(o_pair)
    o = jnp.concatenate(pair_out, -1)
    o_ref[0] = o * lax.rsqrt(jnp.mean(jnp.square(o), -1, keepdims=True) + 1e-6) * g_ref[...]


def mla_attention_pallas(q, kt, v, out_norm, q_start, n_q, n_kv):
    B, _, W = q.shape
    return pl.pallas_call(
        _attn_kernel,
        grid=(B, n_q),
        in_specs=[
            pl.BlockSpec((1, ATTN_TQ, W), lambda b, i: (b, i + q_start, 0)),
            pl.BlockSpec((1, n_kv, W, ATTN_TK), lambda b, i: (b, 0, 0, 0)),
            pl.BlockSpec((1, n_kv * ATTN_TK, W), lambda b, i: (b, 0, 0)),
            pl.BlockSpec((1, GROUP), lambda b, i: (0, 0)),
        ],
        out_specs=pl.BlockSpec((1, ATTN_TQ, GROUP), lambda b, i: (b, i, 0)),
        out_shape=jax.ShapeDtypeStruct((B, n_q * ATTN_TQ, GROUP), F32),
        scratch_shapes=[pltpu.VMEM((n_kv, ATTN_TQ, ATTN_TK), F32)],
        compiler_params=pltpu.CompilerParams(dimension_semantics=("arbitrary", "arbitrary"),
                                             vmem_limit_bytes=VMEM_LIMIT_BYTES),
        name="mla_attention",
    )(q, kt, v, out_norm[None])


MLA_PA_COLS = 4 * LANES
MLA_W = MLA_HEADS * LANES


def _rot_cols(w):
    h = w.shape[1] // 2
    return jnp.concatenate([-w[:, h:], w[:, :h]], 1)


def mla_section_weight(w_a):
    z = jnp.zeros((w_a.shape[0], LANES // 2), w_a.dtype)
    kr = w_a[:, A_KR:A_KR + MLA_ROPE]
    return jnp.concatenate([w_a[:, A_CKV:A_CKV + MLA_KV_RANK], kr, _rot_cols(kr), z,
                            w_a[:, A_CQ:A_CQ + MLA_Q_RANK], z], 1)


def mla_prep_weights(w_uq, w_ukv):
    H, DN, DR, DV = MLA_HEADS, MLA_NOPE, MLA_ROPE, MLA_V
    wq = jnp.zeros((2 * LANES, MLA_W), F32)
    wq_rot = jnp.zeros((2 * LANES, MLA_W), F32)
    wk = jnp.zeros((2 * LANES, MLA_W), F32)
    wv = jnp.zeros((LANES, MLA_W), F32)
    place = jnp.eye(DR, dtype=F32)
    for h in range(H):
        q_h = w_uq[:, h * (DN + DR):(h + 1) * (DN + DR)]
        wq = wq.at[:MLA_Q_RANK, LANES * h:LANES * h + DN + DR].set(q_h)
        wq_rot = wq_rot.at[:MLA_Q_RANK, LANES * h + DN:LANES * h + DN + DR].set(_rot_cols(q_h[:, DN:]))
        kv_h = w_ukv[:, h * (DN + DV):(h + 1) * (DN + DV)]
        wk = wk.at[:MLA_KV_RANK, LANES * h:LANES * h + DN].set(kv_h[:, :DN])
        wk = wk.at[LANES:LANES + DR, LANES * h + DN:LANES * h + DN + DR].set(place)
        v0 = LANES * h + DV * (h % 2)
        wv = wv.at[:, v0:v0 + DV].set(kv_h[:, DN:])
    return wq.astype(BF16), wq_rot.astype(BF16), wk.astype(BF16), wv.astype(BF16)


def mla_tables(cos, sin, n_ctx):
    H, DN, DR = MLA_HEADS, MLA_NOPE, MLA_ROPE
    L = cos.shape[0]
    cf = jnp.concatenate([jnp.ones((n_ctx, DR), F32), jnp.concatenate([cos, cos], -1)], 0)
    sf = jnp.concatenate([jnp.zeros((n_ctx, DR), F32), jnp.concatenate([sin, sin], -1)], 0)
    S = n_ctx + L
    tab_k = jnp.concatenate([cf, sf, jnp.zeros((S, LANES - 2 * DR), F32)], -1)
    scale = (DN + DR) ** -0.5 * math.log2(math.e)
    zpad = jnp.zeros((S, LANES - DN - DR), F32)
    qc_h = jnp.concatenate([jnp.full((S, DN), scale, F32), cf * scale, zpad], -1)
    qs_h = jnp.concatenate([jnp.zeros((S, DN), F32), sf * scale, zpad], -1)
    return tab_k, jnp.tile(qc_h, (1, H)), jnp.tile(qs_h, (1, H))


def _mla_prep_kernel(pa_ref, tk_ref, tqc_ref, tqs_ref, kvn_ref, qn_ref, wq_ref, wqr_ref, wk_ref, wv_ref,
                     q_ref, kt_ref, v_ref):
    pa = pa_ref[0]
    ckv = pa[:, :LANES]
    ckv = ckv * lax.rsqrt(jnp.mean(jnp.square(ckv), -1, keepdims=True) + 1e-6) * kvn_ref[...]
    t = pa[:, LANES:2 * LANES] * tk_ref[...]
    kr = t + pltpu.roll(t, LANES - MLA_ROPE, 1)
    kr = jnp.where(lax.broadcasted_iota(jnp.int32, kr.shape, 1) < MLA_ROPE, kr, 0.0)
    k = jnp.dot(jnp.concatenate([ckv, kr], -1).astype(BF16), wk_ref[...], preferred_element_type=F32)
    kt_ref[0, 0] = k.T.astype(BF16)
    v_ref[0] = jnp.dot(ckv.astype(BF16), wv_ref[...], preferred_element_type=F32).astype(BF16)
    cq = pa[:, 2 * LANES:]
    ms = jnp.sum(jnp.square(cq), -1, keepdims=True) * (1.0 / MLA_Q_RANK)
    cq = (cq * lax.rsqrt(ms + 1e-6) * qn_ref[...]).astype(BF16)
    qa = jnp.dot(cq, wq_ref[...], preferred_element_type=F32)
    qb = jnp.dot(cq, wqr_ref[...], preferred_element_type=F32)
    q_ref[0] = (qa * tqc_ref[...] + qb * tqs_ref[...]).astype(BF16)


def mla_prep_pallas(pa, tables, q_norm, kv_norm, weights):
    B, S, _ = pa.shape
    T = ATTN_TK
    assert S % T == 0
    tab_k, tab_qc, tab_qs = tables
    wq, wq_rot, wk, wv = weights
    qn = jnp.concatenate([q_norm, jnp.zeros((2 * LANES - MLA_Q_RANK,), F32)])[None]

    def full(a):
        return pl.BlockSpec(a.shape, lambda b, i: (0,) * a.ndim)

    def rows(width):
        return pl.BlockSpec((T, width), lambda b, i: (i, 0))

    return pl.pallas_call(
        _mla_prep_kernel,
        grid=(B, S // T),
        in_specs=[pl.BlockSpec((1, T, MLA_PA_COLS), lambda b, i: (b, i, 0)),
                  rows(LANES), rows(MLA_W), rows(MLA_W),
                  pl.BlockSpec((1, LANES), lambda b, i: (0, 0)), full(qn), full(wq), full(wq_rot), full(wk), full(wv)],
        out_specs=[pl.BlockSpec((1, T, MLA_W), lambda b, i: (b, i, 0)),
                   pl.BlockSpec((1, 1, MLA_W, T), lambda b, i: (b, i, 0, 0)),
                   pl.BlockSpec((1, T, MLA_W), lambda b, i: (b, i, 0))],
        out_shape=[jax.ShapeDtypeStruct((B, S, MLA_W), BF16),
                   jax.ShapeDtypeStruct((B, S // T, MLA_W, T), BF16),
                   jax.ShapeDtypeStruct((B, S, MLA_W), BF16)],
        compiler_params=pltpu.CompilerParams(dimension_semantics=("arbitrary", "arbitrary")),
        name="mla_prep",
    )(pa, tab_k, tab_qc, tab_qs, kv_norm[None], qn, wq, wq_rot, wk, wv)


RWKV_PREP_TM = 256


def _rwkv_prep_kernel(x_ref, prev_ref, next_ref, mup_ref, mun_ref, kkp_ref, kap_ref, rk_ref, w0_ref, a0_ref,
                      wl_ref, gup_ref, ones_ref,
                      r_o, kk_o, v_o, w0_o, w1_o, ka0_o, ka1_o, kd0_o, kd1_o, bonus_o, g_o, *, tiles_ctx, tiles_total):
    tm = x_ref.shape[1]
    i = pl.program_id(1)
    x = x_ref[0]
    starts = jnp.logical_or(i == 0, i == tiles_ctx)
    ends = jnp.logical_or(i == tiles_ctx - 1, i == tiles_total - 1)
    prev_row = jnp.where(starts, 0.0, prev_ref[0, SUBLANES - 1:SUBLANES, :])
    next_row = jnp.where(ends, 0.0, next_ref[0, 0:1, :])
    rows = lax.broadcasted_iota(jnp.int32, x.shape, 0)
    x_prev = jnp.where(rows == 0, prev_row, pltpu.roll(x, 1, 0))
    x_next = jnp.where(rows == tm - 1, next_row, pltpu.roll(x, tm - 1, 0))
    z = x + mup_ref[...] * (x_prev - x) + mun_ref[...] * (x_next - x)

    def head_sum(t):
        return jnp.dot(t, ones_ref[...], precision=lax.Precision.HIGHEST, preferred_element_type=F32)

    r = z[:, C_R:C_R + GROUP]
    k = z[:, C_K:C_K + GROUP]
    v = z[:, C_V:C_V + GROUP]
    kk = k * kkp_ref[...]
    kk = kk * lax.rsqrt(jnp.maximum(head_sum(kk * kk), 1e-24))
    lo = z[:, C_WD:C_GD]
    is_decay = lax.broadcasted_iota(jnp.int32, lo.shape, 1) < 2 * RWKV_DECAY_LORA
    ll = jnp.dot(jnp.where(is_decay, jnp.tanh(lo), lo).astype(BF16), wl_ref[...], preferred_element_type=F32)
    bonus = jnp.zeros_like(r)
    for d, (w_o, ka_o, kd_o) in enumerate(((w0_o, ka0_o, kd0_o), (w1_o, ka1_o, kd1_o))):
        xw = w0_ref[pl.ds(d, 1), :] + ll[:, GROUP * d:GROUP * (d + 1)]
        w_o[0] = jnp.exp(-(jax.nn.sigmoid(xw) * math.exp(-0.5)))
        a = jax.nn.sigmoid(a0_ref[pl.ds(d, 1), :] + ll[:, GROUP * (2 + d):GROUP * (3 + d)])
        ka_o[0] = kk * a
        kd = k * (1.0 + (a - 1.0) * kap_ref[...])
        kd_o[0] = kd
        bonus = bonus + head_sum(r * kd * rk_ref[...])
    r_o[0] = r
    kk_o[0] = kk
    v_o[0] = v
    bonus_o[0] = bonus
    gd = z[:, C_GD:C_GD + RWKV_GATE_LORA]
    g_o[0] = jnp.dot(jax.nn.sigmoid(gd).astype(BF16), gup_ref[...], preferred_element_type=F32)


def rwkv_prep_pallas(p, n_ctx, mu_prev, mu_next, w0, w_up, a0, a_up, g_up, k_k, k_a, r_k):
    B, S, _ = p.shape
    tm = RWKV_PREP_TM
    assert S % tm == 0 and n_ctx % tm == 0
    tiles_total, tiles_ctx = S // tm, n_ctx // tm
    tps = tm // SUBLANES
    n_sub = S // SUBLANES
    G = GROUP
    wl = jnp.zeros((4 * RWKV_DECAY_LORA, 4 * G), F32)
    for j, m in enumerate((w_up[0], w_up[1], a_up[0], a_up[1])):
        wl = wl.at[RWKV_DECAY_LORA * j:RWKV_DECAY_LORA * (j + 1), G * j:G * (j + 1)].set(m)
    head_of = jnp.arange(G) // RWKV_HEAD
    ones = (head_of[:, None] == head_of[None, :]).astype(F32)

    def full(a):
        return pl.BlockSpec(a.shape, lambda b, i: (0,) * a.ndim)

    params = [mu_prev[None], mu_next[None], k_k[None], k_a[None], r_k.reshape(1, G), w0, a0,
              wl.astype(BF16), g_up.astype(BF16), ones]
    out_spec = pl.BlockSpec((1, tm, G), lambda b, i: (b, i, 0))
    out = jax.ShapeDtypeStruct((B, S, G), F32)
    return pl.pallas_call(
        partial(_rwkv_prep_kernel, tiles_ctx=tiles_ctx, tiles_total=tiles_total),
        grid=(B, tiles_total),
        in_specs=[pl.BlockSpec((1, tm, C_COLS), lambda b, i: (b, i, 0)),
                  pl.BlockSpec((1, SUBLANES, C_COLS), lambda b, i: (b, jnp.maximum(i * tps - 1, 0), 0)),
                  pl.BlockSpec((1, SUBLANES, C_COLS), lambda b, i: (b, jnp.minimum((i + 1) * tps, n_sub - 1), 0))]
                 + [full(a) for a in params],
        out_specs=[out_spec] * 11,
        out_shape=[out] * 11,
        compiler_params=pltpu.CompilerParams(dimension_semantics=("arbitrary", "arbitrary"),
                                             vmem_limit_bytes=VMEM_LIMIT_BYTES),
        name="rwkv_prep",
    )(p, p, p, *params)


LRU_CHUNK = 256
LRU_HALO = SUBLANES


def _lru_kernel(xl_ref, xc_ref, wbd_ref, bias_ref, c8_ref, cw_ref, cb_ref, gn_ref,
                yl_ref, yc_ref, xs_l, xs_c, hf_l, hf_c, a_s, b_s, hb_s):
    C = GROUP
    CH = LRU_CHUNK
    L = xl_ref.shape[1]
    Lc = xc_ref.shape[1]

    def stage(x_ref, xs, n):
        xs[pl.ds(0, LRU_HALO), :] = jnp.zeros((LRU_HALO, C), F32)
        xs[pl.ds(LRU_HALO + n, LRU_HALO), :] = jnp.zeros((LRU_HALO, C), F32)

        def cp(i, c):
            r0 = pl.multiple_of(i * CH, CH)
            xs[pl.ds(LRU_HALO + r0, CH), :] = x_ref[0, pl.ds(r0, CH), pl.ds(0, C)]
            return c

        lax.fori_loop(0, n // CH, cp, 0)

    stage(xl_ref, xs_l, L)
    stage(xc_ref, xs_c, Lc)

    def coeffs(xs, base, d):
        xv = xs[pl.ds(base, CH + 2 * LRU_HALO), :]
        u = cb_ref[...]
        for j in range(LRU_CONV):
            o = LRU_HALO - LRU_CONV_LEFT + j
            u = u + xv[o:o + CH] * cw_ref[pl.ds(j, 1), :]
        z = jnp.dot(u.astype(BF16), wbd_ref[:, pl.ds(d * 2 * C, 2 * C)], preferred_element_type=F32)
        z = z + bias_ref[:, pl.ds(d * 2 * C, 2 * C)]
        r = jax.nn.sigmoid(z[:, :C])
        i = jax.nn.sigmoid(z[:, C:])
        log_a = r * c8_ref[pl.ds(d, 1), :]
        a = jnp.exp(log_a)
        a_s[...] = a
        b_s[...] = jnp.sqrt(-jnp.tanh(log_a) * (a * a + 1.0)) * (i * u)

    def row_scan(h, out_ref, out_base, reverse):
        n_groups = CH // SUBLANES

        def group(g, h):
            r0 = pl.multiple_of((n_groups - 1 - g if reverse else g) * SUBLANES, SUBLANES)
            for j in (reversed(range(SUBLANES)) if reverse else range(SUBLANES)):
                a_t = jnp.broadcast_to(a_s[pl.ds(r0 + j, 1), :], (SUBLANES, C))
                b_t = jnp.broadcast_to(b_s[pl.ds(r0 + j, 1), :], (SUBLANES, C))
                h = a_t * h + b_t
                out_ref[pl.ds(out_base + r0 + j, 1), :] = h[0:1, :]
            return h

        return lax.fori_loop(0, n_groups, group, h)

    h0 = jnp.zeros((SUBLANES, C), F32)

    h = h0
    for ci in range(Lc // CH):
        coeffs(xs_c, ci * CH, 0)
        h = row_scan(h, hf_c, ci * CH, False)

    def fwd_chunk(ci, h):
        base = pl.multiple_of(ci * CH, CH)
        coeffs(xs_l, base, 0)
        return row_scan(h, hf_l, base, False)

    lax.fori_loop(0, L // CH, fwd_chunk, h)

    def combine(x_ref, hf, base, y_ref):
        hl = hf[pl.ds(base, CH), :] + hb_s[...]
        g = jax.nn.gelu(x_ref[0, pl.ds(base, CH), pl.ds(C, C)])
        v = hl * g
        y = v * lax.rsqrt(jnp.mean(jnp.square(v), -1, keepdims=True) + 1e-6) * gn_ref[...]
        y_ref[0, pl.ds(base, CH), :] = y

    h = h0
    for ci in reversed(range(Lc // CH)):
        coeffs(xs_c, ci * CH, 1)
        h = row_scan(h, hb_s, 0, True)
        combine(xc_ref, hf_c, ci * CH, yc_ref)

    def bwd_chunk(k, h):
        base = pl.multiple_of((L // CH - 1 - k) * CH, CH)
        coeffs(xs_l, base, 1)
        h = row_scan(h, hb_s, 0, True)
        combine(xl_ref, hf_l, base, yl_ref)
        return h

    lax.fori_loop(0, L // CH, bwd_chunk, h)


def rglru_pallas(xg_l, xg_c, conv_w, conv_b, w_r, b_r, w_i, b_i, lam, out_norm):
    B, L, _ = xg_l.shape
    Lc = xg_c.shape[1]
    C = GROUP
    assert L % LRU_CHUNK == 0 and Lc % LRU_CHUNK == 0

    def bd(w):
        return jax.scipy.linalg.block_diag(*[w[n] for n in range(LRU_BLOCKS)])

    wbd = jnp.concatenate([bd(w_r[0]), bd(w_i[0]), bd(w_r[1]), bd(w_i[1])], 1).astype(BF16)
    bias = jnp.concatenate([b_r[0], b_i[0], b_r[1], b_i[1]])[None]
    c8 = -LRU_C * jax.nn.softplus(-lam)

    def full(shape):
        return pl.BlockSpec(shape, lambda b: (0,) * len(shape))

    return pl.pallas_call(
        _lru_kernel,
        grid=(B,),
        in_specs=[
            pl.BlockSpec((1, L, 2 * C), lambda b: (b, 0, 0)),
            pl.BlockSpec((1, Lc, 2 * C), lambda b: (b, 0, 0)),
            full((C, 4 * C)), full((1, 4 * C)), full((2, C)), full((LRU_CONV, C)), full((1, C)), full((1, C)),
        ],
        out_specs=[
            pl.BlockSpec((1, L, C), lambda b: (b, 0, 0)),
            pl.BlockSpec((1, Lc, C), lambda b: (b, 0, 0)),
        ],
        out_shape=[jax.ShapeDtypeStruct((B, L, C), F32), jax.ShapeDtypeStruct((B, Lc, C), F32)],
        scratch_shapes=[
            pltpu.VMEM((L + 2 * LRU_HALO, C), F32),
            pltpu.VMEM((Lc + 2 * LRU_HALO, C), F32),
            pltpu.VMEM((L, C), F32),
            pltpu.VMEM((Lc, C), F32),
            pltpu.VMEM((LRU_CHUNK, C), F32),
            pltpu.VMEM((LRU_CHUNK, C), F32),
            pltpu.VMEM((LRU_CHUNK, C), F32),
        ],
        compiler_params=pltpu.CompilerParams(dimension_semantics=("arbitrary",), vmem_limit_bytes=VMEM_LIMIT_BYTES),
        name="rglru",
    )(xg_l, xg_c, wbd, bias, c8, conv_w, conv_b[None], out_norm[None])


HY_T = 256
HY_CB = 8


def _hyena_kernel(k_ref, d_ref, v_ref, x1_ref, x2_ref, o_ref, u_s, acc_s):
    n_rows = v_ref.shape[1]
    T = HY_T
    nb = k_ref.shape[2] // (2 * T)
    bsz = n_rows // nb

    def conv(ci, order):
        acc_s[...] = jnp.zeros_like(acc_s)
        for dd in range(-(nb - 1), nb):
            w2 = k_ref[order, pl.ds(ci, 1), pl.ds(T * (dd + nb - 1), 2 * T)]
            x = jnp.broadcast_to(w2, (T, 2 * T))
            r = pltpu.roll(x, 0, 1, stride=1, stride_axis=0)
            tb = r[:, T:].astype(BF16)
            j0, j1 = max(0, -dd), min(nb, nb - dd)
            lhs = u_s[pl.ds(bsz * j0, bsz * (j1 - j0)), :]
            dst = pl.ds(bsz * (j0 + dd), bsz * (j1 - j0))
            acc_s[dst, :] = acc_s[dst, :] + jnp.dot(lhs, tb, preferred_element_type=F32)

    def channel(ci, carry):
        v = v_ref[ci]
        u_s[...] = v.astype(BF16)
        conv(ci, 0)
        u = x1_ref[ci] * (acc_s[...] + v * d_ref[pl.ds(ci, 1), pl.ds(0, 1)])
        u_s[...] = u.astype(BF16)
        conv(ci, 1)
        o_ref[ci] = x2_ref[ci] * (acc_s[...] + u * d_ref[pl.ds(ci, 1), pl.ds(1, 1)])
        return carry

    lax.fori_loop(0, HY_CB, channel, 0)


def hyena_conv_pallas(kfull, d_skip, zT):
    C3, R, T = zT.shape
    C = C3 // (HY_ORDER + 1)
    two_l = kfull.shape[2]
    assert T == HY_T and C % HY_CB == 0
    n_cb = C // HY_CB
    blk = pl.BlockSpec((HY_CB, R, T), lambda c: (c, 0, 0))
    return pl.pallas_call(
        _hyena_kernel,
        grid=(n_cb,),
        in_specs=[
            pl.BlockSpec((HY_ORDER, HY_CB, two_l), lambda c: (0, c, 0)),
            pl.BlockSpec((HY_CB, HY_ORDER), lambda c: (c, 0)),
            blk,
            pl.BlockSpec((HY_CB, R, T), lambda c: (c + n_cb, 0, 0)),
            pl.BlockSpec((HY_CB, R, T), lambda c: (c + 2 * n_cb, 0, 0)),
        ],
        out_specs=blk,
        out_shape=jax.ShapeDtypeStruct((C, R, T), F32),
        scratch_shapes=[pltpu.VMEM((R, T), BF16), pltpu.VMEM((R, T), F32)],
        compiler_params=pltpu.CompilerParams(dimension_semantics=("arbitrary",), vmem_limit_bytes=VMEM_LIMIT_BYTES),
        name="hyena_conv",
    )(kfull, d_skip.T, zT, zT, zT)


FFN_TF_MAX = 1408


def _ffn_tile(hidden):
    return max(t for t in range(LANES, FFN_TF_MAX + 1, LANES) if hidden % t == 0)


def _swiglu_kernel(be_ref, nu_ref, x_ref, wg_ref, wu_ref, wd_ref, o_ref, acc_ref):
    i = pl.program_id(0)
    f = pl.program_id(1)

    @pl.when(i < nu_ref[0])
    def _():
        x = x_ref[...]
        g = jnp.dot(x, wg_ref[0], preferred_element_type=F32)
        u = jnp.dot(x, wu_ref[0], preferred_element_type=F32)
        h = (jax.nn.silu(g) * u).astype(BF16)
        part = jnp.dot(h, wd_ref[0], preferred_element_type=F32)

        @pl.when(f == 0)
        def _():
            acc_ref[...] = part

        @pl.when(f > 0)
        def _():
            acc_ref[...] = acc_ref[...] + part

    @pl.when(f == pl.num_programs(1) - 1)
    def _():
        o_ref[...] = jnp.where(i < nu_ref[0], acc_ref[...], 0.0)


def grouped_swiglu_pallas(xs, block_expert, n_used, wg, wu, wd):
    n_rows, D = xs.shape
    F = wg.shape[2]
    TM = MOE_BLOCK
    assert n_rows % TM == 0
    tf = _ffn_tile(F)
    n_blocks = n_rows // TM
    grid_spec = pltpu.PrefetchScalarGridSpec(
        num_scalar_prefetch=2,
        grid=(n_blocks, F // tf),
        in_specs=[
            pl.BlockSpec((TM, D), lambda i, f, be, nu: (i, 0)),
            pl.BlockSpec((1, D, tf), lambda i, f, be, nu: (be[i], 0, f)),
            pl.BlockSpec((1, D, tf), lambda i, f, be, nu: (be[i], 0, f)),
            pl.BlockSpec((1, tf, D), lambda i, f, be, nu: (be[i], f, 0)),
        ],
        out_specs=pl.BlockSpec((TM, D), lambda i, f, be, nu: (i, 0)),
        scratch_shapes=[pltpu.VMEM((TM, D), F32)],
    )
    return pl.pallas_call(
        _swiglu_kernel,
        grid_spec=grid_spec,
        out_shape=jax.ShapeDtypeStruct((n_rows, D), F32),
        compiler_params=pltpu.CompilerParams(dimension_semantics=("arbitrary", "arbitrary"),
                                             vmem_limit_bytes=VMEM_LIMIT_BYTES),
        name="grouped_swiglu",
    )(block_expert.astype(jnp.int32), jnp.reshape(n_used, (1,)).astype(jnp.int32), xs, wg, wu, wd)


PROJ_TM = 512


def _residual_ln(x, branch, gate_ref, g_ref, b_ref):
    s = ALPHA * x + gate_ref[0] * branch
    mu = jnp.mean(s, -1, keepdims=True)
    d = s - mu
    var = jnp.mean(jnp.square(d), -1, keepdims=True)
    return d * lax.rsqrt(var + 1e-5) * g_ref[...] + b_ref[...]


def _in_proj_kernel(*refs, has_ln, n_w):
    x_ref = refs[0]
    pos = 1
    x = x_ref[...]
    if has_ln:
        branch_ref, gate_ref, g_ref, b_ref = refs[1:5]
        pos = 5
        x = _residual_ln(x, branch_ref[...], gate_ref, g_ref, b_ref)
    shift_ref, scale_ref = refs[pos:pos + 2]
    w_refs = refs[pos + 2:pos + 2 + n_w]
    out_refs = refs[pos + 2 + n_w:]
    if has_ln:
        out_refs[0][...] = x
        out_refs = out_refs[1:]
    h = (x * (1.0 + scale_ref[0]) + shift_ref[0]).astype(BF16)
    for w_ref, o_ref in zip(w_refs, out_refs):
        o_ref[...] = jnp.dot(h, w_ref[...], preferred_element_type=F32)


def in_proj_pallas(x, shift, scale, weights, residual=None):
    M, D = x.shape
    tm = PROJ_TM
    G = shift.shape[0]
    tiles_per_group = M // G // tm
    assert M % (G * tm) == 0
    row = pl.BlockSpec((tm, D), lambda i: (i, 0))
    grp = pl.BlockSpec((1, 1, D), lambda i: (i // tiles_per_group, 0, 0))
    vec = pl.BlockSpec((1, D), lambda i: (0, 0))
    args, specs = [x], [row]
    has_ln = residual is not None
    if has_ln:
        branch, first_row, gate, ln_g, ln_b = residual
        assert first_row % tm == 0
        off = first_row // tm
        args += [branch, gate, ln_g[None], ln_b[None]]
        specs += [pl.BlockSpec((tm, D), lambda i: (i + off, 0)), grp, vec, vec]
    args += [shift, scale] + list(weights)
    specs += [grp, grp] + [pl.BlockSpec(w.shape, lambda i: (0, 0)) for w in weights]
    out_shape = [jax.ShapeDtypeStruct((M, w.shape[1]), F32) for w in weights]
    out_specs = [pl.BlockSpec((tm, w.shape[1]), lambda i: (i, 0)) for w in weights]
    if has_ln:
        out_shape = [jax.ShapeDtypeStruct((M, D), F32)] + out_shape
        out_specs = [row] + out_specs
    outs = pl.pallas_call(
        partial(_in_proj_kernel, has_ln=has_ln, n_w=len(weights)),
        grid=(M // tm,),
        in_specs=specs,
        out_specs=out_specs,
        out_shape=out_shape,
        compiler_params=pltpu.CompilerParams(dimension_semantics=("arbitrary",), vmem_limit_bytes=VMEM_LIMIT_BYTES),
        name="in_proj",
    )(*args)
    return (outs[0], outs[1:]) if has_ln else (None, outs)


def _out_proj_kernel(a_ref, b_ref, c_ref, d_ref, x_ref, w_ref, gate_ref, g_ref, bb_ref, shift_ref, scale_ref,
                     xn_ref, f_ref):
    y = jnp.concatenate([a_ref[...], b_ref[...], c_ref[...], d_ref[...]], -1).astype(BF16)
    y = jnp.dot(y, w_ref[...], preferred_element_type=F32)
    xn = _residual_ln(x_ref[...], y, gate_ref, g_ref, bb_ref)
    xn_ref[...] = xn
    f_ref[...] = (xn * (1.0 + scale_ref[0]) + shift_ref[0]).astype(BF16)


def out_proj_pallas(parts, x, w_out, gate, ln_g, ln_b, shift, scale):
    M, D = x.shape
    tm = PROJ_TM
    G = gate.shape[0]
    tiles_per_group = M // G // tm
    assert M % (G * tm) == 0
    row = pl.BlockSpec((tm, D), lambda i: (i, 0))
    part = pl.BlockSpec((tm, GROUP), lambda i: (i, 0))
    grp = pl.BlockSpec((1, 1, D), lambda i: (i // tiles_per_group, 0, 0))
    vec = pl.BlockSpec((1, D), lambda i: (0, 0))
    return pl.pallas_call(
        _out_proj_kernel,
        grid=(M // tm,),
        in_specs=[part] * 4 + [row, pl.BlockSpec(w_out.shape, lambda i: (0, 0)), grp, vec, vec, grp, grp],
        out_specs=[row, row],
        out_shape=[jax.ShapeDtypeStruct((M, D), F32), jax.ShapeDtypeStruct((M, D), BF16)],
        compiler_params=pltpu.CompilerParams(dimension_semantics=("arbitrary",), vmem_limit_bytes=VMEM_LIMIT_BYTES),
        name="out_proj",
    )(*parts, x, w_out, gate, ln_g[None], ln_b[None], shift, scale)


def rms_norm(x, g, eps=1e-6):
    return x * lax.rsqrt(jnp.mean(jnp.square(x), -1, keepdims=True) + eps) * g


def dwconv(x, w, b, left):
    K = w.shape[0]
    L = x.shape[1]
    xp = jnp.pad(x, ((0, 0), (left, K - 1 - left), (0, 0)))
    return sum(xp[:, j:j + L] * w[j] for j in range(K)) + b


def axial_rope(rows):
    r, col = jnp.meshgrid(jnp.arange(rows, dtype=F32), jnp.arange(GRID_W, dtype=F32), indexing='ij')
    half = MLA_ROPE // 2
    inv = 1.0 / (ROPE_BASE ** (jnp.arange(0, half, 2, dtype=F32) / half))
    ang = jnp.concatenate([r.reshape(-1, 1) * inv, col.reshape(-1, 1) * inv], -1)
    return jnp.cos(ang), jnp.sin(ang)


def mla_mixer(Pac, Pal, tables, q_norm, kv_norm, w_uq, w_ukv, out_norm, ctx_out):
    Lc, L = Pac.shape[1], Pal.shape[1]
    q, kt, v = mla_prep_pallas(jnp.concatenate([Pac, Pal], 1), tables, q_norm, kv_norm, mla_prep_weights(w_uq, w_ukv))
    nc, nl = Lc // ATTN_TQ, L // ATTN_TQ
    yl = mla_attention_pallas(q, kt, v, out_norm, nc, nl, nc + nl)
    yc = mla_attention_pallas(q, kt, v, out_norm, 0, nc, nc) if ctx_out else None
    return yl, yc


def rwkv_finish(y, bonus, v, g, ln_g, ln_b):
    B, L = y.shape[:2]
    mu = jnp.mean(y, -1, keepdims=True)
    var = jnp.mean(jnp.square(y - mu), -1, keepdims=True)
    yn = (y - mu) * lax.rsqrt(var + RWKV_GN_EPS) * ln_g.reshape(RWKV_HEADS, RWKV_HEAD) + ln_b.reshape(RWKV_HEADS, RWKV_HEAD)
    return (yn.reshape(B, L, GROUP) + bonus * v) * g


def rwkv_mixer(Pl, Pc, mu_prev, mu_next, w0, w_up, a0, a_up, g_up, k_k, k_a, r_k, ln_g, ln_b, ctx_out):
    B, L = Pl.shape[:2]
    Lc = Pc.shape[1]
    S = Lc + L
    H, N = RWKV_HEADS, RWKV_HEAD
    r, kk, v, w_f, w_b, kka_f, kka_b, kd_f, kd_b, bonus, g = rwkv_prep_pallas(
        jnp.concatenate([Pc, Pl], 1), Lc, mu_prev, mu_next, w0, w_up, a0, a_up, g_up, k_k, k_a, r_k)

    def to_scan(lo, hi):
        return jnp.concatenate([jnp.transpose(t.reshape(B, S, H, N), (1, 3, 0, 2)).reshape(S, N, B * H)
                                for t in (lo, hi)], -1)

    yf, yb = rwkv_scan_pallas(to_scan(r, kk), to_scan(v, v), to_scan(w_f, w_b), to_scan(kka_f, kka_b),
                              to_scan(kd_f, kd_b), Lc)
    y = yf[..., :B * H] + yb[..., B * H:]
    y = jnp.transpose(y.reshape(S, N, B, H), (2, 0, 3, 1))
    out_l = rwkv_finish(y[:, Lc:], bonus[:, Lc:], v[:, Lc:], g[:, Lc:], ln_g, ln_b)
    out_c = None
    if ctx_out:
        out_c = rwkv_finish(y[:, :Lc], bonus[:, :Lc], v[:, :Lc], g[:, :Lc], ln_g, ln_b)
    return out_l, out_c


def hyena_filters(L, w1, b1, w2, b2, w3):
    t01 = jnp.linspace(0.0, 1.0, L, dtype=F32)[:, None]
    bands = jnp.linspace(1e-4, HY_BANDS - 1, HY_BANDS, dtype=F32)[None, :]
    wpos = (2.0 * math.pi / L) * jnp.arange(L, dtype=F32)[:, None]
    z = jnp.concatenate([t01, jnp.cos(bands * wpos), -jnp.sin(bands * wpos)], -1)
    h = jnp.sin(HY_SIN_FREQ * (z @ w1 + b1))
    h = jnp.sin(HY_SIN_FREQ * (h @ w2 + b2))
    h = (h @ w3).reshape(L, HY_ORDER, 2, GROUP)
    deltas = jnp.abs(jnp.linspace(HY_DECAY_MIN, HY_DECAY_MAX, GROUP, dtype=F32))
    window = jnp.exp(-t01 * deltas) + HY_SHIFT
    return h * window[:, None, None, :]


def hyena_sequence(Pd, conv_w, conv_b, w1, b1, w2, b2, w3, d_skip):
    B, L, _ = Pd.shape
    C = GROUP
    T = HY_T
    nb = L // T
    z = dwconv(Pd, conv_w, conv_b, 1)
    zT = jnp.transpose(z.reshape(B, nb, T, 3 * C), (3, 1, 0, 2)).reshape(3 * C, nb * B, T)
    h = hyena_filters(L, w1, b1, w2, b2, w3)
    hf = jnp.transpose(h[:, :, 0], (1, 2, 0))
    hb = jnp.transpose(h[:, :, 1], (1, 2, 0))
    kfull = jnp.concatenate([jnp.zeros((HY_ORDER, C, 1), F32), jnp.flip(hb[..., 1:], -1), hf], -1)
    oT = hyena_conv_pallas(kfull, d_skip, zT)
    return jnp.transpose(oT.reshape(C, nb, B, T), (2, 1, 3, 0)).reshape(B, L, C)


def swiglu(x, wg, wu, wd):
    n_blocks = x.shape[0] // MOE_BLOCK
    return grouped_swiglu_pallas(x.astype(BF16), jnp.zeros((n_blocks,), jnp.int32), jnp.int32(n_blocks),
                                 wg[None].astype(BF16), wu[None].astype(BF16), wd[None].astype(BF16))


def moe_swiglu(x, router, wg, wu, wd):
    N, D = x.shape
    logits = x @ router
    top_v, top_i = lax.top_k(logits, TOP_K)
    gates = jax.nn.softmax(top_v, axis=-1)
    A = N * TOP_K
    e_flat = top_i.reshape(-1)
    tok_flat = jnp.arange(A, dtype=jnp.int32) // TOP_K
    order = jnp.argsort(e_flat)
    e_sorted = e_flat[order]
    counts = jnp.bincount(e_flat, length=N_EXPERTS)
    starts = jnp.cumsum(counts) - counts
    padded = (counts + MOE_BLOCK - 1) // MOE_BLOCK * MOE_BLOCK
    pends = jnp.cumsum(padded)
    pstarts = pends - padded
    dest = (pstarts[e_sorted] + jnp.arange(A, dtype=jnp.int32) - starts[e_sorted]).astype(jnp.int32)
    n_blocks = -(-A // MOE_BLOCK) + N_EXPERTS
    n_slots = n_blocks * MOE_BLOCK
    block_expert = jnp.clip(jnp.searchsorted(pends, jnp.arange(n_blocks) * MOE_BLOCK, side='right'), 0, N_EXPERTS - 1)
    slot_e = jnp.repeat(block_expert, MOE_BLOCK)
    slot_pos = jnp.arange(n_slots, dtype=jnp.int32) - pstarts[slot_e].astype(jnp.int32)
    slot_src = jnp.clip(starts[slot_e].astype(jnp.int32) + slot_pos, 0, A - 1)
    slot_tok = jnp.where(slot_pos < counts[slot_e], tok_flat[order[slot_src]], N)
    xp = jnp.concatenate([x.astype(BF16), jnp.zeros((1, D), BF16)], 0)
    ys = grouped_swiglu_pallas(xp[slot_tok], block_expert, pends[-1] // MOE_BLOCK,
                               wg.astype(BF16), wu.astype(BF16), wd.astype(BF16))
    slot_of = dest[jnp.argsort(order)].reshape(N, TOP_K)
    return ys[slot_of[:, 0]] * gates[:, 0:1] + ys[slot_of[:, 1]] * gates[:, 1:2]


def kernel(x, c, ctx, c_ctx, ada_w, ada_b, w_in, mla_q_norm, mla_kv_norm, mla_w_uq, mla_w_ukv, mla_out_norm, lru_conv_w, lru_conv_b, lru_w_r, lru_b_r, lru_w_i, lru_b_i, lru_lambda, lru_out_norm, rwkv_mu_prev, rwkv_mu_next, rwkv_w0, rwkv_w_up, rwkv_a0, rwkv_a_up, rwkv_g_up, rwkv_k_k, rwkv_k_a, rwkv_r_k, rwkv_ln_g, rwkv_ln_b, hy_conv_w, hy_conv_b, hy_f_w1, hy_f_b1, hy_f_w2, hy_f_b2, hy_f_w3, hy_d, hy_out_norm, w_out, ln1_g, ln1_b, ln2_g, ln2_b, ffn_w_gate, ffn_w_up, ffn_w_down, moe_router, moe_w_gate, moe_w_up, moe_w_down):
    B, L, D = x.shape
    Lc = ctx.shape[1]
    rows = L // GRID_W
    mla_tabs = mla_tables(*axial_rope(rows), Lc)
    s_lat = jax.nn.silu(c)
    s_ctx = jax.nn.silu(c_ctx)
    Ml, Mc = B * L, B * Lc
    xl, xc = x.reshape(Ml, D), ctx.reshape(Mc, D)
    prev = None
    for li in range(DEPTH):
        ctx_out = li < DEPTH - 1
        mod_l = (s_lat @ ada_w[li] + ada_b[li]).reshape(B, 6, 1, D)
        mod_c = (s_ctx @ ada_w[li] + ada_b[li]).reshape(6, 1, D)
        ml = [mod_l[:, q] for q in range(6)]
        mc = [mod_c[q][None] for q in range(6)]

        w_in_l = w_in[li]
        w_secs = [w.astype(BF16) for w in (mla_section_weight(w_in_l[:, :B_X]), w_in_l[:, B_X:C_OFF],
                                           w_in_l[:, C_OFF:D_OFF], w_in_l[:, D_OFF:])]
        res_l = res_c = None
        if prev is not None:
            out, gate_l, gate_c, g2, b2 = prev
            res_l, res_c = (out, 0, gate_l, g2, b2), (out, Ml, gate_c, g2, b2)
        xl_new, (Pal, Pbl, Pcl, Pdl) = in_proj_pallas(xl, ml[0], ml[1], w_secs, res_l)
        xc_new, ctx_secs = in_proj_pallas(xc, mc[0], mc[1], w_secs if ctx_out else w_secs[:3], res_c)
        if prev is not None:
            xl, xc = xl_new, xc_new
        Pal, Pbl, Pcl, Pdl = (p.reshape(B, L, -1) for p in (Pal, Pbl, Pcl, Pdl))
        Pac, Pbc, Pcc = (p.reshape(B, Lc, -1) for p in ctx_secs[:3])
        a_l, a_c = mla_mixer(Pac, Pal, mla_tabs, mla_q_norm[li], mla_kv_norm[li], mla_w_uq[li], mla_w_ukv[li],
                             mla_out_norm[li], ctx_out)
        b_l, b_c = rglru_pallas(Pbl, Pbc, lru_conv_w[li], lru_conv_b[li], lru_w_r[li], lru_b_r[li], lru_w_i[li],
                                lru_b_i[li], lru_lambda[li], lru_out_norm[li])
        c_l, c_c = rwkv_mixer(Pcl, Pcc, rwkv_mu_prev[li], rwkv_mu_next[li], rwkv_w0[li], rwkv_w_up[li], rwkv_a0[li],
                              rwkv_a_up[li], rwkv_g_up[li], rwkv_k_k[li], rwkv_k_a[li], rwkv_r_k[li],
                              rwkv_ln_g[li], rwkv_ln_b[li], ctx_out)
        d_l = rms_norm(hyena_sequence(Pdl, hy_conv_w[li], hy_conv_b[li], hy_f_w1[li], hy_f_b1[li], hy_f_w2[li],
                                      hy_f_b2[li], hy_f_w3[li], hy_d[li]), hy_out_norm[li])
        w_out_l = w_out[li].astype(BF16)
        xl, tokens = out_proj_pallas([t.reshape(Ml, GROUP) for t in (a_l, b_l, c_l, d_l)], xl, w_out_l,
                                     ml[2], ln1_g[li], ln1_b[li], ml[3], ml[4])
        if ctx_out:
            Pdc = ctx_secs[3].reshape(B, Lc, -1)
            d_c = rms_norm(hyena_sequence(Pdc, hy_conv_w[li], hy_conv_b[li], hy_f_w1[li], hy_f_b1[li], hy_f_w2[li],
                                          hy_f_b2[li], hy_f_w3[li], hy_d[li]), hy_out_norm[li])
            xc, fc = out_proj_pallas([t.reshape(Mc, GROUP) for t in (a_c, b_c, c_c, d_c)], xc, w_out_l,
                                     mc[2], ln1_g[li], ln1_b[li], mc[3], mc[4])
            tokens = jnp.concatenate([tokens, fc], 0)
        j = li // 2
        if li % 2 == 0:
            out = swiglu(tokens, ffn_w_gate[j], ffn_w_up[j], ffn_w_down[j])
        else:
            out = moe_swiglu(tokens, moe_router[j], moe_w_gate[j], moe_w_up[j], moe_w_down[j])
        prev = (out, ml[5], mc[5], ln2_g[li], ln2_b[li])
    out, gate_l, _, g2, b2 = prev
    xl, _ = in_proj_pallas(xl, ml[0], ml[1], [], (out, 0, gate_l, g2, b2))
    return xl.reshape(B, L, D)
```

```python
import math
from functools import partial

import jax
import jax.numpy as jnp
from jax import lax
from jax.experimental import pallas as pl
from jax.experimental.pallas import tpu as pltpu

F32 = jnp.float32
BF16 = jnp.bfloat16

SUBLANES = 8
LANES = 128
VMEM_LIMIT_BYTES = 48 * 1024 * 1024

D_MODEL = 1024
DEPTH = 4
GRID_W = 64
GROUP = D_MODEL // 4

MLA_HEADS = 4
MLA_NOPE = 64
MLA_ROPE = 32
MLA_V = 64
MLA_Q_RANK = 192
MLA_KV_RANK = 128
ROPE_BASE = 10000.0
Q_BLOCK = 128

LRU_BLOCKS = 4
LRU_CONV = 4
LRU_CONV_LEFT = 2
LRU_C = 8.0

RWKV_HEADS = 4
RWKV_HEAD = GROUP // RWKV_HEADS
RWKV_DECAY_LORA = 32
RWKV_AAA_LORA = 32
RWKV_GATE_LORA = 64
RWKV_GN_EPS = 64e-5

HY_ORDER = 2
HY_SHORT = 3
HY_BANDS = 16
HY_EMB = 1 + 2 * HY_BANDS
HY_HIDDEN = 64
HY_SIN_FREQ = 1.0
HY_DECAY_MIN = math.log(1e-2) / 1.5
HY_DECAY_MAX = math.log(1e-2) / 0.3
HY_SHIFT = 0.05

N_EXPERTS = 8
TOP_K = 2
MOE_BLOCK = 512

ALPHA = (2.0 * DEPTH) ** 0.25

A_CQ = 0
A_CKV = A_CQ + MLA_Q_RANK
A_KR = A_CKV + MLA_KV_RANK
B_X = A_KR + MLA_ROPE
B_GATE = B_X + GROUP
C_OFF = B_GATE + GROUP
C_R = 0
C_K = GROUP
C_V = 2 * GROUP
C_WD = 3 * GROUP
C_AD = C_WD + 2 * RWKV_DECAY_LORA
C_GD = C_AD + 2 * RWKV_AAA_LORA
C_COLS = C_GD + RWKV_GATE_LORA
D_OFF = C_OFF + C_COLS
D_COLS = (HY_ORDER + 1) * GROUP


RWKV_TIME_BLOCK = 16
V_TILES = RWKV_HEAD // SUBLANES
RWKV_KEY_GROUP = 16


def _rwkv_scan_kernel(rkf, rkb, vvf, vvb, wf, wb, kaf, kab, kdf, kdb, yf_ref, yb_ref, s_ref, m_ref):
    @pl.when(pl.program_id(0) == 0)
    def _():
        s_ref[...] = jnp.zeros_like(s_ref)

    n_t = rkf.shape[0]
    p = rkf.shape[2]
    fwd_lanes = lax.broadcasted_iota(jnp.int32, (RWKV_HEAD, p), 1) < p // 2

    for j in range(n_t):
        jb = n_t - 1 - j
        rk_f, rk_b = rkf[j], rkb[jb]
        m_ref[0, j] = jnp.where(fwd_lanes, rk_f, pltpu.roll(rk_b, p // 2, 1))
        m_ref[2, j] = jnp.where(fwd_lanes, pltpu.roll(rk_f, p // 2, 1), rk_b)
        for q, (f_ref, b_ref) in ((1, (wf, wb)), (3, (kaf, kab)), (4, (vvf, vvb)), (5, (kdf, kdb))):
            m_ref[q, j] = jnp.where(fwd_lanes, f_ref[j], b_ref[jb])

    r_ref, w_ref, kk_ref, kka_ref, v_ref, kd_ref = (m_ref.at[q] for q in range(6))

    def step(t, carry):
        def row(ref, k):
            return jnp.broadcast_to(ref[t, pl.ds(k, 1), :], (SUBLANES, p))[None]

        def sa_group(g, sa):
            k0 = pl.multiple_of(g * RWKV_KEY_GROUP, RWKV_KEY_GROUP)
            for j in range(RWKV_KEY_GROUP):
                sa = sa + s_ref[k0 + j] * row(kk_ref, k0 + j)
            return sa

        zero = jnp.zeros((V_TILES, SUBLANES, p), F32)
        sa = lax.fori_loop(0, RWKV_HEAD // RWKV_KEY_GROUP, sa_group, zero)
        vt = v_ref[t].reshape(V_TILES, SUBLANES, p)

        def update_group(g, y):
            k0 = pl.multiple_of(g * RWKV_KEY_GROUP, RWKV_KEY_GROUP)
            for j in range(RWKV_KEY_GROUP):
                k = k0 + j
                sn = s_ref[k] * row(w_ref, k) - sa * row(kka_ref, k) + vt * row(kd_ref, k)
                s_ref[k] = sn
                y = y + sn * row(r_ref, k)
            return y

        y = lax.fori_loop(0, RWKV_HEAD // RWKV_KEY_GROUP, update_group, zero).reshape(RWKV_HEAD, p)
        yf_ref[t] = y
        yb_ref[n_t - 1 - t] = y
        return carry

    lax.fori_loop(0, n_t, step, 0)


def rwkv_scan_pallas(rk, vv, w, kka, kd, n_ctx):
    n_steps, n, p = rk.shape
    tb = RWKV_TIME_BLOCK
    assert n == RWKV_HEAD and p == LANES and n_steps % tb == 0 and n_ctx % tb == 0
    nb, nc = n_steps // tb, n_ctx // tb
    fwd = pl.BlockSpec((tb, n, p), lambda g: (g, 0, 0))
    bwd = pl.BlockSpec((tb, n, p), lambda g: (jnp.where(g < nc, nc - 1 - g, nb + nc - 1 - g), 0, 0))
    out = jax.ShapeDtypeStruct((n_steps, n, p), F32)
    return pl.pallas_call(
        _rwkv_scan_kernel,
        grid=(nb,),
        in_specs=[fwd, bwd] * 5,
        out_specs=[fwd, bwd],
        out_shape=[out, out],
        scratch_shapes=[pltpu.VMEM((n, V_TILES, SUBLANES, LANES), F32), pltpu.VMEM((6, tb, n, LANES), F32)],
        compiler_params=pltpu.CompilerParams(dimension_semantics=("arbitrary",), vmem_limit_bytes=VMEM_LIMIT_BYTES),
        name="rwkv_scan",
    )(rk, rk, vv, vv, w, w, kka, kka, kd, kd)


ATTN_TQ = 256
ATTN_TK = 256


def _attn_kernel(q_ref, k_ref, v_ref, g_ref, o_ref, s_ref):
    tq = q_ref.shape[1]
    n_chunks = k_ref.shape[1]
    n_tiles = ATTN_TK // LANES
    pair_out = []
    for hp in range(MLA_HEADS // 2):
        o_pair = jnp.zeros((tq, LANES), F32)
        for h in (2 * hp, 2 * hp + 1):
            qh = q_ref[0, :, pl.ds(LANES * h, LANES)]

            m_acc = jnp.full((tq, LANES), -jnp.inf, F32)
            for c in range(n_chunks):
                s = jnp.dot(qh, k_ref[0, c, pl.ds(LANES * h, LANES), :], preferred_element_type=F32)
                s_ref[c] = s
                for j in range(n_tiles):
                    m_acc = jnp.maximum(m_acc, s[:, LANES * j:LANES * (j + 1)])
            m_full = jnp.broadcast_to(jnp.max(m_acc, -1, keepdims=True), (tq, LANES))

            l_acc = jnp.zeros((tq, LANES), F32)
            acc = jnp.zeros((tq, LANES), F32)
            for c in range(n_chunks):
                s = s_ref[c]
                ps = []
                for j in range(n_tiles):
                    p = jnp.exp2(s[:, LANES * j:LANES * (j + 1)] - m_full)
                    l_acc = l_acc + p
                    ps.append(p.astype(BF16))
                vh = v_ref[0, pl.ds(c * ATTN_TK, ATTN_TK), pl.ds(LANES * h, LANES)]
                acc = acc + jnp.dot(jnp.concatenate(ps, -1), vh, preferred_element_type=F32)
            o_pair = o_pair + acc / jnp.sum(l_acc, -1, keepdims=True)
        pair_out.append(o_pair)
    o = jnp.concatenate(pair_out, -1)
    o_ref[0] = o * lax.rsqrt(jnp.mean(jnp.square(o), -1, keepdims=True) + 1e-6) * g_ref[...]


def mla_attention_pallas(q, kt, v, out_norm, q_start, n_q, n_kv):
    B, _, W = q.shape
    return pl.pallas_call(
        _attn_kernel,
        grid=(B, n_q),
        in_specs=[
            pl.BlockSpec((1, ATTN_TQ, W), lambda b, i: (b, i + q_start, 0)),
            pl.BlockSpec((1, n_kv, W, ATTN_TK), lambda b, i: (b, 0, 0, 0)),
            pl.BlockSpec((1, n_kv * ATTN_TK, W), lambda b, i: (b, 0, 0)),
            pl.BlockSpec((1, GROUP), lambda b, i: (0, 0)),
        ],
        out_specs=pl.BlockSpec((1, ATTN_TQ, GROUP), lambda b, i: (b, i, 0)),
        out_shape=jax.ShapeDtypeStruct((B, n_q * ATTN_TQ, GROUP), F32),
        scratch_shapes=[pltpu.VMEM((n_kv, ATTN_TQ, ATTN_TK), F32)],
        compiler_params=pltpu.CompilerParams(dimension_semantics=("arbitrary", "arbitrary"),
                                             vmem_limit_bytes=VMEM_LIMIT_BYTES),
        name="mla_attention",
    )(q, kt, v, out_norm[None])


MLA_PA_COLS = 4 * LANES
MLA_W = MLA_HEADS * LANES


def _rot_cols(w):
    h = w.shape[1] // 2
    return jnp.concatenate([-w[:, h:], w[:, :h]], 1)


def mla_section_weight(w_a):
    z = jnp.zeros((w_a.shape[0], LANES // 2), w_a.dtype)
    kr = w_a[:, A_KR:A_KR + MLA_ROPE]
    return jnp.concatenate([w_a[:, A_CKV:A_CKV + MLA_KV_RANK], kr, _rot_cols(kr), z,
                            w_a[:, A_CQ:A_CQ + MLA_Q_RANK], z], 1)


def mla_prep_weights(w_uq, w_ukv):
    H, DN, DR, DV = MLA_HEADS, MLA_NOPE, MLA_ROPE, MLA_V
    wq = jnp.zeros((2 * LANES, MLA_W), F32)
    wq_rot = jnp.zeros((2 * LANES, MLA_W), F32)
    wk = jnp.zeros((2 * LANES, MLA_W), F32)
    wv = jnp.zeros((LANES, MLA_W), F32)
    place = jnp.eye(DR, dtype=F32)
    for h in range(H):
        q_h = w_uq[:, h * (DN + DR):(h + 1) * (DN + DR)]
        wq = wq.at[:MLA_Q_RANK, LANES * h:LANES * h + DN + DR].set(q_h)
        wq_rot = wq_rot.at[:MLA_Q_RANK, LANES * h + DN:LANES * h + DN + DR].set(_rot_cols(q_h[:, DN:]))
        kv_h = w_ukv[:, h * (DN + DV):(h + 1) * (DN + DV)]
        wk = wk.at[:MLA_KV_RANK, LANES * h:LANES * h + DN].set(kv_h[:, :DN])
        wk = wk.at[LANES:LANES + DR, LANES * h + DN:LANES * h + DN + DR].set(place)
        v0 = LANES * h + DV * (h % 2)
        wv = wv.at[:, v0:v0 + DV].set(kv_h[:, DN:])
    return wq.astype(BF16), wq_rot.astype(BF16), wk.astype(BF16), wv.astype(BF16)


def mla_tables(cos, sin, n_ctx):
    H, DN, DR = MLA_HEADS, MLA_NOPE, MLA_ROPE
    L = cos.shape[0]
    cf = jnp.concatenate([jnp.ones((n_ctx, DR), F32), jnp.concatenate([cos, cos], -1)], 0)
    sf = jnp.concatenate([jnp.zeros((n_ctx, DR), F32), jnp.concatenate([sin, sin], -1)], 0)
    S = n_ctx + L
    tab_k = jnp.concatenate([cf, sf, jnp.zeros((S, LANES - 2 * DR), F32)], -1)
    scale = (DN + DR) ** -0.5 * math.log2(math.e)
    zpad = jnp.zeros((S, LANES - DN - DR), F32)
    qc_h = jnp.concatenate([jnp.full((S, DN), scale, F32), cf * scale, zpad], -1)
    qs_h = jnp.concatenate([jnp.zeros((S, DN), F32), sf * scale, zpad], -1)
    return tab_k, jnp.tile(qc_h, (1, H)), jnp.tile(qs_h, (1, H))


def _mla_prep_kernel(pa_ref, tk_ref, tqc_ref, tqs_ref, kvn_ref, qn_ref, wq_ref, wqr_ref, wk_ref, wv_ref,
                     q_ref, kt_ref, v_ref):
    pa = pa_ref[0]
    ckv = pa[:, :LANES]
    ckv = ckv * lax.rsqrt(jnp.mean(jnp.square(ckv), -1, keepdims=True) + 1e-6) * kvn_ref[...]
    t = pa[:, LANES:2 * LANES] * tk_ref[...]
    kr = t + pltpu.roll(t, LANES - MLA_ROPE, 1)
    kr = jnp.where(lax.broadcasted_iota(jnp.int32, kr.shape, 1) < MLA_ROPE, kr, 0.0)
    k = jnp.dot(jnp.concatenate([ckv, kr], -1).astype(BF16), wk_ref[...], preferred_element_type=F32)
    kt_ref[0, 0] = k.T.astype(BF16)
    v_ref[0] = jnp.dot(ckv.astype(BF16), wv_ref[...], preferred_element_type=F32).astype(BF16)
    cq = pa[:, 2 * LANES:]
    ms = jnp.sum(jnp.square(cq), -1, keepdims=True) * (1.0 / MLA_Q_RANK)
    cq = (cq * lax.rsqrt(ms + 1e-6) * qn_ref[...]).astype(BF16)
    qa = jnp.dot(cq, wq_ref[...], preferred_element_type=F32)
    qb = jnp.dot(cq, wqr_ref[...], preferred_element_type=F32)
    q_ref[0] = (qa * tqc_ref[...] + qb * tqs_ref[...]).astype(BF16)


def mla_prep_pallas(pa, tables, q_norm, kv_norm, weights):
    B, S, _ = pa.shape
    T = ATTN_TK
    assert S % T == 0
    tab_k, tab_qc, tab_qs = tables
    wq, wq_rot, wk, wv = weights
    qn = jnp.concatenate([q_norm, jnp.zeros((2 * LANES - MLA_Q_RANK,), F32)])[None]

    def full(a):
        return pl.BlockSpec(a.shape, lambda b, i: (0,) * a.ndim)

    def rows(width):
        return pl.BlockSpec((T, width), lambda b, i: (i, 0))

    return pl.pallas_call(
        _mla_prep_kernel,
        grid=(B, S // T),
        in_specs=[pl.BlockSpec((1, T, MLA_PA_COLS), lambda b, i: (b, i, 0)),
                  rows(LANES), rows(MLA_W), rows(MLA_W),
                  pl.BlockSpec((1, LANES), lambda b, i: (0, 0)), full(qn), full(wq), full(wq_rot), full(wk), full(wv)],
        out_specs=[pl.BlockSpec((1, T, MLA_W), lambda b, i: (b, i, 0)),
                   pl.BlockSpec((1, 1, MLA_W, T), lambda b, i: (b, i, 0, 0)),
                   pl.BlockSpec((1, T, MLA_W), lambda b, i: (b, i, 0))],
        out_shape=[jax.ShapeDtypeStruct((B, S, MLA_W), BF16),
                   jax.ShapeDtypeStruct((B, S // T, MLA_W, T), BF16),
                   jax.ShapeDtypeStruct((B, S, MLA_W), BF16)],
        compiler_params=pltpu.CompilerParams(dimension_semantics=("arbitrary", "arbitrary")),
        name="mla_prep",
    )(pa, tab_k, tab_qc, tab_qs, kv_norm[None], qn, wq, wq_rot, wk, wv)


LRU_CHUNK = 256
LRU_HALO = SUBLANES


def _lru_kernel(xl_ref, xc_ref, wbd_ref, bias_ref, c8_ref, cw_ref, cb_ref, gn_ref,
                yl_ref, yc_ref, xs_l, xs_c, hf_l, hf_c, a_s, b_s, hb_s):
    C = GROUP
    CH = LRU_CHUNK
    L = xl_ref.shape[1]
    Lc = xc_ref.shape[1]

    def stage(x_ref, xs, n):
        xs[pl.ds(0, LRU_HALO), :] = jnp.zeros((LRU_HALO, C), F32)
        xs[pl.ds(LRU_HALO + n, LRU_HALO), :] = jnp.zeros((LRU_HALO, C), F32)

        def cp(i, c):
            r0 = pl.multiple_of(i * CH, CH)
            xs[pl.ds(LRU_HALO + r0, CH), :] = x_ref[0, pl.ds(r0, CH), pl.ds(0, C)]
            return c

        lax.fori_loop(0, n // CH, cp, 0)

    stage(xl_ref, xs_l, L)
    stage(xc_ref, xs_c, Lc)

    def coeffs(xs, base, d):
        xv = xs[pl.ds(base, CH + 2 * LRU_HALO), :]
        u = cb_ref[...]
        for j in range(LRU_CONV):
            o = LRU_HALO - LRU_CONV_LEFT + j
            u = u + xv[o:o + CH] * cw_ref[pl.ds(j, 1), :]
        z = jnp.dot(u.astype(BF16), wbd_ref[:, pl.ds(d * 2 * C, 2 * C)], preferred_element_type=F32)
        z = z + bias_ref[:, pl.ds(d * 2 * C, 2 * C)]
        r = jax.nn.sigmoid(z[:, :C])
        i = jax.nn.sigmoid(z[:, C:])
        log_a = r * c8_ref[pl.ds(d, 1), :]
        a = jnp.exp(log_a)
        a_s[...] = a
        b_s[...] = jnp.sqrt(-jnp.tanh(log_a) * (a * a + 1.0)) * (i * u)

    def row_scan(h, out_ref, out_base, reverse):
        n_groups = CH // SUBLANES

        def group(g, h):
            r0 = pl.multiple_of((n_groups - 1 - g if reverse else g) * SUBLANES, SUBLANES)
            for j in (reversed(range(SUBLANES)) if reverse else range(SUBLANES)):
                a_t = jnp.broadcast_to(a_s[pl.ds(r0 + j, 1), :], (SUBLANES, C))
                b_t = jnp.broadcast_to(b_s[pl.ds(r0 + j, 1), :], (SUBLANES, C))
                h = a_t * h + b_t
                out_ref[pl.ds(out_base + r0 + j, 1), :] = h[0:1, :]
            return h

        return lax.fori_loop(0, n_groups, group, h)

    h0 = jnp.zeros((SUBLANES, C), F32)

    h = h0
    for ci in range(Lc // CH):
        coeffs(xs_c, ci * CH, 0)
        h = row_scan(h, hf_c, ci * CH, False)

    def fwd_chunk(ci, h):
        base = pl.multiple_of(ci * CH, CH)
        coeffs(xs_l, base, 0)
        return row_scan(h, hf_l, base, False)

    lax.fori_loop(0, L // CH, fwd_chunk, h)

    def combine(x_ref, hf, base, y_ref):
        hl = hf[pl.ds(base, CH), :] + hb_s[...]
        g = jax.nn.gelu(x_ref[0, pl.ds(base, CH), pl.ds(C, C)])
        v = hl * g
        y = v * lax.rsqrt(jnp.mean(jnp.square(v), -1, keepdims=True) + 1e-6) * gn_ref[...]
        y_ref[0, pl.ds(base, CH), :] = y

    h = h0
    for ci in reversed(range(Lc // CH)):
        coeffs(xs_c, ci * CH, 1)
        h = row_scan(h, hb_s, 0, True)
        combine(xc_ref, hf_c, ci * CH, yc_ref)

    def bwd_chunk(k, h):
        base = pl.multiple_of((L // CH - 1 - k) * CH, CH)
        coeffs(xs_l, base, 1)
        h = row_scan(h, hb_s, 0, True)
        combine(xl_ref, hf_l, base, yl_ref)
        return h

    lax.fori_loop(0, L // CH, bwd_chunk, h)


def rglru_pallas(xg_l, xg_c, conv_w, conv_b, w_r, b_r, w_i, b_i, lam, out_norm):
    B, L, _ = xg_l.shape
    Lc = xg_c.shape[1]
    C = GROUP
    assert L % LRU_CHUNK == 0 and Lc % LRU_CHUNK == 0

    def bd(w):
        return jax.scipy.linalg.block_diag(*[w[n] for n in range(LRU_BLOCKS)])

    wbd = jnp.concatenate([bd(w_r[0]), bd(w_i[0]), bd(w_r[1]), bd(w_i[1])], 1).astype(BF16)
    bias = jnp.concatenate([b_r[0], b_i[0], b_r[1], b_i[1]])[None]
    c8 = -LRU_C * jax.nn.softplus(-lam)

    def full(shape):
        return pl.BlockSpec(shape, lambda b: (0,) * len(shape))

    return pl.pallas_call(
        _lru_kernel,
        grid=(B,),
        in_specs=[
            pl.BlockSpec((1, L, 2 * C), lambda b: (b, 0, 0)),
            pl.BlockSpec((1, Lc, 2 * C), lambda b: (b, 0, 0)),
            full((C, 4 * C)), full((1, 4 * C)), full((2, C)), full((LRU_CONV, C)), full((1, C)), full((1, C)),
        ],
        out_specs=[
            pl.BlockSpec((1, L, C), lambda b: (b, 0, 0)),
            pl.BlockSpec((1, Lc, C), lambda b: (b, 0, 0)),
        ],
        out_shape=[jax.ShapeDtypeStruct((B, L, C), F32), jax.ShapeDtypeStruct((B, Lc, C), F32)],
        scratch_shapes=[
            pltpu.VMEM((L + 2 * LRU_HALO, C), F32),
            pltpu.VMEM((Lc + 2 * LRU_HALO, C), F32),
            pltpu.VMEM((L, C), F32),
            pltpu.VMEM((Lc, C), F32),
            pltpu.VMEM((LRU_CHUNK, C), F32),
            pltpu.VMEM((LRU_CHUNK, C), F32),
            pltpu.VMEM((LRU_CHUNK, C), F32),
        ],
        compiler_params=pltpu.CompilerParams(dimension_semantics=("arbitrary",), vmem_limit_bytes=VMEM_LIMIT_BYTES),
        name="rglru",
    )(xg_l, xg_c, wbd, bias, c8, conv_w, conv_b[None], out_norm[None])


HY_T = 256
HY_CB = 8


def _hyena_kernel(k_ref, d_ref, v_ref, x1_ref, x2_ref, o_ref, u_s, acc_s):
    n_rows = v_ref.shape[1]
    T = HY_T
    nb = k_ref.shape[2] // (2 * T)
    bsz = n_rows // nb

    def conv(ci, order):
        acc_s[...] = jnp.zeros_like(acc_s)
        for dd in range(-(nb - 1), nb):
            w2 = k_ref[order, pl.ds(ci, 1), pl.ds(T * (dd + nb - 1), 2 * T)]
            x = jnp.broadcast_to(w2, (T, 2 * T))
            r = pltpu.roll(x, 0, 1, stride=1, stride_axis=0)
            tb = r[:, T:].astype(BF16)
            j0, j1 = max(0, -dd), min(nb, nb - dd)
            lhs = u_s[pl.ds(bsz * j0, bsz * (j1 - j0)), :]
            dst = pl.ds(bsz * (j0 + dd), bsz * (j1 - j0))
            acc_s[dst, :] = acc_s[dst, :] + jnp.dot(lhs, tb, preferred_element_type=F32)

    def channel(ci, carry):
        v = v_ref[ci]
        u_s[...] = v.astype(BF16)
        conv(ci, 0)
        u = x1_ref[ci] * (acc_s[...] + v * d_ref[pl.ds(ci, 1), pl.ds(0, 1)])
        u_s[...] = u.astype(BF16)
        conv(ci, 1)
        o_ref[ci] = x2_ref[ci] * (acc_s[...] + u * d_ref[pl.ds(ci, 1), pl.ds(1, 1)])
        return carry

    lax.fori_loop(0, HY_CB, channel, 0)


def hyena_conv_pallas(kfull, d_skip, zT):
    C3, R, T = zT.shape
    C = C3 // (HY_ORDER + 1)
    two_l = kfull.shape[2]
    assert T == HY_T and C % HY_CB == 0
    n_cb = C // HY_CB
    blk = pl.BlockSpec((HY_CB, R, T), lambda c: (c, 0, 0))
    return pl.pallas_call(
        _hyena_kernel,
        grid=(n_cb,),
        in_specs=[
            pl.BlockSpec((HY_ORDER, HY_CB, two_l), lambda c: (0, c, 0)),
            pl.BlockSpec((HY_CB, HY_ORDER), lambda c: (c, 0)),
            blk,
            pl.BlockSpec((HY_CB, R, T), lambda c: (c + n_cb, 0, 0)),
            pl.BlockSpec((HY_CB, R, T), lambda c: (c + 2 * n_cb, 0, 0)),
        ],
        out_specs=blk,
        out_shape=jax.ShapeDtypeStruct((C, R, T), F32),
        scratch_shapes=[pltpu.VMEM((R, T), BF16), pltpu.VMEM((R, T), F32)],
        compiler_params=pltpu.CompilerParams(dimension_semantics=("arbitrary",), vmem_limit_bytes=VMEM_LIMIT_BYTES),
        name="hyena_conv",
    )(kfull, d_skip.T, zT, zT, zT)


FFN_TF_MAX = 1408


def _ffn_tile(hidden):
    return max(t for t in range(LANES, FFN_TF_MAX + 1, LANES) if hidden % t == 0)


def _swiglu_kernel(be_ref, nu_ref, x_ref, wg_ref, wu_ref, wd_ref, o_ref, acc_ref):
    i = pl.program_id(0)
    f = pl.program_id(1)

    @pl.when(i < nu_ref[0])
    def _():
        x = x_ref[...]
        g = jnp.dot(x, wg_ref[0], preferred_element_type=F32)
        u = jnp.dot(x, wu_ref[0], preferred_element_type=F32)
        h = (jax.nn.silu(g) * u).astype(BF16)
        part = jnp.dot(h, wd_ref[0], preferred_element_type=F32)

        @pl.when(f == 0)
        def _():
            acc_ref[...] = part

        @pl.when(f > 0)
        def _():
            acc_ref[...] = acc_ref[...] + part

    @pl.when(f == pl.num_programs(1) - 1)
    def _():
        o_ref[...] = jnp.where(i < nu_ref[0], acc_ref[...], 0.0)


def grouped_swiglu_pallas(xs, block_expert, n_used, wg, wu, wd):
    n_rows, D = xs.shape
    F = wg.shape[2]
    TM = MOE_BLOCK
    assert n_rows % TM == 0
    tf = _ffn_tile(F)
    n_blocks = n_rows // TM
    grid_spec = pltpu.PrefetchScalarGridSpec(
        num_scalar_prefetch=2,
        grid=(n_blocks, F // tf),
        in_specs=[
            pl.BlockSpec((TM, D), lambda i, f, be, nu: (i, 0)),
            pl.BlockSpec((1, D, tf), lambda i, f, be, nu: (be[i], 0, f)),
            pl.BlockSpec((1, D, tf), lambda i, f, be, nu: (be[i], 0, f)),
            pl.BlockSpec((1, tf, D), lambda i, f, be, nu: (be[i], f, 0)),
        ],
        out_specs=pl.BlockSpec((TM, D), lambda i, f, be, nu: (i, 0)),
        scratch_shapes=[pltpu.VMEM((TM, D), F32)],
    )
    return pl.pallas_call(
        _swiglu_kernel,
        grid_spec=grid_spec,
        out_shape=jax.ShapeDtypeStruct((n_rows, D), F32),
        compiler_params=pltpu.CompilerParams(dimension_semantics=("arbitrary", "arbitrary"),
                                             vmem_limit_bytes=VMEM_LIMIT_BYTES),
        name="grouped_swiglu",
    )(block_expert.astype(jnp.int32), jnp.reshape(n_used, (1,)).astype(jnp.int32), xs, wg, wu, wd)


PROJ_TM = 512


def _residual_ln(x, branch, gate_ref, g_ref, b_ref):
    s = ALPHA * x + gate_ref[0] * branch
    mu = jnp.mean(s, -1, keepdims=True)
    d = s - mu
    var = jnp.mean(jnp.square(d), -1, keepdims=True)
    return d * lax.rsqrt(var + 1e-5) * g_ref[...] + b_ref[...]


def _in_proj_kernel(*refs, has_ln, n_w):
    x_ref = refs[0]
    pos = 1
    x = x_ref[...]
    if has_ln:
        branch_ref, gate_ref, g_ref, b_ref = refs[1:5]
        pos = 5
        x = _residual_ln(x, branch_ref[...], gate_ref, g_ref, b_ref)
    shift_ref, scale_ref = refs[pos:pos + 2]
    w_refs = refs[pos + 2:pos + 2 + n_w]
    out_refs = refs[pos + 2 + n_w:]
    if has_ln:
        out_refs[0][...] = x
        out_refs = out_refs[1:]
    h = (x * (1.0 + scale_ref[0]) + shift_ref[0]).astype(BF16)
    for w_ref, o_ref in zip(w_refs, out_refs):
        o_ref[...] = jnp.dot(h, w_ref[...], preferred_element_type=F32)


def in_proj_pallas(x, shift, scale, weights, residual=None):
    M, D = x.shape
    tm = PROJ_TM
    G = shift.shape[0]
    tiles_per_group = M // G // tm
    assert M % (G * tm) == 0
    row = pl.BlockSpec((tm, D), lambda i: (i, 0))
    grp = pl.BlockSpec((1, 1, D), lambda i: (i // tiles_per_group, 0, 0))
    vec = pl.BlockSpec((1, D), lambda i: (0, 0))
    args, specs = [x], [row]
    has_ln = residual is not None
    if has_ln:
        branch, first_row, gate, ln_g, ln_b = residual
        assert first_row % tm == 0
        off = first_row // tm
        args += [branch, gate, ln_g[None], ln_b[None]]
        specs += [pl.BlockSpec((tm, D), lambda i: (i + off, 0)), grp, vec, vec]
    args += [shift, scale] + list(weights)
    specs += [grp, grp] + [pl.BlockSpec(w.shape, lambda i: (0, 0)) for w in weights]
    out_shape = [jax.ShapeDtypeStruct((M, w.shape[1]), F32) for w in weights]
    out_specs = [pl.BlockSpec((tm, w.shape[1]), lambda i: (i, 0)) for w in weights]
    if has_ln:
        out_shape = [jax.ShapeDtypeStruct((M, D), F32)] + out_shape
        out_specs = [row] + out_specs
    outs = pl.pallas_call(
        partial(_in_proj_kernel, has_ln=has_ln, n_w=len(weights)),
        grid=(M // tm,),
        in_specs=specs,
        out_specs=out_specs,
        out_shape=out_shape,
        compiler_params=pltpu.CompilerParams(dimension_semantics=("arbitrary",), vmem_limit_bytes=VMEM_LIMIT_BYTES),
        name="in_proj",
    )(*args)
    return (outs[0], outs[1:]) if has_ln else (None, outs)


def _out_proj_kernel(a_ref, b_ref, c_ref, d_ref, x_ref, w_ref, gate_ref, g_ref, bb_ref, shift_ref, scale_ref,
                     xn_ref, f_ref):
    y = jnp.concatenate([a_ref[...], b_ref[...], c_ref[...], d_ref[...]], -1).astype(BF16)
    y = jnp.dot(y, w_ref[...], preferred_element_type=F32)
    xn = _residual_ln(x_ref[...], y, gate_ref, g_ref, bb_ref)
    xn_ref[...] = xn
    f_ref[...] = (xn * (1.0 + scale_ref[0]) + shift_ref[0]).astype(BF16)


def out_proj_pallas(parts, x, w_out, gate, ln_g, ln_b, shift, scale):
    M, D = x.shape
    tm = PROJ_TM
    G = gate.shape[0]
    tiles_per_group = M // G // tm
    assert M % (G * tm) == 0
    row = pl.BlockSpec((tm, D), lambda i: (i, 0))
    part = pl.BlockSpec((tm, GROUP), lambda i: (i, 0))
    grp = pl.BlockSpec((1, 1, D), lambda i: (i // tiles_per_group, 0, 0))
    vec = pl.BlockSpec((1, D), lambda i: (0, 0))
    return pl.pallas_call(
        _out_proj_kernel,
        grid=(M // tm,),
        in_specs=[part] * 4 + [row, pl.BlockSpec(w_out.shape, lambda i: (0, 0)), grp, vec, vec, grp, grp],
        out_specs=[row, row],
        out_shape=[jax.ShapeDtypeStruct((M, D), F32), jax.ShapeDtypeStruct((M, D), BF16)],
        compiler_params=pltpu.CompilerParams(dimension_semantics=("arbitrary",), vmem_limit_bytes=VMEM_LIMIT_BYTES),
        name="out_proj",
    )(*parts, x, w_out, gate, ln_g[None], ln_b[None], shift, scale)


def rms_norm(x, g, eps=1e-6):
    return x * lax.rsqrt(jnp.mean(jnp.square(x), -1, keepdims=True) + eps) * g


def dwconv(x, w, b, left):
    K = w.shape[0]
    L = x.shape[1]
    xp = jnp.pad(x, ((0, 0), (left, K - 1 - left), (0, 0)))
    return sum(xp[:, j:j + L] * w[j] for j in range(K)) + b


def axial_rope(rows):
    r, col = jnp.meshgrid(jnp.arange(rows, dtype=F32), jnp.arange(GRID_W, dtype=F32), indexing='ij')
    half = MLA_ROPE // 2
    inv = 1.0 / (ROPE_BASE ** (jnp.arange(0, half, 2, dtype=F32) / half))
    ang = jnp.concatenate([r.reshape(-1, 1) * inv, col.reshape(-1, 1) * inv], -1)
    return jnp.cos(ang), jnp.sin(ang)


def mla_mixer(Pac, Pal, tables, q_norm, kv_norm, w_uq, w_ukv, out_norm, ctx_out):
    Lc, L = Pac.shape[1], Pal.shape[1]
    q, kt, v = mla_prep_pallas(jnp.concatenate([Pac, Pal], 1), tables, q_norm, kv_norm, mla_prep_weights(w_uq, w_ukv))
    nc, nl = Lc // ATTN_TQ, L // ATTN_TQ
    yl = mla_attention_pallas(q, kt, v, out_norm, nc, nl, nc + nl)
    yc = mla_attention_pallas(q, kt, v, out_norm, 0, nc, nc) if ctx_out else None
    return yl, yc


def token_shift(z, mu_prev, mu_next):
    zp = jnp.pad(z, ((0, 0), (1, 1), (0, 0)))
    return z + mu_prev * (zp[:, :-2] - z) + mu_next * (zp[:, 2:] - z)


def rwkv_finish(y, bonus, v, gd, g_up, ln_g, ln_b):
    B, L = y.shape[:2]
    mu = jnp.mean(y, -1, keepdims=True)
    var = jnp.mean(jnp.square(y - mu), -1, keepdims=True)
    yn = (y - mu) * lax.rsqrt(var + RWKV_GN_EPS) * ln_g.reshape(RWKV_HEADS, RWKV_HEAD) + ln_b.reshape(RWKV_HEADS, RWKV_HEAD)
    g = jax.nn.sigmoid(gd) @ g_up
    return (yn + bonus * v).reshape(B, L, GROUP) * g


def rwkv_mixer(Pl, Pc, mu_prev, mu_next, w0, w_up, a0, a_up, g_up, k_k, k_a, r_k, ln_g, ln_b, ctx_out):
    B, L = Pl.shape[:2]
    Lc = Pc.shape[1]
    S = Lc + L
    H, N = RWKV_HEADS, RWKV_HEAD
    z = jnp.concatenate([token_shift(Pc, mu_prev, mu_next), token_shift(Pl, mu_prev, mu_next)], 1)

    def heads(t):
        return t.reshape(B, S, H, N)

    r = heads(z[..., C_R:C_R + GROUP])
    k = heads(z[..., C_K:C_K + GROUP])
    v = heads(z[..., C_V:C_V + GROUP])
    kk = k * k_k.reshape(H, N)
    kk = kk * lax.rsqrt(jnp.maximum(jnp.sum(jnp.square(kk), -1, keepdims=True), 1e-24))
    w, kka, kd = [], [], []
    for d in range(2):
        wd = z[..., C_WD + d * RWKV_DECAY_LORA:C_WD + (d + 1) * RWKV_DECAY_LORA]
        ad = z[..., C_AD + d * RWKV_AAA_LORA:C_AD + (d + 1) * RWKV_AAA_LORA]
        log_w = -jnp.exp(-jax.nn.softplus(-(w0[d] + jnp.tanh(wd) @ w_up[d])) - 0.5)
        a = heads(jax.nn.sigmoid(a0[d] + ad @ a_up[d]))
        w.append(heads(jnp.exp(log_w)))
        kka.append(kk * a)
        kd.append(k * (1.0 + (a - 1.0) * k_a.reshape(H, N)))

    def to_scan(lo, hi):
        return jnp.concatenate([jnp.transpose(t, (1, 3, 0, 2)).reshape(S, N, B * H) for t in (lo, hi)], -1)

    yf, yb = rwkv_scan_pallas(to_scan(r, kk), to_scan(v, v), to_scan(*w), to_scan(*kka), to_scan(*kd), Lc)
    y = yf[..., :B * H] + yb[..., B * H:]
    y = jnp.transpose(y.reshape(S, N, B, H), (2, 0, 3, 1))
    bonus = sum(jnp.sum(r * kd_d * r_k, -1, keepdims=True) for kd_d in kd)
    gd = z[..., C_GD:C_GD + RWKV_GATE_LORA]
    out_l = rwkv_finish(y[:, Lc:], bonus[:, Lc:], v[:, Lc:], gd[:, Lc:], g_up, ln_g, ln_b)
    out_c = None
    if ctx_out:
        out_c = rwkv_finish(y[:, :Lc], bonus[:, :Lc], v[:, :Lc], gd[:, :Lc], g_up, ln_g, ln_b)
    return out_l, out_c


def hyena_filters(L, w1, b1, w2, b2, w3):
    t01 = jnp.linspace(0.0, 1.0, L, dtype=F32)[:, None]
    bands = jnp.linspace(1e-4, HY_BANDS - 1, HY_BANDS, dtype=F32)[None, :]
    wpos = (2.0 * math.pi / L) * jnp.arange(L, dtype=F32)[:, None]
    z = jnp.concatenate([t01, jnp.cos(bands * wpos), -jnp.sin(bands * wpos)], -1)
    h = jnp.sin(HY_SIN_FREQ * (z @ w1 + b1))
    h = jnp.sin(HY_SIN_FREQ * (h @ w2 + b2))
    h = (h @ w3).reshape(L, HY_ORDER, 2, GROUP)
    deltas = jnp.abs(jnp.linspace(HY_DECAY_MIN, HY_DECAY_MAX, GROUP, dtype=F32))
    window = jnp.exp(-t01 * deltas) + HY_SHIFT
    return h * window[:, None, None, :]


def hyena_sequence(Pd, conv_w, conv_b, w1, b1, w2, b2, w3, d_skip):
    B, L, _ = Pd.shape
    C = GROUP
    T = HY_T
    nb = L // T
    z = dwconv(Pd, conv_w, conv_b, 1)
    zT = jnp.transpose(z.reshape(B, nb, T, 3 * C), (3, 1, 0, 2)).reshape(3 * C, nb * B, T)
    h = hyena_filters(L, w1, b1, w2, b2, w3)
    hf = jnp.transpose(h[:, :, 0], (1, 2, 0))
    hb = jnp.transpose(h[:, :, 1], (1, 2, 0))
    kfull = jnp.concatenate([jnp.zeros((HY_ORDER, C, 1), F32), jnp.flip(hb[..., 1:], -1), hf], -1)
    oT = hyena_conv_pallas(kfull, d_skip, zT)
    return jnp.transpose(oT.reshape(C, nb, B, T), (2, 1, 3, 0)).reshape(B, L, C)


def swiglu(x, wg, wu, wd):
    n_blocks = x.shape[0] // MOE_BLOCK
    return grouped_swiglu_pallas(x.astype(BF16), jnp.zeros((n_blocks,), jnp.int32), jnp.int32(n_blocks),
                                 wg[None].astype(BF16), wu[None].astype(BF16), wd[None].astype(BF16))


def moe_swiglu(x, router, wg, wu, wd):
    N, D = x.shape
    logits = x @ router
    top_v, top_i = lax.top_k(logits, TOP_K)
    gates = jax.nn.softmax(top_v, axis=-1)
    A = N * TOP_K
    e_flat = top_i.reshape(-1)
    tok_flat = jnp.arange(A, dtype=jnp.int32) // TOP_K
    order = jnp.argsort(e_flat)
    e_sorted = e_flat[order]
    counts = jnp.bincount(e_flat, length=N_EXPERTS)
    starts = jnp.cumsum(counts) - counts
    padded = (counts + MOE_BLOCK - 1) // MOE_BLOCK * MOE_BLOCK
    pends = jnp.cumsum(padded)
    pstarts = pends - padded
    dest = (pstarts[e_sorted] + jnp.arange(A, dtype=jnp.int32) - starts[e_sorted]).astype(jnp.int32)
    n_blocks = -(-A // MOE_BLOCK) + N_EXPERTS
    n_slots = n_blocks * MOE_BLOCK
    block_expert = jnp.clip(jnp.searchsorted(pends, jnp.arange(n_blocks) * MOE_BLOCK, side='right'), 0, N_EXPERTS - 1)
    slot_e = jnp.repeat(block_expert, MOE_BLOCK)
    slot_pos = jnp.arange(n_slots, dtype=jnp.int32) - pstarts[slot_e].astype(jnp.int32)
    slot_src = jnp.clip(starts[slot_e].astype(jnp.int32) + slot_pos, 0, A - 1)
    slot_tok = jnp.where(slot_pos < counts[slot_e], tok_flat[order[slot_src]], N)
    xp = jnp.concatenate([x.astype(BF16), jnp.zeros((1, D), BF16)], 0)
    ys = grouped_swiglu_pallas(xp[slot_tok], block_expert, pends[-1] // MOE_BLOCK,
                               wg.astype(BF16), wu.astype(BF16), wd.astype(BF16))
    slot_of = dest[jnp.argsort(order)].reshape(N, TOP_K)
    return ys[slot_of[:, 0]] * gates[:, 0:1] + ys[slot_of[:, 1]] * gates[:, 1:2]


def kernel(x, c, ctx, c_ctx, ada_w, ada_b, w_in, mla_q_norm, mla_kv_norm, mla_w_uq, mla_w_ukv, mla_out_norm, lru_conv_w, lru_conv_b, lru_w_r, lru_b_r, lru_w_i, lru_b_i, lru_lambda, lru_out_norm, rwkv_mu_prev, rwkv_mu_next, rwkv_w0, rwkv_w_up, rwkv_a0, rwkv_a_up, rwkv_g_up, rwkv_k_k, rwkv_k_a, rwkv_r_k, rwkv_ln_g, rwkv_ln_b, hy_conv_w, hy_conv_b, hy_f_w1, hy_f_b1, hy_f_w2, hy_f_b2, hy_f_w3, hy_d, hy_out_norm, w_out, ln1_g, ln1_b, ln2_g, ln2_b, ffn_w_gate, ffn_w_up, ffn_w_down, moe_router, moe_w_gate, moe_w_up, moe_w_down):
    B, L, D = x.shape
    Lc = ctx.shape[1]
    rows = L // GRID_W
    mla_tabs = mla_tables(*axial_rope(rows), Lc)
    s_lat = jax.nn.silu(c)
    s_ctx = jax.nn.silu(c_ctx)
    Ml, Mc = B * L, B * Lc
    xl, xc = x.reshape(Ml, D), ctx.reshape(Mc, D)
    prev = None
    for li in range(DEPTH):
        ctx_out = li < DEPTH - 1
        mod_l = (s_lat @ ada_w[li] + ada_b[li]).reshape(B, 6, 1, D)
        mod_c = (s_ctx @ ada_w[li] + ada_b[li]).reshape(6, 1, D)
        ml = [mod_l[:, q] for q in range(6)]
        mc = [mod_c[q][None] for q in range(6)]

        w_in_l = w_in[li]
        w_secs = [w.astype(BF16) for w in (mla_section_weight(w_in_l[:, :B_X]), w_in_l[:, B_X:C_OFF],
                                           w_in_l[:, C_OFF:D_OFF], w_in_l[:, D_OFF:])]
        res_l = res_c = None
        if prev is not None:
            out, gate_l, gate_c, g2, b2 = prev
            res_l, res_c = (out, 0, gate_l, g2, b2), (out, Ml, gate_c, g2, b2)
        xl_new, (Pal, Pbl, Pcl, Pdl) = in_proj_pallas(xl, ml[0], ml[1], w_secs, res_l)
        xc_new, ctx_secs = in_proj_pallas(xc, mc[0], mc[1], w_secs if ctx_out else w_secs[:3], res_c)
        if prev is not None:
            xl, xc = xl_new, xc_new
        Pal, Pbl, Pcl, Pdl = (p.reshape(B, L, -1) for p in (Pal, Pbl, Pcl, Pdl))
        Pac, Pbc, Pcc = (p.reshape(B, Lc, -1) for p in ctx_secs[:3])
        a_l, a_c = mla_mixer(Pac, Pal, mla_tabs, mla_q_norm[li], mla_kv_norm[li], mla_w_uq[li], mla_w_ukv[li],
                             mla_out_norm[li], ctx_out)
        b_l, b_c = rglru_pallas(Pbl, Pbc, lru_conv_w[li], lru_conv_b[li], lru_w_r[li], lru_b_r[li], lru_w_i[li],
                                lru_b_i[li], lru_lambda[li], lru_out_norm[li])
        c_l, c_c = rwkv_mixer(Pcl, Pcc, rwkv_mu_prev[li], rwkv_mu_next[li], rwkv_w0[li], rwkv_w_up[li], rwkv_a0[li],
                              rwkv_a_up[li], rwkv_g_up[li], rwkv_k_k[li], rwkv_k_a[li], rwkv_r_k[li],
                              rwkv_ln_g[li], rwkv_ln_b[li], ctx_out)
        d_l = rms_norm(hyena_sequence(Pdl, hy_conv_w[li], hy_conv_b[li], hy_f_w1[li], hy_f_b1[li], hy_f_w2[li],
                                      hy_f_b2[li], hy_f_w3[li], hy_d[li]), hy_out_norm[li])
        w_out_l = w_out[li].astype(BF16)
        xl, tokens = out_proj_pallas([t.reshape(Ml, GROUP) for t in (a_l, b_l, c_l, d_l)], xl, w_out_l,
                                     ml[2], ln1_g[li], ln1_b[li], ml[3], ml[4])
        if ctx_out:
            Pdc = ctx_secs[3].reshape(B, Lc, -1)
            d_c = rms_norm(hyena_sequence(Pdc, hy_conv_w[li], hy_conv_b[li], hy_f_w1[li], hy_f_b1[li], hy_f_w2[li],
                                          hy_f_b2[li], hy_f_w3[li], hy_d[li]), hy_out_norm[li])
            xc, fc = out_proj_pallas([t.reshape(Mc, GROUP) for t in (a_c, b_c, c_c, d_c)], xc, w_out_l,
                                     mc[2], ln1_g[li], ln1_b[li], mc[3], mc[4])
            tokens = jnp.concatenate([tokens, fc], 0)
        j = li // 2
        if li % 2 == 0:
            out = swiglu(tokens, ffn_w_gate[j], ffn_w_up[j], ffn_w_down[j])
        else:
            out = moe_swiglu(tokens, moe_router[j], moe_w_gate[j], moe_w_up[j], moe_w_down[j])
        prev = (out, ml[5], mc[5], ln2_g[li], ln2_b[li])
    out, gate_l, _, g2, b2 = prev
    xl, _ = in_proj_pallas(xl, ml[0], ml[1], [], (out, 0, gate_l, g2, b2))
    return xl.reshape(B, L, D)
```

```python
import math
from functools import partial

import jax
import jax.numpy as jnp
from jax import lax
from jax.experimental import pallas as pl
from jax.experimental.pallas import tpu as pltpu

F32 = jnp.float32
BF16 = jnp.bfloat16

SUBLANES = 8
LANES = 128
VMEM_LIMIT_BYTES = 48 * 1024 * 1024

D_MODEL = 1024
DEPTH = 4
GRID_W = 64
GROUP = D_MODEL // 4

MLA_HEADS = 4
MLA_NOPE = 64
MLA_ROPE = 32
MLA_V = 64
MLA_Q_RANK = 192
MLA_KV_RANK = 128
ROPE_BASE = 10000.0
Q_BLOCK = 128

LRU_BLOCKS = 4
LRU_CONV = 4
LRU_CONV_LEFT = 2
LRU_C = 8.0

RWKV_HEADS = 4
RWKV_HEAD = GROUP // RWKV_HEADS
RWKV_DECAY_LORA = 32
RWKV_AAA_LORA = 32
RWKV_GATE_LORA = 64
RWKV_GN_EPS = 64e-5

HY_ORDER = 2
HY_SHORT = 3
HY_BANDS = 16
HY_EMB = 1 + 2 * HY_BANDS
HY_HIDDEN = 64
HY_SIN_FREQ = 1.0
HY_DECAY_MIN = math.log(1e-2) / 1.5
HY_DECAY_MAX = math.log(1e-2) / 0.3
HY_SHIFT = 0.05

N_EXPERTS = 8
TOP_K = 2
MOE_BLOCK = 512

ALPHA = (2.0 * DEPTH) ** 0.25

A_CQ = 0
A_CKV = A_CQ + MLA_Q_RANK
A_KR = A_CKV + MLA_KV_RANK
B_X = A_KR + MLA_ROPE
B_GATE = B_X + GROUP
C_OFF = B_GATE + GROUP
C_R = 0
C_K = GROUP
C_V = 2 * GROUP
C_WD = 3 * GROUP
C_AD = C_WD + 2 * RWKV_DECAY_LORA
C_GD = C_AD + 2 * RWKV_AAA_LORA
C_COLS = C_GD + RWKV_GATE_LORA
D_OFF = C_OFF + C_COLS
D_COLS = (HY_ORDER + 1) * GROUP


RWKV_TIME_BLOCK = 16
V_TILES = RWKV_HEAD // SUBLANES
RWKV_KEY_GROUP = 32


def _rwkv_scan_kernel(rkf, rkb, vvf, vvb, wf, wb, kaf, kab, kdf, kdb, yf_ref, yb_ref, s_ref, m_ref):
    @pl.when(pl.program_id(0) == 0)
    def _():
        s_ref[...] = jnp.zeros_like(s_ref)

    n_t = rkf.shape[0]
    p = rkf.shape[2]
    fwd_lanes = lax.broadcasted_iota(jnp.int32, (RWKV_HEAD, p), 1) < p // 2

    for j in range(n_t):
        jb = n_t - 1 - j
        rk_f, rk_b = rkf[j], rkb[jb]
        m_ref[0, j] = jnp.where(fwd_lanes, rk_f, pltpu.roll(rk_b, p // 2, 1))
        m_ref[2, j] = jnp.where(fwd_lanes, pltpu.roll(rk_f, p // 2, 1), rk_b)
        for q, (f_ref, b_ref) in ((1, (wf, wb)), (3, (kaf, kab)), (4, (vvf, vvb)), (5, (kdf, kdb))):
            m_ref[q, j] = jnp.where(fwd_lanes, f_ref[j], b_ref[jb])

    r_ref, w_ref, kk_ref, kka_ref, v_ref, kd_ref = (m_ref.at[q] for q in range(6))

    def step(t, carry):
        def row(ref, k):
            return jnp.broadcast_to(ref[t, pl.ds(k, 1), :], (SUBLANES, p))[None]

        def sa_group(g, sa):
            k0 = pl.multiple_of(g * RWKV_KEY_GROUP, RWKV_KEY_GROUP)
            for j in range(RWKV_KEY_GROUP):
                sa = sa + s_ref[k0 + j] * row(kk_ref, k0 + j)
            return sa

        zero = jnp.zeros((V_TILES, SUBLANES, p), F32)
        sa = lax.fori_loop(0, RWKV_HEAD // RWKV_KEY_GROUP, sa_group, zero)
        vt = v_ref[t].reshape(V_TILES, SUBLANES, p)

        def update_group(g, y):
            k0 = pl.multiple_of(g * RWKV_KEY_GROUP, RWKV_KEY_GROUP)
            for j in range(RWKV_KEY_GROUP):
                k = k0 + j
                sn = s_ref[k] * row(w_ref, k) - sa * row(kka_ref, k) + vt * row(kd_ref, k)
                s_ref[k] = sn
                y = y + sn * row(r_ref, k)
            return y

        y = lax.fori_loop(0, RWKV_HEAD // RWKV_KEY_GROUP, update_group, zero).reshape(RWKV_HEAD, p)
        yf_ref[t] = y
        yb_ref[n_t - 1 - t] = y
        return carry

    lax.fori_loop(0, n_t, step, 0)


def rwkv_scan_pallas(rk, vv, w, kka, kd, n_ctx):
    n_steps, n, p = rk.shape
    tb = RWKV_TIME_BLOCK
    assert n == RWKV_HEAD and p == LANES and n_steps % tb == 0 and n_ctx % tb == 0
    nb, nc = n_steps // tb, n_ctx // tb
    fwd = pl.BlockSpec((tb, n, p), lambda g: (g, 0, 0))
    bwd = pl.BlockSpec((tb, n, p), lambda g: (jnp.where(g < nc, nc - 1 - g, nb + nc - 1 - g), 0, 0))
    out = jax.ShapeDtypeStruct((n_steps, n, p), F32)
    return pl.pallas_call(
        _rwkv_scan_kernel,
        grid=(nb,),
        in_specs=[fwd, bwd] * 5,
        out_specs=[fwd, bwd],
        out_shape=[out, out],
        scratch_shapes=[pltpu.VMEM((n, V_TILES, SUBLANES, LANES), F32), pltpu.VMEM((6, tb, n, LANES), F32)],
        compiler_params=pltpu.CompilerParams(dimension_semantics=("arbitrary",), vmem_limit_bytes=VMEM_LIMIT_BYTES),
        name="rwkv_scan",
    )(rk, rk, vv, vv, w, w, kka, kka, kd, kd)


ATTN_TQ = 256
ATTN_TK = 256


def _attn_kernel(q_ref, k_ref, v_ref, g_ref, o_ref, s_ref):
    tq = q_ref.shape[1]
    n_chunks = k_ref.shape[1]
    n_tiles = ATTN_TK // LANES
    pair_out = []
    for hp in range(MLA_HEADS // 2):
        o_pair = jnp.zeros((tq, LANES), F32)
        for h in (2 * hp, 2 * hp + 1):
            qh = q_ref[0, :, pl.ds(LANES * h, LANES)]

            m_acc = jnp.full((tq, LANES), -jnp.inf, F32)
            for c in range(n_chunks):
                s = jnp.dot(qh, k_ref[0, c, pl.ds(LANES * h, LANES), :], preferred_element_type=F32)
                s_ref[c] = s
                for j in range(n_tiles):
                    m_acc = jnp.maximum(m_acc, s[:, LANES * j:LANES * (j + 1)])
            m_full = jnp.broadcast_to(jnp.max(m_acc, -1, keepdims=True), (tq, LANES))

            l_acc = jnp.zeros((tq, LANES), F32)
            acc = jnp.zeros((tq, LANES), F32)
            for c in range(n_chunks):
                s = s_ref[c]
                ps = []
                for j in range(n_tiles):
                    p = jnp.exp2(s[:, LANES * j:LANES * (j + 1)] - m_full)
                    l_acc = l_acc + p
                    ps.append(p.astype(BF16))
                vh = v_ref[0, pl.ds(c * ATTN_TK, ATTN_TK), pl.ds(LANES * h, LANES)]
                acc = acc + jnp.dot(jnp.concatenate(ps, -1), vh, preferred_element_type=F32)
            o_pair = o_pair + acc / jnp.sum(l_acc, -1, keepdims=True)
        pair_out.append(o_pair)
    o = jnp.concatenate(pair_out, -1)
    o_ref[0] = o * lax.rsqrt(jnp.mean(jnp.square(o), -1, keepdims=True) + 1e-6) * g_ref[...]


def mla_attention_pallas(q, kt, v, out_norm, q_start, n_q, n_kv):
    B, _, W = q.shape
    return pl.pallas_call(
        _attn_kernel,
        grid=(B, n_q),
        in_specs=[
            pl.BlockSpec((1, ATTN_TQ, W), lambda b, i: (b, i + q_start, 0)),
            pl.BlockSpec((1, n_kv, W, ATTN_TK), lambda b, i: (b, 0, 0, 0)),
            pl.BlockSpec((1, n_kv * ATTN_TK, W), lambda b, i: (b, 0, 0)),
            pl.BlockSpec((1, GROUP), lambda b, i: (0, 0)),
        ],
        out_specs=pl.BlockSpec((1, ATTN_TQ, GROUP), lambda b, i: (b, i, 0)),
        out_shape=jax.ShapeDtypeStruct((B, n_q * ATTN_TQ, GROUP), F32),
        scratch_shapes=[pltpu.VMEM((n_kv, ATTN_TQ, ATTN_TK), F32)],
        compiler_params=pltpu.CompilerParams(dimension_semantics=("arbitrary", "arbitrary"),
                                             vmem_limit_bytes=VMEM_LIMIT_BYTES),
        name="mla_attention",
    )(q, kt, v, out_norm[None])


MLA_PA_COLS = 4 * LANES
MLA_W = MLA_HEADS * LANES


def _rot_cols(w):
    h = w.shape[1] // 2
    return jnp.concatenate([-w[:, h:], w[:, :h]], 1)


def mla_section_weight(w_a):
    z = jnp.zeros((w_a.shape[0], LANES // 2), w_a.dtype)
    kr = w_a[:, A_KR:A_KR + MLA_ROPE]
    return jnp.concatenate([w_a[:, A_CKV:A_CKV + MLA_KV_RANK], kr, _rot_cols(kr), z,
                            w_a[:, A_CQ:A_CQ + MLA_Q_RANK], z], 1)


def mla_prep_weights(w_uq, w_ukv):
    H, DN, DR, DV = MLA_HEADS, MLA_NOPE, MLA_ROPE, MLA_V
    wq = jnp.zeros((2 * LANES, MLA_W), F32)
    wq_rot = jnp.zeros((2 * LANES, MLA_W), F32)
    wk = jnp.zeros((2 * LANES, MLA_W), F32)
    wv = jnp.zeros((LANES, MLA_W), F32)
    place = jnp.eye(DR, dtype=F32)
    for h in range(H):
        q_h = w_uq[:, h * (DN + DR):(h + 1) * (DN + DR)]
        wq = wq.at[:MLA_Q_RANK, LANES * h:LANES * h + DN + DR].set(q_h)
        wq_rot = wq_rot.at[:MLA_Q_RANK, LANES * h + DN:LANES * h + DN + DR].set(_rot_cols(q_h[:, DN:]))
        kv_h = w_ukv[:, h * (DN + DV):(h + 1) * (DN + DV)]
        wk = wk.at[:MLA_KV_RANK, LANES * h:LANES * h + DN].set(kv_h[:, :DN])
        wk = wk.at[LANES:LANES + DR, LANES * h + DN:LANES * h + DN + DR].set(place)
        v0 = LANES * h + DV * (h % 2)
        wv = wv.at[:, v0:v0 + DV].set(kv_h[:, DN:])
    return wq.astype(BF16), wq_rot.astype(BF16), wk.astype(BF16), wv.astype(BF16)


def mla_tables(cos, sin, n_ctx):
    H, DN, DR = MLA_HEADS, MLA_NOPE, MLA_ROPE
    L = cos.shape[0]
    cf = jnp.concatenate([jnp.ones((n_ctx, DR), F32), jnp.concatenate([cos, cos], -1)], 0)
    sf = jnp.concatenate([jnp.zeros((n_ctx, DR), F32), jnp.concatenate([sin, sin], -1)], 0)
    S = n_ctx + L
    tab_k = jnp.concatenate([cf, sf, jnp.zeros((S, LANES - 2 * DR), F32)], -1)
    scale = (DN + DR) ** -0.5 * math.log2(math.e)
    zpad = jnp.zeros((S, LANES - DN - DR), F32)
    qc_h = jnp.concatenate([jnp.full((S, DN), scale, F32), cf * scale, zpad], -1)
    qs_h = jnp.concatenate([jnp.zeros((S, DN), F32), sf * scale, zpad], -1)
    return tab_k, jnp.tile(qc_h, (1, H)), jnp.tile(qs_h, (1, H))


def _mla_prep_kernel(pa_ref, tk_ref, tqc_ref, tqs_ref, kvn_ref, qn_ref, wq_ref, wqr_ref, wk_ref, wv_ref,
                     q_ref, kt_ref, v_ref):
    pa = pa_ref[0]
    ckv = pa[:, :LANES]
    ckv = ckv * lax.rsqrt(jnp.mean(jnp.square(ckv), -1, keepdims=True) + 1e-6) * kvn_ref[...]
    t = pa[:, LANES:2 * LANES] * tk_ref[...]
    kr = t + pltpu.roll(t, LANES - MLA_ROPE, 1)
    kr = jnp.where(lax.broadcasted_iota(jnp.int32, kr.shape, 1) < MLA_ROPE, kr, 0.0)
    k = jnp.dot(jnp.concatenate([ckv, kr], -1).astype(BF16), wk_ref[...], preferred_element_type=F32)
    kt_ref[0, 0] = k.T.astype(BF16)
    v_ref[0] = jnp.dot(ckv.astype(BF16), wv_ref[...], preferred_element_type=F32).astype(BF16)
    cq = pa[:, 2 * LANES:]
    ms = jnp.sum(jnp.square(cq), -1, keepdims=True) * (1.0 / MLA_Q_RANK)
    cq = (cq * lax.rsqrt(ms + 1e-6) * qn_ref[...]).astype(BF16)
    qa = jnp.dot(cq, wq_ref[...], preferred_element_type=F32)
    qb = jnp.dot(cq, wqr_ref[...], preferred_element_type=F32)
    q_ref[0] = (qa * tqc_ref[...] + qb * tqs_ref[...]).astype(BF16)


def mla_prep_pallas(pa, tables, q_norm, kv_norm, weights):
    B, S, _ = pa.shape
    T = ATTN_TK
    assert S % T == 0
    tab_k, tab_qc, tab_qs = tables
    wq, wq_rot, wk, wv = weights
    qn = jnp.concatenate([q_norm, jnp.zeros((2 * LANES - MLA_Q_RANK,), F32)])[None]

    def full(a):
        return pl.BlockSpec(a.shape, lambda b, i: (0,) * a.ndim)

    def rows(width):
        return pl.BlockSpec((T, width), lambda b, i: (i, 0))

    return pl.pallas_call(
        _mla_prep_kernel,
        grid=(B, S // T),
        in_specs=[pl.BlockSpec((1, T, MLA_PA_COLS), lambda b, i: (b, i, 0)),
                  rows(LANES), rows(MLA_W), rows(MLA_W),
                  pl.BlockSpec((1, LANES), lambda b, i: (0, 0)), full(qn), full(wq), full(wq_rot), full(wk), full(wv)],
        out_specs=[pl.BlockSpec((1, T, MLA_W), lambda b, i: (b, i, 0)),
                   pl.BlockSpec((1, 1, MLA_W, T), lambda b, i: (b, i, 0, 0)),
                   pl.BlockSpec((1, T, MLA_W), lambda b, i: (b, i, 0))],
        out_shape=[jax.ShapeDtypeStruct((B, S, MLA_W), BF16),
                   jax.ShapeDtypeStruct((B, S // T, MLA_W, T), BF16),
                   jax.ShapeDtypeStruct((B, S, MLA_W), BF16)],
        compiler_params=pltpu.CompilerParams(dimension_semantics=("arbitrary", "arbitrary")),
        name="mla_prep",
    )(pa, tab_k, tab_qc, tab_qs, kv_norm[None], qn, wq, wq_rot, wk, wv)


LRU_CHUNK = 256
LRU_HALO = SUBLANES


def _lru_kernel(xl_ref, xc_ref, wbd_ref, bias_ref, c8_ref, cw_ref, cb_ref, gn_ref,
                yl_ref, yc_ref, xs_l, xs_c, hf_l, hf_c, a_s, b_s, hb_s):
    C = GROUP
    CH = LRU_CHUNK
    L = xl_ref.shape[1]
    Lc = xc_ref.shape[1]

    def stage(x_ref, xs, n):
        xs[pl.ds(0, LRU_HALO), :] = jnp.zeros((LRU_HALO, C), F32)
        xs[pl.ds(LRU_HALO + n, LRU_HALO), :] = jnp.zeros((LRU_HALO, C), F32)

        def cp(i, c):
            r0 = pl.multiple_of(i * CH, CH)
            xs[pl.ds(LRU_HALO + r0, CH), :] = x_ref[0, pl.ds(r0, CH), pl.ds(0, C)]
            return c

        lax.fori_loop(0, n // CH, cp, 0)

    stage(xl_ref, xs_l, L)
    stage(xc_ref, xs_c, Lc)

    def coeffs(xs, base, d):
        xv = xs[pl.ds(base, CH + 2 * LRU_HALO), :]
        u = cb_ref[...]
        for j in range(LRU_CONV):
            o = LRU_HALO - LRU_CONV_LEFT + j
            u = u + xv[o:o + CH] * cw_ref[pl.ds(j, 1), :]
        z = jnp.dot(u.astype(BF16), wbd_ref[:, pl.ds(d * 2 * C, 2 * C)], preferred_element_type=F32)
        z = z + bias_ref[:, pl.ds(d * 2 * C, 2 * C)]
        r = jax.nn.sigmoid(z[:, :C])
        i = jax.nn.sigmoid(z[:, C:])
        log_a = r * c8_ref[pl.ds(d, 1), :]
        a = jnp.exp(log_a)
        a_s[...] = a
        b_s[...] = jnp.sqrt(-jnp.tanh(log_a) * (a * a + 1.0)) * (i * u)

    def row_scan(h, out_ref, out_base, reverse):
        n_groups = CH // SUBLANES

        def group(g, h):
            r0 = pl.multiple_of((n_groups - 1 - g if reverse else g) * SUBLANES, SUBLANES)
            for j in (reversed(range(SUBLANES)) if reverse else range(SUBLANES)):
                a_t = jnp.broadcast_to(a_s[pl.ds(r0 + j, 1), :], (SUBLANES, C))
                b_t = jnp.broadcast_to(b_s[pl.ds(r0 + j, 1), :], (SUBLANES, C))
                h = a_t * h + b_t
                out_ref[pl.ds(out_base + r0 + j, 1), :] = h[0:1, :]
            return h

        return lax.fori_loop(0, n_groups, group, h)

    h0 = jnp.zeros((SUBLANES, C), F32)

    h = h0
    for ci in range(Lc // CH):
        coeffs(xs_c, ci * CH, 0)
        h = row_scan(h, hf_c, ci * CH, False)

    def fwd_chunk(ci, h):
        base = pl.multiple_of(ci * CH, CH)
        coeffs(xs_l, base, 0)
        return row_scan(h, hf_l, base, False)

    lax.fori_loop(0, L // CH, fwd_chunk, h)

    def combine(x_ref, hf, base, y_ref):
        hl = hf[pl.ds(base, CH), :] + hb_s[...]
        g = jax.nn.gelu(x_ref[0, pl.ds(base, CH), pl.ds(C, C)])
        v = hl * g
        y = v * lax.rsqrt(jnp.mean(jnp.square(v), -1, keepdims=True) + 1e-6) * gn_ref[...]
        y_ref[0, pl.ds(base, CH), :] = y

    h = h0
    for ci in reversed(range(Lc // CH)):
        coeffs(xs_c, ci * CH, 1)
        h = row_scan(h, hb_s, 0, True)
        combine(xc_ref, hf_c, ci * CH, yc_ref)

    def bwd_chunk(k, h):
        base = pl.multiple_of((L // CH - 1 - k) * CH, CH)
        coeffs(xs_l, base, 1)
        h = row_scan(h, hb_s, 0, True)
        combine(xl_ref, hf_l, base, yl_ref)
        return h

    lax.fori_loop(0, L // CH, bwd_chunk, h)


def rglru_pallas(xg_l, xg_c, conv_w, conv_b, w_r, b_r, w_i, b_i, lam, out_norm):
    B, L, _ = xg_l.shape
    Lc = xg_c.shape[1]
    C = GROUP
    assert L % LRU_CHUNK == 0 and Lc % LRU_CHUNK == 0

    def bd(w):
        return jax.scipy.linalg.block_diag(*[w[n] for n in range(LRU_BLOCKS)])

    wbd = jnp.concatenate([bd(w_r[0]), bd(w_i[0]), bd(w_r[1]), bd(w_i[1])], 1).astype(BF16)
    bias = jnp.concatenate([b_r[0], b_i[0], b_r[1], b_i[1]])[None]
    c8 = -LRU_C * jax.nn.softplus(-lam)

    def full(shape):
        return pl.BlockSpec(shape, lambda b: (0,) * len(shape))

    return pl.pallas_call(
        _lru_kernel,
        grid=(B,),
        in_specs=[
            pl.BlockSpec((1, L, 2 * C), lambda b: (b, 0, 0)),
            pl.BlockSpec((1, Lc, 2 * C), lambda b: (b, 0, 0)),
            full((C, 4 * C)), full((1, 4 * C)), full((2, C)), full((LRU_CONV, C)), full((1, C)), full((1, C)),
        ],
        out_specs=[
            pl.BlockSpec((1, L, C), lambda b: (b, 0, 0)),
            pl.BlockSpec((1, Lc, C), lambda b: (b, 0, 0)),
        ],
        out_shape=[jax.ShapeDtypeStruct((B, L, C), F32), jax.ShapeDtypeStruct((B, Lc, C), F32)],
        scratch_shapes=[
            pltpu.VMEM((L + 2 * LRU_HALO, C), F32),
            pltpu.VMEM((Lc + 2 * LRU_HALO, C), F32),
            pltpu.VMEM((L, C), F32),
            pltpu.VMEM((Lc, C), F32),
            pltpu.VMEM((LRU_CHUNK, C), F32),
            pltpu.VMEM((LRU_CHUNK, C), F32),
            pltpu.VMEM((LRU_CHUNK, C), F32),
        ],
        compiler_params=pltpu.CompilerParams(dimension_semantics=("arbitrary",), vmem_limit_bytes=VMEM_LIMIT_BYTES),
        name="rglru",
    )(xg_l, xg_c, wbd, bias, c8, conv_w, conv_b[None], out_norm[None])


HY_T = 256
HY_CB = 8


def _hyena_kernel(k_ref, d_ref, v_ref, x1_ref, x2_ref, o_ref, u_s, acc_s):
    n_rows = v_ref.shape[1]
    T = HY_T
    nb = k_ref.shape[2] // (2 * T)
    bsz = n_rows // nb

    def conv(ci, order):
        acc_s[...] = jnp.zeros_like(acc_s)
        for dd in range(-(nb - 1), nb):
            w2 = k_ref[order, pl.ds(ci, 1), pl.ds(T * (dd + nb - 1), 2 * T)]
            x = jnp.broadcast_to(w2, (T, 2 * T))
            r = pltpu.roll(x, 0, 1, stride=1, stride_axis=0)
            tb = r[:, T:].astype(BF16)
            j0, j1 = max(0, -dd), min(nb, nb - dd)
            lhs = u_s[pl.ds(bsz * j0, bsz * (j1 - j0)), :]
            dst = pl.ds(bsz * (j0 + dd), bsz * (j1 - j0))
            acc_s[dst, :] = acc_s[dst, :] + jnp.dot(lhs, tb, preferred_element_type=F32)

    def channel(ci, carry):
        v = v_ref[ci]
        u_s[...] = v.astype(BF16)
        conv(ci, 0)
        u = x1_ref[ci] * (acc_s[...] + v * d_ref[pl.ds(ci, 1), pl.ds(0, 1)])
        u_s[...] = u.astype(BF16)
        conv(ci, 1)
        o_ref[ci] = x2_ref[ci] * (acc_s[...] + u * d_ref[pl.ds(ci, 1), pl.ds(1, 1)])
        return carry

    lax.fori_loop(0, HY_CB, channel, 0)


def hyena_conv_pallas(kfull, d_skip, zT):
    C3, R, T = zT.shape
    C = C3 // (HY_ORDER + 1)
    two_l = kfull.shape[2]
    assert T == HY_T and C % HY_CB == 0
    n_cb = C // HY_CB
    blk = pl.BlockSpec((HY_CB, R, T), lambda c: (c, 0, 0))
    return pl.pallas_call(
        _hyena_kernel,
        grid=(n_cb,),
        in_specs=[
            pl.BlockSpec((HY_ORDER, HY_CB, two_l), lambda c: (0, c, 0)),
            pl.BlockSpec((HY_CB, HY_ORDER), lambda c: (c, 0)),
            blk,
            pl.BlockSpec((HY_CB, R, T), lambda c: (c + n_cb, 0, 0)),
            pl.BlockSpec((HY_CB, R, T), lambda c: (c + 2 * n_cb, 0, 0)),
        ],
        out_specs=blk,
        out_shape=jax.ShapeDtypeStruct((C, R, T), F32),
        scratch_shapes=[pltpu.VMEM((R, T), BF16), pltpu.VMEM((R, T), F32)],
        compiler_params=pltpu.CompilerParams(dimension_semantics=("arbitrary",), vmem_limit_bytes=VMEM_LIMIT_BYTES),
        name="hyena_conv",
    )(kfull, d_skip.T, zT, zT, zT)


FFN_TF_MAX = 1408


def _ffn_tile(hidden):
    return max(t for t in range(LANES, FFN_TF_MAX + 1, LANES) if hidden % t == 0)


def _swiglu_kernel(be_ref, nu_ref, x_ref, wg_ref, wu_ref, wd_ref, o_ref, acc_ref):
    i = pl.program_id(0)
    f = pl.program_id(1)

    @pl.when(i < nu_ref[0])
    def _():
        x = x_ref[...]
        g = jnp.dot(x, wg_ref[0], preferred_element_type=F32)
        u = jnp.dot(x, wu_ref[0], preferred_element_type=F32)
        h = (jax.nn.silu(g) * u).astype(BF16)
        part = jnp.dot(h, wd_ref[0], preferred_element_type=F32)

        @pl.when(f == 0)
        def _():
            acc_ref[...] = part

        @pl.when(f > 0)
        def _():
            acc_ref[...] = acc_ref[...] + part

    @pl.when(f == pl.num_programs(1) - 1)
    def _():
        o_ref[...] = jnp.where(i < nu_ref[0], acc_ref[...], 0.0)


def grouped_swiglu_pallas(xs, block_expert, n_used, wg, wu, wd):
    n_rows, D = xs.shape
    F = wg.shape[2]
    TM = MOE_BLOCK
    assert n_rows % TM == 0
    tf = _ffn_tile(F)
    n_blocks = n_rows // TM
    grid_spec = pltpu.PrefetchScalarGridSpec(
        num_scalar_prefetch=2,
        grid=(n_blocks, F // tf),
        in_specs=[
            pl.BlockSpec((TM, D), lambda i, f, be, nu: (i, 0)),
            pl.BlockSpec((1, D, tf), lambda i, f, be, nu: (be[i], 0, f)),
            pl.BlockSpec((1, D, tf), lambda i, f, be, nu: (be[i], 0, f)),
            pl.BlockSpec((1, tf, D), lambda i, f, be, nu: (be[i], f, 0)),
        ],
        out_specs=pl.BlockSpec((TM, D), lambda i, f, be, nu: (i, 0)),
        scratch_shapes=[pltpu.VMEM((TM, D), F32)],
    )
    return pl.pallas_call(
        _swiglu_kernel,
        grid_spec=grid_spec,
        out_shape=jax.ShapeDtypeStruct((n_rows, D), F32),
        compiler_params=pltpu.CompilerParams(dimension_semantics=("arbitrary", "arbitrary"),
                                             vmem_limit_bytes=VMEM_LIMIT_BYTES),
        name="grouped_swiglu",
    )(block_expert.astype(jnp.int32), jnp.reshape(n_used, (1,)).astype(jnp.int32), xs, wg, wu, wd)


PROJ_TM = 512


def _residual_ln(x, branch, gate_ref, g_ref, b_ref):
    s = ALPHA * x + gate_ref[0] * branch
    mu = jnp.mean(s, -1, keepdims=True)
    d = s - mu
    var = jnp.mean(jnp.square(d), -1, keepdims=True)
    return d * lax.rsqrt(var + 1e-5) * g_ref[...] + b_ref[...]


def _in_proj_kernel(*refs, has_ln, n_w):
    x_ref = refs[0]
    pos = 1
    x = x_ref[...]
    if has_ln:
        branch_ref, gate_ref, g_ref, b_ref = refs[1:5]
        pos = 5
        x = _residual_ln(x, branch_ref[...], gate_ref, g_ref, b_ref)
    shift_ref, scale_ref = refs[pos:pos + 2]
    w_refs = refs[pos + 2:pos + 2 + n_w]
    out_refs = refs[pos + 2 + n_w:]
    if has_ln:
        out_refs[0][...] = x
        out_refs = out_refs[1:]
    h = (x * (1.0 + scale_ref[0]) + shift_ref[0]).astype(BF16)
    for w_ref, o_ref in zip(w_refs, out_refs):
        o_ref[...] = jnp.dot(h, w_ref[...], preferred_element_type=F32)


def in_proj_pallas(x, shift, scale, weights, residual=None):
    M, D = x.shape
    tm = PROJ_TM
    G = shift.shape[0]
    tiles_per_group = M // G // tm
    assert M % (G * tm) == 0
    row = pl.BlockSpec((tm, D), lambda i: (i, 0))
    grp = pl.BlockSpec((1, 1, D), lambda i: (i // tiles_per_group, 0, 0))
    vec = pl.BlockSpec((1, D), lambda i: (0, 0))
    args, specs = [x], [row]
    has_ln = residual is not None
    if has_ln:
        branch, first_row, gate, ln_g, ln_b = residual
        assert first_row % tm == 0
        off = first_row // tm
        args += [branch, gate, ln_g[None], ln_b[None]]
        specs += [pl.BlockSpec((tm, D), lambda i: (i + off, 0)), grp, vec, vec]
    args += [shift, scale] + list(weights)
    specs += [grp, grp] + [pl.BlockSpec(w.shape, lambda i: (0, 0)) for w in weights]
    out_shape = [jax.ShapeDtypeStruct((M, w.shape[1]), F32) for w in weights]
    out_specs = [pl.BlockSpec((tm, w.shape[1]), lambda i: (i, 0)) for w in weights]
    if has_ln:
        out_shape = [jax.ShapeDtypeStruct((M, D), F32)] + out_shape
        out_specs = [row] + out_specs
    outs = pl.pallas_call(
        partial(_in_proj_kernel, has_ln=has_ln, n_w=len(weights)),
        grid=(M // tm,),
        in_specs=specs,
        out_specs=out_specs,
        out_shape=out_shape,
        compiler_params=pltpu.CompilerParams(dimension_semantics=("arbitrary",), vmem_limit_bytes=VMEM_LIMIT_BYTES),
        name="in_proj",
    )(*args)
    return (outs[0], outs[1:]) if has_ln else (None, outs)


def _out_proj_kernel(a_ref, b_ref, c_ref, d_ref, x_ref, w_ref, gate_ref, g_ref, bb_ref, shift_ref, scale_ref,
                     xn_ref, f_ref):
    y = jnp.concatenate([a_ref[...], b_ref[...], c_ref[...], d_ref[...]], -1).astype(BF16)
    y = jnp.dot(y, w_ref[...], preferred_element_type=F32)
    xn = _residual_ln(x_ref[...], y, gate_ref, g_ref, bb_ref)
    xn_ref[...] = xn
    f_ref[...] = (xn * (1.0 + scale_ref[0]) + shift_ref[0]).astype(BF16)


def out_proj_pallas(parts, x, w_out, gate, ln_g, ln_b, shift, scale):
    M, D = x.shape
    tm = PROJ_TM
    G = gate.shape[0]
    tiles_per_group = M // G // tm
    assert M % (G * tm) == 0
    row = pl.BlockSpec((tm, D), lambda i: (i, 0))
    part = pl.BlockSpec((tm, GROUP), lambda i: (i, 0))
    grp = pl.BlockSpec((1, 1, D), lambda i: (i // tiles_per_group, 0, 0))
    vec = pl.BlockSpec((1, D), lambda i: (0, 0))
    return pl.pallas_call(
        _out_proj_kernel,
        grid=(M // tm,),
        in_specs=[part] * 4 + [row, pl.BlockSpec(w_out.shape, lambda i: (0, 0)), grp, vec, vec, grp, grp],
        out_specs=[row, row],
        out_shape=[jax.ShapeDtypeStruct((M, D), F32), jax.ShapeDtypeStruct((M, D), BF16)],
        compiler_params=pltpu.CompilerParams(dimension_semantics=("arbitrary",), vmem_limit_bytes=VMEM_LIMIT_BYTES),
        name="out_proj",
    )(*parts, x, w_out, gate, ln_g[None], ln_b[None], shift, scale)


def modulate(x, shift, scale):
    return x * (1.0 + scale) + shift


def layer_norm(x, g, b, eps=1e-5):
    mu = jnp.mean(x, -1, keepdims=True)
    var = jnp.mean(jnp.square(x - mu), -1, keepdims=True)
    return (x - mu) * lax.rsqrt(var + eps) * g + b


def rms_norm(x, g, eps=1e-6):
    return x * lax.rsqrt(jnp.mean(jnp.square(x), -1, keepdims=True) + eps) * g


def dwconv(x, w, b, left):
    K = w.shape[0]
    L = x.shape[1]
    xp = jnp.pad(x, ((0, 0), (left, K - 1 - left), (0, 0)))
    return sum(xp[:, j:j + L] * w[j] for j in range(K)) + b


def token_shift(z, mu_prev, mu_next):
    zp = jnp.pad(z, ((0, 0), (1, 1), (0, 0)))
    return z + mu_prev * (zp[:, :-2] - z) + mu_next * (zp[:, 2:] - z)


def axial_rope(rows):
    r, col = jnp.meshgrid(jnp.arange(rows, dtype=F32), jnp.arange(GRID_W, dtype=F32), indexing='ij')
    half = MLA_ROPE // 2
    inv = 1.0 / (ROPE_BASE ** (jnp.arange(0, half, 2, dtype=F32) / half))
    ang = jnp.concatenate([r.reshape(-1, 1) * inv, col.reshape(-1, 1) * inv], -1)
    return jnp.cos(ang), jnp.sin(ang)


def mla_mixer(Pac, Pal, tables, q_norm, kv_norm, w_uq, w_ukv, out_norm, ctx_out):
    Lc, L = Pac.shape[1], Pal.shape[1]
    q, kt, v = mla_prep_pallas(jnp.concatenate([Pac, Pal], 1), tables, q_norm, kv_norm, mla_prep_weights(w_uq, w_ukv))
    nc, nl = Lc // ATTN_TQ, L // ATTN_TQ
    yl = mla_attention_pallas(q, kt, v, out_norm, nc, nl, nc + nl)
    yc = mla_attention_pallas(q, kt, v, out_norm, 0, nc, nc) if ctx_out else None
    return yl, yc


def rwkv_finish(y, bonus, v, gd, g_up, ln_g, ln_b):
    B, L = y.shape[:2]
    mu = jnp.mean(y, -1, keepdims=True)
    var = jnp.mean(jnp.square(y - mu), -1, keepdims=True)
    yn = (y - mu) * lax.rsqrt(var + RWKV_GN_EPS) * ln_g.reshape(RWKV_HEADS, RWKV_HEAD) + ln_b.reshape(RWKV_HEADS, RWKV_HEAD)
    g = jax.nn.sigmoid(gd) @ g_up
    return (yn + bonus * v).reshape(B, L, GROUP) * g


def rwkv_mixer(Pl, Pc, mu_prev, mu_next, w0, w_up, a0, a_up, g_up, k_k, k_a, r_k, ln_g, ln_b, ctx_out):
    B, L = Pl.shape[:2]
    Lc = Pc.shape[1]
    S = Lc + L
    H, N = RWKV_HEADS, RWKV_HEAD
    z = jnp.concatenate([token_shift(Pc, mu_prev, mu_next), token_shift(Pl, mu_prev, mu_next)], 1)

    def heads(t):
        return t.reshape(B, S, H, N)

    r = heads(z[..., C_R:C_R + GROUP])
    k = heads(z[..., C_K:C_K + GROUP])
    v = heads(z[..., C_V:C_V + GROUP])
    kk = k * k_k.reshape(H, N)
    kk = kk * lax.rsqrt(jnp.maximum(jnp.sum(jnp.square(kk), -1, keepdims=True), 1e-24))
    w, kka, kd = [], [], []
    for d in range(2):
        wd = z[..., C_WD + d * RWKV_DECAY_LORA:C_WD + (d + 1) * RWKV_DECAY_LORA]
        ad = z[..., C_AD + d * RWKV_AAA_LORA:C_AD + (d + 1) * RWKV_AAA_LORA]
        log_w = -jnp.exp(-jax.nn.softplus(-(w0[d] + jnp.tanh(wd) @ w_up[d])) - 0.5)
        a = heads(jax.nn.sigmoid(a0[d] + ad @ a_up[d]))
        w.append(heads(jnp.exp(log_w)))
        kka.append(kk * a)
        kd.append(k * (1.0 + (a - 1.0) * k_a.reshape(H, N)))

    def to_scan(lo, hi):
        return jnp.concatenate([jnp.transpose(t, (1, 3, 0, 2)).reshape(S, N, B * H) for t in (lo, hi)], -1)

    yf, yb = rwkv_scan_pallas(to_scan(r, kk), to_scan(v, v), to_scan(*w), to_scan(*kka), to_scan(*kd), Lc)
    y = yf[..., :B * H] + yb[..., B * H:]
    y = jnp.transpose(y.reshape(S, N, B, H), (2, 0, 3, 1))
    bonus = sum(jnp.sum(r * kd_d * r_k, -1, keepdims=True) for kd_d in kd)
    gd = z[..., C_GD:C_GD + RWKV_GATE_LORA]
    out_l = rwkv_finish(y[:, Lc:], bonus[:, Lc:], v[:, Lc:], gd[:, Lc:], g_up, ln_g, ln_b)
    out_c = None
    if ctx_out:
        out_c = rwkv_finish(y[:, :Lc], bonus[:, :Lc], v[:, :Lc], gd[:, :Lc], g_up, ln_g, ln_b)
    return out_l, out_c


def hyena_filters(L, w1, b1, w2, b2, w3):
    t01 = jnp.linspace(0.0, 1.0, L, dtype=F32)[:, None]
    bands = jnp.linspace(1e-4, HY_BANDS - 1, HY_BANDS, dtype=F32)[None, :]
    wpos = (2.0 * math.pi / L) * jnp.arange(L, dtype=F32)[:, None]
    z = jnp.concatenate([t01, jnp.cos(bands * wpos), -jnp.sin(bands * wpos)], -1)
    h = jnp.sin(HY_SIN_FREQ * (z @ w1 + b1))
    h = jnp.sin(HY_SIN_FREQ * (h @ w2 + b2))
    h = (h @ w3).reshape(L, HY_ORDER, 2, GROUP)
    deltas = jnp.abs(jnp.linspace(HY_DECAY_MIN, HY_DECAY_MAX, GROUP, dtype=F32))
    window = jnp.exp(-t01 * deltas) + HY_SHIFT
    return h * window[:, None, None, :]


def hyena_sequence(Pd, conv_w, conv_b, w1, b1, w2, b2, w3, d_skip):
    B, L, _ = Pd.shape
    C = GROUP
    T = HY_T
    nb = L // T
    z = dwconv(Pd, conv_w, conv_b, 1)
    zT = jnp.transpose(z.reshape(B, nb, T, 3 * C), (3, 1, 0, 2)).reshape(3 * C, nb * B, T)
    h = hyena_filters(L, w1, b1, w2, b2, w3)
    hf = jnp.transpose(h[:, :, 0], (1, 2, 0))
    hb = jnp.transpose(h[:, :, 1], (1, 2, 0))
    kfull = jnp.concatenate([jnp.zeros((HY_ORDER, C, 1), F32), jnp.flip(hb[..., 1:], -1), hf], -1)
    oT = hyena_conv_pallas(kfull, d_skip, zT)
    return jnp.transpose(oT.reshape(C, nb, B, T), (2, 1, 3, 0)).reshape(B, L, C)


def swiglu(x, wg, wu, wd):
    n_blocks = x.shape[0] // MOE_BLOCK
    return grouped_swiglu_pallas(x.astype(BF16), jnp.zeros((n_blocks,), jnp.int32), jnp.int32(n_blocks),
                                 wg[None].astype(BF16), wu[None].astype(BF16), wd[None].astype(BF16))


def moe_swiglu(x, router, wg, wu, wd):
    N, D = x.shape
    logits = x @ router
    top_v, top_i = lax.top_k(logits, TOP_K)
    gates = jax.nn.softmax(top_v, axis=-1)
    A = N * TOP_K
    e_flat = top_i.reshape(-1)
    tok_flat = jnp.arange(A, dtype=jnp.int32) // TOP_K
    order = jnp.argsort(e_flat)
    e_sorted = e_flat[order]
    counts = jnp.bincount(e_flat, length=N_EXPERTS)
    starts = jnp.cumsum(counts) - counts
    padded = (counts + MOE_BLOCK - 1) // MOE_BLOCK * MOE_BLOCK
    pends = jnp.cumsum(padded)
    pstarts = pends - padded
    dest = (pstarts[e_sorted] + jnp.arange(A, dtype=jnp.int32) - starts[e_sorted]).astype(jnp.int32)
    n_blocks = -(-A // MOE_BLOCK) + N_EXPERTS
    n_slots = n_blocks * MOE_BLOCK
    block_expert = jnp.clip(jnp.searchsorted(pends, jnp.arange(n_blocks) * MOE_BLOCK, side='right'), 0, N_EXPERTS - 1)
    slot_e = jnp.repeat(block_expert, MOE_BLOCK)
    slot_pos = jnp.arange(n_slots, dtype=jnp.int32) - pstarts[slot_e].astype(jnp.int32)
    slot_src = jnp.clip(starts[slot_e].astype(jnp.int32) + slot_pos, 0, A - 1)
    slot_tok = jnp.where(slot_pos < counts[slot_e], tok_flat[order[slot_src]], N)
    xp = jnp.concatenate([x.astype(BF16), jnp.zeros((1, D), BF16)], 0)
    ys = grouped_swiglu_pallas(xp[slot_tok], block_expert, pends[-1] // MOE_BLOCK,
                               wg.astype(BF16), wu.astype(BF16), wd.astype(BF16))
    slot_of = dest[jnp.argsort(order)].reshape(N, TOP_K)
    return ys[slot_of[:, 0]] * gates[:, 0:1] + ys[slot_of[:, 1]] * gates[:, 1:2]


def kernel(x, c, ctx, c_ctx, ada_w, ada_b, w_in, mla_q_norm, mla_kv_norm, mla_w_uq, mla_w_ukv, mla_out_norm, lru_conv_w, lru_conv_b, lru_w_r, lru_b_r, lru_w_i, lru_b_i, lru_lambda, lru_out_norm, rwkv_mu_prev, rwkv_mu_next, rwkv_w0, rwkv_w_up, rwkv_a0, rwkv_a_up, rwkv_g_up, rwkv_k_k, rwkv_k_a, rwkv_r_k, rwkv_ln_g, rwkv_ln_b, hy_conv_w, hy_conv_b, hy_f_w1, hy_f_b1, hy_f_w2, hy_f_b2, hy_f_w3, hy_d, hy_out_norm, w_out, ln1_g, ln1_b, ln2_g, ln2_b, ffn_w_gate, ffn_w_up, ffn_w_down, moe_router, moe_w_gate, moe_w_up, moe_w_down):
    B, L, D = x.shape
    Lc = ctx.shape[1]
    rows = L // GRID_W
    mla_tabs = mla_tables(*axial_rope(rows), Lc)
    s_lat = jax.nn.silu(c)
    s_ctx = jax.nn.silu(c_ctx)
    Ml, Mc = B * L, B * Lc
    xl, xc = x.reshape(Ml, D), ctx.reshape(Mc, D)
    prev = None
    for li in range(DEPTH):
        ctx_out = li < DEPTH - 1
        mod_l = (s_lat @ ada_w[li] + ada_b[li]).reshape(B, 6, 1, D)
        mod_c = (s_ctx @ ada_w[li] + ada_b[li]).reshape(6, 1, D)
        ml = [mod_l[:, q] for q in range(6)]
        mc = [mod_c[q][None] for q in range(6)]

        w_in_l = w_in[li]
        w_secs = [w.astype(BF16) for w in (mla_section_weight(w_in_l[:, :B_X]), w_in_l[:, B_X:C_OFF],
                                           w_in_l[:, C_OFF:D_OFF], w_in_l[:, D_OFF:])]
        res_l = res_c = None
        if prev is not None:
            out, gate_l, gate_c, g2, b2 = prev
            res_l, res_c = (out, 0, gate_l, g2, b2), (out, Ml, gate_c, g2, b2)
        xl_new, (Pal, Pbl, Pcl, Pdl) = in_proj_pallas(xl, ml[0], ml[1], w_secs, res_l)
        xc_new, ctx_secs = in_proj_pallas(xc, mc[0], mc[1], w_secs if ctx_out else w_secs[:3], res_c)
        if prev is not None:
            xl, xc = xl_new, xc_new
        Pal, Pbl, Pcl, Pdl = (p.reshape(B, L, -1) for p in (Pal, Pbl, Pcl, Pdl))
        Pac, Pbc, Pcc = (p.reshape(B, Lc, -1) for p in ctx_secs[:3])
        a_l, a_c = mla_mixer(Pac, Pal, mla_tabs, mla_q_norm[li], mla_kv_norm[li], mla_w_uq[li], mla_w_ukv[li],
                             mla_out_norm[li], ctx_out)
        b_l, b_c = rglru_pallas(Pbl, Pbc, lru_conv_w[li], lru_conv_b[li], lru_w_r[li], lru_b_r[li], lru_w_i[li],
                                lru_b_i[li], lru_lambda[li], lru_out_norm[li])
        c_l, c_c = rwkv_mixer(Pcl, Pcc, rwkv_mu_prev[li], rwkv_mu_next[li], rwkv_w0[li], rwkv_w_up[li], rwkv_a0[li],
                              rwkv_a_up[li], rwkv_g_up[li], rwkv_k_k[li], rwkv_k_a[li], rwkv_r_k[li],
                              rwkv_ln_g[li], rwkv_ln_b[li], ctx_out)
        d_l = rms_norm(hyena_sequence(Pdl, hy_conv_w[li], hy_conv_b[li], hy_f_w1[li], hy_f_b1[li], hy_f_w2[li],
                                      hy_f_b2[li], hy_f_w3[li], hy_d[li]), hy_out_norm[li])
        w_out_l = w_out[li].astype(BF16)
        xl, tokens = out_proj_pallas([t.reshape(Ml, GROUP) for t in (a_l, b_l, c_l, d_l)], xl, w_out_l,
                                     ml[2], ln1_g[li], ln1_b[li], ml[3], ml[4])
        if ctx_out:
            Pdc = ctx_secs[3].reshape(B, Lc, -1)
            d_c = rms_norm(hyena_sequence(Pdc, hy_conv_w[li], hy_conv_b[li], hy_f_w1[li], hy_f_b1[li], hy_f_w2[li],
                                          hy_f_b2[li], hy_f_w3[li], hy_d[li]), hy_out_norm[li])
            xc, fc = out_proj_pallas([t.reshape(Mc, GROUP) for t in (a_c, b_c, c_c, d_c)], xc, w_out_l,
                                     mc[2], ln1_g[li], ln1_b[li], mc[3], mc[4])
            tokens = jnp.concatenate([tokens, fc], 0)
        j = li // 2
        if li % 2 == 0:
            out = swiglu(tokens, ffn_w_gate[j], ffn_w_up[j], ffn_w_down[j])
        else:
            out = moe_swiglu(tokens, moe_router[j], moe_w_gate[j], moe_w_up[j], moe_w_down[j])
        prev = (out, ml[5], mc[5], ln2_g[li], ln2_b[li])
    out, gate_l, _, g2, b2 = prev
    xl, _ = in_proj_pallas(xl, ml[0], ml[1], [], (out, 0, gate_l, g2, b2))
    return xl.reshape(B, L, D)
```

```python
import math
from functools import partial

import jax
import jax.numpy as jnp
from jax import lax
from jax.experimental import pallas as pl
from jax.experimental.pallas import tpu as pltpu

F32 = jnp.float32
BF16 = jnp.bfloat16

SUBLANES = 8
LANES = 128
VMEM_LIMIT_BYTES = 48 * 1024 * 1024

D_MODEL = 1024
DEPTH = 4
GRID_W = 64
GROUP = D_MODEL // 4

MLA_HEADS = 4
MLA_NOPE = 64
MLA_ROPE = 32
MLA_V = 64
MLA_Q_RANK = 192
MLA_KV_RANK = 128
ROPE_BASE = 10000.0
Q_BLOCK = 128

LRU_BLOCKS = 4
LRU_CONV = 4
LRU_CONV_LEFT = 2
LRU_C = 8.0

RWKV_HEADS = 4
RWKV_HEAD = GROUP // RWKV_HEADS
RWKV_DECAY_LORA = 32
RWKV_AAA_LORA = 32
RWKV_GATE_LORA = 64
RWKV_GN_EPS = 64e-5

HY_ORDER = 2
HY_SHORT = 3
HY_BANDS = 16
HY_EMB = 1 + 2 * HY_BANDS
HY_HIDDEN = 64
HY_SIN_FREQ = 1.0
HY_DECAY_MIN = math.log(1e-2) / 1.5
HY_DECAY_MAX = math.log(1e-2) / 0.3
HY_SHIFT = 0.05

N_EXPERTS = 8
TOP_K = 2
MOE_BLOCK = 512

ALPHA = (2.0 * DEPTH) ** 0.25

A_CQ = 0
A_CKV = A_CQ + MLA_Q_RANK
A_KR = A_CKV + MLA_KV_RANK
B_X = A_KR + MLA_ROPE
B_GATE = B_X + GROUP
C_OFF = B_GATE + GROUP
C_R = 0
C_K = GROUP
C_V = 2 * GROUP
C_WD = 3 * GROUP
C_AD = C_WD + 2 * RWKV_DECAY_LORA
C_GD = C_AD + 2 * RWKV_AAA_LORA
C_COLS = C_GD + RWKV_GATE_LORA
D_OFF = C_OFF + C_COLS
D_COLS = (HY_ORDER + 1) * GROUP


RWKV_TIME_BLOCK = 16
V_TILES = RWKV_HEAD // SUBLANES
RWKV_KEY_GROUP = 32


def _rwkv_scan_kernel(rkf, rkb, vvf, vvb, wf, wb, kaf, kab, kdf, kdb, yf_ref, yb_ref, s_ref, m_ref):
    @pl.when(pl.program_id(0) == 0)
    def _():
        s_ref[...] = jnp.zeros_like(s_ref)

    n_t = rkf.shape[0]
    p = rkf.shape[2]
    fwd_lanes = lax.broadcasted_iota(jnp.int32, (RWKV_HEAD, p), 1) < p // 2

    for j in range(n_t):
        jb = n_t - 1 - j
        rk_f, rk_b = rkf[j], rkb[jb]
        m_ref[0, j] = jnp.where(fwd_lanes, rk_f, pltpu.roll(rk_b, p // 2, 1))
        m_ref[2, j] = jnp.where(fwd_lanes, pltpu.roll(rk_f, p // 2, 1), rk_b)
        for q, (f_ref, b_ref) in ((1, (wf, wb)), (3, (kaf, kab)), (4, (vvf, vvb)), (5, (kdf, kdb))):
            m_ref[q, j] = jnp.where(fwd_lanes, f_ref[j], b_ref[jb])

    r_ref, w_ref, kk_ref, kka_ref, v_ref, kd_ref = (m_ref.at[q] for q in range(6))

    def step(t, carry):
        def row(ref, k):
            return jnp.broadcast_to(ref[t, pl.ds(k, 1), :], (SUBLANES, p))[None]

        def sa_group(g, sa):
            k0 = pl.multiple_of(g * RWKV_KEY_GROUP, RWKV_KEY_GROUP)
            for j in range(RWKV_KEY_GROUP):
                sa = sa + s_ref[k0 + j] * row(kk_ref, k0 + j)
            return sa

        zero = jnp.zeros((V_TILES, SUBLANES, p), F32)
        sa = lax.fori_loop(0, RWKV_HEAD // RWKV_KEY_GROUP, sa_group, zero)
        vt = v_ref[t].reshape(V_TILES, SUBLANES, p)

        def update_group(g, y):
            k0 = pl.multiple_of(g * RWKV_KEY_GROUP, RWKV_KEY_GROUP)
            for j in range(RWKV_KEY_GROUP):
                k = k0 + j
                sn = s_ref[k] * row(w_ref, k) - sa * row(kka_ref, k) + vt * row(kd_ref, k)
                s_ref[k] = sn
                y = y + sn * row(r_ref, k)
            return y

        y = lax.fori_loop(0, RWKV_HEAD // RWKV_KEY_GROUP, update_group, zero).reshape(RWKV_HEAD, p)
        yf_ref[t] = y
        yb_ref[n_t - 1 - t] = y
        return carry

    lax.fori_loop(0, n_t, step, 0)


def rwkv_scan_pallas(rk, vv, w, kka, kd, n_ctx):
    n_steps, n, p = rk.shape
    tb = RWKV_TIME_BLOCK
    assert n == RWKV_HEAD and p == LANES and n_steps % tb == 0 and n_ctx % tb == 0
    nb, nc = n_steps // tb, n_ctx // tb
    fwd = pl.BlockSpec((tb, n, p), lambda g: (g, 0, 0))
    bwd = pl.BlockSpec((tb, n, p), lambda g: (jnp.where(g < nc, nc - 1 - g, nb + nc - 1 - g), 0, 0))
    out = jax.ShapeDtypeStruct((n_steps, n, p), F32)
    return pl.pallas_call(
        _rwkv_scan_kernel,
        grid=(nb,),
        in_specs=[fwd, bwd] * 5,
        out_specs=[fwd, bwd],
        out_shape=[out, out],
        scratch_shapes=[pltpu.VMEM((n, V_TILES, SUBLANES, LANES), F32), pltpu.VMEM((6, tb, n, LANES), F32)],
        compiler_params=pltpu.CompilerParams(dimension_semantics=("arbitrary",), vmem_limit_bytes=VMEM_LIMIT_BYTES),
        name="rwkv_scan",
    )(rk, rk, vv, vv, w, w, kka, kka, kd, kd)


ATTN_TQ = 256
ATTN_TK = 256


def _attn_kernel(q_ref, k_ref, v_ref, g_ref, o_ref, s_ref):
    tq = q_ref.shape[1]
    n_chunks = k_ref.shape[1]
    n_tiles = ATTN_TK // LANES
    pair_out = []
    for hp in range(MLA_HEADS // 2):
        o_pair = jnp.zeros((tq, LANES), F32)
        for h in (2 * hp, 2 * hp + 1):
            qh = q_ref[0, :, pl.ds(LANES * h, LANES)]

            m_acc = jnp.full((tq, LANES), -jnp.inf, F32)
            for c in range(n_chunks):
                s = jnp.dot(qh, k_ref[0, c, pl.ds(LANES * h, LANES), :], preferred_element_type=F32)
                s_ref[c] = s
                for j in range(n_tiles):
                    m_acc = jnp.maximum(m_acc, s[:, LANES * j:LANES * (j + 1)])
            m_full = jnp.broadcast_to(jnp.max(m_acc, -1, keepdims=True), (tq, LANES))

            l_acc = jnp.zeros((tq, LANES), F32)
            acc = jnp.zeros((tq, LANES), F32)
            for c in range(n_chunks):
                s = s_ref[c]
                ps = []
                for j in range(n_tiles):
                    p = jnp.exp2(s[:, LANES * j:LANES * (j + 1)] - m_full)
                    l_acc = l_acc + p
                    ps.append(p.astype(BF16))
                vh = v_ref[0, pl.ds(c * ATTN_TK, ATTN_TK), pl.ds(LANES * h, LANES)]
                acc = acc + jnp.dot(jnp.concatenate(ps, -1), vh, preferred_element_type=F32)
            o_pair = o_pair + acc / jnp.sum(l_acc, -1, keepdims=True)
        pair_out.append(o_pair)
    o = jnp.concatenate(pair_out, -1)
    o_ref[0] = o * lax.rsqrt(jnp.mean(jnp.square(o), -1, keepdims=True) + 1e-6) * g_ref[...]


def mla_attention_pallas(q, kt, v, out_norm, q_start, n_q, n_kv):
    B, _, W = q.shape
    return pl.pallas_call(
        _attn_kernel,
        grid=(B, n_q),
        in_specs=[
            pl.BlockSpec((1, ATTN_TQ, W), lambda b, i: (b, i + q_start, 0)),
            pl.BlockSpec((1, n_kv, W, ATTN_TK), lambda b, i: (b, 0, 0, 0)),
            pl.BlockSpec((1, n_kv * ATTN_TK, W), lambda b, i: (b, 0, 0)),
            pl.BlockSpec((1, GROUP), lambda b, i: (0, 0)),
        ],
        out_specs=pl.BlockSpec((1, ATTN_TQ, GROUP), lambda b, i: (b, i, 0)),
        out_shape=jax.ShapeDtypeStruct((B, n_q * ATTN_TQ, GROUP), F32),
        scratch_shapes=[pltpu.VMEM((n_kv, ATTN_TQ, ATTN_TK), F32)],
        compiler_params=pltpu.CompilerParams(dimension_semantics=("arbitrary", "arbitrary"),
                                             vmem_limit_bytes=VMEM_LIMIT_BYTES),
        name="mla_attention",
    )(q, kt, v, out_norm[None])


MLA_PA_COLS = 4 * LANES
MLA_W = MLA_HEADS * LANES


def _rot_cols(w):
    h = w.shape[1] // 2
    return jnp.concatenate([-w[:, h:], w[:, :h]], 1)


def mla_section_weight(w_a):
    z = jnp.zeros((w_a.shape[0], LANES // 2), w_a.dtype)
    kr = w_a[:, A_KR:A_KR + MLA_ROPE]
    return jnp.concatenate([w_a[:, A_CKV:A_CKV + MLA_KV_RANK], kr, _rot_cols(kr), z,
                            w_a[:, A_CQ:A_CQ + MLA_Q_RANK], z], 1)


def mla_prep_weights(w_uq, w_ukv):
    H, DN, DR, DV = MLA_HEADS, MLA_NOPE, MLA_ROPE, MLA_V
    wq = jnp.zeros((2 * LANES, MLA_W), F32)
    wq_rot = jnp.zeros((2 * LANES, MLA_W), F32)
    wk = jnp.zeros((2 * LANES, MLA_W), F32)
    wv = jnp.zeros((LANES, MLA_W), F32)
    place = jnp.eye(DR, dtype=F32)
    for h in range(H):
        q_h = w_uq[:, h * (DN + DR):(h + 1) * (DN + DR)]
        wq = wq.at[:MLA_Q_RANK, LANES * h:LANES * h + DN + DR].set(q_h)
        wq_rot = wq_rot.at[:MLA_Q_RANK, LANES * h + DN:LANES * h + DN + DR].set(_rot_cols(q_h[:, DN:]))
        kv_h = w_ukv[:, h * (DN + DV):(h + 1) * (DN + DV)]
        wk = wk.at[:MLA_KV_RANK, LANES * h:LANES * h + DN].set(kv_h[:, :DN])
        wk = wk.at[LANES:LANES + DR, LANES * h + DN:LANES * h + DN + DR].set(place)
        v0 = LANES * h + DV * (h % 2)
        wv = wv.at[:, v0:v0 + DV].set(kv_h[:, DN:])
    return wq.astype(BF16), wq_rot.astype(BF16), wk.astype(BF16), wv.astype(BF16)


def mla_tables(cos, sin, n_ctx):
    H, DN, DR = MLA_HEADS, MLA_NOPE, MLA_ROPE
    L = cos.shape[0]
    cf = jnp.concatenate([jnp.ones((n_ctx, DR), F32), jnp.concatenate([cos, cos], -1)], 0)
    sf = jnp.concatenate([jnp.zeros((n_ctx, DR), F32), jnp.concatenate([sin, sin], -1)], 0)
    S = n_ctx + L
    tab_k = jnp.concatenate([cf, sf, jnp.zeros((S, LANES - 2 * DR), F32)], -1)
    scale = (DN + DR) ** -0.5 * math.log2(math.e)
    zpad = jnp.zeros((S, LANES - DN - DR), F32)
    qc_h = jnp.concatenate([jnp.full((S, DN), scale, F32), cf * scale, zpad], -1)
    qs_h = jnp.concatenate([jnp.zeros((S, DN), F32), sf * scale, zpad], -1)
    return tab_k, jnp.tile(qc_h, (1, H)), jnp.tile(qs_h, (1, H))


def _mla_prep_kernel(pa_ref, tk_ref, tqc_ref, tqs_ref, kvn_ref, qn_ref, wq_ref, wqr_ref, wk_ref, wv_ref,
                     q_ref, kt_ref, v_ref):
    pa = pa_ref[0]
    ckv = pa[:, :LANES]
    ckv = ckv * lax.rsqrt(jnp.mean(jnp.square(ckv), -1, keepdims=True) + 1e-6) * kvn_ref[...]
    t = pa[:, LANES:2 * LANES] * tk_ref[...]
    kr = t + pltpu.roll(t, LANES - MLA_ROPE, 1)
    kr = jnp.where(lax.broadcasted_iota(jnp.int32, kr.shape, 1) < MLA_ROPE, kr, 0.0)
    k = jnp.dot(jnp.concatenate([ckv, kr], -1).astype(BF16), wk_ref[...], preferred_element_type=F32)
    kt_ref[0, 0] = k.T.astype(BF16)
    v_ref[0] = jnp.dot(ckv.astype(BF16), wv_ref[...], preferred_element_type=F32).astype(BF16)
    cq = pa[:, 2 * LANES:]
    ms = jnp.sum(jnp.square(cq), -1, keepdims=True) * (1.0 / MLA_Q_RANK)
    cq = (cq * lax.rsqrt(ms + 1e-6) * qn_ref[...]).astype(BF16)
    qa = jnp.dot(cq, wq_ref[...], preferred_element_type=F32)
    qb = jnp.dot(cq, wqr_ref[...], preferred_element_type=F32)
    q_ref[0] = (qa * tqc_ref[...] + qb * tqs_ref[...]).astype(BF16)


def mla_prep_pallas(pa, tables, q_norm, kv_norm, weights):
    B, S, _ = pa.shape
    T = ATTN_TK
    assert S % T == 0
    tab_k, tab_qc, tab_qs = tables
    wq, wq_rot, wk, wv = weights
    qn = jnp.concatenate([q_norm, jnp.zeros((2 * LANES - MLA_Q_RANK,), F32)])[None]

    def full(a):
        return pl.BlockSpec(a.shape, lambda b, i: (0,) * a.ndim)

    def rows(width):
        return pl.BlockSpec((T, width), lambda b, i: (i, 0))

    return pl.pallas_call(
        _mla_prep_kernel,
        grid=(B, S // T),
        in_specs=[pl.BlockSpec((1, T, MLA_PA_COLS), lambda b, i: (b, i, 0)),
                  rows(LANES), rows(MLA_W), rows(MLA_W),
                  pl.BlockSpec((1, LANES), lambda b, i: (0, 0)), full(qn), full(wq), full(wq_rot), full(wk), full(wv)],
        out_specs=[pl.BlockSpec((1, T, MLA_W), lambda b, i: (b, i, 0)),
                   pl.BlockSpec((1, 1, MLA_W, T), lambda b, i: (b, i, 0, 0)),
                   pl.BlockSpec((1, T, MLA_W), lambda b, i: (b, i, 0))],
        out_shape=[jax.ShapeDtypeStruct((B, S, MLA_W), BF16),
                   jax.ShapeDtypeStruct((B, S // T, MLA_W, T), BF16),
                   jax.ShapeDtypeStruct((B, S, MLA_W), BF16)],
        compiler_params=pltpu.CompilerParams(dimension_semantics=("arbitrary", "arbitrary")),
        name="mla_prep",
    )(pa, tab_k, tab_qc, tab_qs, kv_norm[None], qn, wq, wq_rot, wk, wv)


LRU_CHUNK = 256
LRU_HALO = SUBLANES


def _lru_kernel(xl_ref, xc_ref, wbd_ref, bias_ref, c8_ref, cw_ref, cb_ref, gn_ref,
                yl_ref, yc_ref, xs_l, xs_c, hf_l, hf_c, a_s, b_s, hb_s):
    C = GROUP
    CH = LRU_CHUNK
    L = xl_ref.shape[1]
    Lc = xc_ref.shape[1]

    def stage(x_ref, xs, n):
        xs[pl.ds(0, LRU_HALO), :] = jnp.zeros((LRU_HALO, C), F32)
        xs[pl.ds(LRU_HALO + n, LRU_HALO), :] = jnp.zeros((LRU_HALO, C), F32)

        def cp(i, c):
            r0 = pl.multiple_of(i * CH, CH)
            xs[pl.ds(LRU_HALO + r0, CH), :] = x_ref[0, pl.ds(r0, CH), pl.ds(0, C)]
            return c

        lax.fori_loop(0, n // CH, cp, 0)

    stage(xl_ref, xs_l, L)
    stage(xc_ref, xs_c, Lc)

    def coeffs(xs, base, d):
        xv = xs[pl.ds(base, CH + 2 * LRU_HALO), :]
        u = cb_ref[...]
        for j in range(LRU_CONV):
            o = LRU_HALO - LRU_CONV_LEFT + j
            u = u + xv[o:o + CH] * cw_ref[pl.ds(j, 1), :]
        z = jnp.dot(u.astype(BF16), wbd_ref[:, pl.ds(d * 2 * C, 2 * C)], preferred_element_type=F32)
        z = z + bias_ref[:, pl.ds(d * 2 * C, 2 * C)]
        r = jax.nn.sigmoid(z[:, :C])
        i = jax.nn.sigmoid(z[:, C:])
        log_a = r * c8_ref[pl.ds(d, 1), :]
        a = jnp.exp(log_a)
        a_s[...] = a
        b_s[...] = jnp.sqrt(-jnp.tanh(log_a) * (a * a + 1.0)) * (i * u)

    def row_scan(h, out_ref, out_base, reverse):
        n_groups = CH // SUBLANES

        def group(g, h):
            r0 = pl.multiple_of((n_groups - 1 - g if reverse else g) * SUBLANES, SUBLANES)
            for j in (reversed(range(SUBLANES)) if reverse else range(SUBLANES)):
                a_t = jnp.broadcast_to(a_s[pl.ds(r0 + j, 1), :], (SUBLANES, C))
                b_t = jnp.broadcast_to(b_s[pl.ds(r0 + j, 1), :], (SUBLANES, C))
                h = a_t * h + b_t
                out_ref[pl.ds(out_base + r0 + j, 1), :] = h[0:1, :]
            return h

        return lax.fori_loop(0, n_groups, group, h)

    h0 = jnp.zeros((SUBLANES, C), F32)

    h = h0
    for ci in range(Lc // CH):
        coeffs(xs_c, ci * CH, 0)
        h = row_scan(h, hf_c, ci * CH, False)

    def fwd_chunk(ci, h):
        base = pl.multiple_of(ci * CH, CH)
        coeffs(xs_l, base, 0)
        return row_scan(h, hf_l, base, False)

    lax.fori_loop(0, L // CH, fwd_chunk, h)

    def combine(x_ref, hf, base, y_ref):
        hl = hf[pl.ds(base, CH), :] + hb_s[...]
        g = jax.nn.gelu(x_ref[0, pl.ds(base, CH), pl.ds(C, C)])
        v = hl * g
        y = v * lax.rsqrt(jnp.mean(jnp.square(v), -1, keepdims=True) + 1e-6) * gn_ref[...]
        y_ref[0, pl.ds(base, CH), :] = y

    h = h0
    for ci in reversed(range(Lc // CH)):
        coeffs(xs_c, ci * CH, 1)
        h = row_scan(h, hb_s, 0, True)
        combine(xc_ref, hf_c, ci * CH, yc_ref)

    def bwd_chunk(k, h):
        base = pl.multiple_of((L // CH - 1 - k) * CH, CH)
        coeffs(xs_l, base, 1)
        h = row_scan(h, hb_s, 0, True)
        combine(xl_ref, hf_l, base, yl_ref)
        return h

    lax.fori_loop(0, L // CH, bwd_chunk, h)


def rglru_pallas(xg_l, xg_c, conv_w, conv_b, w_r, b_r, w_i, b_i, lam, out_norm):
    B, L, _ = xg_l.shape
    Lc = xg_c.shape[1]
    C = GROUP
    assert L % LRU_CHUNK == 0 and Lc % LRU_CHUNK == 0

    def bd(w):
        return jax.scipy.linalg.block_diag(*[w[n] for n in range(LRU_BLOCKS)])

    wbd = jnp.concatenate([bd(w_r[0]), bd(w_i[0]), bd(w_r[1]), bd(w_i[1])], 1).astype(BF16)
    bias = jnp.concatenate([b_r[0], b_i[0], b_r[1], b_i[1]])[None]
    c8 = -LRU_C * jax.nn.softplus(-lam)

    def full(shape):
        return pl.BlockSpec(shape, lambda b: (0,) * len(shape))

    return pl.pallas_call(
        _lru_kernel,
        grid=(B,),
        in_specs=[
            pl.BlockSpec((1, L, 2 * C), lambda b: (b, 0, 0)),
            pl.BlockSpec((1, Lc, 2 * C), lambda b: (b, 0, 0)),
            full((C, 4 * C)), full((1, 4 * C)), full((2, C)), full((LRU_CONV, C)), full((1, C)), full((1, C)),
        ],
        out_specs=[
            pl.BlockSpec((1, L, C), lambda b: (b, 0, 0)),
            pl.BlockSpec((1, Lc, C), lambda b: (b, 0, 0)),
        ],
        out_shape=[jax.ShapeDtypeStruct((B, L, C), F32), jax.ShapeDtypeStruct((B, Lc, C), F32)],
        scratch_shapes=[
            pltpu.VMEM((L + 2 * LRU_HALO, C), F32),
            pltpu.VMEM((Lc + 2 * LRU_HALO, C), F32),
            pltpu.VMEM((L, C), F32),
            pltpu.VMEM((Lc, C), F32),
            pltpu.VMEM((LRU_CHUNK, C), F32),
            pltpu.VMEM((LRU_CHUNK, C), F32),
            pltpu.VMEM((LRU_CHUNK, C), F32),
        ],
        compiler_params=pltpu.CompilerParams(dimension_semantics=("arbitrary",), vmem_limit_bytes=VMEM_LIMIT_BYTES),
        name="rglru",
    )(xg_l, xg_c, wbd, bias, c8, conv_w, conv_b[None], out_norm[None])


HY_T = 256
HY_CB = 8


def _hyena_kernel(k_ref, d_ref, v_ref, x1_ref, x2_ref, o_ref, u_s, acc_s):
    n_rows = v_ref.shape[1]
    T = HY_T
    nb = k_ref.shape[2] // (2 * T)
    bsz = n_rows // nb

    def conv(ci, order):
        acc_s[...] = jnp.zeros_like(acc_s)
        for dd in range(-(nb - 1), nb):
            w2 = k_ref[order, pl.ds(ci, 1), pl.ds(T * (dd + nb - 1), 2 * T)]
            x = jnp.broadcast_to(w2, (T, 2 * T))
            r = pltpu.roll(x, 0, 1, stride=1, stride_axis=0)
            tb = r[:, T:].astype(BF16)
            j0, j1 = max(0, -dd), min(nb, nb - dd)
            lhs = u_s[pl.ds(bsz * j0, bsz * (j1 - j0)), :]
            dst = pl.ds(bsz * (j0 + dd), bsz * (j1 - j0))
            acc_s[dst, :] = acc_s[dst, :] + jnp.dot(lhs, tb, preferred_element_type=F32)

    def channel(ci, carry):
        v = v_ref[ci]
        u_s[...] = v.astype(BF16)
        conv(ci, 0)
        u = x1_ref[ci] * (acc_s[...] + v * d_ref[pl.ds(ci, 1), pl.ds(0, 1)])
        u_s[...] = u.astype(BF16)
        conv(ci, 1)
        o_ref[ci] = x2_ref[ci] * (acc_s[...] + u * d_ref[pl.ds(ci, 1), pl.ds(1, 1)])
        return carry

    lax.fori_loop(0, HY_CB, channel, 0)


def hyena_conv_pallas(kfull, d_skip, zT):
    C3, R, T = zT.shape
    C = C3 // (HY_ORDER + 1)
    two_l = kfull.shape[2]
    assert T == HY_T and C % HY_CB == 0
    n_cb = C // HY_CB
    blk = pl.BlockSpec((HY_CB, R, T), lambda c: (c, 0, 0))
    return pl.pallas_call(
        _hyena_kernel,
        grid=(n_cb,),
        in_specs=[
            pl.BlockSpec((HY_ORDER, HY_CB, two_l), lambda c: (0, c, 0)),
            pl.BlockSpec((HY_CB, HY_ORDER), lambda c: (c, 0)),
            blk,
            pl.BlockSpec((HY_CB, R, T), lambda c: (c + n_cb, 0, 0)),
            pl.BlockSpec((HY_CB, R, T), lambda c: (c + 2 * n_cb, 0, 0)),
        ],
        out_specs=blk,
        out_shape=jax.ShapeDtypeStruct((C, R, T), F32),
        scratch_shapes=[pltpu.VMEM((R, T), BF16), pltpu.VMEM((R, T), F32)],
        compiler_params=pltpu.CompilerParams(dimension_semantics=("arbitrary",), vmem_limit_bytes=VMEM_LIMIT_BYTES),
        name="hyena_conv",
    )(kfull, d_skip.T, zT, zT, zT)


FFN_TF_MAX = 1792


def _ffn_tile(hidden):
    return max(t for t in range(LANES, FFN_TF_MAX + 1, LANES) if hidden % t == 0)


def _swiglu_kernel(be_ref, nu_ref, x_ref, wg_ref, wu_ref, wd_ref, o_ref, acc_ref):
    i = pl.program_id(0)
    f = pl.program_id(1)

    @pl.when(i < nu_ref[0])
    def _():
        x = x_ref[...]
        g = jnp.dot(x, wg_ref[0], preferred_element_type=F32)
        u = jnp.dot(x, wu_ref[0], preferred_element_type=F32)
        h = (jax.nn.silu(g) * u).astype(BF16)
        part = jnp.dot(h, wd_ref[0], preferred_element_type=F32)

        @pl.when(f == 0)
        def _():
            acc_ref[...] = part

        @pl.when(f > 0)
        def _():
            acc_ref[...] = acc_ref[...] + part

    @pl.when(f == pl.num_programs(1) - 1)
    def _():
        o_ref[...] = jnp.where(i < nu_ref[0], acc_ref[...], 0.0)


def grouped_swiglu_pallas(xs, block_expert, n_used, wg, wu, wd):
    n_rows, D = xs.shape
    F = wg.shape[2]
    TM = MOE_BLOCK
    assert n_rows % TM == 0
    tf = _ffn_tile(F)
    n_blocks = n_rows // TM
    grid_spec = pltpu.PrefetchScalarGridSpec(
        num_scalar_prefetch=2,
        grid=(n_blocks, F // tf),
        in_specs=[
            pl.BlockSpec((TM, D), lambda i, f, be, nu: (i, 0)),
            pl.BlockSpec((1, D, tf), lambda i, f, be, nu: (be[i], 0, f)),
            pl.BlockSpec((1, D, tf), lambda i, f, be, nu: (be[i], 0, f)),
            pl.BlockSpec((1, tf, D), lambda i, f, be, nu: (be[i], f, 0)),
        ],
        out_specs=pl.BlockSpec((TM, D), lambda i, f, be, nu: (i, 0)),
        scratch_shapes=[pltpu.VMEM((TM, D), F32)],
    )
    return pl.pallas_call(
        _swiglu_kernel,
        grid_spec=grid_spec,
        out_shape=jax.ShapeDtypeStruct((n_rows, D), F32),
        compiler_params=pltpu.CompilerParams(dimension_semantics=("arbitrary", "arbitrary"),
                                             vmem_limit_bytes=VMEM_LIMIT_BYTES),
        name="grouped_swiglu",
    )(block_expert.astype(jnp.int32), jnp.reshape(n_used, (1,)).astype(jnp.int32), xs, wg, wu, wd)


PROJ_TM = 512


def _residual_ln(x, branch, gate_ref, g_ref, b_ref):
    s = ALPHA * x + gate_ref[0] * branch
    mu = jnp.mean(s, -1, keepdims=True)
    d = s - mu
    var = jnp.mean(jnp.square(d), -1, keepdims=True)
    return d * lax.rsqrt(var + 1e-5) * g_ref[...] + b_ref[...]


def _in_proj_kernel(*refs, has_ln, n_w):
    x_ref = refs[0]
    pos = 1
    x = x_ref[...]
    if has_ln:
        branch_ref, gate_ref, g_ref, b_ref = refs[1:5]
        pos = 5
        x = _residual_ln(x, branch_ref[...], gate_ref, g_ref, b_ref)
    shift_ref, scale_ref = refs[pos:pos + 2]
    w_refs = refs[pos + 2:pos + 2 + n_w]
    out_refs = refs[pos + 2 + n_w:]
    if has_ln:
        out_refs[0][...] = x
        out_refs = out_refs[1:]
    h = (x * (1.0 + scale_ref[0]) + shift_ref[0]).astype(BF16)
    for w_ref, o_ref in zip(w_refs, out_refs):
        o_ref[...] = jnp.dot(h, w_ref[...], preferred_element_type=F32)


def in_proj_pallas(x, shift, scale, weights, residual=None):
    M, D = x.shape
    tm = PROJ_TM
    G = shift.shape[0]
    tiles_per_group = M // G // tm
    assert M % (G * tm) == 0
    row = pl.BlockSpec((tm, D), lambda i: (i, 0))
    grp = pl.BlockSpec((1, 1, D), lambda i: (i // tiles_per_group, 0, 0))
    vec = pl.BlockSpec((1, D), lambda i: (0, 0))
    args, specs = [x], [row]
    has_ln = residual is not None
    if has_ln:
        branch, first_row, gate, ln_g, ln_b = residual
        assert first_row % tm == 0
        off = first_row // tm
        args += [branch, gate, ln_g[None], ln_b[None]]
        specs += [pl.BlockSpec((tm, D), lambda i: (i + off, 0)), grp, vec, vec]
    args += [shift, scale] + list(weights)
    specs += [grp, grp] + [pl.BlockSpec(w.shape, lambda i: (0, 0)) for w in weights]
    out_shape = [jax.ShapeDtypeStruct((M, w.shape[1]), F32) for w in weights]
    out_specs = [pl.BlockSpec((tm, w.shape[1]), lambda i: (i, 0)) for w in weights]
    if has_ln:
        out_shape = [jax.ShapeDtypeStruct((M, D), F32)] + out_shape
        out_specs = [row] + out_specs
    outs = pl.pallas_call(
        partial(_in_proj_kernel, has_ln=has_ln, n_w=len(weights)),
        grid=(M // tm,),
        in_specs=specs,
        out_specs=out_specs,
        out_shape=out_shape,
        compiler_params=pltpu.CompilerParams(dimension_semantics=("arbitrary",), vmem_limit_bytes=VMEM_LIMIT_BYTES),
        name="in_proj",
    )(*args)
    return (outs[0], outs[1:]) if has_ln else (None, outs)


def _out_proj_kernel(a_ref, b_ref, c_ref, d_ref, x_ref, w_ref, gate_ref, g_ref, bb_ref, shift_ref, scale_ref,
                     xn_ref, f_ref):
    y = jnp.concatenate([a_ref[...], b_ref[...], c_ref[...], d_ref[...]], -1).astype(BF16)
    y = jnp.dot(y, w_ref[...], preferred_element_type=F32)
    xn = _residual_ln(x_ref[...], y, gate_ref, g_ref, bb_ref)
    xn_ref[...] = xn
    f_ref[...] = (xn * (1.0 + scale_ref[0]) + shift_ref[0]).astype(BF16)


def out_proj_pallas(parts, x, w_out, gate, ln_g, ln_b, shift, scale):
    M, D = x.shape
    tm = PROJ_TM
    G = gate.shape[0]
    tiles_per_group = M // G // tm
    assert M % (G * tm) == 0
    row = pl.BlockSpec((tm, D), lambda i: (i, 0))
    part = pl.BlockSpec((tm, GROUP), lambda i: (i, 0))
    grp = pl.BlockSpec((1, 1, D), lambda i: (i // tiles_per_group, 0, 0))
    vec = pl.BlockSpec((1, D), lambda i: (0, 0))
    return pl.pallas_call(
        _out_proj_kernel,
        grid=(M // tm,),
        in_specs=[part] * 4 + [row, pl.BlockSpec(w_out.shape, lambda i: (0, 0)), grp, vec, vec, grp, grp],
        out_specs=[row, row],
        out_shape=[jax.ShapeDtypeStruct((M, D), F32), jax.ShapeDtypeStruct((M, D), BF16)],
        compiler_params=pltpu.CompilerParams(dimension_semantics=("arbitrary",), vmem_limit_bytes=VMEM_LIMIT_BYTES),
        name="out_proj",
    )(*parts, x, w_out, gate, ln_g[None], ln_b[None], shift, scale)


def modulate(x, shift, scale):
    return x * (1.0 + scale) + shift


def layer_norm(x, g, b, eps=1e-5):
    mu = jnp.mean(x, -1, keepdims=True)
    var = jnp.mean(jnp.square(x - mu), -1, keepdims=True)
    return (x - mu) * lax.rsqrt(var + eps) * g + b


def rms_norm(x, g, eps=1e-6):
    return x * lax.rsqrt(jnp.mean(jnp.square(x), -1, keepdims=True) + eps) * g


def dwconv(x, w, b, left):
    K = w.shape[0]
    L = x.shape[1]
    xp = jnp.pad(x, ((0, 0), (left, K - 1 - left), (0, 0)))
    return sum(xp[:, j:j + L] * w[j] for j in range(K)) + b


def token_shift(z, mu_prev, mu_next):
    zp = jnp.pad(z, ((0, 0), (1, 1), (0, 0)))
    return z + mu_prev * (zp[:, :-2] - z) + mu_next * (zp[:, 2:] - z)


def axial_rope(rows):
    r, col = jnp.meshgrid(jnp.arange(rows, dtype=F32), jnp.arange(GRID_W, dtype=F32), indexing='ij')
    half = MLA_ROPE // 2
    inv = 1.0 / (ROPE_BASE ** (jnp.arange(0, half, 2, dtype=F32) / half))
    ang = jnp.concatenate([r.reshape(-1, 1) * inv, col.reshape(-1, 1) * inv], -1)
    return jnp.cos(ang), jnp.sin(ang)


def mla_mixer(Pac, Pal, tables, q_norm, kv_norm, w_uq, w_ukv, out_norm, ctx_out):
    Lc, L = Pac.shape[1], Pal.shape[1]
    q, kt, v = mla_prep_pallas(jnp.concatenate([Pac, Pal], 1), tables, q_norm, kv_norm, mla_prep_weights(w_uq, w_ukv))
    nc, nl = Lc // ATTN_TQ, L // ATTN_TQ
    yl = mla_attention_pallas(q, kt, v, out_norm, nc, nl, nc + nl)
    yc = mla_attention_pallas(q, kt, v, out_norm, 0, nc, nc) if ctx_out else None
    return yl, yc


def rwkv_finish(y, bonus, v, gd, g_up, ln_g, ln_b):
    B, L = y.shape[:2]
    mu = jnp.mean(y, -1, keepdims=True)
    var = jnp.mean(jnp.square(y - mu), -1, keepdims=True)
    yn = (y - mu) * lax.rsqrt(var + RWKV_GN_EPS) * ln_g.reshape(RWKV_HEADS, RWKV_HEAD) + ln_b.reshape(RWKV_HEADS, RWKV_HEAD)
    g = jax.nn.sigmoid(gd) @ g_up
    return (yn + bonus * v).reshape(B, L, GROUP) * g


def rwkv_mixer(Pl, Pc, mu_prev, mu_next, w0, w_up, a0, a_up, g_up, k_k, k_a, r_k, ln_g, ln_b, ctx_out):
    B, L = Pl.shape[:2]
    Lc = Pc.shape[1]
    S = Lc + L
    H, N = RWKV_HEADS, RWKV_HEAD
    z = jnp.concatenate([token_shift(Pc, mu_prev, mu_next), token_shift(Pl, mu_prev, mu_next)], 1)

    def heads(t):
        return t.reshape(B, S, H, N)

    r = heads(z[..., C_R:C_R + GROUP])
    k = heads(z[..., C_K:C_K + GROUP])
    v = heads(z[..., C_V:C_V + GROUP])
    kk = k * k_k.reshape(H, N)
    kk = kk * lax.rsqrt(jnp.maximum(jnp.sum(jnp.square(kk), -1, keepdims=True), 1e-24))
    w, kka, kd = [], [], []
    for d in range(2):
        wd = z[..., C_WD + d * RWKV_DECAY_LORA:C_WD + (d + 1) * RWKV_DECAY_LORA]
        ad = z[..., C_AD + d * RWKV_AAA_LORA:C_AD + (d + 1) * RWKV_AAA_LORA]
        log_w = -jnp.exp(-jax.nn.softplus(-(w0[d] + jnp.tanh(wd) @ w_up[d])) - 0.5)
        a = heads(jax.nn.sigmoid(a0[d] + ad @ a_up[d]))
        w.append(heads(jnp.exp(log_w)))
        kka.append(kk * a)
        kd.append(k * (1.0 + (a - 1.0) * k_a.reshape(H, N)))

    def to_scan(lo, hi):
        return jnp.concatenate([jnp.transpose(t, (1, 3, 0, 2)).reshape(S, N, B * H) for t in (lo, hi)], -1)

    yf, yb = rwkv_scan_pallas(to_scan(r, kk), to_scan(v, v), to_scan(*w), to_scan(*kka), to_scan(*kd), Lc)
    y = yf[..., :B * H] + yb[..., B * H:]
    y = jnp.transpose(y.reshape(S, N, B, H), (2, 0, 3, 1))
    bonus = sum(jnp.sum(r * kd_d * r_k, -1, keepdims=True) for kd_d in kd)
    gd = z[..., C_GD:C_GD + RWKV_GATE_LORA]
    out_l = rwkv_finish(y[:, Lc:], bonus[:, Lc:], v[:, Lc:], gd[:, Lc:], g_up, ln_g, ln_b)
    out_c = None
    if ctx_out:
        out_c = rwkv_finish(y[:, :Lc], bonus[:, :Lc], v[:, :Lc], gd[:, :Lc], g_up, ln_g, ln_b)
    return out_l, out_c


def hyena_filters(L, w1, b1, w2, b2, w3):
    t01 = jnp.linspace(0.0, 1.0, L, dtype=F32)[:, None]
    bands = jnp.linspace(1e-4, HY_BANDS - 1, HY_BANDS, dtype=F32)[None, :]
    wpos = (2.0 * math.pi / L) * jnp.arange(L, dtype=F32)[:, None]
    z = jnp.concatenate([t01, jnp.cos(bands * wpos), -jnp.sin(bands * wpos)], -1)
    h = jnp.sin(HY_SIN_FREQ * (z @ w1 + b1))
    h = jnp.sin(HY_SIN_FREQ * (h @ w2 + b2))
    h = (h @ w3).reshape(L, HY_ORDER, 2, GROUP)
    deltas = jnp.abs(jnp.linspace(HY_DECAY_MIN, HY_DECAY_MAX, GROUP, dtype=F32))
    window = jnp.exp(-t01 * deltas) + HY_SHIFT
    return h * window[:, None, None, :]


def hyena_sequence(Pd, conv_w, conv_b, w1, b1, w2, b2, w3, d_skip):
    B, L, _ = Pd.shape
    C = GROUP
    T = HY_T
    nb = L // T
    z = dwconv(Pd, conv_w, conv_b, 1)
    zT = jnp.transpose(z.reshape(B, nb, T, 3 * C), (3, 1, 0, 2)).reshape(3 * C, nb * B, T)
    h = hyena_filters(L, w1, b1, w2, b2, w3)
    hf = jnp.transpose(h[:, :, 0], (1, 2, 0))
    hb = jnp.transpose(h[:, :, 1], (1, 2, 0))
    kfull = jnp.concatenate([jnp.zeros((HY_ORDER, C, 1), F32), jnp.flip(hb[..., 1:], -1), hf], -1)
    oT = hyena_conv_pallas(kfull, d_skip, zT)
    return jnp.transpose(oT.reshape(C, nb, B, T), (2, 1, 3, 0)).reshape(B, L, C)


def swiglu(x, wg, wu, wd):
    n_blocks = x.shape[0] // MOE_BLOCK
    return grouped_swiglu_pallas(x.astype(BF16), jnp.zeros((n_blocks,), jnp.int32), jnp.int32(n_blocks),
                                 wg[None].astype(BF16), wu[None].astype(BF16), wd[None].astype(BF16))


def moe_swiglu(x, router, wg, wu, wd):
    N, D = x.shape
    logits = x @ router
    top_v, top_i = lax.top_k(logits, TOP_K)
    gates = jax.nn.softmax(top_v, axis=-1)
    A = N * TOP_K
    e_flat = top_i.reshape(-1)
    tok_flat = jnp.arange(A, dtype=jnp.int32) // TOP_K
    order = jnp.argsort(e_flat)
    e_sorted = e_flat[order]
    counts = jnp.bincount(e_flat, length=N_EXPERTS)
    starts = jnp.cumsum(counts) - counts
    padded = (counts + MOE_BLOCK - 1) // MOE_BLOCK * MOE_BLOCK
    pends = jnp.cumsum(padded)
    pstarts = pends - padded
    dest = (pstarts[e_sorted] + jnp.arange(A, dtype=jnp.int32) - starts[e_sorted]).astype(jnp.int32)
    n_blocks = -(-A // MOE_BLOCK) + N_EXPERTS
    n_slots = n_blocks * MOE_BLOCK
    block_expert = jnp.clip(jnp.searchsorted(pends, jnp.arange(n_blocks) * MOE_BLOCK, side='right'), 0, N_EXPERTS - 1)
    slot_e = jnp.repeat(block_expert, MOE_BLOCK)
    slot_pos = jnp.arange(n_slots, dtype=jnp.int32) - pstarts[slot_e].astype(jnp.int32)
    slot_src = jnp.clip(starts[slot_e].astype(jnp.int32) + slot_pos, 0, A - 1)
    slot_tok = jnp.where(slot_pos < counts[slot_e], tok_flat[order[slot_src]], N)
    xp = jnp.concatenate([x.astype(BF16), jnp.zeros((1, D), BF16)], 0)
    ys = grouped_swiglu_pallas(xp[slot_tok], block_expert, pends[-1] // MOE_BLOCK,
                               wg.astype(BF16), wu.astype(BF16), wd.astype(BF16))
    slot_of = dest[jnp.argsort(order)].reshape(N, TOP_K)
    return ys[slot_of[:, 0]] * gates[:, 0:1] + ys[slot_of[:, 1]] * gates[:, 1:2]


def kernel(x, c, ctx, c_ctx, ada_w, ada_b, w_in, mla_q_norm, mla_kv_norm, mla_w_uq, mla_w_ukv, mla_out_norm, lru_conv_w, lru_conv_b, lru_w_r, lru_b_r, lru_w_i, lru_b_i, lru_lambda, lru_out_norm, rwkv_mu_prev, rwkv_mu_next, rwkv_w0, rwkv_w_up, rwkv_a0, rwkv_a_up, rwkv_g_up, rwkv_k_k, rwkv_k_a, rwkv_r_k, rwkv_ln_g, rwkv_ln_b, hy_conv_w, hy_conv_b, hy_f_w1, hy_f_b1, hy_f_w2, hy_f_b2, hy_f_w3, hy_d, hy_out_norm, w_out, ln1_g, ln1_b, ln2_g, ln2_b, ffn_w_gate, ffn_w_up, ffn_w_down, moe_router, moe_w_gate, moe_w_up, moe_w_down):
    B, L, D = x.shape
    Lc = ctx.shape[1]
    rows = L // GRID_W
    mla_tabs = mla_tables(*axial_rope(rows), Lc)
    s_lat = jax.nn.silu(c)
    s_ctx = jax.nn.silu(c_ctx)
    Ml, Mc = B * L, B * Lc
    xl, xc = x.reshape(Ml, D), ctx.reshape(Mc, D)
    prev = None
    for li in range(DEPTH):
        ctx_out = li < DEPTH - 1
        mod_l = (s_lat @ ada_w[li] + ada_b[li]).reshape(B, 6, 1, D)
        mod_c = (s_ctx @ ada_w[li] + ada_b[li]).reshape(6, 1, D)
        ml = [mod_l[:, q] for q in range(6)]
        mc = [mod_c[q][None] for q in range(6)]

        w_in_l = w_in[li]
        w_secs = [w.astype(BF16) for w in (mla_section_weight(w_in_l[:, :B_X]), w_in_l[:, B_X:C_OFF],
                                           w_in_l[:, C_OFF:D_OFF], w_in_l[:, D_OFF:])]
        res_l = res_c = None
        if prev is not None:
            out, gate_l, gate_c, g2, b2 = prev
            res_l, res_c = (out, 0, gate_l, g2, b2), (out, Ml, gate_c, g2, b2)
        xl_new, (Pal, Pbl, Pcl, Pdl) = in_proj_pallas(xl, ml[0], ml[1], w_secs, res_l)
        xc_new, ctx_secs = in_proj_pallas(xc, mc[0], mc[1], w_secs if ctx_out else w_secs[:3], res_c)
        if prev is not None:
            xl, xc = xl_new, xc_new
        Pal, Pbl, Pcl, Pdl = (p.reshape(B, L, -1) for p in (Pal, Pbl, Pcl, Pdl))
        Pac, Pbc, Pcc = (p.reshape(B, Lc, -1) for p in ctx_secs[:3])
        a_l, a_c = mla_mixer(Pac, Pal, mla_tabs, mla_q_norm[li], mla_kv_norm[li], mla_w_uq[li], mla_w_ukv[li],
                             mla_out_norm[li], ctx_out)
        b_l, b_c = rglru_pallas(Pbl, Pbc, lru_conv_w[li], lru_conv_b[li], lru_w_r[li], lru_b_r[li], lru_w_i[li],
                                lru_b_i[li], lru_lambda[li], lru_out_norm[li])
        c_l, c_c = rwkv_mixer(Pcl, Pcc, rwkv_mu_prev[li], rwkv_mu_next[li], rwkv_w0[li], rwkv_w_up[li], rwkv_a0[li],
                              rwkv_a_up[li], rwkv_g_up[li], rwkv_k_k[li], rwkv_k_a[li], rwkv_r_k[li],
                              rwkv_ln_g[li], rwkv_ln_b[li], ctx_out)
        d_l = rms_norm(hyena_sequence(Pdl, hy_conv_w[li], hy_conv_b[li], hy_f_w1[li], hy_f_b1[li], hy_f_w2[li],
                                      hy_f_b2[li], hy_f_w3[li], hy_d[li]), hy_out_norm[li])
        w_out_l = w_out[li].astype(BF16)
        xl, tokens = out_proj_pallas([t.reshape(Ml, GROUP) for t in (a_l, b_l, c_l, d_l)], xl, w_out_l,
                                     ml[2], ln1_g[li], ln1_b[li], ml[3], ml[4])
        if ctx_out:
            Pdc = ctx_secs[3].reshape(B, Lc, -1)
            d_c = rms_norm(hyena_sequence(Pdc, hy_conv_w[li], hy_conv_b[li], hy_f_w1[li], hy_f_b1[li], hy_f_w2[li],
                                          hy_f_b2[li], hy_f_w3[li], hy_d[li]), hy_out_norm[li])
            xc, fc = out_proj_pallas([t.reshape(Mc, GROUP) for t in (a_c, b_c, c_c, d_c)], xc, w_out_l,
                                     mc[2], ln1_g[li], ln1_b[li], mc[3], mc[4])
            tokens = jnp.concatenate([tokens, fc], 0)
        j = li // 2
        if li % 2 == 0:
            out = swiglu(tokens, ffn_w_gate[j], ffn_w_up[j], ffn_w_down[j])
        else:
            out = moe_swiglu(tokens, moe_router[j], moe_w_gate[j], moe_w_up[j], moe_w_down[j])
        prev = (out, ml[5], mc[5], ln2_g[li], ln2_b[li])
    out, gate_l, _, g2, b2 = prev
    xl, _ = in_proj_pallas(xl, ml[0], ml[1], [], (out, 0, gate_l, g2, b2))
    return xl.reshape(B, L, D)
```

```python
import math
from functools import partial

import jax
import jax.numpy as jnp
from jax import lax
from jax.experimental import pallas as pl
from jax.experimental.pallas import tpu as pltpu

F32 = jnp.float32
BF16 = jnp.bfloat16

SUBLANES = 8
LANES = 128
VMEM_LIMIT_BYTES = 48 * 1024 * 1024

D_MODEL = 1024
DEPTH = 4
GRID_W = 64
GROUP = D_MODEL // 4

MLA_HEADS = 4
MLA_NOPE = 64
MLA_ROPE = 32
MLA_V = 64
MLA_Q_RANK = 192
MLA_KV_RANK = 128
ROPE_BASE = 10000.0
Q_BLOCK = 128

LRU_BLOCKS = 4
LRU_CONV = 4
LRU_CONV_LEFT = 2
LRU_C = 8.0

RWKV_HEADS = 4
RWKV_HEAD = GROUP // RWKV_HEADS
RWKV_DECAY_LORA = 32
RWKV_AAA_LORA = 32
RWKV_GATE_LORA = 64
RWKV_GN_EPS = 64e-5

HY_ORDER = 2
HY_SHORT = 3
HY_BANDS = 16
HY_EMB = 1 + 2 * HY_BANDS
HY_HIDDEN = 64
HY_SIN_FREQ = 1.0
HY_DECAY_MIN = math.log(1e-2) / 1.5
HY_DECAY_MAX = math.log(1e-2) / 0.3
HY_SHIFT = 0.05

N_EXPERTS = 8
TOP_K = 2
MOE_BLOCK = 512

ALPHA = (2.0 * DEPTH) ** 0.25

A_CQ = 0
A_CKV = A_CQ + MLA_Q_RANK
A_KR = A_CKV + MLA_KV_RANK
B_X = A_KR + MLA_ROPE
B_GATE = B_X + GROUP
C_OFF = B_GATE + GROUP
C_R = 0
C_K = GROUP
C_V = 2 * GROUP
C_WD = 3 * GROUP
C_AD = C_WD + 2 * RWKV_DECAY_LORA
C_GD = C_AD + 2 * RWKV_AAA_LORA
C_COLS = C_GD + RWKV_GATE_LORA
D_OFF = C_OFF + C_COLS
D_COLS = (HY_ORDER + 1) * GROUP


RWKV_TIME_BLOCK = 16
V_TILES = RWKV_HEAD // SUBLANES
RWKV_KEY_GROUP = 32


def _rwkv_scan_kernel(rkf, rkb, vvf, vvb, wf, wb, kaf, kab, kdf, kdb, yf_ref, yb_ref, s_ref, m_ref):
    @pl.when(pl.program_id(0) == 0)
    def _():
        s_ref[...] = jnp.zeros_like(s_ref)

    n_t = rkf.shape[0]
    p = rkf.shape[2]
    fwd_lanes = lax.broadcasted_iota(jnp.int32, (RWKV_HEAD, p), 1) < p // 2

    for j in range(n_t):
        jb = n_t - 1 - j
        rk_f, rk_b = rkf[j], rkb[jb]
        m_ref[0, j] = jnp.where(fwd_lanes, rk_f, pltpu.roll(rk_b, p // 2, 1))
        m_ref[2, j] = jnp.where(fwd_lanes, pltpu.roll(rk_f, p // 2, 1), rk_b)
        for q, (f_ref, b_ref) in ((1, (wf, wb)), (3, (kaf, kab)), (4, (vvf, vvb)), (5, (kdf, kdb))):
            m_ref[q, j] = jnp.where(fwd_lanes, f_ref[j], b_ref[jb])

    r_ref, w_ref, kk_ref, kka_ref, v_ref, kd_ref = (m_ref.at[q] for q in range(6))

    def step(t, carry):
        def row(ref, k):
            return jnp.broadcast_to(ref[t, pl.ds(k, 1), :], (SUBLANES, p))[None]

        def sa_group(g, sa):
            k0 = pl.multiple_of(g * RWKV_KEY_GROUP, RWKV_KEY_GROUP)
            for j in range(RWKV_KEY_GROUP):
                sa = sa + s_ref[k0 + j] * row(kk_ref, k0 + j)
            return sa

        zero = jnp.zeros((V_TILES, SUBLANES, p), F32)
        sa = lax.fori_loop(0, RWKV_HEAD // RWKV_KEY_GROUP, sa_group, zero)
        vt = v_ref[t].reshape(V_TILES, SUBLANES, p)

        def update_group(g, y):
            k0 = pl.multiple_of(g * RWKV_KEY_GROUP, RWKV_KEY_GROUP)
            for j in range(RWKV_KEY_GROUP):
                k = k0 + j
                sn = s_ref[k] * row(w_ref, k) - sa * row(kka_ref, k) + vt * row(kd_ref, k)
                s_ref[k] = sn
                y = y + sn * row(r_ref, k)
            return y

        y = lax.fori_loop(0, RWKV_HEAD // RWKV_KEY_GROUP, update_group, zero).reshape(RWKV_HEAD, p)
        yf_ref[t] = y
        yb_ref[n_t - 1 - t] = y
        return carry

    lax.fori_loop(0, n_t, step, 0)


def rwkv_scan_pallas(rk, vv, w, kka, kd, n_ctx):
    n_steps, n, p = rk.shape
    tb = RWKV_TIME_BLOCK
    assert n == RWKV_HEAD and p == LANES and n_steps % tb == 0 and n_ctx % tb == 0
    nb, nc = n_steps // tb, n_ctx // tb
    fwd = pl.BlockSpec((tb, n, p), lambda g: (g, 0, 0))
    bwd = pl.BlockSpec((tb, n, p), lambda g: (jnp.where(g < nc, nc - 1 - g, nb + nc - 1 - g), 0, 0))
    out = jax.ShapeDtypeStruct((n_steps, n, p), F32)
    return pl.pallas_call(
        _rwkv_scan_kernel,
        grid=(nb,),
        in_specs=[fwd, bwd] * 5,
        out_specs=[fwd, bwd],
        out_shape=[out, out],
        scratch_shapes=[pltpu.VMEM((n, V_TILES, SUBLANES, LANES), F32), pltpu.VMEM((6, tb, n, LANES), F32)],
        compiler_params=pltpu.CompilerParams(dimension_semantics=("arbitrary",), vmem_limit_bytes=VMEM_LIMIT_BYTES),
        name="rwkv_scan",
    )(rk, rk, vv, vv, w, w, kka, kka, kd, kd)


ATTN_TQ = 256
ATTN_TK = 256


def _attn_kernel(q_ref, k_ref, v_ref, g_ref, o_ref, s_ref):
    tq = q_ref.shape[1]
    n_chunks = k_ref.shape[1]
    n_tiles = ATTN_TK // LANES
    pair_out = []
    for hp in range(MLA_HEADS // 2):
        o_pair = jnp.zeros((tq, LANES), F32)
        for h in (2 * hp, 2 * hp + 1):
            qh = q_ref[0, :, pl.ds(LANES * h, LANES)]

            m_acc = jnp.full((tq, LANES), -jnp.inf, F32)
            for c in range(n_chunks):
                s = jnp.dot(qh, k_ref[0, c, pl.ds(LANES * h, LANES), :], preferred_element_type=F32)
                s_ref[c] = s
                for j in range(n_tiles):
                    m_acc = jnp.maximum(m_acc, s[:, LANES * j:LANES * (j + 1)])
            m_full = jnp.broadcast_to(jnp.max(m_acc, -1, keepdims=True), (tq, LANES))

            l_acc = jnp.zeros((tq, LANES), F32)
            acc = jnp.zeros((tq, LANES), F32)
            for c in range(n_chunks):
                s = s_ref[c]
                ps = []
                for j in range(n_tiles):
                    p = jnp.exp2(s[:, LANES * j:LANES * (j + 1)] - m_full)
                    l_acc = l_acc + p
                    ps.append(p.astype(BF16))
                vh = v_ref[0, pl.ds(c * ATTN_TK, ATTN_TK), pl.ds(LANES * h, LANES)]
                acc = acc + jnp.dot(jnp.concatenate(ps, -1), vh, preferred_element_type=F32)
            o_pair = o_pair + acc / jnp.sum(l_acc, -1, keepdims=True)
        pair_out.append(o_pair)
    o = jnp.concatenate(pair_out, -1)
    o_ref[0] = o * lax.rsqrt(jnp.mean(jnp.square(o), -1, keepdims=True) + 1e-6) * g_ref[...]


def mla_attention_pallas(q, kt, v, out_norm, q_start, n_q, n_kv):
    B, _, W = q.shape
    return pl.pallas_call(
        _attn_kernel,
        grid=(B, n_q),
        in_specs=[
            pl.BlockSpec((1, ATTN_TQ, W), lambda b, i: (b, i + q_start, 0)),
            pl.BlockSpec((1, n_kv, W, ATTN_TK), lambda b, i: (b, 0, 0, 0)),
            pl.BlockSpec((1, n_kv * ATTN_TK, W), lambda b, i: (b, 0, 0)),
            pl.BlockSpec((1, GROUP), lambda b, i: (0, 0)),
        ],
        out_specs=pl.BlockSpec((1, ATTN_TQ, GROUP), lambda b, i: (b, i, 0)),
        out_shape=jax.ShapeDtypeStruct((B, n_q * ATTN_TQ, GROUP), F32),
        scratch_shapes=[pltpu.VMEM((n_kv, ATTN_TQ, ATTN_TK), F32)],
        compiler_params=pltpu.CompilerParams(dimension_semantics=("arbitrary", "arbitrary"),
                                             vmem_limit_bytes=VMEM_LIMIT_BYTES),
        name="mla_attention",
    )(q, kt, v, out_norm[None])


MLA_PA_COLS = 4 * LANES
MLA_W = MLA_HEADS * LANES


def _rot_cols(w):
    h = w.shape[1] // 2
    return jnp.concatenate([-w[:, h:], w[:, :h]], 1)


def mla_section_weight(w_a):
    z = jnp.zeros((w_a.shape[0], LANES // 2), w_a.dtype)
    kr = w_a[:, A_KR:A_KR + MLA_ROPE]
    return jnp.concatenate([w_a[:, A_CKV:A_CKV + MLA_KV_RANK], kr, _rot_cols(kr), z,
                            w_a[:, A_CQ:A_CQ + MLA_Q_RANK], z], 1)


def mla_prep_weights(w_uq, w_ukv):
    H, DN, DR, DV = MLA_HEADS, MLA_NOPE, MLA_ROPE, MLA_V
    wq = jnp.zeros((2 * LANES, MLA_W), F32)
    wq_rot = jnp.zeros((2 * LANES, MLA_W), F32)
    wk = jnp.zeros((2 * LANES, MLA_W), F32)
    wv = jnp.zeros((LANES, MLA_W), F32)
    place = jnp.eye(DR, dtype=F32)
    for h in range(H):
        q_h = w_uq[:, h * (DN + DR):(h + 1) * (DN + DR)]
        wq = wq.at[:MLA_Q_RANK, LANES * h:LANES * h + DN + DR].set(q_h)
        wq_rot = wq_rot.at[:MLA_Q_RANK, LANES * h + DN:LANES * h + DN + DR].set(_rot_cols(q_h[:, DN:]))
        kv_h = w_ukv[:, h * (DN + DV):(h + 1) * (DN + DV)]
        wk = wk.at[:MLA_KV_RANK, LANES * h:LANES * h + DN].set(kv_h[:, :DN])
        wk = wk.at[LANES:LANES + DR, LANES * h + DN:LANES * h + DN + DR].set(place)
        v0 = LANES * h + DV * (h % 2)
        wv = wv.at[:, v0:v0 + DV].set(kv_h[:, DN:])
    return wq.astype(BF16), wq_rot.astype(BF16), wk.astype(BF16), wv.astype(BF16)


def mla_tables(cos, sin, n_ctx):
    H, DN, DR = MLA_HEADS, MLA_NOPE, MLA_ROPE
    L = cos.shape[0]
    cf = jnp.concatenate([jnp.ones((n_ctx, DR), F32), jnp.concatenate([cos, cos], -1)], 0)
    sf = jnp.concatenate([jnp.zeros((n_ctx, DR), F32), jnp.concatenate([sin, sin], -1)], 0)
    S = n_ctx + L
    tab_k = jnp.concatenate([cf, sf, jnp.zeros((S, LANES - 2 * DR), F32)], -1)
    scale = (DN + DR) ** -0.5 * math.log2(math.e)
    zpad = jnp.zeros((S, LANES - DN - DR), F32)
    qc_h = jnp.concatenate([jnp.full((S, DN), scale, F32), cf * scale, zpad], -1)
    qs_h = jnp.concatenate([jnp.zeros((S, DN), F32), sf * scale, zpad], -1)
    return tab_k, jnp.tile(qc_h, (1, H)), jnp.tile(qs_h, (1, H))


def _mla_prep_kernel(pa_ref, tk_ref, tqc_ref, tqs_ref, kvn_ref, qn_ref, wq_ref, wqr_ref, wk_ref, wv_ref,
                     q_ref, kt_ref, v_ref):
    pa = pa_ref[0]
    ckv = pa[:, :LANES]
    ckv = ckv * lax.rsqrt(jnp.mean(jnp.square(ckv), -1, keepdims=True) + 1e-6) * kvn_ref[...]
    t = pa[:, LANES:2 * LANES] * tk_ref[...]
    kr = t + pltpu.roll(t, LANES - MLA_ROPE, 1)
    kr = jnp.where(lax.broadcasted_iota(jnp.int32, kr.shape, 1) < MLA_ROPE, kr, 0.0)
    k = jnp.dot(jnp.concatenate([ckv, kr], -1).astype(BF16), wk_ref[...], preferred_element_type=F32)
    kt_ref[0, 0] = k.T.astype(BF16)
    v_ref[0] = jnp.dot(ckv.astype(BF16), wv_ref[...], preferred_element_type=F32).astype(BF16)
    cq = pa[:, 2 * LANES:]
    ms = jnp.sum(jnp.square(cq), -1, keepdims=True) * (1.0 / MLA_Q_RANK)
    cq = (cq * lax.rsqrt(ms + 1e-6) * qn_ref[...]).astype(BF16)
    qa = jnp.dot(cq, wq_ref[...], preferred_element_type=F32)
    qb = jnp.dot(cq, wqr_ref[...], preferred_element_type=F32)
    q_ref[0] = (qa * tqc_ref[...] + qb * tqs_ref[...]).astype(BF16)


def mla_prep_pallas(pa, tables, q_norm, kv_norm, weights):
    B, S, _ = pa.shape
    T = ATTN_TK
    assert S % T == 0
    tab_k, tab_qc, tab_qs = tables
    wq, wq_rot, wk, wv = weights
    qn = jnp.concatenate([q_norm, jnp.zeros((2 * LANES - MLA_Q_RANK,), F32)])[None]

    def full(a):
        return pl.BlockSpec(a.shape, lambda b, i: (0,) * a.ndim)

    def rows(width):
        return pl.BlockSpec((T, width), lambda b, i: (i, 0))

    return pl.pallas_call(
        _mla_prep_kernel,
        grid=(B, S // T),
        in_specs=[pl.BlockSpec((1, T, MLA_PA_COLS), lambda b, i: (b, i, 0)),
                  rows(LANES), rows(MLA_W), rows(MLA_W),
                  pl.BlockSpec((1, LANES), lambda b, i: (0, 0)), full(qn), full(wq), full(wq_rot), full(wk), full(wv)],
        out_specs=[pl.BlockSpec((1, T, MLA_W), lambda b, i: (b, i, 0)),
                   pl.BlockSpec((1, 1, MLA_W, T), lambda b, i: (b, i, 0, 0)),
                   pl.BlockSpec((1, T, MLA_W), lambda b, i: (b, i, 0))],
        out_shape=[jax.ShapeDtypeStruct((B, S, MLA_W), BF16),
                   jax.ShapeDtypeStruct((B, S // T, MLA_W, T), BF16),
                   jax.ShapeDtypeStruct((B, S, MLA_W), BF16)],
        compiler_params=pltpu.CompilerParams(dimension_semantics=("arbitrary", "arbitrary")),
        name="mla_prep",
    )(pa, tab_k, tab_qc, tab_qs, kv_norm[None], qn, wq, wq_rot, wk, wv)


LRU_CHUNK = 256
LRU_HALO = SUBLANES


def _lru_kernel(xl_ref, xc_ref, wbd_ref, bias_ref, c8_ref, cw_ref, cb_ref, gn_ref,
                yl_ref, yc_ref, xs_l, xs_c, hf_l, hf_c, a_s, b_s, hb_s):
    C = GROUP
    CH = LRU_CHUNK
    L = xl_ref.shape[1]
    Lc = xc_ref.shape[1]

    def stage(x_ref, xs, n):
        xs[pl.ds(0, LRU_HALO), :] = jnp.zeros((LRU_HALO, C), F32)
        xs[pl.ds(LRU_HALO + n, LRU_HALO), :] = jnp.zeros((LRU_HALO, C), F32)

        def cp(i, c):
            r0 = pl.multiple_of(i * CH, CH)
            xs[pl.ds(LRU_HALO + r0, CH), :] = x_ref[0, pl.ds(r0, CH), pl.ds(0, C)]
            return c

        lax.fori_loop(0, n // CH, cp, 0)

    stage(xl_ref, xs_l, L)
    stage(xc_ref, xs_c, Lc)

    def coeffs(xs, base, d):
        xv = xs[pl.ds(base, CH + 2 * LRU_HALO), :]
        u = cb_ref[...]
        for j in range(LRU_CONV):
            o = LRU_HALO - LRU_CONV_LEFT + j
            u = u + xv[o:o + CH] * cw_ref[pl.ds(j, 1), :]
        z = jnp.dot(u.astype(BF16), wbd_ref[:, pl.ds(d * 2 * C, 2 * C)], preferred_element_type=F32)
        z = z + bias_ref[:, pl.ds(d * 2 * C, 2 * C)]
        r = jax.nn.sigmoid(z[:, :C])
        i = jax.nn.sigmoid(z[:, C:])
        log_a = r * c8_ref[pl.ds(d, 1), :]
        a = jnp.exp(log_a)
        a_s[...] = a
        b_s[...] = jnp.sqrt(-jnp.tanh(log_a) * (a * a + 1.0)) * (i * u)

    def row_scan(h, out_ref, out_base, reverse):
        n_groups = CH // SUBLANES

        def group(g, h):
            r0 = pl.multiple_of((n_groups - 1 - g if reverse else g) * SUBLANES, SUBLANES)
            for j in (reversed(range(SUBLANES)) if reverse else range(SUBLANES)):
                a_t = jnp.broadcast_to(a_s[pl.ds(r0 + j, 1), :], (SUBLANES, C))
                b_t = jnp.broadcast_to(b_s[pl.ds(r0 + j, 1), :], (SUBLANES, C))
                h = a_t * h + b_t
                out_ref[pl.ds(out_base + r0 + j, 1), :] = h[0:1, :]
            return h

        return lax.fori_loop(0, n_groups, group, h)

    h0 = jnp.zeros((SUBLANES, C), F32)

    h = h0
    for ci in range(Lc // CH):
        coeffs(xs_c, ci * CH, 0)
        h = row_scan(h, hf_c, ci * CH, False)

    def fwd_chunk(ci, h):
        base = pl.multiple_of(ci * CH, CH)
        coeffs(xs_l, base, 0)
        return row_scan(h, hf_l, base, False)

    lax.fori_loop(0, L // CH, fwd_chunk, h)

    def combine(x_ref, hf, base, y_ref):
        hl = hf[pl.ds(base, CH), :] + hb_s[...]
        g = jax.nn.gelu(x_ref[0, pl.ds(base, CH), pl.ds(C, C)])
        v = hl * g
        y = v * lax.rsqrt(jnp.mean(jnp.square(v), -1, keepdims=True) + 1e-6) * gn_ref[...]
        y_ref[0, pl.ds(base, CH), :] = y

    h = h0
    for ci in reversed(range(Lc // CH)):
        coeffs(xs_c, ci * CH, 1)
        h = row_scan(h, hb_s, 0, True)
        combine(xc_ref, hf_c, ci * CH, yc_ref)

    def bwd_chunk(k, h):
        base = pl.multiple_of((L // CH - 1 - k) * CH, CH)
        coeffs(xs_l, base, 1)
        h = row_scan(h, hb_s, 0, True)
        combine(xl_ref, hf_l, base, yl_ref)
        return h

    lax.fori_loop(0, L // CH, bwd_chunk, h)


def rglru_pallas(xg_l, xg_c, conv_w, conv_b, w_r, b_r, w_i, b_i, lam, out_norm):
    B, L, _ = xg_l.shape
    Lc = xg_c.shape[1]
    C = GROUP
    assert L % LRU_CHUNK == 0 and Lc % LRU_CHUNK == 0

    def bd(w):
        return jax.scipy.linalg.block_diag(*[w[n] for n in range(LRU_BLOCKS)])

    wbd = jnp.concatenate([bd(w_r[0]), bd(w_i[0]), bd(w_r[1]), bd(w_i[1])], 1).astype(BF16)
    bias = jnp.concatenate([b_r[0], b_i[0], b_r[1], b_i[1]])[None]
    c8 = -LRU_C * jax.nn.softplus(-lam)

    def full(shape):
        return pl.BlockSpec(shape, lambda b: (0,) * len(shape))

    return pl.pallas_call(
        _lru_kernel,
        grid=(B,),
        in_specs=[
            pl.BlockSpec((1, L, 2 * C), lambda b: (b, 0, 0)),
            pl.BlockSpec((1, Lc, 2 * C), lambda b: (b, 0, 0)),
            full((C, 4 * C)), full((1, 4 * C)), full((2, C)), full((LRU_CONV, C)), full((1, C)), full((1, C)),
        ],
        out_specs=[
            pl.BlockSpec((1, L, C), lambda b: (b, 0, 0)),
            pl.BlockSpec((1, Lc, C), lambda b: (b, 0, 0)),
        ],
        out_shape=[jax.ShapeDtypeStruct((B, L, C), F32), jax.ShapeDtypeStruct((B, Lc, C), F32)],
        scratch_shapes=[
            pltpu.VMEM((L + 2 * LRU_HALO, C), F32),
            pltpu.VMEM((Lc + 2 * LRU_HALO, C), F32),
            pltpu.VMEM((L, C), F32),
            pltpu.VMEM((Lc, C), F32),
            pltpu.VMEM((LRU_CHUNK, C), F32),
            pltpu.VMEM((LRU_CHUNK, C), F32),
            pltpu.VMEM((LRU_CHUNK, C), F32),
        ],
        compiler_params=pltpu.CompilerParams(dimension_semantics=("arbitrary",), vmem_limit_bytes=VMEM_LIMIT_BYTES),
        name="rglru",
    )(xg_l, xg_c, wbd, bias, c8, conv_w, conv_b[None], out_norm[None])


HY_T = 256
HY_CB = 8


def _hyena_kernel(k_ref, d_ref, v_ref, x1_ref, x2_ref, o_ref, u_s, acc_s):
    n_rows = v_ref.shape[1]
    T = HY_T
    nb = k_ref.shape[2] // (2 * T)
    bsz = n_rows // nb

    def conv(ci, order):
        acc_s[...] = jnp.zeros_like(acc_s)

        def rolled(e):
            seg = k_ref[order, pl.ds(ci, 1), pl.ds(T * (e + nb), T)]
            return pltpu.roll(jnp.broadcast_to(seg, (T, T)), 0, 1, stride=1, stride_axis=0)

        upper = lax.broadcasted_iota(jnp.int32, (T, T), 1) >= lax.broadcasted_iota(jnp.int32, (T, T), 0)
        r_prev = rolled(-nb)
        for dd in range(-(nb - 1), nb):
            r_cur = rolled(dd)
            tb = jnp.where(upper, r_cur, r_prev).astype(BF16)
            r_prev = r_cur
            j0, j1 = max(0, -dd), min(nb, nb - dd)
            lhs = u_s[pl.ds(bsz * j0, bsz * (j1 - j0)), :]
            dst = pl.ds(bsz * (j0 + dd), bsz * (j1 - j0))
            for hh in range(2):
                cols = pl.ds(hh * (T // 2), T // 2)
                acc_s[dst, cols] = acc_s[dst, cols] + jnp.dot(lhs, tb[:, hh * (T // 2):(hh + 1) * (T // 2)],
                                                              preferred_element_type=F32)

    def channel(ci, carry):
        v = v_ref[ci]
        u_s[...] = v.astype(BF16)
        conv(ci, 0)
        u = x1_ref[ci] * (acc_s[...] + v * d_ref[pl.ds(ci, 1), pl.ds(0, 1)])
        u_s[...] = u.astype(BF16)
        conv(ci, 1)
        o_ref[ci] = x2_ref[ci] * (acc_s[...] + u * d_ref[pl.ds(ci, 1), pl.ds(1, 1)])
        return carry

    lax.fori_loop(0, HY_CB, channel, 0)


def hyena_conv_pallas(kfull, d_skip, zT):
    C3, R, T = zT.shape
    C = C3 // (HY_ORDER + 1)
    two_l = kfull.shape[2]
    assert T == HY_T and C % HY_CB == 0
    n_cb = C // HY_CB
    blk = pl.BlockSpec((HY_CB, R, T), lambda c: (c, 0, 0))
    return pl.pallas_call(
        _hyena_kernel,
        grid=(n_cb,),
        in_specs=[
            pl.BlockSpec((HY_ORDER, HY_CB, two_l), lambda c: (0, c, 0)),
            pl.BlockSpec((HY_CB, HY_ORDER), lambda c: (c, 0)),
            blk,
            pl.BlockSpec((HY_CB, R, T), lambda c: (c + n_cb, 0, 0)),
            pl.BlockSpec((HY_CB, R, T), lambda c: (c + 2 * n_cb, 0, 0)),
        ],
        out_specs=blk,
        out_shape=jax.ShapeDtypeStruct((C, R, T), F32),
        scratch_shapes=[pltpu.VMEM((R, T), BF16), pltpu.VMEM((R, T), F32)],
        compiler_params=pltpu.CompilerParams(dimension_semantics=("arbitrary",), vmem_limit_bytes=VMEM_LIMIT_BYTES),
        name="hyena_conv",
    )(kfull, d_skip.T, zT, zT, zT)


FFN_TF_MAX = 1792


def _ffn_tile(hidden):
    return max(t for t in range(LANES, FFN_TF_MAX + 1, LANES) if hidden % t == 0)


def _swiglu_kernel(be_ref, nu_ref, x_ref, wg_ref, wu_ref, wd_ref, o_ref, acc_ref):
    i = pl.program_id(0)
    f = pl.program_id(1)

    @pl.when(i < nu_ref[0])
    def _():
        x = x_ref[...]
        g = jnp.dot(x, wg_ref[0], preferred_element_type=F32)
        u = jnp.dot(x, wu_ref[0], preferred_element_type=F32)
        h = (jax.nn.silu(g) * u).astype(BF16)
        part = jnp.dot(h, wd_ref[0], preferred_element_type=F32)

        @pl.when(f == 0)
        def _():
            acc_ref[...] = part

        @pl.when(f > 0)
        def _():
            acc_ref[...] = acc_ref[...] + part

    @pl.when(f == pl.num_programs(1) - 1)
    def _():
        o_ref[...] = jnp.where(i < nu_ref[0], acc_ref[...], 0.0)


def grouped_swiglu_pallas(xs, block_expert, n_used, wg, wu, wd):
    n_rows, D = xs.shape
    F = wg.shape[2]
    TM = MOE_BLOCK
    assert n_rows % TM == 0
    tf = _ffn_tile(F)
    n_blocks = n_rows // TM
    grid_spec = pltpu.PrefetchScalarGridSpec(
        num_scalar_prefetch=2,
        grid=(n_blocks, F // tf),
        in_specs=[
            pl.BlockSpec((TM, D), lambda i, f, be, nu: (i, 0)),
            pl.BlockSpec((1, D, tf), lambda i, f, be, nu: (be[i], 0, f)),
            pl.BlockSpec((1, D, tf), lambda i, f, be, nu: (be[i], 0, f)),
            pl.BlockSpec((1, tf, D), lambda i, f, be, nu: (be[i], f, 0)),
        ],
        out_specs=pl.BlockSpec((TM, D), lambda i, f, be, nu: (i, 0)),
        scratch_shapes=[pltpu.VMEM((TM, D), F32)],
    )
    return pl.pallas_call(
        _swiglu_kernel,
        grid_spec=grid_spec,
        out_shape=jax.ShapeDtypeStruct((n_rows, D), F32),
        compiler_params=pltpu.CompilerParams(dimension_semantics=("arbitrary", "arbitrary"),
                                             vmem_limit_bytes=VMEM_LIMIT_BYTES),
        name="grouped_swiglu",
    )(block_expert.astype(jnp.int32), jnp.reshape(n_used, (1,)).astype(jnp.int32), xs, wg, wu, wd)


PROJ_TM = 512


def _residual_ln(x, branch, gate_ref, g_ref, b_ref):
    s = ALPHA * x + gate_ref[0] * branch
    mu = jnp.mean(s, -1, keepdims=True)
    d = s - mu
    var = jnp.mean(jnp.square(d), -1, keepdims=True)
    return d * lax.rsqrt(var + 1e-5) * g_ref[...] + b_ref[...]


def _in_proj_kernel(*refs, has_ln, n_w):
    x_ref = refs[0]
    pos = 1
    x = x_ref[...]
    if has_ln:
        branch_ref, gate_ref, g_ref, b_ref = refs[1:5]
        pos = 5
        x = _residual_ln(x, branch_ref[...], gate_ref, g_ref, b_ref)
    shift_ref, scale_ref = refs[pos:pos + 2]
    w_refs = refs[pos + 2:pos + 2 + n_w]
    out_refs = refs[pos + 2 + n_w:]
    if has_ln:
        out_refs[0][...] = x
        out_refs = out_refs[1:]
    h = (x * (1.0 + scale_ref[0]) + shift_ref[0]).astype(BF16)
    for w_ref, o_ref in zip(w_refs, out_refs):
        o_ref[...] = jnp.dot(h, w_ref[...], preferred_element_type=F32)


def in_proj_pallas(x, shift, scale, weights, residual=None):
    M, D = x.shape
    tm = PROJ_TM
    G = shift.shape[0]
    tiles_per_group = M // G // tm
    assert M % (G * tm) == 0
    row = pl.BlockSpec((tm, D), lambda i: (i, 0))
    grp = pl.BlockSpec((1, 1, D), lambda i: (i // tiles_per_group, 0, 0))
    vec = pl.BlockSpec((1, D), lambda i: (0, 0))
    args, specs = [x], [row]
    has_ln = residual is not None
    if has_ln:
        branch, first_row, gate, ln_g, ln_b = residual
        assert first_row % tm == 0
        off = first_row // tm
        args += [branch, gate, ln_g[None], ln_b[None]]
        specs += [pl.BlockSpec((tm, D), lambda i: (i + off, 0)), grp, vec, vec]
    args += [shift, scale] + list(weights)
    specs += [grp, grp] + [pl.BlockSpec(w.shape, lambda i: (0, 0)) for w in weights]
    out_shape = [jax.ShapeDtypeStruct((M, w.shape[1]), F32) for w in weights]
    out_specs = [pl.BlockSpec((tm, w.shape[1]), lambda i: (i, 0)) for w in weights]
    if has_ln:
        out_shape = [jax.ShapeDtypeStruct((M, D), F32)] + out_shape
        out_specs = [row] + out_specs
    outs = pl.pallas_call(
        partial(_in_proj_kernel, has_ln=has_ln, n_w=len(weights)),
        grid=(M // tm,),
        in_specs=specs,
        out_specs=out_specs,
        out_shape=out_shape,
        compiler_params=pltpu.CompilerParams(dimension_semantics=("arbitrary",), vmem_limit_bytes=VMEM_LIMIT_BYTES),
        name="in_proj",
    )(*args)
    return (outs[0], outs[1:]) if has_ln else (None, outs)


def _out_proj_kernel(a_ref, b_ref, c_ref, d_ref, x_ref, w_ref, gate_ref, g_ref, bb_ref, shift_ref, scale_ref,
                     xn_ref, f_ref):
    y = jnp.concatenate([a_ref[...], b_ref[...], c_ref[...], d_ref[...]], -1).astype(BF16)
    y = jnp.dot(y, w_ref[...], preferred_element_type=F32)
    xn = _residual_ln(x_ref[...], y, gate_ref, g_ref, bb_ref)
    xn_ref[...] = xn
    f_ref[...] = (xn * (1.0 + scale_ref[0]) + shift_ref[0]).astype(BF16)


def out_proj_pallas(parts, x, w_out, gate, ln_g, ln_b, shift, scale):
    M, D = x.shape
    tm = PROJ_TM
    G = gate.shape[0]
    tiles_per_group = M // G // tm
    assert M % (G * tm) == 0
    row = pl.BlockSpec((tm, D), lambda i: (i, 0))
    part = pl.BlockSpec((tm, GROUP), lambda i: (i, 0))
    grp = pl.BlockSpec((1, 1, D), lambda i: (i // tiles_per_group, 0, 0))
    vec = pl.BlockSpec((1, D), lambda i: (0, 0))
    return pl.pallas_call(
        _out_proj_kernel,
        grid=(M // tm,),
        in_specs=[part] * 4 + [row, pl.BlockSpec(w_out.shape, lambda i: (0, 0)), grp, vec, vec, grp, grp],
        out_specs=[row, row],
        out_shape=[jax.ShapeDtypeStruct((M, D), F32), jax.ShapeDtypeStruct((M, D), BF16)],
        compiler_params=pltpu.CompilerParams(dimension_semantics=("arbitrary",), vmem_limit_bytes=VMEM_LIMIT_BYTES),
        name="out_proj",
    )(*parts, x, w_out, gate, ln_g[None], ln_b[None], shift, scale)


def modulate(x, shift, scale):
    return x * (1.0 + scale) + shift


def layer_norm(x, g, b, eps=1e-5):
    mu = jnp.mean(x, -1, keepdims=True)
    var = jnp.mean(jnp.square(x - mu), -1, keepdims=True)
    return (x - mu) * lax.rsqrt(var + eps) * g + b


def rms_norm(x, g, eps=1e-6):
    return x * lax.rsqrt(jnp.mean(jnp.square(x), -1, keepdims=True) + eps) * g


def dwconv(x, w, b, left):
    K = w.shape[0]
    L = x.shape[1]
    xp = jnp.pad(x, ((0, 0), (left, K - 1 - left), (0, 0)))
    return sum(xp[:, j:j + L] * w[j] for j in range(K)) + b


def token_shift(z, mu_prev, mu_next):
    zp = jnp.pad(z, ((0, 0), (1, 1), (0, 0)))
    return z + mu_prev * (zp[:, :-2] - z) + mu_next * (zp[:, 2:] - z)


def axial_rope(rows):
    r, col = jnp.meshgrid(jnp.arange(rows, dtype=F32), jnp.arange(GRID_W, dtype=F32), indexing='ij')
    half = MLA_ROPE // 2
    inv = 1.0 / (ROPE_BASE ** (jnp.arange(0, half, 2, dtype=F32) / half))
    ang = jnp.concatenate([r.reshape(-1, 1) * inv, col.reshape(-1, 1) * inv], -1)
    return jnp.cos(ang), jnp.sin(ang)


def mla_mixer(Pac, Pal, tables, q_norm, kv_norm, w_uq, w_ukv, out_norm, ctx_out):
    Lc, L = Pac.shape[1], Pal.shape[1]
    q, kt, v = mla_prep_pallas(jnp.concatenate([Pac, Pal], 1), tables, q_norm, kv_norm, mla_prep_weights(w_uq, w_ukv))
    nc, nl = Lc // ATTN_TQ, L // ATTN_TQ
    yl = mla_attention_pallas(q, kt, v, out_norm, nc, nl, nc + nl)
    yc = mla_attention_pallas(q, kt, v, out_norm, 0, nc, nc) if ctx_out else None
    return yl, yc


def rwkv_finish(y, bonus, v, gd, g_up, ln_g, ln_b):
    B, L = y.shape[:2]
    mu = jnp.mean(y, -1, keepdims=True)
    var = jnp.mean(jnp.square(y - mu), -1, keepdims=True)
    yn = (y - mu) * lax.rsqrt(var + RWKV_GN_EPS) * ln_g.reshape(RWKV_HEADS, RWKV_HEAD) + ln_b.reshape(RWKV_HEADS, RWKV_HEAD)
    g = jax.nn.sigmoid(gd) @ g_up
    return (yn + bonus * v).reshape(B, L, GROUP) * g


def rwkv_mixer(Pl, Pc, mu_prev, mu_next, w0, w_up, a0, a_up, g_up, k_k, k_a, r_k, ln_g, ln_b, ctx_out):
    B, L = Pl.shape[:2]
    Lc = Pc.shape[1]
    S = Lc + L
    H, N = RWKV_HEADS, RWKV_HEAD
    z = jnp.concatenate([token_shift(Pc, mu_prev, mu_next), token_shift(Pl, mu_prev, mu_next)], 1)

    def heads(t):
        return t.reshape(B, S, H, N)

    r = heads(z[..., C_R:C_R + GROUP])
    k = heads(z[..., C_K:C_K + GROUP])
    v = heads(z[..., C_V:C_V + GROUP])
    kk = k * k_k.reshape(H, N)
    kk = kk * lax.rsqrt(jnp.maximum(jnp.sum(jnp.square(kk), -1, keepdims=True), 1e-24))
    w, kka, kd = [], [], []
    for d in range(2):
        wd = z[..., C_WD + d * RWKV_DECAY_LORA:C_WD + (d + 1) * RWKV_DECAY_LORA]
        ad = z[..., C_AD + d * RWKV_AAA_LORA:C_AD + (d + 1) * RWKV_AAA_LORA]
        log_w = -jnp.exp(-jax.nn.softplus(-(w0[d] + jnp.tanh(wd) @ w_up[d])) - 0.5)
        a = heads(jax.nn.sigmoid(a0[d] + ad @ a_up[d]))
        w.append(heads(jnp.exp(log_w)))
        kka.append(kk * a)
        kd.append(k * (1.0 + (a - 1.0) * k_a.reshape(H, N)))

    def to_scan(lo, hi):
        return jnp.concatenate([jnp.transpose(t, (1, 3, 0, 2)).reshape(S, N, B * H) for t in (lo, hi)], -1)

    yf, yb = rwkv_scan_pallas(to_scan(r, kk), to_scan(v, v), to_scan(*w), to_scan(*kka), to_scan(*kd), Lc)
    y = yf[..., :B * H] + yb[..., B * H:]
    y = jnp.transpose(y.reshape(S, N, B, H), (2, 0, 3, 1))
    bonus = sum(jnp.sum(r * kd_d * r_k, -1, keepdims=True) for kd_d in kd)
    gd = z[..., C_GD:C_GD + RWKV_GATE_LORA]
    out_l = rwkv_finish(y[:, Lc:], bonus[:, Lc:], v[:, Lc:], gd[:, Lc:], g_up, ln_g, ln_b)
    out_c = None
    if ctx_out:
        out_c = rwkv_finish(y[:, :Lc], bonus[:, :Lc], v[:, :Lc], gd[:, :Lc], g_up, ln_g, ln_b)
    return out_l, out_c


def hyena_filters(L, w1, b1, w2, b2, w3):
    t01 = jnp.linspace(0.0, 1.0, L, dtype=F32)[:, None]
    bands = jnp.linspace(1e-4, HY_BANDS - 1, HY_BANDS, dtype=F32)[None, :]
    wpos = (2.0 * math.pi / L) * jnp.arange(L, dtype=F32)[:, None]
    z = jnp.concatenate([t01, jnp.cos(bands * wpos), -jnp.sin(bands * wpos)], -1)
    h = jnp.sin(HY_SIN_FREQ * (z @ w1 + b1))
    h = jnp.sin(HY_SIN_FREQ * (h @ w2 + b2))
    h = (h @ w3).reshape(L, HY_ORDER, 2, GROUP)
    deltas = jnp.abs(jnp.linspace(HY_DECAY_MIN, HY_DECAY_MAX, GROUP, dtype=F32))
    window = jnp.exp(-t01 * deltas) + HY_SHIFT
    return h * window[:, None, None, :]


def hyena_sequence(Pd, conv_w, conv_b, w1, b1, w2, b2, w3, d_skip):
    B, L, _ = Pd.shape
    C = GROUP
    T = HY_T
    nb = L // T
    z = dwconv(Pd, conv_w, conv_b, 1)
    zT = jnp.transpose(z.reshape(B, nb, T, 3 * C), (3, 1, 0, 2)).reshape(3 * C, nb * B, T)
    h = hyena_filters(L, w1, b1, w2, b2, w3)
    hf = jnp.transpose(h[:, :, 0], (1, 2, 0))
    hb = jnp.transpose(h[:, :, 1], (1, 2, 0))
    kfull = jnp.concatenate([jnp.zeros((HY_ORDER, C, 1), F32), jnp.flip(hb[..., 1:], -1), hf], -1)
    oT = hyena_conv_pallas(kfull, d_skip, zT)
    return jnp.transpose(oT.reshape(C, nb, B, T), (2, 1, 3, 0)).reshape(B, L, C)


def swiglu(x, wg, wu, wd):
    n_blocks = x.shape[0] // MOE_BLOCK
    return grouped_swiglu_pallas(x.astype(BF16), jnp.zeros((n_blocks,), jnp.int32), jnp.int32(n_blocks),
                                 wg[None].astype(BF16), wu[None].astype(BF16), wd[None].astype(BF16))


def moe_swiglu(x, router, wg, wu, wd):
    N, D = x.shape
    logits = x @ router
    top_v, top_i = lax.top_k(logits, TOP_K)
    gates = jax.nn.softmax(top_v, axis=-1)
    A = N * TOP_K
    e_flat = top_i.reshape(-1)
    tok_flat = jnp.arange(A, dtype=jnp.int32) // TOP_K
    order = jnp.argsort(e_flat)
    e_sorted = e_flat[order]
    counts = jnp.bincount(e_flat, length=N_EXPERTS)
    starts = jnp.cumsum(counts) - counts
    padded = (counts + MOE_BLOCK - 1) // MOE_BLOCK * MOE_BLOCK
    pends = jnp.cumsum(padded)
    pstarts = pends - padded
    dest = (pstarts[e_sorted] + jnp.arange(A, dtype=jnp.int32) - starts[e_sorted]).astype(jnp.int32)
    n_blocks = -(-A // MOE_BLOCK) + N_EXPERTS
    n_slots = n_blocks * MOE_BLOCK
    block_expert = jnp.clip(jnp.searchsorted(pends, jnp.arange(n_blocks) * MOE_BLOCK, side='right'), 0, N_EXPERTS - 1)
    slot_e = jnp.repeat(block_expert, MOE_BLOCK)
    slot_pos = jnp.arange(n_slots, dtype=jnp.int32) - pstarts[slot_e].astype(jnp.int32)
    slot_src = jnp.clip(starts[slot_e].astype(jnp.int32) + slot_pos, 0, A - 1)
    slot_tok = jnp.where(slot_pos < counts[slot_e], tok_flat[order[slot_src]], N)
    xp = jnp.concatenate([x.astype(BF16), jnp.zeros((1, D), BF16)], 0)
    ys = grouped_swiglu_pallas(xp[slot_tok], block_expert, pends[-1] // MOE_BLOCK,
                               wg.astype(BF16), wu.astype(BF16), wd.astype(BF16))
    slot_of = dest[jnp.argsort(order)].reshape(N, TOP_K)
    return ys[slot_of[:, 0]] * gates[:, 0:1] + ys[slot_of[:, 1]] * gates[:, 1:2]


def kernel(x, c, ctx, c_ctx, ada_w, ada_b, w_in, mla_q_norm, mla_kv_norm, mla_w_uq, mla_w_ukv, mla_out_norm, lru_conv_w, lru_conv_b, lru_w_r, lru_b_r, lru_w_i, lru_b_i, lru_lambda, lru_out_norm, rwkv_mu_prev, rwkv_mu_next, rwkv_w0, rwkv_w_up, rwkv_a0, rwkv_a_up, rwkv_g_up, rwkv_k_k, rwkv_k_a, rwkv_r_k, rwkv_ln_g, rwkv_ln_b, hy_conv_w, hy_conv_b, hy_f_w1, hy_f_b1, hy_f_w2, hy_f_b2, hy_f_w3, hy_d, hy_out_norm, w_out, ln1_g, ln1_b, ln2_g, ln2_b, ffn_w_gate, ffn_w_up, ffn_w_down, moe_router, moe_w_gate, moe_w_up, moe_w_down):
    B, L, D = x.shape
    Lc = ctx.shape[1]
    rows = L // GRID_W
    mla_tabs = mla_tables(*axial_rope(rows), Lc)
    s_lat = jax.nn.silu(c)
    s_ctx = jax.nn.silu(c_ctx)
    Ml, Mc = B * L, B * Lc
    xl, xc = x.reshape(Ml, D), ctx.reshape(Mc, D)
    prev = None
    for li in range(DEPTH):
        ctx_out = li < DEPTH - 1
        mod_l = (s_lat @ ada_w[li] + ada_b[li]).reshape(B, 6, 1, D)
        mod_c = (s_ctx @ ada_w[li] + ada_b[li]).reshape(6, 1, D)
        ml = [mod_l[:, q] for q in range(6)]
        mc = [mod_c[q][None] for q in range(6)]

        w_in_l = w_in[li]
        w_secs = [w.astype(BF16) for w in (mla_section_weight(w_in_l[:, :B_X]), w_in_l[:, B_X:C_OFF],
                                           w_in_l[:, C_OFF:D_OFF], w_in_l[:, D_OFF:])]
        res_l = res_c = None
        if prev is not None:
            out, gate_l, gate_c, g2, b2 = prev
            res_l, res_c = (out, 0, gate_l, g2, b2), (out, Ml, gate_c, g2, b2)
        xl_new, (Pal, Pbl, Pcl, Pdl) = in_proj_pallas(xl, ml[0], ml[1], w_secs, res_l)
        xc_new, ctx_secs = in_proj_pallas(xc, mc[0], mc[1], w_secs if ctx_out else w_secs[:3], res_c)
        if prev is not None:
            xl, xc = xl_new, xc_new
        Pal, Pbl, Pcl, Pdl = (p.reshape(B, L, -1) for p in (Pal, Pbl, Pcl, Pdl))
        Pac, Pbc, Pcc = (p.reshape(B, Lc, -1) for p in ctx_secs[:3])
        a_l, a_c = mla_mixer(Pac, Pal, mla_tabs, mla_q_norm[li], mla_kv_norm[li], mla_w_uq[li], mla_w_ukv[li],
                             mla_out_norm[li], ctx_out)
        b_l, b_c = rglru_pallas(Pbl, Pbc, lru_conv_w[li], lru_conv_b[li], lru_w_r[li], lru_b_r[li], lru_w_i[li],
                                lru_b_i[li], lru_lambda[li], lru_out_norm[li])
        c_l, c_c = rwkv_mixer(Pcl, Pcc, rwkv_mu_prev[li], rwkv_mu_next[li], rwkv_w0[li], rwkv_w_up[li], rwkv_a0[li],
                              rwkv_a_up[li], rwkv_g_up[li], rwkv_k_k[li], rwkv_k_a[li], rwkv_r_k[li],
                              rwkv_ln_g[li], rwkv_ln_b[li], ctx_out)
        d_l = rms_norm(hyena_sequence(Pdl, hy_conv_w[li], hy_conv_b[li], hy_f_w1[li], hy_f_b1[li], hy_f_w2[li],
                                      hy_f_b2[li], hy_f_w3[li], hy_d[li]), hy_out_norm[li])
        w_out_l = w_out[li].astype(BF16)
        xl, tokens = out_proj_pallas([t.reshape(Ml, GROUP) for t in (a_l, b_l, c_l, d_l)], xl, w_out_l,
                                     ml[2], ln1_g[li], ln1_b[li], ml[3], ml[4])
        if ctx_out:
            Pdc = ctx_secs[3].reshape(B, Lc, -1)
            d_c = rms_norm(hyena_sequence(Pdc, hy_conv_w[li], hy_conv_b[li], hy_f_w1[li], hy_f_b1[li], hy_f_w2[li],
                                          hy_f_b2[li], hy_f_w3[li], hy_d[li]), hy_out_norm[li])
            xc, fc = out_proj_pallas([t.reshape(Mc, GROUP) for t in (a_c, b_c, c_c, d_c)], xc, w_out_l,
                                     mc[2], ln1_g[li], ln1_b[li], mc[3], mc[4])
            tokens = jnp.concatenate([tokens, fc], 0)
        j = li // 2
        if li % 2 == 0:
            out = swiglu(tokens, ffn_w_gate[j], ffn_w_up[j], ffn_w_down[j])
        else:
            out = moe_swiglu(tokens, moe_router[j], moe_w_gate[j], moe_w_up[j], moe_w_down[j])
        prev = (out, ml[5], mc[5], ln2_g[li], ln2_b[li])
    out, gate_l, _, g2, b2 = prev
    xl, _ = in_proj_pallas(xl, ml[0], ml[1], [], (out, 0, gate_l, g2, b2))
    return xl.reshape(B, L, D)
```
